```python
import jax, jax.numpy as jnp
from jax import lax
import numpy as np

D_MODEL = 2048
BATCH = 8
SEQ = 2048
DEPTH = 4

CHUNK = 64
Q_BLOCK = 128
N_MIXERS = 3
N_MLA_LAYERS = (DEPTH + 2) // 3
N_SB_LAYERS = (DEPTH + 1) // 3
N_CA_LAYERS = DEPTH // 3

MLA_HEADS = 16
MLA_Q_LORA = 512
MLA_KV_LORA = 512
MLA_NOPE = 128
MLA_ROPE = 64
MLA_V = 128
ROPE_THETA = 10000.0

SB_HEADS = 16
SB_HEAD_DIM = 128

CA_HEADS = 16
CA_HEAD_DIM = 128
CA_LEFT_CHUNKS = 8
REL_CLIP_LEFT = 128
REL_TABLE = REL_CLIP_LEFT + CHUNK

D_FF = 4 * D_MODEL

LN_EPS = 1e-5
RMS_EPS = 1e-6
DEEPNORM_ALPHA = (2.0 * DEPTH) ** 0.25
DEEPNORM_BETA = (8.0 * DEPTH) ** -0.25
NEG = -1e30

kernel_name = "hybrid_streaming_mla_stickbreak_chunkrel_deepnorm"


def layer_norm(x, g, b):
    xf = x.astype(jnp.float32)
    mu = jnp.mean(xf, -1, keepdims=True)
    var = jnp.mean(jnp.square(xf - mu), -1, keepdims=True)
    return ((xf - mu) * lax.rsqrt(var + LN_EPS) * g.astype(jnp.float32) + b.astype(jnp.float32)).astype(x.dtype)


def rms_norm(x, g):
    xf = x.astype(jnp.float32)
    return (xf * lax.rsqrt(jnp.mean(xf * xf, -1, keepdims=True) + RMS_EPS) * g.astype(jnp.float32)).astype(x.dtype)


def rope(x, pos):
    half = x.shape[-1] // 2
    inv = ROPE_THETA ** (-jnp.arange(half, dtype=jnp.float32) / half)
    ang = pos.astype(jnp.float32)[:, None] * inv[None, :]
    cos = jnp.cos(ang)[None, :, None, :]
    sin = jnp.sin(ang)[None, :, None, :]
    x1 = x[..., :half].astype(jnp.float32)
    x2 = x[..., half:].astype(jnp.float32)
    return jnp.concatenate([x1 * cos - x2 * sin, x2 * cos + x1 * sin], -1).astype(x.dtype)


def mla_mixer(x, w_down, q_norm_g, w_uq, kv_norm_g, w_ukv, w_o):
    B, S, _ = x.shape
    H = MLA_HEADS
    down = x @ w_down
    c_q, c_kv, k_rope = jnp.split(down, [MLA_Q_LORA, MLA_Q_LORA + MLA_KV_LORA], axis=-1)
    q = (rms_norm(c_q, q_norm_g) @ w_uq).reshape(B, S, H, MLA_NOPE + MLA_ROPE)
    kv = (rms_norm(c_kv, kv_norm_g) @ w_ukv).reshape(B, S, H, MLA_NOPE + MLA_V)
    q_nope, q_rope = q[..., :MLA_NOPE], q[..., MLA_NOPE:]
    k_nope, v = kv[..., :MLA_NOPE], kv[..., MLA_NOPE:]
    pos = jnp.arange(S)
    q_rope = rope(q_rope, pos)
    k_rope = rope(k_rope[:, :, None, :], pos)[:, :, 0, :]
    scale = (MLA_NOPE + MLA_ROPE) ** -0.5
    chunk_id = pos // CHUNK
    outs = []
    for qs in range(0, S, Q_BLOCK):
        ke = qs + Q_BLOCK
        s = (jnp.einsum('bqhd,bkhd->bhqk', q_nope[:, qs:ke], k_nope[:, :ke])
             + jnp.einsum('bqhr,bkr->bhqk', q_rope[:, qs:ke], k_rope[:, :ke])).astype(jnp.float32) * scale
        mask = chunk_id[None, :ke] <= chunk_id[qs:ke, None]
        p = jax.nn.softmax(jnp.where(mask, s, NEG), axis=-1).astype(v.dtype)
        outs.append(jnp.einsum('bhqk,bkhd->bqhd', p, v[:, :ke]))
    o = jnp.concatenate(outs, axis=1).reshape(B, S, H * MLA_V)
    return o @ w_o


def stick_breaking_mixer(x, w_qkv, w_o):
    B, S, _ = x.shape
    H, Dh = SB_HEADS, SB_HEAD_DIM
    q, k, v = jnp.split((x @ w_qkv).reshape(B, S, 3, H, Dh), 3, axis=2)
    q, k, v = q[:, :, 0], k[:, :, 0], v[:, :, 0]
    scale = Dh ** -0.5
    pos = jnp.arange(S)
    outs = []
    for qs in range(0, S, Q_BLOCK):
        ke = qs + Q_BLOCK
        z = jnp.einsum('bqhd,bkhd->bhqk', q[:, qs:ke], k[:, :ke]).astype(jnp.float32) * scale
        strict = pos[None, :ke] < pos[qs:ke, None]
        log_beta = jax.nn.log_sigmoid(z)
        log_1m = jnp.where(strict, jax.nn.log_sigmoid(-z), 0.0)
        log_surv = lax.cumsum(log_1m, axis=3, reverse=True) - log_1m
        a = jnp.where(strict, jnp.exp(log_beta + log_surv), 0.0).astype(v.dtype)
        outs.append(jnp.einsum('bhqk,bkhd->bqhd', a, v[:, :ke]))
    o = jnp.concatenate(outs, axis=1).reshape(B, S, H * Dh)
    return o @ w_o


def chunked_relpos_mixer(x, w_qkv, rel_bias, w_o):
    B, S, _ = x.shape
    H, Dh = CA_HEADS, CA_HEAD_DIM
    n_chunks = S // CHUNK
    pad = CA_LEFT_CHUNKS * CHUNK
    band = pad + CHUNK
    q, k, v = jnp.split((x @ w_qkv).reshape(B, S, 3, H, Dh), 3, axis=2)
    q, k, v = q[:, :, 0], k[:, :, 0], v[:, :, 0]
    qc = q.reshape(B, n_chunks, CHUNK, H, Dh)
    k_pad = jnp.pad(k, ((0, 0), (pad, 0), (0, 0), (0, 0)))
    v_pad = jnp.pad(v, ((0, 0), (pad, 0), (0, 0), (0, 0)))
    key_idx = jnp.arange(n_chunks)[:, None] * CHUNK + jnp.arange(band)[None, :]
    kb = k_pad[:, key_idx]
    vb = v_pad[:, key_idx]
    s = jnp.einsum('bcqhd,bckhd->bhcqk', qc, kb).astype(jnp.float32) * (Dh ** -0.5)
    rel = jnp.arange(band)[None, :] - pad - jnp.arange(CHUNK)[:, None]
    bias = rel_bias[jnp.clip(rel, -REL_CLIP_LEFT, CHUNK - 1) + REL_CLIP_LEFT]
    s = s + jnp.transpose(bias, (2, 0, 1)).astype(jnp.float32)[None, :, None]
    valid = (key_idx - pad) >= 0
    s = jnp.where(valid[None, None, :, None, :], s, NEG)
    p = jax.nn.softmax(s, axis=-1).astype(vb.dtype)
    o = jnp.einsum('bhcqk,bckhd->bcqhd', p, vb).reshape(B, S, H * Dh)
    return o @ w_o


def sq_relu_mlp(x, w_in, w_out):
    return jnp.square(jax.nn.relu(x @ w_in)) @ w_out


def _fwd_setup_inputs(seed: int = 0) -> dict:
    key = jax.random.key(seed)
    keys = iter(jax.random.split(key, 32))

    def nrm(shape, scale):
        return jax.random.normal(next(keys), shape, jnp.float32) * scale

    x = nrm((BATCH, SEQ, D_MODEL), 1.0)
    ln_mix_g = 1.0 + nrm((DEPTH, D_MODEL), 0.02)
    ln_mix_b = nrm((DEPTH, D_MODEL), 0.02)
    ln_ffn_g = 1.0 + nrm((DEPTH, D_MODEL), 0.02)
    ln_ffn_b = nrm((DEPTH, D_MODEL), 0.02)
    ffn_w_in = nrm((DEPTH, D_MODEL, D_FF), D_MODEL ** -0.5 * DEEPNORM_BETA)
    ffn_w_out = nrm((DEPTH, D_FF, D_MODEL), D_FF ** -0.5 * DEEPNORM_BETA)

    nA = N_MLA_LAYERS
    mla_w_down = nrm((nA, D_MODEL, MLA_Q_LORA + MLA_KV_LORA + MLA_ROPE), D_MODEL ** -0.5)
    mla_q_norm_g = 1.0 + nrm((nA, MLA_Q_LORA), 0.02)
    mla_w_uq = nrm((nA, MLA_Q_LORA, MLA_HEADS * (MLA_NOPE + MLA_ROPE)), MLA_Q_LORA ** -0.5)
    mla_kv_norm_g = 1.0 + nrm((nA, MLA_KV_LORA), 0.02)
    w_uk = nrm((nA, MLA_KV_LORA, MLA_HEADS, MLA_NOPE), MLA_KV_LORA ** -0.5)
    w_uv = nrm((nA, MLA_KV_LORA, MLA_HEADS, MLA_V), MLA_KV_LORA ** -0.5 * DEEPNORM_BETA)
    mla_w_ukv = jnp.concatenate([w_uk, w_uv], -1).reshape(nA, MLA_KV_LORA, MLA_HEADS * (MLA_NOPE + MLA_V))
    mla_w_o = nrm((nA, MLA_HEADS * MLA_V, D_MODEL), (MLA_HEADS * MLA_V) ** -0.5 * DEEPNORM_BETA)

    nB = N_SB_LAYERS
    sb_qk = nrm((nB, D_MODEL, 2, SB_HEADS * SB_HEAD_DIM), D_MODEL ** -0.5)
    sb_v = nrm((nB, D_MODEL, 1, SB_HEADS * SB_HEAD_DIM), D_MODEL ** -0.5 * DEEPNORM_BETA)
    sb_w_qkv = jnp.concatenate([sb_qk, sb_v], 2).reshape(nB, D_MODEL, 3 * SB_HEADS * SB_HEAD_DIM)
    sb_w_o = nrm((nB, SB_HEADS * SB_HEAD_DIM, D_MODEL), (SB_HEADS * SB_HEAD_DIM) ** -0.5 * DEEPNORM_BETA)

    nC = N_CA_LAYERS
    ca_qk = nrm((nC, D_MODEL, 2, CA_HEADS * CA_HEAD_DIM), D_MODEL ** -0.5)
    ca_v = nrm((nC, D_MODEL, 1, CA_HEADS * CA_HEAD_DIM), D_MODEL ** -0.5 * DEEPNORM_BETA)
    ca_w_qkv = jnp.concatenate([ca_qk, ca_v], 2).reshape(nC, D_MODEL, 3 * CA_HEADS * CA_HEAD_DIM)
    ca_rel_bias = nrm((nC, REL_TABLE, CA_HEADS), 0.5)
    ca_w_o = nrm((nC, CA_HEADS * CA_HEAD_DIM, D_MODEL), (CA_HEADS * CA_HEAD_DIM) ** -0.5 * DEEPNORM_BETA)

    return {"x": x, "ln_mix_g": ln_mix_g, "ln_mix_b": ln_mix_b, "ln_ffn_g": ln_ffn_g, "ln_ffn_b": ln_ffn_b,
            "ffn_w_in": ffn_w_in, "ffn_w_out": ffn_w_out,
            "mla_w_down": mla_w_down, "mla_q_norm_g": mla_q_norm_g, "mla_w_uq": mla_w_uq,
            "mla_kv_norm_g": mla_kv_norm_g, "mla_w_ukv": mla_w_ukv, "mla_w_o": mla_w_o,
            "sb_w_qkv": sb_w_qkv, "sb_w_o": sb_w_o,
            "ca_w_qkv": ca_w_qkv, "ca_rel_bias": ca_rel_bias, "ca_w_o": ca_w_o}


def _fwd_reference(x, ln_mix_g, ln_mix_b, ln_ffn_g, ln_ffn_b, ffn_w_in, ffn_w_out,
              mla_w_down, mla_q_norm_g, mla_w_uq, mla_kv_norm_g, mla_w_ukv, mla_w_o,
              sb_w_qkv, sb_w_o, ca_w_qkv, ca_rel_bias, ca_w_o):
    h = x
    for i in range(DEPTH):
        kind = i % N_MIXERS
        slot = i // N_MIXERS
        if kind == 0:
            m = mla_mixer(h, mla_w_down[slot], mla_q_norm_g[slot], mla_w_uq[slot],
                          mla_kv_norm_g[slot], mla_w_ukv[slot], mla_w_o[slot])
        elif kind == 1:
            m = stick_breaking_mixer(h, sb_w_qkv[slot], sb_w_o[slot])
        else:
            m = chunked_relpos_mixer(h, ca_w_qkv[slot], ca_rel_bias[slot], ca_w_o[slot])
        h = layer_norm(DEEPNORM_ALPHA * h + m, ln_mix_g[i], ln_mix_b[i])
        h = layer_norm(DEEPNORM_ALPHA * h + sq_relu_mlp(h, ffn_w_in[i], ffn_w_out[i]),
                       ln_ffn_g[i], ln_ffn_b[i])
    return h


import jax as _jax
import jax.numpy as _jnp

TWIN_FORMAT = 'train_step'
FWD_PARAMS = ['x', 'ln_mix_g', 'ln_mix_b', 'ln_ffn_g', 'ln_ffn_b', 'ffn_w_in', 'ffn_w_out', 'mla_w_down', 'mla_q_norm_g', 'mla_w_uq', 'mla_kv_norm_g', 'mla_w_ukv', 'mla_w_o', 'sb_w_qkv', 'sb_w_o', 'ca_w_qkv', 'ca_rel_bias', 'ca_w_o']
TWIN_WEIGHTS = ['ln_mix_g', 'ln_mix_b', 'ln_ffn_g', 'ln_ffn_b', 'ffn_w_in', 'ffn_w_out', 'mla_w_down', 'mla_q_norm_g', 'mla_w_uq', 'mla_kv_norm_g', 'mla_w_ukv', 'mla_w_o', 'sb_w_qkv', 'sb_w_o', 'ca_w_qkv', 'ca_rel_bias', 'ca_w_o']
TWIN_DIFF_INPUT = 'x'
TWIN_INPUTS = ['x', 'ln_mix_g', 'ln_mix_b', 'ln_ffn_g', 'ln_ffn_b', 'ffn_w_in', 'ffn_w_out', 'mla_w_down', 'mla_q_norm_g', 'mla_w_uq', 'mla_kv_norm_g', 'mla_w_ukv', 'mla_w_o', 'sb_w_qkv', 'sb_w_o', 'ca_w_qkv', 'ca_rel_bias', 'ca_w_o', 'loss_target', 'm_ln_mix_g', 'm_ln_mix_b', 'm_ln_ffn_g', 'm_ln_ffn_b', 'm_ffn_w_in', 'm_ffn_w_out', 'm_mla_w_down', 'm_mla_q_norm_g', 'm_mla_w_uq', 'm_mla_kv_norm_g', 'm_mla_w_ukv', 'm_mla_w_o', 'm_sb_w_qkv', 'm_sb_w_o', 'm_ca_w_qkv', 'm_ca_rel_bias', 'm_ca_w_o', 'v_ln_mix_g', 'v_ln_mix_b', 'v_ln_ffn_g', 'v_ln_ffn_b', 'v_ffn_w_in', 'v_ffn_w_out', 'v_mla_w_down', 'v_mla_q_norm_g', 'v_mla_w_uq', 'v_mla_kv_norm_g', 'v_mla_w_ukv', 'v_mla_w_o', 'v_sb_w_qkv', 'v_sb_w_o', 'v_ca_w_qkv', 'v_ca_rel_bias', 'v_ca_w_o']
TWIN_OUTPUTS = ['loss', 'grad_x', 'grad_ln_mix_g', 'grad_ln_mix_b', 'grad_ln_ffn_g', 'grad_ln_ffn_b', 'grad_ffn_w_in', 'grad_ffn_w_out', 'grad_mla_w_down', 'grad_mla_q_norm_g', 'grad_mla_w_uq', 'grad_mla_kv_norm_g', 'grad_mla_w_ukv', 'grad_mla_w_o', 'grad_sb_w_qkv', 'grad_sb_w_o', 'grad_ca_w_qkv', 'grad_ca_rel_bias', 'grad_ca_w_o', 'delta_ln_mix_g', 'delta_ln_mix_b', 'delta_ln_ffn_g', 'delta_ln_ffn_b', 'delta_ffn_w_in', 'delta_ffn_w_out', 'delta_mla_w_down', 'delta_mla_q_norm_g', 'delta_mla_w_uq', 'delta_mla_kv_norm_g', 'delta_mla_w_ukv', 'delta_mla_w_o', 'delta_sb_w_qkv', 'delta_sb_w_o', 'delta_ca_w_qkv', 'delta_ca_rel_bias', 'delta_ca_w_o', 'new_m_ln_mix_g', 'new_m_ln_mix_b', 'new_m_ln_ffn_g', 'new_m_ln_ffn_b', 'new_m_ffn_w_in', 'new_m_ffn_w_out', 'new_m_mla_w_down', 'new_m_mla_q_norm_g', 'new_m_mla_w_uq', 'new_m_mla_kv_norm_g', 'new_m_mla_w_ukv', 'new_m_mla_w_o', 'new_m_sb_w_qkv', 'new_m_sb_w_o', 'new_m_ca_w_qkv', 'new_m_ca_rel_bias', 'new_m_ca_w_o', 'new_v_ln_mix_g', 'new_v_ln_mix_b', 'new_v_ln_ffn_g', 'new_v_ln_ffn_b', 'new_v_ffn_w_in', 'new_v_ffn_w_out', 'new_v_mla_w_down', 'new_v_mla_q_norm_g', 'new_v_mla_w_uq', 'new_v_mla_kv_norm_g', 'new_v_mla_w_ukv', 'new_v_mla_w_o', 'new_v_sb_w_qkv', 'new_v_sb_w_o', 'new_v_ca_w_qkv', 'new_v_ca_rel_bias', 'new_v_ca_w_o']
TWIN_LEAF_KINDS = {'loss': 'loss', 'grad_x': 'grad_x', 'grad_ln_mix_g': 'grad_w', 'grad_ln_mix_b': 'grad_w', 'grad_ln_ffn_g': 'grad_w', 'grad_ln_ffn_b': 'grad_w', 'grad_ffn_w_in': 'grad_w', 'grad_ffn_w_out': 'grad_w', 'grad_mla_w_down': 'grad_w', 'grad_mla_q_norm_g': 'grad_w', 'grad_mla_w_uq': 'grad_w', 'grad_mla_kv_norm_g': 'grad_w', 'grad_mla_w_ukv': 'grad_w', 'grad_mla_w_o': 'grad_w', 'grad_sb_w_qkv': 'grad_w', 'grad_sb_w_o': 'grad_w', 'grad_ca_w_qkv': 'grad_w', 'grad_ca_rel_bias': 'grad_w', 'grad_ca_w_o': 'grad_w', 'delta_ln_mix_g': 'delta_w', 'delta_ln_mix_b': 'delta_w', 'delta_ln_ffn_g': 'delta_w', 'delta_ln_ffn_b': 'delta_w', 'delta_ffn_w_in': 'delta_w', 'delta_ffn_w_out': 'delta_w', 'delta_mla_w_down': 'delta_w', 'delta_mla_q_norm_g': 'delta_w', 'delta_mla_w_uq': 'delta_w', 'delta_mla_kv_norm_g': 'delta_w', 'delta_mla_w_ukv': 'delta_w', 'delta_mla_w_o': 'delta_w', 'delta_sb_w_qkv': 'delta_w', 'delta_sb_w_o': 'delta_w', 'delta_ca_w_qkv': 'delta_w', 'delta_ca_rel_bias': 'delta_w', 'delta_ca_w_o': 'delta_w', 'new_m_ln_mix_g': 'new_m', 'new_m_ln_mix_b': 'new_m', 'new_m_ln_ffn_g': 'new_m', 'new_m_ln_ffn_b': 'new_m', 'new_m_ffn_w_in': 'new_m', 'new_m_ffn_w_out': 'new_m', 'new_m_mla_w_down': 'new_m', 'new_m_mla_q_norm_g': 'new_m', 'new_m_mla_w_uq': 'new_m', 'new_m_mla_kv_norm_g': 'new_m', 'new_m_mla_w_ukv': 'new_m', 'new_m_mla_w_o': 'new_m', 'new_m_sb_w_qkv': 'new_m', 'new_m_sb_w_o': 'new_m', 'new_m_ca_w_qkv': 'new_m', 'new_m_ca_rel_bias': 'new_m', 'new_m_ca_w_o': 'new_m', 'new_v_ln_mix_g': 'new_v', 'new_v_ln_mix_b': 'new_v', 'new_v_ln_ffn_g': 'new_v', 'new_v_ln_ffn_b': 'new_v', 'new_v_ffn_w_in': 'new_v', 'new_v_ffn_w_out': 'new_v', 'new_v_mla_w_down': 'new_v', 'new_v_mla_q_norm_g': 'new_v', 'new_v_mla_w_uq': 'new_v', 'new_v_mla_kv_norm_g': 'new_v', 'new_v_mla_w_ukv': 'new_v', 'new_v_mla_w_o': 'new_v', 'new_v_sb_w_qkv': 'new_v', 'new_v_sb_w_o': 'new_v', 'new_v_ca_w_qkv': 'new_v', 'new_v_ca_rel_bias': 'new_v', 'new_v_ca_w_o': 'new_v'}


def _forward(args):
    return _fwd_reference(*[args[k] for k in FWD_PARAMS])


def _output_shape():
    out = _jax.eval_shape(lambda: _forward(_fwd_setup_inputs(0)))
    return out.shape, out.dtype

N_MICROBATCH = 1
ADAM_LR = 0.001
ADAM_B1 = 0.9
ADAM_B2 = 0.999
ADAM_EPS = 1e-08
ADAM_WD = 0.01
ADAM_STEP = 10
PER_EXAMPLE_BATCH_AXIS = {'x': 0, 'loss_target': 0}
SHARED_INPUTS = []
_WEIGHT_DTYPES = {'ln_mix_g': _jnp.float32, 'ln_mix_b': _jnp.float32, 'ln_ffn_g': _jnp.float32, 'ln_ffn_b': _jnp.float32, 'ffn_w_in': _jnp.float32, 'ffn_w_out': _jnp.float32, 'mla_w_down': _jnp.float32, 'mla_q_norm_g': _jnp.float32, 'mla_w_uq': _jnp.float32, 'mla_kv_norm_g': _jnp.float32, 'mla_w_ukv': _jnp.float32, 'mla_w_o': _jnp.float32, 'sb_w_qkv': _jnp.float32, 'sb_w_o': _jnp.float32, 'ca_w_qkv': _jnp.float32, 'ca_rel_bias': _jnp.float32, 'ca_w_o': _jnp.float32}
MOMENT_SCALE = {'ln_mix_g': 2.935530e-01, 'ln_mix_b': 1.466333e-01, 'ln_ffn_g': 4.025849e+00, 'ln_ffn_b': 2.930613e-01, 'ffn_w_in': 4.648796e-03, 'ffn_w_out': 1.008095e-02, 'mla_w_down': 2.293407e-03, 'mla_q_norm_g': 1.929936e-03, 'mla_w_uq': 7.682837e-04, 'mla_kv_norm_g': 2.782929e-03, 'mla_w_ukv': 1.858001e-03, 'mla_w_o': 2.508249e-03, 'sb_w_qkv': 6.056869e-03, 'sb_w_o': 1.013651e-02, 'ca_w_qkv': 1.750579e-03, 'ca_rel_bias': 7.585417e-04, 'ca_w_o': 2.686715e-03}


def _to_microbatches(a, axis):
    t = _jnp.moveaxis(a, axis, 0)
    t = t.reshape((N_MICROBATCH, t.shape[0] // N_MICROBATCH) + t.shape[1:])
    return _jnp.moveaxis(t, 1, axis + 1)


def setup_inputs(seed: int = 0) -> dict:
    inp = _fwd_setup_inputs(seed)
    key = _jax.random.fold_in(_jax.random.key(seed), 7919)
    shape, _ = _output_shape()
    out = dict(inp)
    out["loss_target"] = _jax.random.normal(_jax.random.fold_in(key, 0), shape, _jnp.float32)
    for i, name in enumerate(TWIN_WEIGHTS):
        w = inp[name].astype(_jnp.float32)
        if MOMENT_SCALE is None:
            s = _jnp.sqrt(_jnp.mean(_jnp.square(w)) + 1e-30)
        else:
            s = MOMENT_SCALE[name]
        km, kv = _jax.random.split(_jax.random.fold_in(key, i + 1))
        out[name] = w
        out["m_" + name] = s * _jax.random.normal(km, w.shape, _jnp.float32)
        out["v_" + name] = (s * s) * _jax.random.uniform(kv, w.shape, _jnp.float32, 0.5, 1.5)
    if N_MICROBATCH > 1:
        for name, axis in PER_EXAMPLE_BATCH_AXIS.items():
            out[name] = _to_microbatches(out[name], axis)
    return {'x': out['x'], 'ln_mix_g': out['ln_mix_g'], 'ln_mix_b': out['ln_mix_b'], 'ln_ffn_g': out['ln_ffn_g'], 'ln_ffn_b': out['ln_ffn_b'], 'ffn_w_in': out['ffn_w_in'], 'ffn_w_out': out['ffn_w_out'], 'mla_w_down': out['mla_w_down'], 'mla_q_norm_g': out['mla_q_norm_g'], 'mla_w_uq': out['mla_w_uq'], 'mla_kv_norm_g': out['mla_kv_norm_g'], 'mla_w_ukv': out['mla_w_ukv'], 'mla_w_o': out['mla_w_o'], 'sb_w_qkv': out['sb_w_qkv'], 'sb_w_o': out['sb_w_o'], 'ca_w_qkv': out['ca_w_qkv'], 'ca_rel_bias': out['ca_rel_bias'], 'ca_w_o': out['ca_w_o'], 'loss_target': out['loss_target'], 'm_ln_mix_g': out['m_ln_mix_g'], 'm_ln_mix_b': out['m_ln_mix_b'], 'm_ln_ffn_g': out['m_ln_ffn_g'], 'm_ln_ffn_b': out['m_ln_ffn_b'], 'm_ffn_w_in': out['m_ffn_w_in'], 'm_ffn_w_out': out['m_ffn_w_out'], 'm_mla_w_down': out['m_mla_w_down'], 'm_mla_q_norm_g': out['m_mla_q_norm_g'], 'm_mla_w_uq': out['m_mla_w_uq'], 'm_mla_kv_norm_g': out['m_mla_kv_norm_g'], 'm_mla_w_ukv': out['m_mla_w_ukv'], 'm_mla_w_o': out['m_mla_w_o'], 'm_sb_w_qkv': out['m_sb_w_qkv'], 'm_sb_w_o': out['m_sb_w_o'], 'm_ca_w_qkv': out['m_ca_w_qkv'], 'm_ca_rel_bias': out['m_ca_rel_bias'], 'm_ca_w_o': out['m_ca_w_o'], 'v_ln_mix_g': out['v_ln_mix_g'], 'v_ln_mix_b': out['v_ln_mix_b'], 'v_ln_ffn_g': out['v_ln_ffn_g'], 'v_ln_ffn_b': out['v_ln_ffn_b'], 'v_ffn_w_in': out['v_ffn_w_in'], 'v_ffn_w_out': out['v_ffn_w_out'], 'v_mla_w_down': out['v_mla_w_down'], 'v_mla_q_norm_g': out['v_mla_q_norm_g'], 'v_mla_w_uq': out['v_mla_w_uq'], 'v_mla_kv_norm_g': out['v_mla_kv_norm_g'], 'v_mla_w_ukv': out['v_mla_w_ukv'], 'v_mla_w_o': out['v_mla_w_o'], 'v_sb_w_qkv': out['v_sb_w_qkv'], 'v_sb_w_o': out['v_sb_w_o'], 'v_ca_w_qkv': out['v_ca_w_qkv'], 'v_ca_rel_bias': out['v_ca_rel_bias'], 'v_ca_w_o': out['v_ca_w_o']}


def _loss(weights, diff, rest, loss_target):
    with _jax.named_scope("forward"):
        args = {**rest, TWIN_DIFF_INPUT: diff, **{k: w.astype(_WEIGHT_DTYPES[k]) for k, w in weights.items()}}
        y = _forward(args)
    with _jax.named_scope("loss_head"):
        err = _jnp.square(y.astype(_jnp.float32) - loss_target)
        return 0.5 * _jnp.sum(_jnp.mean(err, axis=-1)) if err.ndim else 0.5 * err


def _adamw(w, g, m, v):
    m = ADAM_B1 * m + (1.0 - ADAM_B1) * g
    v = ADAM_B2 * v + (1.0 - ADAM_B2) * _jnp.square(g)
    m_hat = m / (1.0 - ADAM_B1 ** ADAM_STEP)
    v_hat = v / (1.0 - ADAM_B2 ** ADAM_STEP)
    delta = -ADAM_LR * (m_hat / (_jnp.sqrt(v_hat) + ADAM_EPS) + ADAM_WD * w)
    return delta, m, v


def reference(x, ln_mix_g, ln_mix_b, ln_ffn_g, ln_ffn_b, ffn_w_in, ffn_w_out, mla_w_down, mla_q_norm_g, mla_w_uq, mla_kv_norm_g, mla_w_ukv, mla_w_o, sb_w_qkv, sb_w_o, ca_w_qkv, ca_rel_bias, ca_w_o, loss_target, m_ln_mix_g, m_ln_mix_b, m_ln_ffn_g, m_ln_ffn_b, m_ffn_w_in, m_ffn_w_out, m_mla_w_down, m_mla_q_norm_g, m_mla_w_uq, m_mla_kv_norm_g, m_mla_w_ukv, m_mla_w_o, m_sb_w_qkv, m_sb_w_o, m_ca_w_qkv, m_ca_rel_bias, m_ca_w_o, v_ln_mix_g, v_ln_mix_b, v_ln_ffn_g, v_ln_ffn_b, v_ffn_w_in, v_ffn_w_out, v_mla_w_down, v_mla_q_norm_g, v_mla_w_uq, v_mla_kv_norm_g, v_mla_w_ukv, v_mla_w_o, v_sb_w_qkv, v_sb_w_o, v_ca_w_qkv, v_ca_rel_bias, v_ca_w_o):
    given = dict(x=x, ln_mix_g=ln_mix_g, ln_mix_b=ln_mix_b, ln_ffn_g=ln_ffn_g, ln_ffn_b=ln_ffn_b, ffn_w_in=ffn_w_in, ffn_w_out=ffn_w_out, mla_w_down=mla_w_down, mla_q_norm_g=mla_q_norm_g, mla_w_uq=mla_w_uq, mla_kv_norm_g=mla_kv_norm_g, mla_w_ukv=mla_w_ukv, mla_w_o=mla_w_o, sb_w_qkv=sb_w_qkv, sb_w_o=sb_w_o, ca_w_qkv=ca_w_qkv, ca_rel_bias=ca_rel_bias, ca_w_o=ca_w_o, loss_target=loss_target, m_ln_mix_g=m_ln_mix_g, m_ln_mix_b=m_ln_mix_b, m_ln_ffn_g=m_ln_ffn_g, m_ln_ffn_b=m_ln_ffn_b, m_ffn_w_in=m_ffn_w_in, m_ffn_w_out=m_ffn_w_out, m_mla_w_down=m_mla_w_down, m_mla_q_norm_g=m_mla_q_norm_g, m_mla_w_uq=m_mla_w_uq, m_mla_kv_norm_g=m_mla_kv_norm_g, m_mla_w_ukv=m_mla_w_ukv, m_mla_w_o=m_mla_w_o, m_sb_w_qkv=m_sb_w_qkv, m_sb_w_o=m_sb_w_o, m_ca_w_qkv=m_ca_w_qkv, m_ca_rel_bias=m_ca_rel_bias, m_ca_w_o=m_ca_w_o, v_ln_mix_g=v_ln_mix_g, v_ln_mix_b=v_ln_mix_b, v_ln_ffn_g=v_ln_ffn_g, v_ln_ffn_b=v_ln_ffn_b, v_ffn_w_in=v_ffn_w_in, v_ffn_w_out=v_ffn_w_out, v_mla_w_down=v_mla_w_down, v_mla_q_norm_g=v_mla_q_norm_g, v_mla_w_uq=v_mla_w_uq, v_mla_kv_norm_g=v_mla_kv_norm_g, v_mla_w_ukv=v_mla_w_ukv, v_mla_w_o=v_mla_w_o, v_sb_w_qkv=v_sb_w_qkv, v_sb_w_o=v_sb_w_o, v_ca_w_qkv=v_ca_w_qkv, v_ca_rel_bias=v_ca_rel_bias, v_ca_w_o=v_ca_w_o)
    weights = {n: given[n] for n in TWIN_WEIGHTS}
    shared = {n: given[n] for n in SHARED_INPUTS}
    per_example = {n: given[n] for n in ['x']}
    grad_fn = _jax.value_and_grad(_loss, argnums=(0, 1))

    def one_microbatch(ex, loss_target):
        ex = dict(ex)
        diff = ex.pop(TWIN_DIFF_INPUT)
        return grad_fn(weights, diff, {**shared, **ex}, loss_target)

    if N_MICROBATCH == 1:
        loss, (grad_w, grad_x) = one_microbatch(per_example, given["loss_target"])
    else:
        def body(carry, xs):
            loss_sum, grad_sum = carry
            l_k, (gw_k, gx_k) = one_microbatch(xs[0], xs[1])
            with _jax.named_scope("update"):
                return (loss_sum + l_k, _jax.tree.map(_jnp.add, grad_sum, gw_k)), gx_k

        init = (_jnp.zeros((), _jnp.float32), _jax.tree.map(_jnp.zeros_like, weights))
        (loss, grad_w), grad_x = _jax.lax.scan(body, init, (per_example, given["loss_target"]))
    with _jax.named_scope("update"):
        delta_w, new_m, new_v = {}, {}, {}
        for n in TWIN_WEIGHTS:
            delta_w[n], new_m[n], new_v[n] = _adamw(weights[n], grad_w[n], given["m_" + n], given["v_" + n])
    return (loss, grad_x, *[grad_w[n] for n in TWIN_WEIGHTS], *[delta_w[n] for n in TWIN_WEIGHTS],
            *[new_m[n] for n in TWIN_WEIGHTS], *[new_v[n] for n in TWIN_WEIGHTS])
```

```python
import functools
import math

import numpy as np
import jax
import jax.numpy as jnp
from jax import lax
from jax.experimental import pallas as pl
from jax.experimental.pallas import tpu as pltpu

F32 = jnp.float32
BF16 = jnp.bfloat16
MESH = pl.DeviceIdType.MESH
N_DEV = 8

DEPTH = 4
CHUNK = 64
CHUNK_SHIFT = 6
HEADS = 16
HEAD_DIM = 128
MLA_Q_LORA = 512
MLA_KV_LORA = 512
MLA_NOPE = 128
MLA_ROPE = 64
ROPE_THETA = 10000.0
CA_LEFT_CHUNKS = 8
REL_CLIP_LEFT = 128
REL_TABLE = REL_CLIP_LEFT + CHUNK
LN_EPS = 1e-5
RMS_EPS = 1e-6
ALPHA = (2.0 * DEPTH) ** 0.25
NEG = -1e30
ADAM_LR = 0.001
ADAM_B1 = 0.9
ADAM_B2 = 0.999
ADAM_EPS = 1e-08
ADAM_WD = 0.01
ADAM_STEP = 10

V7X_VMEM_BYTES = 64 * 1024 * 1024
VMEM_LIMIT = V7X_VMEM_BYTES - 8 * 1024 * 1024
ATT_T = 256


def _params(*sem):
    return pltpu.CompilerParams(dimension_semantics=sem if sem else None, vmem_limit_bytes=VMEM_LIMIT)


def _matmul(name, a, b, *, contract, grid, a_spec, b_spec, o_specs, out_shape, acc_shape,
            epilogue=None, extra=(), extra_specs=()):
    nk = grid[2]
    n_extra = len(extra)
    n_out = len(out_shape)

    def body(*refs):
        a_ref, b_ref = refs[0], refs[1]
        e_refs = refs[2:2 + n_extra]
        o_refs = refs[2 + n_extra:2 + n_extra + n_out]
        acc_ref = refs[-1]
        k = pl.program_id(2)

        @pl.when(k == 0)
        def _():
            acc_ref[...] = jnp.zeros_like(acc_ref)

        acc_ref[...] += lax.dot_general(a_ref[...], b_ref[...], (contract, ((), ())),
                                        preferred_element_type=F32)

        @pl.when(k == nk - 1)
        def _():
            acc = acc_ref[...]
            outs = epilogue(acc, *[e[...] for e in e_refs]) if epilogue else (acc,)
            for o_ref, val in zip(o_refs, outs):
                o_ref[...] = val.astype(o_ref.dtype)

    return pl.pallas_call(
        body, name=name, grid=grid, in_specs=[a_spec, b_spec, *extra_specs], out_specs=list(o_specs),
        out_shape=list(out_shape), scratch_shapes=[pltpu.VMEM(acc_shape, F32)],
        compiler_params=_params("parallel", "parallel", "arbitrary"),
    )(a, b, *extra)


def _tile(n, pref):
    if n <= pref:
        return n
    t = pref
    while t >= 128:
        if n % t == 0 and t % 128 == 0:
            return t
        t -= 128
    return n


class Weight:
    def __init__(self, kind, arr):
        self.kind = kind
        self.arr = arr
        self.R, self.C = arr.shape[1], arr.shape[2]

    @property
    def two_d(self):
        return self.arr.reshape(N_DEV * self.R, self.C)


def mm_nn(name, a, w, out_dtypes, epilogue=None):
    M, K = a.shape
    tm = _tile(M, 1024)
    tk = _tile(K, 512)
    if w.kind == "row":
        b = w.two_d
        N = w.C
        tn = _tile(N, 1024) if N % 1024 == 0 else N
        b_spec = pl.BlockSpec((tk, tn), lambda i, j, k: (k, j))
    else:
        b = w.arr
        N = N_DEV * w.C
        tn = w.C
        b_spec = pl.BlockSpec((None, tk, tn), lambda i, j, k: (j, k, 0))
    grid = (M // tm, N // tn, K // tk)
    return _matmul(
        name, a, b, contract=((1,), (0,)), grid=grid,
        a_spec=pl.BlockSpec((tm, tk), lambda i, j, k: (i, k)), b_spec=b_spec,
        o_specs=[pl.BlockSpec((tm, tn), lambda i, j, k: (i, j)) for _ in out_dtypes],
        out_shape=[jax.ShapeDtypeStruct((M, N), d) for d in out_dtypes], acc_shape=(tm, tn), epilogue=epilogue)


def mm_nt(name, dy, w, out_dtype, epilogue=None, extra=None):
    M, N = dy.shape
    tm = _tile(M, 1024)
    if w.kind == "row":
        b = w.two_d
        kin = N_DEV * w.R
        tn = _tile(kin, 1024)
        tk = _tile(N, 512) if N % 512 == 0 else _tile(N, 384)
        b_spec = pl.BlockSpec((tn, tk), lambda i, j, k: (j, k))
    else:
        b = w.arr
        kin = w.R
        tn = _tile(kin, 1024)
        tk = w.C
        b_spec = pl.BlockSpec((None, tn, tk), lambda i, j, k: (k, j, 0))
    grid = (M // tm, kin // tn, N // tk)
    o_spec = pl.BlockSpec((tm, tn), lambda i, j, k: (i, j))
    return _matmul(
        name, dy, b, contract=((1,), (1,)), grid=grid,
        a_spec=pl.BlockSpec((tm, tk), lambda i, j, k: (i, k)), b_spec=b_spec, o_specs=[o_spec],
        out_shape=[jax.ShapeDtypeStruct((M, kin), out_dtype)], acc_shape=(tm, tn), epilogue=epilogue,
        extra=() if extra is None else (extra,), extra_specs=() if extra is None else (o_spec,))[0]


def mm_tn(name, x, dy, kind, R, C):
    S, kin = x.shape
    N = dy.shape[1]
    tk = _tile(S, 512)
    if kind == "col":
        tm = _tile(kin, 1024)
        tn = C
        grid = (kin // tm, N_DEV, S // tk)
        o_spec = pl.BlockSpec((None, None, tm, tn), lambda i, j, k: (j % 2, j // 2, i, 0))
    else:
        tm = _tile(R, 1024)
        tn = _tile(N, 1024) if N % 1024 == 0 else N
        per = R // tm
        grid = (kin // tm, N // tn, S // tk)
        o_spec = pl.BlockSpec((None, None, tm, tn), lambda i, j, k: ((i // per) % 2, (i // per) // 2, i % per, j))
    return _matmul(
        name, x, dy, contract=((0,), (0,)), grid=grid,
        a_spec=pl.BlockSpec((tk, tm), lambda i, j, k: (k, i)),
        b_spec=pl.BlockSpec((tk, tn), lambda i, j, k: (k, j)), o_specs=[o_spec],
        out_shape=[jax.ShapeDtypeStruct((2, 4, R, C), BF16)], acc_shape=(tm, tn))[0]


def _relu2_epilogue(acc):
    r = jnp.maximum(acc, 0.0)
    return acc, r * r


def _mulrelu_epilogue(acc, a):
    return (acc * (2.0 * jnp.maximum(a, 0.0)),)


ROW_TILE = 256


def ln_fwd(name, h, m, g, b):
    S, D = h.shape
    ts = _tile(S, ROW_TILE)

    def body(h_ref, m_ref, g_ref, b_ref, y_ref, y16_ref, xh_ref, rs_ref):
        u = ALPHA * h_ref[...] + m_ref[...]
        mu = jnp.mean(u, axis=-1, keepdims=True)
        d = u - mu
        var = jnp.mean(d * d, axis=-1, keepdims=True)
        rstd = lax.rsqrt(var + LN_EPS)
        xh = d * rstd
        y = xh * g_ref[...] + b_ref[...]
        y_ref[...] = y
        y16_ref[...] = y.astype(BF16)
        xh_ref[...] = xh
        rs_ref[...] = jnp.broadcast_to(rstd, rs_ref.shape)

    row = pl.BlockSpec((ts, D), lambda i: (i, 0))
    vec = pl.BlockSpec((1, D), lambda i: (0, 0))
    return pl.pallas_call(
        body, name=name, grid=(S // ts,), in_specs=[row, row, vec, vec],
        out_specs=[row, row, row, pl.BlockSpec((ts, 128), lambda i: (i, 0))],
        out_shape=[jax.ShapeDtypeStruct((S, D), F32), jax.ShapeDtypeStruct((S, D), BF16),
                   jax.ShapeDtypeStruct((S, D), F32), jax.ShapeDtypeStruct((S, 128), F32)],
        compiler_params=_params("parallel"),
    )(h, m, g.reshape(1, D), b.reshape(1, D))


def ln_bwd(name, ga, gb, xhat, rstd, g):
    S, D = xhat.shape
    ts = _tile(S, ROW_TILE)
    two = gb is not None

    def body(*refs):
        if two:
            ga_ref, gb_ref, xh_ref, rs_ref, g_ref, du_ref, du16_ref, dg_ref, db_ref = refs
            dy = ALPHA * ga_ref[...] + gb_ref[...]
        else:
            ga_ref, xh_ref, rs_ref, g_ref, du_ref, du16_ref, dg_ref, db_ref = refs
            dy = ga_ref[...]
        xh = xh_ref[...]

        @pl.when(pl.program_id(0) == 0)
        def _():
            dg_ref[...] = jnp.zeros_like(dg_ref)
            db_ref[...] = jnp.zeros_like(db_ref)

        dg_ref[...] += jnp.sum(dy * xh, axis=0, keepdims=True)
        db_ref[...] += jnp.sum(dy, axis=0, keepdims=True)
        dxh = dy * g_ref[...]
        m1 = jnp.mean(dxh, axis=-1, keepdims=True)
        m2 = jnp.mean(dxh * xh, axis=-1, keepdims=True)
        du = rs_ref[:, 0:1] * (dxh - m1 - xh * m2)
        du_ref[...] = du
        du16_ref[...] = du.astype(BF16)

    row = pl.BlockSpec((ts, D), lambda i: (i, 0))
    vec = pl.BlockSpec((1, D), lambda i: (0, 0))
    stat = pl.BlockSpec((ts, 128), lambda i: (i, 0))
    ins = [ga, gb, xhat, rstd, g.reshape(1, D)] if two else [ga, xhat, rstd, g.reshape(1, D)]
    in_specs = [row, row, row, stat, vec] if two else [row, row, stat, vec]
    return pl.pallas_call(
        body, name=name, grid=(S // ts,), in_specs=in_specs, out_specs=[row, row, vec, vec],
        out_shape=[jax.ShapeDtypeStruct((S, D), F32), jax.ShapeDtypeStruct((S, D), BF16),
                   jax.ShapeDtypeStruct((1, D), F32), jax.ShapeDtypeStruct((1, D), F32)],
        compiler_params=_params("arbitrary"),
    )(*ins)


def loss_fwd_bwd(name, y, target):
    S, D = y.shape
    ts = _tile(S, ROW_TILE)

    def body(y_ref, t_ref, l_ref, dy_ref):
        @pl.when(pl.program_id(0) == 0)
        def _():
            l_ref[...] = jnp.zeros_like(l_ref)

        e = y_ref[...] - t_ref[...]
        l_ref[...] += jnp.sum(e * e)
        dy_ref[...] = e * (1.0 / D)

    row = pl.BlockSpec((ts, D), lambda i: (i, 0))
    return pl.pallas_call(
        body, name=name, grid=(S // ts,), in_specs=[row, row],
        out_specs=[pl.BlockSpec((1, 128), lambda i: (0, 0)), row],
        out_shape=[jax.ShapeDtypeStruct((1, 128), F32), jax.ShapeDtypeStruct((S, D), F32)],
        compiler_params=_params("arbitrary"),
    )(y, target)


def axpy(name, ga, gb):
    S, D = ga.shape
    ts = _tile(S, ROW_TILE)

    def body(a_ref, b_ref, o_ref):
        o_ref[...] = ALPHA * a_ref[...] + b_ref[...]

    row = pl.BlockSpec((ts, D), lambda i: (i, 0))
    return pl.pallas_call(body, name=name, grid=(S // ts,), in_specs=[row, row], out_specs=row,
                          out_shape=jax.ShapeDtypeStruct((S, D), F32), compiler_params=_params("parallel"))(ga, gb)


def rms_fwd(name, down, gq, gkv):
    S = down.shape[0]
    ts = _tile(S, ROW_TILE)
    L = MLA_Q_LORA

    def body(d_ref, gq_ref, gkv_ref, q_ref, kv_ref):
        for lo, g_ref, o_ref in ((0, gq_ref, q_ref), (L, gkv_ref, kv_ref)):
            x = d_ref[:, lo:lo + L]
            r = lax.rsqrt(jnp.mean(x * x, axis=-1, keepdims=True) + RMS_EPS)
            o_ref[...] = (x * r * g_ref[...]).astype(BF16)

    vec = pl.BlockSpec((1, L), lambda i: (0, 0))
    out = pl.BlockSpec((ts, L), lambda i: (i, 0))
    return pl.pallas_call(
        body, name=name, grid=(S // ts,), in_specs=[pl.BlockSpec((ts, down.shape[1]), lambda i: (i, 0)), vec, vec],
        out_specs=[out, out], out_shape=[jax.ShapeDtypeStruct((S, L), BF16)] * 2, compiler_params=_params("parallel"),
    )(down, gq.reshape(1, L), gkv.reshape(1, L))


def rms_bwd(name, down, dq, dkv, dkr, gq, gkv):
    S, W = down.shape
    ts = _tile(S, ROW_TILE)
    L = MLA_Q_LORA

    def body(d_ref, dq_ref, dkv_ref, dkr_ref, gq_ref, gkv_ref, o_ref, dgq_ref, dgkv_ref):
        @pl.when(pl.program_id(0) == 0)
        def _():
            dgq_ref[...] = jnp.zeros_like(dgq_ref)
            dgkv_ref[...] = jnp.zeros_like(dgkv_ref)

        for lo, dy_ref, g_ref, dg_ref in ((0, dq_ref, gq_ref, dgq_ref), (L, dkv_ref, gkv_ref, dgkv_ref)):
            x = d_ref[:, lo:lo + L]
            dy = dy_ref[...]
            r = lax.rsqrt(jnp.mean(x * x, axis=-1, keepdims=True) + RMS_EPS)
            dg_ref[...] += jnp.sum(dy * x * r, axis=0, keepdims=True)
            dyg = dy * g_ref[...]
            dx = r * dyg - x * (r * r * r) * jnp.mean(dyg * x, axis=-1, keepdims=True)
            o_ref[:, lo:lo + L] = dx.astype(BF16)
        o_ref[:, 2 * L:] = dkr_ref[...].astype(BF16)

    vec = pl.BlockSpec((1, L), lambda i: (0, 0))
    lat = pl.BlockSpec((ts, L), lambda i: (i, 0))
    full = pl.BlockSpec((ts, W), lambda i: (i, 0))
    return pl.pallas_call(
        body, name=name, grid=(S // ts,),
        in_specs=[full, lat, lat, pl.BlockSpec((ts, 128), lambda i: (i, 0)), vec, vec],
        out_specs=[full, vec, vec],
        out_shape=[jax.ShapeDtypeStruct((S, W), BF16), jax.ShapeDtypeStruct((1, L), F32), jax.ShapeDtypeStruct((1, L), F32)],
        compiler_params=_params("arbitrary"),
    )(down, dq, dkv, dkr, gq.reshape(1, L), gkv.reshape(1, L))


def rope_tables(S):
    half = MLA_ROPE // 2
    inv = (np.float32(ROPE_THETA) ** (-np.arange(half, dtype=np.float32) / np.float32(half))).astype(np.float32)
    ang = np.arange(S, dtype=np.float32)[:, None] * inv[None, :]
    cos, sin = np.cos(ang).astype(np.float32), np.sin(ang).astype(np.float32)
    z = np.zeros_like(cos)
    return (jnp.asarray(np.concatenate([cos, z, cos, z], 1)), jnp.asarray(np.concatenate([-sin, z, sin, z], 1)))


def _rot(x, cos, sin):
    return x * cos + pltpu.roll(x, 64, 1) * sin


def mla_prep_fwd(name, q, kv, down, cos, sin):
    S = q.shape[0]
    ts = _tile(S, 512)

    def body(q_ref, kv_ref, kr_ref, c_ref, s_ref, qo_ref, ko_ref):
        c, s = c_ref[...], s_ref[...]
        qo_ref[:, :128] = q_ref[:, :128].astype(BF16)
        qo_ref[:, 128:] = _rot(q_ref[:, 128:], c, s).astype(BF16)
        ko_ref[:, :128] = kv_ref[:, :128]
        ko_ref[:, 128:] = _rot(kr_ref[...], c, s).astype(BF16)

    head = pl.BlockSpec((ts, 256), lambda i, h: (i, h))
    tab = pl.BlockSpec((ts, 128), lambda i, h: (i, 0))
    return pl.pallas_call(
        body, name=name, grid=(S // ts, HEADS),
        in_specs=[head, head, pl.BlockSpec((ts, 128), lambda i, h: (i, 2 * MLA_Q_LORA // 128)), tab, tab],
        out_specs=[head, head], out_shape=[jax.ShapeDtypeStruct(q.shape, BF16)] * 2,
        compiler_params=_params("parallel", "parallel"),
    )(q, kv, down, cos, sin)


def mla_prep_bwd(name, dq, dk, dv, cos, sin):
    S = dq.shape[0]
    ts = _tile(S, 512)

    def body(dq_ref, dk_ref, dv_ref, c_ref, s_ref, qo_ref, kvo_ref, kr_ref):
        c, s = c_ref[...], -s_ref[...]
        qo_ref[:, :128] = dq_ref[:, :128].astype(BF16)
        qo_ref[:, 128:] = _rot(dq_ref[:, 128:], c, s).astype(BF16)
        kvo_ref[:, :128] = dk_ref[:, :128].astype(BF16)
        kvo_ref[:, 128:] = dv_ref[...].astype(BF16)

        @pl.when(pl.program_id(1) == 0)
        def _():
            kr_ref[...] = jnp.zeros_like(kr_ref)

        kr_ref[...] += _rot(dk_ref[:, 128:], c, s)

    head = pl.BlockSpec((ts, 256), lambda i, h: (i, h))
    tab = pl.BlockSpec((ts, 128), lambda i, h: (i, 0))
    return pl.pallas_call(
        body, name=name, grid=(S // ts, HEADS),
        in_specs=[head, head, pl.BlockSpec((ts, 128), lambda i, h: (i, h)), tab, tab],
        out_specs=[head, head, tab],
        out_shape=[jax.ShapeDtypeStruct(dq.shape, BF16), jax.ShapeDtypeStruct(dq.shape, BF16),
                   jax.ShapeDtypeStruct((S, 128), F32)],
        compiler_params=_params("parallel", "arbitrary"),
    )(dq, dk, dv, cos, sin)


def _dot_nt(a, b):
    return lax.dot_general(a, b, (((1,), (1,)), ((), ())), preferred_element_type=F32)


def _dot_tn(a, b):
    return lax.dot_general(a, b, (((0,), (0,)), ((), ())), preferred_element_type=F32)


def _dot(a, b):
    return jnp.dot(a, b, preferred_element_type=F32)


def _positions(i, j, T):
    row = i * T + lax.broadcasted_iota(jnp.int32, (T, T), 0)
    col = j * T + lax.broadcasted_iota(jnp.int32, (T, T), 1)
    return row, col


def _softmax_mask(mode, row, col):
    rc, cc = row >> CHUNK_SHIFT, col >> CHUNK_SHIFT
    if mode == "mla":
        return cc <= rc
    return (cc <= rc) & (cc >= rc - CA_LEFT_CHUNKS)


def _first_key_block(mode, i, T):
    if mode == "ca":
        return jnp.maximum(i - (CA_LEFT_CHUNKS * CHUNK) // T, 0)
    return 0


class HeadCols:
    def __init__(self, arr, width, index):
        self.arr, self.width, self.index = arr, width, index

    def rows(self, T):
        return pl.BlockSpec((T, self.width), lambda h, i: (i, self.index(h)))

    def full(self, S):
        return pl.BlockSpec((S, self.width), lambda h, i: (0, self.index(h)))


def softmax_attn_fwd(name, mode, q, k, v, scale, bias=None):
    S, T = q.arr.shape[0], min(ATT_T, q.arr.shape[0])
    dv = v.width

    def body(*refs):
        if bias is not None:
            q_ref, k_ref, v_ref, b_ref, o_ref, lse_ref = refs
        else:
            q_ref, k_ref, v_ref, o_ref, lse_ref = refs
        i = pl.program_id(1)
        qb = q_ref[...]

        def step(j, carry):
            m, l, acc = carry
            ks = pl.ds(pl.multiple_of(j * T, T), T)
            s = _dot_nt(qb, k_ref[ks, :]) * scale
            if bias is not None:
                s = s + b_ref[jnp.minimum(i - j, 2)]
            row, col = _positions(i, j, T)
            mask = _softmax_mask(mode, row, col)
            s = jnp.where(mask, s, NEG)
            m_new = jnp.maximum(m, jnp.max(s, axis=-1, keepdims=True))
            a = jnp.exp(m - m_new)
            p = jnp.where(mask, jnp.exp(s - m_new), 0.0)
            l = a * l + jnp.sum(p, axis=-1, keepdims=True)
            acc = a * acc + _dot(p.astype(BF16), v_ref[ks, :])
            return m_new, l, acc

        init = (jnp.full((T, 1), NEG, F32), jnp.zeros((T, 1), F32), jnp.zeros((T, dv), F32))
        m, l, acc = lax.fori_loop(_first_key_block(mode, i, T), i + 1, step, init)
        o_ref[...] = (acc / l).astype(BF16)
        lse_ref[...] = jnp.broadcast_to(m + jnp.log(l), lse_ref.shape)

    in_specs = [q.rows(T), k.full(S), v.full(S)]
    ins = [q.arr, k.arr, v.arr]
    if bias is not None:
        in_specs.append(pl.BlockSpec((None, 3, T, T), lambda h, i: (h, 0, 0, 0)))
        ins.append(bias)
    return pl.pallas_call(
        body, name=name, grid=(HEADS, S // T), in_specs=in_specs,
        out_specs=[pl.BlockSpec((T, dv), lambda h, i: (i, h)), pl.BlockSpec((T, 128), lambda h, i: (i, h))],
        out_shape=[jax.ShapeDtypeStruct((S, HEADS * dv), BF16), jax.ShapeDtypeStruct((S, HEADS * 128), F32)],
        compiler_params=_params("parallel", "parallel"),
    )(*ins)


def softmax_attn_bwd(name, mode, q, k, v, o, do, lse, scale, bias=None):
    S, T = q.arr.shape[0], min(ATT_T, q.arr.shape[0])
    dqk, dv = q.width, v.width

    def body(*refs):
        if bias is not None:
            q_ref, k_ref, v_ref, o_ref, do_ref, lse_ref, b_ref, dq_ref, dk_ref, dv_ref, db_ref = refs
        else:
            q_ref, k_ref, v_ref, o_ref, do_ref, lse_ref, dq_ref, dk_ref, dv_ref = refs
        i = pl.program_id(1)

        @pl.when(i == 0)
        def _():
            dk_ref[...] = jnp.zeros_like(dk_ref)
            dv_ref[...] = jnp.zeros_like(dv_ref)
            if bias is not None:
                db_ref[...] = jnp.zeros_like(db_ref)

        qb, dob = q_ref[...], do_ref[...]
        lse_b = lse_ref[:, 0:1]
        delta = jnp.sum(dob.astype(F32) * o_ref[...].astype(F32), axis=-1, keepdims=True)

        def step(j, dq):
            ks = pl.ds(pl.multiple_of(j * T, T), T)
            kb, vb = k_ref[ks, :], v_ref[ks, :]
            s = _dot_nt(qb, kb) * scale
            if bias is not None:
                slot = jnp.minimum(i - j, 2)
                s = s + b_ref[slot]
            row, col = _positions(i, j, T)
            mask = _softmax_mask(mode, row, col)
            p = jnp.where(mask, jnp.exp(s - lse_b), 0.0)
            ds = p * (_dot_nt(dob, vb) - delta)
            if bias is not None:
                db_ref[slot] += ds
            dsb = (ds * scale).astype(BF16)
            dk_ref[ks, :] += _dot_tn(dsb, qb)
            dv_ref[ks, :] += _dot_tn(p.astype(BF16), dob)
            return dq + _dot(dsb, kb)

        dq_ref[...] = lax.fori_loop(_first_key_block(mode, i, T), i + 1, step, jnp.zeros((T, dqk), F32))

    in_specs = [q.rows(T), k.full(S), v.full(S), o.rows(T), do.rows(T), pl.BlockSpec((T, 128), lambda h, i: (i, h))]
    ins = [q.arr, k.arr, v.arr, o.arr, do.arr, lse]
    out_specs = [pl.BlockSpec((T, dqk), lambda h, i: (i, h)), pl.BlockSpec((S, dqk), lambda h, i: (0, h)),
                 pl.BlockSpec((S, dv), lambda h, i: (0, h))]
    out_shape = [jax.ShapeDtypeStruct((S, HEADS * dqk), F32), jax.ShapeDtypeStruct((S, HEADS * dqk), F32),
                 jax.ShapeDtypeStruct((S, HEADS * dv), F32)]
    if bias is not None:
        bspec = pl.BlockSpec((None, 3, T, T), lambda h, i: (h, 0, 0, 0))
        in_specs.append(bspec)
        ins.append(bias)
        out_specs.append(bspec)
        out_shape.append(jax.ShapeDtypeStruct(bias.shape, F32))
    return pl.pallas_call(
        body, name=name, grid=(HEADS, S // T), in_specs=in_specs, out_specs=out_specs, out_shape=out_shape,
        compiler_params=_params("parallel", "arbitrary"),
    )(*ins)


def _split2(x):
    hi = x.astype(BF16)
    return hi, (x - hi.astype(F32)).astype(BF16)


def _split3(x):
    hi = x.astype(BF16)
    r = x - hi.astype(F32)
    mid = r.astype(BF16)
    return hi, mid, (r - mid.astype(F32)).astype(BF16)


def _stick_block(qb, kb, i, j, T, scale):
    z = _dot_nt(qb, kb) * scale
    row, col = _positions(i, j, T)
    strict = col < row
    sp = jnp.log(1.0 + jnp.exp(-jnp.abs(z)))
    lb = jnp.minimum(z, 0.0) - sp
    l1m = jnp.where(strict, jnp.minimum(-z, 0.0) - sp, 0.0)
    return z, strict, lb, l1m


def _tri(T, inclusive):
    r = lax.broadcasted_iota(jnp.int32, (T, T), 0)
    c = lax.broadcasted_iota(jnp.int32, (T, T), 1)
    return ((r >= c) if inclusive else (r > c)).astype(BF16)


def _suffix(parts, tri):
    out = _dot(parts[0], tri)
    for p in parts[1:]:
        out = out + _dot(p, tri)
    return out


def stick_attn_fwd(name, q, k, v, scale):
    S, T = q.arr.shape[0], min(ATT_T, q.arr.shape[0])
    dv = v.width

    def body(q_ref, k_ref, v_ref, o_ref):
        i = pl.program_id(1)
        qb = q_ref[...]
        tri = _tri(T, False)

        def step(t, carry):
            right, acc = carry
            j = i - t
            ks = pl.ds(pl.multiple_of(j * T, T), T)
            z, strict, lb, l1m = _stick_block(qb, k_ref[ks, :], i, j, T, scale)
            surv = _suffix(_split2(l1m), tri) + right
            a = jnp.where(strict, jnp.exp(lb + surv), 0.0)
            acc = acc + _dot(a.astype(BF16), v_ref[ks, :])
            return right + jnp.sum(l1m, axis=-1, keepdims=True), acc

        _, acc = lax.fori_loop(0, i + 1, step, (jnp.zeros((T, 1), F32), jnp.zeros((T, dv), F32)))
        o_ref[...] = acc.astype(BF16)

    return pl.pallas_call(
        body, name=name, grid=(HEADS, S // T), in_specs=[q.rows(T), k.full(S), v.full(S)],
        out_specs=pl.BlockSpec((T, dv), lambda h, i: (i, h)),
        out_shape=jax.ShapeDtypeStruct((S, HEADS * dv), BF16), compiler_params=_params("parallel", "parallel"),
    )(q.arr, k.arr, v.arr)


def stick_attn_bwd(name, q, k, v, do, scale):
    S, T = q.arr.shape[0], min(ATT_T, q.arr.shape[0])
    dqk, dv = q.width, v.width

    def body(q_ref, k_ref, v_ref, do_ref, dq_ref, dk_ref, dv_ref):
        i = pl.program_id(1)

        @pl.when(i == 0)
        def _():
            dk_ref[...] = jnp.zeros_like(dk_ref)
            dv_ref[...] = jnp.zeros_like(dv_ref)

        qb, dob = q_ref[...], do_ref[...]
        tri = _tri(T, False)
        tri_inc = _tri(T, True)

        def block(j, right):
            ks = pl.ds(pl.multiple_of(j * T, T), T)
            kb, vb = k_ref[ks, :], v_ref[ks, :]
            z, strict, lb, l1m = _stick_block(qb, kb, i, j, T, scale)
            surv = _suffix(_split2(l1m), tri) + right
            a = jnp.where(strict, jnp.exp(lb + surv), 0.0)
            g = a * _dot_nt(dob, vb)
            return ks, kb, z, strict, l1m, a, g

        def total(t, carry):
            right, gtot = carry
            _, _, _, _, l1m, _, g = block(i - t, right)
            return right + jnp.sum(l1m, axis=-1, keepdims=True), gtot + jnp.sum(g, axis=-1, keepdims=True)

        zero = jnp.zeros((T, 1), F32)
        _, gtot = lax.fori_loop(0, i + 1, total, (zero, zero))

        def step(t, carry):
            right, gright, dq = carry
            ks, kb, z, strict, l1m, a, g = block(i - t, right)
            c = gtot - (_suffix(_split3(g), tri_inc) + gright)
            sig = 1.0 / (1.0 + jnp.exp(-z))
            dz = jnp.where(strict, g * (1.0 - sig) - c * sig, 0.0)
            dzb = (dz * scale).astype(BF16)
            dk_ref[ks, :] += _dot_tn(dzb, qb)
            dv_ref[ks, :] += _dot_tn(a.astype(BF16), dob)
            return (right + jnp.sum(l1m, axis=-1, keepdims=True), gright + jnp.sum(g, axis=-1, keepdims=True),
                    dq + _dot(dzb, kb))

        _, _, dq = lax.fori_loop(0, i + 1, step, (zero, zero, jnp.zeros((T, dqk), F32)))
        dq_ref[...] = dq

    return pl.pallas_call(
        body, name=name, grid=(HEADS, S // T), in_specs=[q.rows(T), k.full(S), v.full(S), do.rows(T)],
        out_specs=[pl.BlockSpec((T, dqk), lambda h, i: (i, h)), pl.BlockSpec((S, dqk), lambda h, i: (0, h)),
                   pl.BlockSpec((S, dv), lambda h, i: (0, h))],
        out_shape=[jax.ShapeDtypeStruct((S, HEADS * dqk), F32), jax.ShapeDtypeStruct((S, HEADS * dqk), F32),
                   jax.ShapeDtypeStruct((S, HEADS * dv), F32)],
        compiler_params=_params("parallel", "arbitrary"),
    )(q.arr, k.arr, v.arr, do.arr)


def _rel_index(T, d):
    qq = lax.broadcasted_iota(jnp.int32, (T, T), 0)
    kk = lax.broadcasted_iota(jnp.int32, (T, T), 1)
    return jnp.clip(kk - qq - T * d, -REL_CLIP_LEFT, CHUNK - 1) + REL_CLIP_LEFT


def _rel_range(T, d):
    hi = min(T - 1 - T * d, CHUNK - 1) + REL_CLIP_LEFT
    lo = max(-(T - 1) - T * d, -REL_CLIP_LEFT) + REL_CLIP_LEFT
    return lo, max(hi, lo) + 1


def rel_bias_blocks(name, table, T):
    def body(t_ref, o_ref):
        h = pl.program_id(0)
        for d in range(3):
            idx = _rel_index(T, d)
            lo, hi = _rel_range(T, d)
            o_ref[d] = lax.fori_loop(lo, hi, lambda r, acc: jnp.where(idx == r, t_ref[r, h], acc),
                                     jnp.zeros((T, T), F32))

    return pl.pallas_call(
        body, name=name, grid=(HEADS,), in_specs=[pl.BlockSpec(memory_space=pltpu.SMEM)],
        out_specs=pl.BlockSpec((None, 3, T, T), lambda h: (h, 0, 0, 0)),
        out_shape=jax.ShapeDtypeStruct((HEADS, 3, T, T), F32), compiler_params=_params("parallel"),
    )(table)


def rel_bias_grad(name, dbias):
    T = dbias.shape[-1]

    def body(d_ref, o_ref):
        h = pl.program_id(0)

        def zero(r, _):
            o_ref[r, h] = 0.0
            return 0

        lax.fori_loop(0, REL_TABLE, zero, 0)
        for d in range(3):
            idx = _rel_index(T, d)
            lo, hi = _rel_range(T, d)
            blk = d_ref[d]

            def add(r, _):
                o_ref[r, h] = o_ref[r, h] + jnp.sum(jnp.where(idx == r, blk, 0.0))
                return 0

            lax.fori_loop(lo, hi, add, 0)

    return pl.pallas_call(
        body, name=name, grid=(HEADS,), in_specs=[pl.BlockSpec((None, 3, T, T), lambda h: (h, 0, 0, 0))],
        out_specs=pl.BlockSpec(memory_space=pltpu.SMEM), out_shape=jax.ShapeDtypeStruct((REL_TABLE, HEADS), F32),
        compiler_params=_params("arbitrary"),
    )(dbias)


HBM = pl.BlockSpec(memory_space=pl.ANY)


def _place():
    return lax.axis_index("x"), lax.axis_index("y"), lax.axis_index("c")


def all_gather(name, shards):
    n = len(shards)

    def body(*refs):
        x_refs, out_refs = refs[:n], refs[n:2 * n]
        send_sems, recv_sems, local_sems = refs[2 * n:]
        x, y, c = _place()
        me, sibling = (x, y, c), (x, y, 1 - c)
        chips = [(1 - x, y), (x, 1 - y), (1 - x, 1 - y)]

        def block(t, dev):
            return out_refs[t].at[4 * dev[0] + 2 * dev[1] + dev[2]]

        def copy(t, k, dev, to, src=None):
            return pltpu.make_async_remote_copy(
                src_ref=block(t, dev) if src is None else src, dst_ref=block(t, dev),
                send_sem=send_sems.at[t, k], recv_sem=recv_sems.at[t, k], device_id=to, device_id_type=MESH)

        mine = [pltpu.make_async_copy(x_refs[t], block(t, me), local_sems.at[t]) for t in range(n)]
        for cp in mine:
            cp.start()
        first = []
        for t in range(n):
            first.append(copy(t, 0, me, sibling, src=x_refs[t]))
            first += [copy(t, 1 + j, me, (*chip, c), src=x_refs[t]) for j, chip in enumerate(chips)]
        for cp in first:
            cp.start()
        passed = []
        for j, chip in enumerate(chips):
            for t in range(n):
                copy(t, 1 + j, (*chip, c), me).wait_recv()
                cp = copy(t, 4 + j, (*chip, c), sibling)
                cp.start()
                passed.append(cp)
        for t in range(n):
            copy(t, 0, sibling, me).wait_recv()
            for j, chip in enumerate(chips):
                copy(t, 4 + j, (*chip, 1 - c), me).wait_recv()
        for cp in first + passed:
            cp.wait_send()
        for cp in mine:
            cp.wait()

    return pl.pallas_call(
        body, name=name, in_specs=[HBM] * n, out_specs=[HBM] * n,
        out_shape=[jax.ShapeDtypeStruct((N_DEV, *s.shape), s.dtype) for s in shards],
        scratch_shapes=[pltpu.SemaphoreType.DMA((n, 7)), pltpu.SemaphoreType.DMA((n, 7)), pltpu.SemaphoreType.DMA((n,))],
    )(*shards)


def exchange_sibling(name, grads):
    n = len(grads)

    def body(*refs):
        g_refs, out_refs = refs[:n], refs[n:2 * n]
        send_sems, recv_sems = refs[2 * n:]
        x, y, c = _place()
        cps = [pltpu.make_async_remote_copy(
            src_ref=g_refs[t].at[1 - c], dst_ref=out_refs[t], send_sem=send_sems.at[t], recv_sem=recv_sems.at[t],
            device_id=(x, y, 1 - c), device_id_type=MESH) for t in range(n)]
        for cp in cps:
            cp.start()
        for cp in cps:
            cp.wait()

    return pl.pallas_call(
        body, name=name, in_specs=[HBM] * n, out_specs=[HBM] * n,
        out_shape=[jax.ShapeDtypeStruct(g.shape[1:], g.dtype) for g in grads],
        scratch_shapes=[pltpu.SemaphoreType.DMA((n,)), pltpu.SemaphoreType.DMA((n,))],
    )(*grads)


def exchange_chips(name, parts):
    n = len(parts)

    def body(*refs):
        p_refs, out_refs = refs[:n], refs[n:2 * n]
        send_sems, recv_sems = refs[2 * n:]
        x, y, c = _place()
        chips = [(1 - x, y), (x, 1 - y), (1 - x, 1 - y)]
        cps = [pltpu.make_async_remote_copy(
            src_ref=p_refs[t].at[2 * chip[0] + chip[1]], dst_ref=out_refs[t].at[j],
            send_sem=send_sems.at[t, j], recv_sem=recv_sems.at[t, j], device_id=(*chip, c), device_id_type=MESH)
            for t in range(n) for j, chip in enumerate(chips)]
        for cp in cps:
            cp.start()
        for cp in cps:
            cp.wait()

    return pl.pallas_call(
        body, name=name, in_specs=[HBM] * n, out_specs=[HBM] * n,
        out_shape=[jax.ShapeDtypeStruct((3, *p.shape[1:]), p.dtype) for p in parts],
        scratch_shapes=[pltpu.SemaphoreType.DMA((n, 3)), pltpu.SemaphoreType.DMA((n, 3))],
    )(*parts)


def _as_rows(shape):
    return (int(np.prod(shape[:-1])), shape[-1])


ELEMENTWISE_BLOCK = 256 * 1024


def _row_tile(rows, cols):
    return _tile(rows, max(128, ELEMENTWISE_BLOCK // cols // 128 * 128))


def add_sibling(name, grad, recv, core):
    rows, cols = _as_rows(grad.shape[2:])
    tr = _row_tile(rows, cols)

    def body(c_ref, g_ref, r_ref, o_ref):
        o_ref[...] = (g_ref[...].astype(F32) + r_ref[...].astype(F32)).astype(BF16)

    blk = pl.BlockSpec((None, tr, cols), lambda k, i, c_ref: (k, i, 0))
    return pl.pallas_call(
        body, name=name,
        grid_spec=pltpu.PrefetchScalarGridSpec(
            num_scalar_prefetch=1, grid=(4, rows // tr),
            in_specs=[pl.BlockSpec((None, None, tr, cols), lambda k, i, c_ref: (c_ref[0], k, i, 0)), blk],
            out_specs=blk),
        out_shape=jax.ShapeDtypeStruct((4, rows, cols), BF16), compiler_params=_params("parallel", "parallel"),
    )(core, grad.reshape(2, 4, rows, cols), recv.reshape(4, rows, cols)).reshape(recv.shape)


def sum_chips(name, part, recv, chip):
    shape = part.shape[1:]
    rows, cols = _as_rows(shape)
    tr = _row_tile(rows, cols)

    def body(c_ref, p_ref, r_ref, o_ref):
        o_ref[...] = (p_ref[...].astype(F32) + r_ref[0].astype(F32) + r_ref[1].astype(F32) + r_ref[2].astype(F32))

    return pl.pallas_call(
        body, name=name,
        grid_spec=pltpu.PrefetchScalarGridSpec(
            num_scalar_prefetch=1, grid=(rows // tr,),
            in_specs=[pl.BlockSpec((None, tr, cols), lambda i, c_ref: (c_ref[0], i, 0)),
                      pl.BlockSpec((3, tr, cols), lambda i, c_ref: (0, i, 0))],
            out_specs=pl.BlockSpec((tr, cols), lambda i, c_ref: (i, 0))),
        out_shape=jax.ShapeDtypeStruct((rows, cols), F32), compiler_params=_params("parallel"),
    )(chip, part.reshape(4, rows, cols), recv.reshape(3, rows, cols)).reshape(shape)


def sum_devices(name, gathered):
    _, rows, cols = gathered.shape

    def body(g_ref, o_ref):
        acc = g_ref[0]
        for d in range(1, N_DEV):
            acc = acc + g_ref[d]
        o_ref[...] = acc

    return pl.pallas_call(body, name=name, out_shape=jax.ShapeDtypeStruct((rows, cols), F32))(gathered)


def adamw(name, w, g, m, v):
    shape = w.shape
    rows, cols = _as_rows(shape)
    tr = _row_tile(rows, cols) if rows % 8 == 0 else rows
    c1 = 1.0 / (1.0 - ADAM_B1 ** ADAM_STEP)
    c2 = 1.0 / (1.0 - ADAM_B2 ** ADAM_STEP)

    def body(w_ref, g_ref, m_ref, v_ref, d_ref, mo_ref, vo_ref):
        g_ = g_ref[...]
        m_ = ADAM_B1 * m_ref[...] + (1.0 - ADAM_B1) * g_
        v_ = ADAM_B2 * v_ref[...] + (1.0 - ADAM_B2) * (g_ * g_)
        d_ref[...] = -ADAM_LR * ((m_ * c1) / (jnp.sqrt(v_ * c2) + ADAM_EPS) + ADAM_WD * w_ref[...])
        mo_ref[...] = m_
        vo_ref[...] = v_

    blk = pl.BlockSpec((tr, cols), lambda i: (i, 0))
    outs = pl.pallas_call(
        body, name=name, grid=(rows // tr,), in_specs=[blk] * 4, out_specs=[blk] * 3,
        out_shape=[jax.ShapeDtypeStruct((rows, cols), F32)] * 3, compiler_params=_params("parallel"),
    )(*[a.reshape(rows, cols) for a in (w, g, m, v)])
    return [o.reshape(shape) for o in outs]


def _spread_rope(r):
    z = jnp.zeros_like(r[..., :32])
    return jnp.concatenate([r[..., :32], z, r[..., 32:], z], -1)


def _gather_rope(r):
    return jnp.concatenate([r[..., :32], r[..., 64:96]], -1)


def pad_w_uq(w):
    w = w.reshape(w.shape[0], -1, MLA_NOPE + MLA_ROPE)
    return jnp.concatenate([w[..., :MLA_NOPE], _spread_rope(w[..., MLA_NOPE:])], -1).reshape(w.shape[0], -1)


def unpad_w_uq(g):
    g = g.reshape(g.shape[0], -1, 2 * MLA_NOPE)
    return jnp.concatenate([g[..., :MLA_NOPE], _gather_rope(g[..., MLA_NOPE:])], -1).reshape(g.shape[0], -1)


def pad_w_down(w):
    lat = MLA_Q_LORA + MLA_KV_LORA
    return jnp.concatenate([w[:, :lat], _spread_rope(w[:, lat:])], -1)


def unpad_w_down(g):
    lat = MLA_Q_LORA + MLA_KV_LORA
    return jnp.concatenate([g[:, :lat], _gather_rope(g[:, lat:])], -1)


def _heads(arr, width, first=0, stride=1):
    return HeadCols(arr, width, lambda h: first + stride * h)


def mla_forward(h16, w, gq, gkv, tables):
    cos, sin = tables
    down = mm_nn("mla_down", h16, w["down"], [F32])[0]
    cq, ckv = rms_fwd("mla_rms", down, gq, gkv)
    q = mm_nn("mla_uq", cq, w["uq"], [F32])[0]
    kv = mm_nn("mla_ukv", ckv, w["ukv"], [BF16])[0]
    qr, kp = mla_prep_fwd("mla_prep", q, kv, down, cos, sin)
    scale = (MLA_NOPE + MLA_ROPE) ** -0.5
    o, lse = softmax_attn_fwd("mla_attn", "mla", _heads(qr, 256), _heads(kp, 256), _heads(kv, 128, 1, 2), scale)
    m = mm_nn("mla_wo", o, w["wo"], [F32])[0]
    return m, (down, cq, ckv, qr, kp, kv, o, lse)


def mla_backward(du16, h16, saved, w, gq, gkv, tables):
    cos, sin = tables
    down, cq, ckv, qr, kp, kv, o, lse = saved
    scale = (MLA_NOPE + MLA_ROPE) ** -0.5
    g = {"wo": mm_tn("mla_dwo", o, du16, "row", w["wo"].R, w["wo"].C)}
    do = mm_nt("mla_do", du16, w["wo"], BF16)
    dq, dk, dv = softmax_attn_bwd("mla_attn_bwd", "mla", _heads(qr, 256), _heads(kp, 256), _heads(kv, 128, 1, 2),
                                  _heads(o, 128), _heads(do, 128), lse, scale)
    dq16, dkv16, dkr = mla_prep_bwd("mla_prep_bwd", dq, dk, dv, cos, sin)
    g["uq"] = mm_tn("mla_duq", cq, dq16, "col", w["uq"].R, w["uq"].C)
    dcq = mm_nt("mla_dcq", dq16, w["uq"], F32)
    g["ukv"] = mm_tn("mla_dukv", ckv, dkv16, "col", w["ukv"].R, w["ukv"].C)
    dckv = mm_nt("mla_dckv", dkv16, w["ukv"], F32)
    ddown, dgq, dgkv = rms_bwd("mla_rms_bwd", down, dcq, dckv, dkr, gq, gkv)
    g["down"] = mm_tn("mla_ddown", h16, ddown, "row", w["down"].R, w["down"].C)
    dh = mm_nt("mla_dh", ddown, w["down"], F32)
    return dh, g, (dgq, dgkv)


def qkv_forward(kind, h16, w, bias=None):
    qkv = mm_nn(kind + "_qkv", h16, w["qkv"], [BF16])[0]
    q, k, v = _heads(qkv, 128), _heads(qkv, 128, HEADS), _heads(qkv, 128, 2 * HEADS)
    scale = HEAD_DIM ** -0.5
    if kind == "sb":
        o, lse = stick_attn_fwd("sb_attn", q, k, v, scale), None
    else:
        o, lse = softmax_attn_fwd("ca_attn", "ca", q, k, v, scale, bias)
    m = mm_nn(kind + "_wo", o, w["wo"], [F32])[0]
    return m, (qkv, o, lse)


def qkv_backward(kind, du16, h16, saved, w, bias=None):
    qkv, o, lse = saved
    q, k, v = _heads(qkv, 128), _heads(qkv, 128, HEADS), _heads(qkv, 128, 2 * HEADS)
    scale = HEAD_DIM ** -0.5
    g = {"wo": mm_tn(kind + "_dwo", o, du16, "row", w["wo"].R, w["wo"].C)}
    do = mm_nt(kind + "_do", du16, w["wo"], BF16)
    dbias = None
    if kind == "sb":
        dq, dk, dv = stick_attn_bwd("sb_attn_bwd", q, k, v, _heads(do, 128), scale)
    else:
        dq, dk, dv, dbias = softmax_attn_bwd("ca_attn_bwd", "ca", q, k, v, _heads(o, 128), _heads(do, 128), lse,
                                             scale, bias)
    dqkv = jnp.concatenate([dq, dk, dv], axis=1).astype(BF16)
    g["qkv"] = mm_tn(kind + "_dqkv", h16, dqkv, "col", w["qkv"].R, w["qkv"].C)
    dh = mm_nt(kind + "_dh", dqkv, w["qkv"], F32)
    return dh, g, dbias


def mlp_forward(h16, w):
    a, z = mm_nn("ffn_in", h16, w["w_in"], [F32, BF16], epilogue=_relu2_epilogue)
    f = mm_nn("ffn_out", z, w["w_out"], [F32])[0]
    return f, (a, z)


def mlp_backward(du16, h16, saved, w):
    a, z = saved
    da = mm_nt("ffn_da", du16, w["w_out"], BF16, epilogue=_mulrelu_epilogue, extra=a)
    g = {"w_out": mm_tn("ffn_dwout", z, du16, "row", w["w_out"].R, w["w_out"].C)}
    dh = mm_nt("ffn_dh", da, w["w_in"], F32)
    g["w_in"] = mm_tn("ffn_dwin", h16, da, "col", w["w_in"].R, w["w_in"].C)
    return dh, g


WEIGHTS = ("ln_mix_g", "ln_mix_b", "ln_ffn_g", "ln_ffn_b", "ffn_w_in", "ffn_w_out", "mla_w_down", "mla_q_norm_g",
           "mla_w_uq", "mla_kv_norm_g", "mla_w_ukv", "mla_w_o", "sb_w_qkv", "sb_w_o", "ca_w_qkv", "ca_rel_bias",
           "ca_w_o")
MIXERS = ("mla", "sb", "ca")
LAYER_WEIGHTS = {
    "mla": (("down", "mla_w_down", "row"), ("uq", "mla_w_uq", "col"), ("ukv", "mla_w_ukv", "col"),
            ("wo", "mla_w_o", "row")),
    "sb": (("qkv", "sb_w_qkv", "col"), ("wo", "sb_w_o", "row")),
    "ca": (("qkv", "ca_w_qkv", "col"), ("wo", "ca_w_o", "row")),
    "ffn": (("w_in", "ffn_w_in", "col"), ("w_out", "ffn_w_out", "row")),
}
PAD = {"mla_w_down": pad_w_down, "mla_w_uq": pad_w_uq}
UNPAD = {"mla_w_down": unpad_w_down, "mla_w_uq": unpad_w_uq}


def _pack_rows(vectors):
    flat = jnp.concatenate([v.reshape(-1) for v in vectors])
    n = flat.shape[0]
    rows = -(-n // 1024) * 8
    offsets = np.cumsum([0] + [int(np.prod(v.shape)) for v in vectors])
    return jnp.pad(flat, (0, rows * 128 - n)).reshape(rows, 128), offsets


def reduce_scatter(tag, grads, core, chip):
    from_sibling = exchange_sibling("rs_sibling_" + tag, grads)
    parts = [add_sibling(f"rs_add_{tag}{t}", g, r, core) for t, (g, r) in enumerate(zip(grads, from_sibling))]
    from_chips = exchange_chips("rs_chips_" + tag, parts)
    return [sum_chips(f"rs_sum_{tag}{t}", p, r, chip) for t, (p, r) in enumerate(zip(parts, from_chips))]


def kernel(x, ln_mix_g, ln_mix_b, ln_ffn_g, ln_ffn_b, ffn_w_in, ffn_w_out, mla_w_down, mla_q_norm_g, mla_w_uq, mla_kv_norm_g, mla_w_ukv, mla_w_o, sb_w_qkv, sb_w_o, ca_w_qkv, ca_rel_bias, ca_w_o, loss_target, m_ln_mix_g, m_ln_mix_b, m_ln_ffn_g, m_ln_ffn_b, m_ffn_w_in, m_ffn_w_out, m_mla_w_down, m_mla_q_norm_g, m_mla_w_uq, m_mla_kv_norm_g, m_mla_w_ukv, m_mla_w_o, m_sb_w_qkv, m_sb_w_o, m_ca_w_qkv, m_ca_rel_bias, m_ca_w_o, v_ln_mix_g, v_ln_mix_b, v_ln_ffn_g, v_ln_ffn_b, v_ffn_w_in, v_ffn_w_out, v_mla_w_down, v_mla_q_norm_g, v_mla_w_uq, v_mla_kv_norm_g, v_mla_w_ukv, v_mla_w_o, v_sb_w_qkv, v_sb_w_o, v_ca_w_qkv, v_ca_rel_bias, v_ca_w_o):
    w = dict(zip(WEIGHTS, (ln_mix_g, ln_mix_b, ln_ffn_g, ln_ffn_b, ffn_w_in, ffn_w_out, mla_w_down, mla_q_norm_g,
                           mla_w_uq, mla_kv_norm_g, mla_w_ukv, mla_w_o, sb_w_qkv, sb_w_o, ca_w_qkv, ca_rel_bias,
                           ca_w_o)))
    mom = dict(zip(WEIGHTS, (m_ln_mix_g, m_ln_mix_b, m_ln_ffn_g, m_ln_ffn_b, m_ffn_w_in, m_ffn_w_out, m_mla_w_down,
                             m_mla_q_norm_g, m_mla_w_uq, m_mla_kv_norm_g, m_mla_w_ukv, m_mla_w_o, m_sb_w_qkv,
                             m_sb_w_o, m_ca_w_qkv, m_ca_rel_bias, m_ca_w_o)))
    var = dict(zip(WEIGHTS, (v_ln_mix_g, v_ln_mix_b, v_ln_ffn_g, v_ln_ffn_b, v_ffn_w_in, v_ffn_w_out, v_mla_w_down,
                             v_mla_q_norm_g, v_mla_w_uq, v_mla_kv_norm_g, v_mla_w_ukv, v_mla_w_o, v_sb_w_qkv,
                             v_sb_w_o, v_ca_w_qkv, v_ca_rel_bias, v_ca_w_o)))
    S, D = x.shape[1], x.shape[2]
    xi, yi, ci = _place()
    core = ci.astype(jnp.int32).reshape(1)
    chip = (2 * xi + yi).astype(jnp.int32).reshape(1)
    me = 4 * xi + 2 * yi + ci
    tables = rope_tables(S)
    n_mla = mla_w_down.shape[0]
    lat = MLA_Q_LORA // N_DEV

    gains = jnp.pad(jnp.stack([mla_q_norm_g.reshape(-1), mla_kv_norm_g.reshape(-1)]), ((0, 6), (0, 128 - n_mla * lat)))
    gains = all_gather("ag_gains", [gains])[0]

    def full_gain(row, slot):
        return gains[:, row, slot * lat:(slot + 1) * lat].reshape(-1)

    def gather_layer(i):
        kind, slot = MIXERS[i % 3], i // 3
        specs = [(key, name, how, slot) for key, name, how in LAYER_WEIGHTS[kind]]
        specs += [(key, name, how, i) for key, name, how in LAYER_WEIGHTS["ffn"]]
        shards = [PAD.get(name, lambda a: a)(w[name][idx]).astype(BF16) for _, name, _, idx in specs]
        gathered = all_gather("ag_" + kind, shards)
        return {key: Weight(how, g) for (key, _, how, _), g in zip(specs, gathered)}, specs

    layers = [gather_layer(i) for i in range(DEPTH)]
    bias = rel_bias_blocks("ca_bias", ca_rel_bias[0], min(ATT_T, S))

    h, h16 = x[0], x[0].astype(BF16)
    saved = []
    for i in range(DEPTH):
        kind, slot = MIXERS[i % 3], i // 3
        lw = layers[i][0]
        if kind == "mla":
            mix, s_mix = mla_forward(h16, lw, full_gain(0, slot), full_gain(1, slot), tables)
        else:
            mix, s_mix = qkv_forward(kind, h16, lw, bias if kind == "ca" else None)
        y, y16, xh1, rs1 = ln_fwd("ln_mix", h, mix, ln_mix_g[i], ln_mix_b[i])
        f, s_mlp = mlp_forward(y16, lw)
        y2, y2_16, xh2, rs2 = ln_fwd("ln_ffn", y, f, ln_ffn_g[i], ln_ffn_b[i])
        saved.append((h16, s_mix, xh1, rs1, y16, s_mlp, xh2, rs2))
        h, h16 = y2, y2_16
    sq, dy = loss_fwd_bwd("loss", h, loss_target[0])
    loss = 0.5 / D * lax.psum(sq[0, 0], ("x", "y", "c"))

    ga, gb = dy, None
    grads = {name: [None] * w[name].shape[0] for name in WEIGHTS}
    dbias = None
    for i in reversed(range(DEPTH)):
        kind, slot = MIXERS[i % 3], i // 3
        lw, specs = layers[i]
        h16_in, s_mix, xh1, rs1, y16, s_mlp, xh2, rs2 = saved[i]
        du, du16, grads["ln_ffn_g"][i], grads["ln_ffn_b"][i] = ln_bwd("ln_ffn_bwd", ga, gb, xh2, rs2, ln_ffn_g[i])
        dh_mlp, g_mlp = mlp_backward(du16, y16, s_mlp, lw)
        du, du16, grads["ln_mix_g"][i], grads["ln_mix_b"][i] = ln_bwd("ln_mix_bwd", du, dh_mlp, xh1, rs1, ln_mix_g[i])
        if kind == "mla":
            dh_mix, g_mix, (dgq, dgkv) = mla_backward(du16, h16_in, s_mix, lw, full_gain(0, slot), full_gain(1, slot),
                                                      tables)
            grads["mla_q_norm_g"][slot], grads["mla_kv_norm_g"][slot] = dgq, dgkv
        else:
            dh_mix, g_mix, db = qkv_backward(kind, du16, h16_in, s_mix, lw, bias if kind == "ca" else None)
            dbias = db if kind == "ca" else dbias
        ga, gb = du, dh_mix
        full = {**g_mix, **g_mlp}
        reduced = reduce_scatter(kind, [full[key] for key, _, _, _ in specs], core, chip)
        for (_, name, _, idx), g in zip(specs, reduced):
            grads[name][idx] = UNPAD.get(name, lambda a: a)(g)
    grad_x = axpy("grad_x", ga, gb)[None]
    grads["ca_rel_bias"][0] = rel_bias_grad("ca_bias_grad", dbias)

    small = ("ln_mix_g", "ln_mix_b", "ln_ffn_g", "ln_ffn_b", "ca_rel_bias", "mla_q_norm_g", "mla_kv_norm_g")
    packed, offsets = _pack_rows([g for name in small for g in grads[name]])
    total = sum_devices("sum_small", all_gather("ag_small", [packed])[0]).reshape(-1)
    pos = 0
    for name in small:
        for idx, g in enumerate(grads[name]):
            full = total[offsets[pos]:offsets[pos + 1]]
            pos += 1
            if name in ("mla_q_norm_g", "mla_kv_norm_g"):
                full = lax.dynamic_slice(full, (me * lat,), (lat,))
            grads[name][idx] = full.reshape(w[name].shape[1:])

    g_out, d_out, m_out, v_out = [], [], [], []
    for name in WEIGHTS:
        g = jnp.stack(grads[name])
        delta, new_m, new_v = adamw("adamw_" + name, w[name], g, mom[name], var[name])
        g_out.append(g)
        d_out.append(delta)
        m_out.append(new_m)
        v_out.append(new_v)
    return (loss, grad_x, *g_out, *d_out, *m_out, *v_out)
```

```python
import functools
import math

import numpy as np
import jax
import jax.numpy as jnp
from jax import lax
from jax.experimental import pallas as pl
from jax.experimental.pallas import tpu as pltpu

F32 = jnp.float32
BF16 = jnp.bfloat16
MESH = pl.DeviceIdType.MESH
N_DEV = 8

DEPTH = 4
CHUNK = 64
CHUNK_SHIFT = 6
HEADS = 16
HEAD_DIM = 128
MLA_Q_LORA = 512
MLA_KV_LORA = 512
MLA_NOPE = 128
MLA_ROPE = 64
ROPE_THETA = 10000.0
CA_LEFT_CHUNKS = 8
REL_CLIP_LEFT = 128
REL_TABLE = REL_CLIP_LEFT + CHUNK
LN_EPS = 1e-5
RMS_EPS = 1e-6
ALPHA = (2.0 * DEPTH) ** 0.25
NEG = -1e30
ADAM_LR = 0.001
ADAM_B1 = 0.9
ADAM_B2 = 0.999
ADAM_EPS = 1e-08
ADAM_WD = 0.01
ADAM_STEP = 10

V7X_VMEM_BYTES = 64 * 1024 * 1024
VMEM_LIMIT = V7X_VMEM_BYTES - 8 * 1024 * 1024
ATT_TQ = 512
ATT_TK = 256
ATT_G = 2


def _params(*sem):
    return pltpu.CompilerParams(dimension_semantics=sem if sem else None, vmem_limit_bytes=VMEM_LIMIT)


def _matmul(name, a, b, *, contract, grid, a_spec, b_spec, o_specs, out_shape, acc_shape,
            epilogue=None, extra=(), extra_specs=()):
    nk = grid[2]
    n_extra = len(extra)
    n_out = len(out_shape)

    def finish(acc, e_refs, o_refs):
        outs = epilogue(acc, *[e[...] for e in e_refs]) if epilogue else (acc,)
        for o_ref, val in zip(o_refs, outs):
            o_ref[...] = val.astype(o_ref.dtype)

    def product(a_ref, b_ref):
        return lax.dot_general(a_ref[...], b_ref[...], (contract, ((), ())), preferred_element_type=F32)

    def body_single(*refs):
        finish(product(refs[0], refs[1]), refs[2:2 + n_extra], refs[2 + n_extra:2 + n_extra + n_out])

    def body(*refs):
        a_ref, b_ref = refs[0], refs[1]
        acc_ref = refs[-1]
        k = pl.program_id(2)

        @pl.when(k == 0)
        def _():
            acc_ref[...] = jnp.zeros_like(acc_ref)

        acc_ref[...] += product(a_ref, b_ref)

        @pl.when(k == nk - 1)
        def _():
            finish(acc_ref[...], refs[2:2 + n_extra], refs[2 + n_extra:2 + n_extra + n_out])

    return pl.pallas_call(
        body_single if nk == 1 else body, name=name, grid=grid, in_specs=[a_spec, b_spec, *extra_specs],
        out_specs=list(o_specs), out_shape=list(out_shape),
        scratch_shapes=[] if nk == 1 else [pltpu.VMEM(acc_shape, F32)],
        compiler_params=_params("parallel", "parallel", "arbitrary"),
    )(a, b, *extra)


MATMUL_BLOCK_BYTES = 40 * 1024 * 1024
MAX_TK = 2048
MULTI_TK = 512


def _fit_tn(n, tm, tk, nk, out_bytes):
    cands = sorted({n} | {t for t in range(128, n, 128) if n % t == 0}, reverse=True)
    for tn in cands:
        need = 2 * 2 * (tm * tk + tk * tn) + 2 * tm * tn * out_bytes + (tm * tn * 4 if nk > 1 else 0) + tm * tn * 4
        if need <= MATMUL_BLOCK_BYTES:
            return tn
    return cands[-1]


def _itemsize(dtypes):
    return sum(jnp.dtype(d).itemsize for d in dtypes)


def _tile(n, pref):
    if n <= pref:
        return n
    t = pref
    while t >= 128:
        if n % t == 0 and t % 128 == 0:
            return t
        t -= 128
    return n


class Weight:
    def __init__(self, kind, arr):
        self.kind = kind
        self.arr = arr
        self.R, self.C = arr.shape[1], arr.shape[2]

    @property
    def two_d(self):
        return self.arr.reshape(N_DEV * self.R, self.C)


def mm_nn(name, a, w, out_dtypes, epilogue=None):
    M, K = a.shape
    tm = M
    tk = K if K <= MAX_TK else _tile(K, MULTI_TK)
    nk = K // tk
    if w.kind == "row":
        b = w.two_d
        N = w.C
        tn = _fit_tn(N, tm, tk, nk, _itemsize(out_dtypes))
        b_spec = pl.BlockSpec((tk, tn), lambda i, j, k: (k, j))
    else:
        b = w.arr
        N = N_DEV * w.C
        tn = _fit_tn(w.C, tm, tk, nk, _itemsize(out_dtypes))
        per = w.C // tn
        b_spec = pl.BlockSpec((None, tk, tn), lambda i, j, k: (j // per, k, j % per))
    grid = (M // tm, N // tn, nk)
    return _matmul(
        name, a, b, contract=((1,), (0,)), grid=grid,
        a_spec=pl.BlockSpec((tm, tk), lambda i, j, k: (i, k)), b_spec=b_spec,
        o_specs=[pl.BlockSpec((tm, tn), lambda i, j, k: (i, j)) for _ in out_dtypes],
        out_shape=[jax.ShapeDtypeStruct((M, N), d) for d in out_dtypes], acc_shape=(tm, tn), epilogue=epilogue)


def mm_nt(name, dy, w, out_dtype, epilogue=None, extra=None):
    M, N = dy.shape
    tm = M
    out_bytes = jnp.dtype(out_dtype).itemsize + (0 if extra is None else extra.dtype.itemsize)
    if w.kind == "row":
        b = w.two_d
        kin = N_DEV * w.R
        tk = N if N <= MAX_TK else _tile(N, MULTI_TK)
        tn = _fit_tn(kin, tm, tk, N // tk, out_bytes)
        b_spec = pl.BlockSpec((tn, tk), lambda i, j, k: (j, k))
    else:
        b = w.arr
        kin = w.R
        tk = _tile(w.C, MULTI_TK)
        per = w.C // tk
        tn = _fit_tn(kin, tm, tk, N // tk, out_bytes)
        b_spec = pl.BlockSpec((None, tn, tk), lambda i, j, k: (k // per, j, k % per))
    grid = (M // tm, kin // tn, N // tk)
    o_spec = pl.BlockSpec((tm, tn), lambda i, j, k: (i, j))
    return _matmul(
        name, dy, b, contract=((1,), (1,)), grid=grid,
        a_spec=pl.BlockSpec((tm, tk), lambda i, j, k: (i, k)), b_spec=b_spec, o_specs=[o_spec],
        out_shape=[jax.ShapeDtypeStruct((M, kin), out_dtype)], acc_shape=(tm, tn), epilogue=epilogue,
        extra=() if extra is None else (extra,), extra_specs=() if extra is None else (o_spec,))[0]


TRANSPOSE_TILE = 512


def transpose(name, x):
    S, n = x.shape
    ts, tn = _tile(S, TRANSPOSE_TILE), _tile(n, TRANSPOSE_TILE)

    def body(x_ref, o_ref):
        o_ref[...] = x_ref[...].T

    return pl.pallas_call(
        body, name=name, grid=(S // ts, n // tn), in_specs=[pl.BlockSpec((ts, tn), lambda i, j: (i, j))],
        out_specs=pl.BlockSpec((tn, ts), lambda i, j: (j, i)), out_shape=jax.ShapeDtypeStruct((n, S), x.dtype),
        compiler_params=_params("parallel", "parallel"),
    )(x)


def mm_tn(name, x, dy, kind, R, C):
    S, kin = x.shape
    x = transpose(name + "_t", x)
    N = dy.shape[1]
    tk = S if S <= MAX_TK else _tile(S, MULTI_TK)
    nk = S // tk
    if kind == "col":
        tm = kin
        tn = _fit_tn(C, tm, tk, nk, 2)
        per = C // tn
        grid = (1, N // tn, nk)
        o_spec = pl.BlockSpec((None, None, tm, tn), lambda i, j, k: ((j // per) % 2, (j // per) // 2, 0, j % per))
    else:
        tm = R
        tn = _fit_tn(N, tm, tk, nk, 2)
        grid = (N_DEV, N // tn, nk)
        o_spec = pl.BlockSpec((None, None, tm, tn), lambda i, j, k: (i % 2, i // 2, 0, j))
    return _matmul(
        name, x, dy, contract=((1,), (0,)), grid=grid,
        a_spec=pl.BlockSpec((tm, tk), lambda i, j, k: (i, k)),
        b_spec=pl.BlockSpec((tk, tn), lambda i, j, k: (k, j)), o_specs=[o_spec],
        out_shape=[jax.ShapeDtypeStruct((2, 4, R, C), BF16)], acc_shape=(tm, tn))[0]


def _relu2_epilogue(acc):
    r = jnp.maximum(acc, 0.0)
    return acc, r * r


def _mulrelu_epilogue(acc, a):
    return (acc * (2.0 * jnp.maximum(a, 0.0)),)


ROW_TILE = 256


def ln_fwd(name, h, m, g, b):
    S, D = h.shape
    ts = _tile(S, ROW_TILE)

    def body(h_ref, m_ref, g_ref, b_ref, y_ref, y16_ref, xh_ref, rs_ref):
        u = ALPHA * h_ref[...] + m_ref[...]
        mu = jnp.mean(u, axis=-1, keepdims=True)
        d = u - mu
        var = jnp.mean(d * d, axis=-1, keepdims=True)
        rstd = lax.rsqrt(var + LN_EPS)
        xh = d * rstd
        y = xh * g_ref[...] + b_ref[...]
        y_ref[...] = y
        y16_ref[...] = y.astype(BF16)
        xh_ref[...] = xh
        rs_ref[...] = jnp.broadcast_to(rstd, rs_ref.shape)

    row = pl.BlockSpec((ts, D), lambda i: (i, 0))
    vec = pl.BlockSpec((1, D), lambda i: (0, 0))
    return pl.pallas_call(
        body, name=name, grid=(S // ts,), in_specs=[row, row, vec, vec],
        out_specs=[row, row, row, pl.BlockSpec((ts, 128), lambda i: (i, 0))],
        out_shape=[jax.ShapeDtypeStruct((S, D), F32), jax.ShapeDtypeStruct((S, D), BF16),
                   jax.ShapeDtypeStruct((S, D), F32), jax.ShapeDtypeStruct((S, 128), F32)],
        compiler_params=_params("parallel"),
    )(h, m, g.reshape(1, D), b.reshape(1, D))


def ln_bwd(name, ga, gb, xhat, rstd, g):
    S, D = xhat.shape
    ts = _tile(S, ROW_TILE)
    two = gb is not None

    def body(*refs):
        if two:
            ga_ref, gb_ref, xh_ref, rs_ref, g_ref, du_ref, du16_ref, dg_ref, db_ref = refs
            dy = ALPHA * ga_ref[...] + gb_ref[...]
        else:
            ga_ref, xh_ref, rs_ref, g_ref, du_ref, du16_ref, dg_ref, db_ref = refs
            dy = ga_ref[...]
        xh = xh_ref[...]

        @pl.when(pl.program_id(0) == 0)
        def _():
            dg_ref[...] = jnp.zeros_like(dg_ref)
            db_ref[...] = jnp.zeros_like(db_ref)

        dg_ref[...] += jnp.sum(dy * xh, axis=0, keepdims=True)
        db_ref[...] += jnp.sum(dy, axis=0, keepdims=True)
        dxh = dy * g_ref[...]
        m1 = jnp.mean(dxh, axis=-1, keepdims=True)
        m2 = jnp.mean(dxh * xh, axis=-1, keepdims=True)
        du = rs_ref[:, 0:1] * (dxh - m1 - xh * m2)
        du_ref[...] = du
        du16_ref[...] = du.astype(BF16)

    row = pl.BlockSpec((ts, D), lambda i: (i, 0))
    vec = pl.BlockSpec((1, D), lambda i: (0, 0))
    stat = pl.BlockSpec((ts, 128), lambda i: (i, 0))
    ins = [ga, gb, xhat, rstd, g.reshape(1, D)] if two else [ga, xhat, rstd, g.reshape(1, D)]
    in_specs = [row, row, row, stat, vec] if two else [row, row, stat, vec]
    return pl.pallas_call(
        body, name=name, grid=(S // ts,), in_specs=in_specs, out_specs=[row, row, vec, vec],
        out_shape=[jax.ShapeDtypeStruct((S, D), F32), jax.ShapeDtypeStruct((S, D), BF16),
                   jax.ShapeDtypeStruct((1, D), F32), jax.ShapeDtypeStruct((1, D), F32)],
        compiler_params=_params("arbitrary"),
    )(*ins)


def loss_fwd_bwd(name, y, target):
    S, D = y.shape
    ts = _tile(S, ROW_TILE)

    def body(y_ref, t_ref, l_ref, dy_ref):
        @pl.when(pl.program_id(0) == 0)
        def _():
            l_ref[...] = jnp.zeros_like(l_ref)

        e = y_ref[...] - t_ref[...]
        l_ref[...] += jnp.sum(e * e)
        dy_ref[...] = e * (1.0 / D)

    row = pl.BlockSpec((ts, D), lambda i: (i, 0))
    return pl.pallas_call(
        body, name=name, grid=(S // ts,), in_specs=[row, row],
        out_specs=[pl.BlockSpec((1, 128), lambda i: (0, 0)), row],
        out_shape=[jax.ShapeDtypeStruct((1, 128), F32), jax.ShapeDtypeStruct((S, D), F32)],
        compiler_params=_params("arbitrary"),
    )(y, target)


def axpy(name, ga, gb):
    S, D = ga.shape
    ts = _tile(S, ROW_TILE)

    def body(a_ref, b_ref, o_ref):
        o_ref[...] = ALPHA * a_ref[...] + b_ref[...]

    row = pl.BlockSpec((ts, D), lambda i: (i, 0))
    return pl.pallas_call(body, name=name, grid=(S // ts,), in_specs=[row, row], out_specs=row,
                          out_shape=jax.ShapeDtypeStruct((S, D), F32), compiler_params=_params("parallel"))(ga, gb)


def rms_fwd(name, down, gq, gkv):
    S = down.shape[0]
    ts = _tile(S, ROW_TILE)
    L = MLA_Q_LORA

    def body(d_ref, gq_ref, gkv_ref, q_ref, kv_ref):
        for lo, g_ref, o_ref in ((0, gq_ref, q_ref), (L, gkv_ref, kv_ref)):
            x = d_ref[:, lo:lo + L]
            r = lax.rsqrt(jnp.mean(x * x, axis=-1, keepdims=True) + RMS_EPS)
            o_ref[...] = (x * r * g_ref[...]).astype(BF16)

    vec = pl.BlockSpec((1, L), lambda i: (0, 0))
    out = pl.BlockSpec((ts, L), lambda i: (i, 0))
    return pl.pallas_call(
        body, name=name, grid=(S // ts,), in_specs=[pl.BlockSpec((ts, down.shape[1]), lambda i: (i, 0)), vec, vec],
        out_specs=[out, out], out_shape=[jax.ShapeDtypeStruct((S, L), BF16)] * 2, compiler_params=_params("parallel"),
    )(down, gq.reshape(1, L), gkv.reshape(1, L))


def rms_bwd(name, down, dq, dkv, dkr, gq, gkv):
    S, W = down.shape
    ts = _tile(S, ROW_TILE)
    L = MLA_Q_LORA

    def body(d_ref, dq_ref, dkv_ref, dkr_ref, gq_ref, gkv_ref, o_ref, dgq_ref, dgkv_ref):
        @pl.when(pl.program_id(0) == 0)
        def _():
            dgq_ref[...] = jnp.zeros_like(dgq_ref)
            dgkv_ref[...] = jnp.zeros_like(dgkv_ref)

        for lo, dy_ref, g_ref, dg_ref in ((0, dq_ref, gq_ref, dgq_ref), (L, dkv_ref, gkv_ref, dgkv_ref)):
            x = d_ref[:, lo:lo + L]
            dy = dy_ref[...]
            r = lax.rsqrt(jnp.mean(x * x, axis=-1, keepdims=True) + RMS_EPS)
            dg_ref[...] += jnp.sum(dy * x * r, axis=0, keepdims=True)
            dyg = dy * g_ref[...]
            dx = r * dyg - x * (r * r * r) * jnp.mean(dyg * x, axis=-1, keepdims=True)
            o_ref[:, lo:lo + L] = dx.astype(BF16)
        o_ref[:, 2 * L:] = dkr_ref[...].astype(BF16)

    vec = pl.BlockSpec((1, L), lambda i: (0, 0))
    lat = pl.BlockSpec((ts, L), lambda i: (i, 0))
    full = pl.BlockSpec((ts, W), lambda i: (i, 0))
    return pl.pallas_call(
        body, name=name, grid=(S // ts,),
        in_specs=[full, lat, lat, pl.BlockSpec((ts, 128), lambda i: (i, 0)), vec, vec],
        out_specs=[full, vec, vec],
        out_shape=[jax.ShapeDtypeStruct((S, W), BF16), jax.ShapeDtypeStruct((1, L), F32), jax.ShapeDtypeStruct((1, L), F32)],
        compiler_params=_params("arbitrary"),
    )(down, dq, dkv, dkr, gq.reshape(1, L), gkv.reshape(1, L))


def rope_tables(S):
    half = MLA_ROPE // 2
    inv = (np.float32(ROPE_THETA) ** (-np.arange(half, dtype=np.float32) / np.float32(half))).astype(np.float32)
    ang = np.arange(S, dtype=np.float32)[:, None] * inv[None, :]
    cos, sin = np.cos(ang).astype(np.float32), np.sin(ang).astype(np.float32)
    z = np.zeros_like(cos)
    return (jnp.asarray(np.concatenate([cos, z, cos, z], 1)), jnp.asarray(np.concatenate([-sin, z, sin, z], 1)))


def _rot(x, cos, sin):
    return x * cos + pltpu.roll(x, 64, 1) * sin


def mla_prep_fwd(name, q, kv, down, cos, sin):
    S = q.shape[0]
    ts = _tile(S, 512)

    def body(q_ref, kv_ref, kr_ref, c_ref, s_ref, qo_ref, ko_ref):
        c, s = c_ref[...], s_ref[...]
        qo_ref[:, :128] = q_ref[:, :128].astype(BF16)
        qo_ref[:, 128:] = _rot(q_ref[:, 128:], c, s).astype(BF16)
        ko_ref[:, :128] = kv_ref[:, :128]
        ko_ref[:, 128:] = _rot(kr_ref[...], c, s).astype(BF16)

    head = pl.BlockSpec((ts, 256), lambda i, h: (i, h))
    tab = pl.BlockSpec((ts, 128), lambda i, h: (i, 0))
    return pl.pallas_call(
        body, name=name, grid=(S // ts, HEADS),
        in_specs=[head, head, pl.BlockSpec((ts, 128), lambda i, h: (i, 2 * MLA_Q_LORA // 128)), tab, tab],
        out_specs=[head, head], out_shape=[jax.ShapeDtypeStruct(q.shape, BF16)] * 2,
        compiler_params=_params("parallel", "parallel"),
    )(q, kv, down, cos, sin)


def mla_prep_bwd(name, dq, dk, dv, cos, sin):
    S = dq.shape[0]
    ts = _tile(S, 512)

    def body(dq_ref, dk_ref, dv_ref, c_ref, s_ref, qo_ref, kvo_ref, kr_ref):
        c, s = c_ref[...], -s_ref[...]
        qo_ref[:, :128] = dq_ref[:, :128].astype(BF16)
        qo_ref[:, 128:] = _rot(dq_ref[:, 128:], c, s).astype(BF16)
        kvo_ref[:, :128] = dk_ref[:, :128].astype(BF16)
        kvo_ref[:, 128:] = dv_ref[...].astype(BF16)

        @pl.when(pl.program_id(1) == 0)
        def _():
            kr_ref[...] = jnp.zeros_like(kr_ref)

        kr_ref[...] += _rot(dk_ref[:, 128:], c, s)

    head = pl.BlockSpec((ts, 256), lambda i, h: (i, h))
    tab = pl.BlockSpec((ts, 128), lambda i, h: (i, 0))
    return pl.pallas_call(
        body, name=name, grid=(S // ts, HEADS),
        in_specs=[head, head, pl.BlockSpec((ts, 128), lambda i, h: (i, h)), tab, tab],
        out_specs=[head, head, tab],
        out_shape=[jax.ShapeDtypeStruct(dq.shape, BF16), jax.ShapeDtypeStruct(dq.shape, BF16),
                   jax.ShapeDtypeStruct((S, 128), F32)],
        compiler_params=_params("parallel", "arbitrary"),
    )(dq, dk, dv, cos, sin)


def _dot_nt(a, b):
    return lax.dot_general(a, b, (((1,), (1,)), ((), ())), preferred_element_type=F32)


def _dot_tn(a, b):
    return lax.dot_general(a, b, (((0,), (0,)), ((), ())), preferred_element_type=F32)


def _dot(a, b):
    return jnp.dot(a, b, preferred_element_type=F32)


def _positions(i, j, TQ, TK):
    row = i * TQ + lax.broadcasted_iota(jnp.int32, (TQ, TK), 0)
    col = j * TK + lax.broadcasted_iota(jnp.int32, (TQ, TK), 1)
    return row, col


def _softmax_mask(mode, row, col):
    rc, cc = row >> CHUNK_SHIFT, col >> CHUNK_SHIFT
    if mode == "mla":
        return cc <= rc
    return (cc <= rc) & (cc >= rc - CA_LEFT_CHUNKS)


def _key_blocks(mode, i, TQ, TK):
    per = TQ // TK
    if mode == "ca":
        lo = jnp.maximum(i - (CA_LEFT_CHUNKS * CHUNK) // TK, 0)
        return lo, 0, i - lo + 1
    return 0, i * per, per


class HeadCols:
    def __init__(self, arr, width, index, off=0, w=None):
        self.arr, self.width, self.index, self.off = arr, width, index, off
        self.w = width if w is None else w

    def rows(self, T):
        return pl.BlockSpec((T, ATT_G * self.width), lambda p, i: (i, self.index(p)))

    def full(self, S):
        return pl.BlockSpec((S, ATT_G * self.width), lambda p, i: (0, self.index(p)))

    def lanes(self, g):
        lo = g * self.width + self.off
        return slice(lo, lo + self.w)


def _att_tiles(mode, S):
    tk = min(ATT_TK, S)
    return (tk if mode == "ca" else min(ATT_TQ, S)), tk


def _walk(lo, n, per, step, carry, descending=False):
    tail = [lo + n + d for d in range(per)]
    if descending:
        for j in reversed(tail):
            carry = step(j, carry, True)
        return lax.fori_loop(0, n, lambda t, c: step(lo + n - 1 - t, c, False), carry)
    carry = lax.fori_loop(0, n, lambda t, c: step(lo + t, c, False), carry)
    for j in tail:
        carry = step(j, carry, True)
    return carry


def softmax_attn_fwd(name, mode, q, k, v, scale, bias=None):
    S = q.arr.shape[0]
    TQ, TK = _att_tiles(mode, S)
    G, dv = ATT_G, v.w

    def body(*refs):
        if bias is not None:
            q_ref, k_ref, v_ref, b_ref, o_ref, lse_ref = refs
        else:
            q_ref, k_ref, v_ref, o_ref, lse_ref = refs
        i = pl.program_id(1)
        qs = [q_ref[:, q.lanes(g)] for g in range(G)]

        def block(g, j, carry, mask, ks):
            m, l, acc = carry
            s = _dot_nt(qs[g], k_ref[ks, k.lanes(g)]) * scale
            if bias is not None:
                s = s + b_ref[g, jnp.minimum(i - j, 2)]
            if mask is not None:
                s = jnp.where(mask, s, NEG)
            m_new = jnp.maximum(m, jnp.max(s, axis=-1, keepdims=True))
            a = jnp.exp(m - m_new)
            p = jnp.exp(s - m_new)
            if mask is not None:
                p = jnp.where(mask, p, 0.0)
            l = a * l + jnp.sum(p, axis=-1, keepdims=True)
            acc = a * acc + _dot(p.astype(BF16), v_ref[ks, v.lanes(g)])
            return m_new, l, acc

        def step(j, carry, masked):
            ks = pl.ds(pl.multiple_of(j * TK, TK), TK)
            mask = _softmax_mask(mode, *_positions(i, j, TQ, TK)) if masked or mode == "ca" else None
            return tuple(block(g, j, carry[g], mask, ks) for g in range(G))

        init = (jnp.full((TQ, 1), NEG, F32), jnp.zeros((TQ, 1), F32), jnp.zeros((TQ, dv), F32))
        lo, n, per = _key_blocks(mode, i, TQ, TK)
        if mode == "ca":
            out = lax.fori_loop(lo, lo + per, lambda j, c: step(j, c, True), (init,) * G)
        else:
            out = _walk(lo, n, per, step, (init,) * G)
        for g, (m, l, acc) in enumerate(out):
            o_ref[:, g * dv:(g + 1) * dv] = (acc / l).astype(BF16)
            lse_ref[:, g * 128:(g + 1) * 128] = jnp.broadcast_to(m + jnp.log(l), (TQ, 128))

    in_specs = [q.rows(TQ), k.full(S), v.full(S)]
    ins = [q.arr, k.arr, v.arr]
    if bias is not None:
        in_specs.append(pl.BlockSpec((G, 3, TK, TK), lambda p, i: (p, 0, 0, 0)))
        ins.append(bias)
    return pl.pallas_call(
        body, name=name, grid=(HEADS // G, S // TQ), in_specs=in_specs,
        out_specs=[pl.BlockSpec((TQ, G * dv), lambda p, i: (i, p)), pl.BlockSpec((TQ, G * 128), lambda p, i: (i, p))],
        out_shape=[jax.ShapeDtypeStruct((S, HEADS * dv), BF16), jax.ShapeDtypeStruct((S, HEADS * 128), F32)],
        compiler_params=_params("parallel", "parallel"),
    )(*ins)


def softmax_attn_bwd(name, mode, q, k, v, o, do, lse, scale, bias=None):
    S = q.arr.shape[0]
    TQ, TK = _att_tiles(mode, S)
    G, dqk, dv = ATT_G, q.w, v.w

    def body(*refs):
        if bias is not None:
            q_ref, k_ref, v_ref, o_ref, do_ref, lse_ref, b_ref, dq_ref, dk_ref, dv_ref, db_ref = refs
        else:
            q_ref, k_ref, v_ref, o_ref, do_ref, lse_ref, dq_ref, dk_ref, dv_ref = refs
        i = pl.program_id(1)

        @pl.when(i == 0)
        def _():
            dk_ref[...] = jnp.zeros_like(dk_ref)
            dv_ref[...] = jnp.zeros_like(dv_ref)
            if bias is not None:
                db_ref[...] = jnp.zeros_like(db_ref)

        qs = [q_ref[:, q.lanes(g)] for g in range(G)]
        dos = [do_ref[:, do.lanes(g)] for g in range(G)]
        lses = [lse_ref[:, g * 128:g * 128 + 1] for g in range(G)]
        deltas = [jnp.sum(dos[g].astype(F32) * o_ref[:, o.lanes(g)].astype(F32), axis=-1, keepdims=True)
                  for g in range(G)]

        def block(g, j, dq, mask, ks):
            kb, vb = k_ref[ks, k.lanes(g)], v_ref[ks, v.lanes(g)]
            s = _dot_nt(qs[g], kb) * scale
            if bias is not None:
                slot = jnp.minimum(i - j, 2)
                s = s + b_ref[g, slot]
            p = jnp.exp(s - lses[g])
            if mask is not None:
                p = jnp.where(mask, p, 0.0)
            ds = p * (_dot_nt(dos[g], vb) - deltas[g])
            if bias is not None:
                db_ref[g, slot] += ds
            dsb = (ds * scale).astype(BF16)
            dk_ref[ks, g * dqk:(g + 1) * dqk] += _dot_tn(dsb, qs[g])
            dv_ref[ks, g * dv:(g + 1) * dv] += _dot_tn(p.astype(BF16), dos[g])
            return dq + _dot(dsb, kb)

        def step(j, carry, masked):
            ks = pl.ds(pl.multiple_of(j * TK, TK), TK)
            mask = _softmax_mask(mode, *_positions(i, j, TQ, TK)) if masked or mode == "ca" else None
            return tuple(block(g, j, carry[g], mask, ks) for g in range(G))

        init = (jnp.zeros((TQ, dqk), F32),) * G
        lo, n, per = _key_blocks(mode, i, TQ, TK)
        if mode == "ca":
            out = lax.fori_loop(lo, lo + per, lambda j, c: step(j, c, True), init)
        else:
            out = _walk(lo, n, per, step, init)
        for g in range(G):
            dq_ref[:, g * dqk:(g + 1) * dqk] = out[g]

    in_specs = [q.rows(TQ), k.full(S), v.full(S), o.rows(TQ), do.rows(TQ),
                pl.BlockSpec((TQ, G * 128), lambda p, i: (i, p))]
    ins = [q.arr, k.arr, v.arr, o.arr, do.arr, lse]
    out_specs = [pl.BlockSpec((TQ, G * dqk), lambda p, i: (i, p)), pl.BlockSpec((S, G * dqk), lambda p, i: (0, p)),
                 pl.BlockSpec((S, G * dv), lambda p, i: (0, p))]
    out_shape = [jax.ShapeDtypeStruct((S, HEADS * dqk), F32), jax.ShapeDtypeStruct((S, HEADS * dqk), F32),
                 jax.ShapeDtypeStruct((S, HEADS * dv), F32)]
    if bias is not None:
        bspec = pl.BlockSpec((G, 3, TK, TK), lambda p, i: (p, 0, 0, 0))
        in_specs.append(bspec)
        ins.append(bias)
        out_specs.append(bspec)
        out_shape.append(jax.ShapeDtypeStruct(bias.shape, F32))
    return pl.pallas_call(
        body, name=name, grid=(HEADS // G, S // TQ), in_specs=in_specs, out_specs=out_specs, out_shape=out_shape,
        compiler_params=_params("parallel", "arbitrary"),
    )(*ins)


def _split2(x):
    hi = x.astype(BF16)
    return hi, (x - hi.astype(F32)).astype(BF16)


def _split3(x):
    hi = x.astype(BF16)
    r = x - hi.astype(F32)
    mid = r.astype(BF16)
    return hi, mid, (r - mid.astype(F32)).astype(BF16)


def _stick_block(qb, kb, strict, scale):
    z = _dot_nt(qb, kb) * scale
    sp = jnp.log(1.0 + jnp.exp(-jnp.abs(z)))
    lb = jnp.minimum(z, 0.0) - sp
    l1m = jnp.minimum(-z, 0.0) - sp
    if strict is not None:
        l1m = jnp.where(strict, l1m, 0.0)
    return z, lb, l1m


def _strict_mask(i, j, TQ, TK):
    row, col = _positions(i, j, TQ, TK)
    return col < row


def _tri(T, inclusive):
    r = lax.broadcasted_iota(jnp.int32, (T, T), 0)
    c = lax.broadcasted_iota(jnp.int32, (T, T), 1)
    return ((r >= c) if inclusive else (r > c)).astype(BF16)


def _suffix(parts, tri):
    out = _dot(parts[0], tri)
    for p in parts[1:]:
        out = out + _dot(p, tri)
    return out


def stick_attn_fwd(name, q, k, v, scale):
    S = q.arr.shape[0]
    TQ, TK = _att_tiles("sb", S)
    G, dv = ATT_G, v.w

    def body(q_ref, k_ref, v_ref, o_ref):
        i = pl.program_id(1)
        qs = [q_ref[:, q.lanes(g)] for g in range(G)]
        tri = _tri(TK, False)

        def block(g, carry, strict, ks):
            right, acc = carry
            z, lb, l1m = _stick_block(qs[g], k_ref[ks, k.lanes(g)], strict, scale)
            a = jnp.exp(lb + _suffix(_split2(l1m), tri) + right)
            if strict is not None:
                a = jnp.where(strict, a, 0.0)
            acc = acc + _dot(a.astype(BF16), v_ref[ks, v.lanes(g)])
            return right + jnp.sum(l1m, axis=-1, keepdims=True), acc

        def step(j, carry, masked):
            ks = pl.ds(pl.multiple_of(j * TK, TK), TK)
            strict = _strict_mask(i, j, TQ, TK) if masked else None
            return tuple(block(g, carry[g], strict, ks) for g in range(G))

        init = (jnp.zeros((TQ, 1), F32), jnp.zeros((TQ, dv), F32))
        lo, n, per = _key_blocks("sb", i, TQ, TK)
        out = _walk(lo, n, per, step, (init,) * G, descending=True)
        for g in range(G):
            o_ref[:, g * dv:(g + 1) * dv] = out[g][1].astype(BF16)

    return pl.pallas_call(
        body, name=name, grid=(HEADS // G, S // TQ), in_specs=[q.rows(TQ), k.full(S), v.full(S)],
        out_specs=pl.BlockSpec((TQ, G * dv), lambda p, i: (i, p)),
        out_shape=jax.ShapeDtypeStruct((S, HEADS * dv), BF16), compiler_params=_params("parallel", "parallel"),
    )(q.arr, k.arr, v.arr)


def stick_attn_bwd(name, q, k, v, do, scale):
    S = q.arr.shape[0]
    TQ, TK = _att_tiles("sb", S)
    G, dqk, dv = ATT_G, q.w, v.w

    def body(q_ref, k_ref, v_ref, do_ref, dq_ref, dk_ref, dv_ref):
        i = pl.program_id(1)

        @pl.when(i == 0)
        def _():
            dk_ref[...] = jnp.zeros_like(dk_ref)
            dv_ref[...] = jnp.zeros_like(dv_ref)

        qs = [q_ref[:, q.lanes(g)] for g in range(G)]
        dos = [do_ref[:, do.lanes(g)] for g in range(G)]
        tri = _tri(TK, False)
        tri_inc = _tri(TK, True)

        def block(g, right, strict, ks):
            kb = k_ref[ks, k.lanes(g)]
            z, lb, l1m = _stick_block(qs[g], kb, strict, scale)
            a = jnp.exp(lb + _suffix(_split2(l1m), tri) + right)
            if strict is not None:
                a = jnp.where(strict, a, 0.0)
            return kb, z, l1m, a, a * _dot_nt(dos[g], v_ref[ks, v.lanes(g)])

        def total(j, carry, masked):
            ks = pl.ds(pl.multiple_of(j * TK, TK), TK)
            strict = _strict_mask(i, j, TQ, TK) if masked else None
            out = []
            for g in range(G):
                right, gtot = carry[g]
                _, _, l1m, _, gg = block(g, right, strict, ks)
                out.append((right + jnp.sum(l1m, axis=-1, keepdims=True), gtot + jnp.sum(gg, axis=-1, keepdims=True)))
            return tuple(out)

        zero = jnp.zeros((TQ, 1), F32)
        lo, n, per = _key_blocks("sb", i, TQ, TK)
        gtots = [c[1] for c in _walk(lo, n, per, total, ((zero, zero),) * G, descending=True)]

        def step(j, carry, masked):
            ks = pl.ds(pl.multiple_of(j * TK, TK), TK)
            strict = _strict_mask(i, j, TQ, TK) if masked else None
            out = []
            for g in range(G):
                right, gright, dq = carry[g]
                kb, z, l1m, a, gg = block(g, right, strict, ks)
                c = gtots[g] - (_suffix(_split3(gg), tri_inc) + gright)
                sig = 1.0 / (1.0 + jnp.exp(-z))
                dz = gg * (1.0 - sig) - c * sig
                if strict is not None:
                    dz = jnp.where(strict, dz, 0.0)
                dzb = (dz * scale).astype(BF16)
                dk_ref[ks, g * dqk:(g + 1) * dqk] += _dot_tn(dzb, qs[g])
                dv_ref[ks, g * dv:(g + 1) * dv] += _dot_tn(a.astype(BF16), dos[g])
                out.append((right + jnp.sum(l1m, axis=-1, keepdims=True),
                            gright + jnp.sum(gg, axis=-1, keepdims=True), dq + _dot(dzb, kb)))
            return tuple(out)

        out = _walk(lo, n, per, step, ((zero, zero, jnp.zeros((TQ, dqk), F32)),) * G, descending=True)
        for g in range(G):
            dq_ref[:, g * dqk:(g + 1) * dqk] = out[g][2]

    return pl.pallas_call(
        body, name=name, grid=(HEADS // G, S // TQ), in_specs=[q.rows(TQ), k.full(S), v.full(S), do.rows(TQ)],
        out_specs=[pl.BlockSpec((TQ, G * dqk), lambda p, i: (i, p)), pl.BlockSpec((S, G * dqk), lambda p, i: (0, p)),
                   pl.BlockSpec((S, G * dv), lambda p, i: (0, p))],
        out_shape=[jax.ShapeDtypeStruct((S, HEADS * dqk), F32), jax.ShapeDtypeStruct((S, HEADS * dqk), F32),
                   jax.ShapeDtypeStruct((S, HEADS * dv), F32)],
        compiler_params=_params("parallel", "arbitrary"),
    )(q.arr, k.arr, v.arr, do.arr)


def _skew(x, back):
    T = x.shape[0]
    rows = lax.broadcasted_iota(jnp.int32, (T, T), 0)
    for b in range(T.bit_length() - 1):
        shift = T - (1 << b) if back else 1 << b
        x = jnp.where(((rows >> b) & 1) == 1, pltpu.roll(x, shift, 1), x)
    return x


def _table_rows(table):
    t = jnp.pad(table.T, ((0, 0), (0, 2 * REL_CLIP_LEFT - REL_TABLE)))
    return t.reshape(table.shape[1], 2, REL_CLIP_LEFT)


def rel_bias_blocks(name, table, T):
    assert T == 2 * REL_CLIP_LEFT, "the base rows below are laid out for blocks of 256"

    def body(t_ref, o_ref):
        low, high = t_ref[0:1, :], t_ref[1:2, :]
        first = jnp.broadcast_to(t_ref[0:1, 0:1], (1, REL_CLIP_LEFT))
        qq = lax.broadcasted_iota(jnp.int32, (T, T), 0)
        kk = lax.broadcasted_iota(jnp.int32, (T, T), 1)

        def rolled(row):
            return _skew(jnp.broadcast_to(row, (T, T)), False)

        far = jnp.concatenate([first, low], axis=1)
        near = jnp.concatenate([high, jnp.zeros_like(high)], axis=1)
        o_ref[0] = jnp.where(kk >= qq, rolled(near), rolled(far))
        o_ref[1] = jnp.where(kk >= qq, rolled(far), jnp.broadcast_to(t_ref[0:1, 0:1], (T, T)))
        o_ref[2] = jnp.broadcast_to(t_ref[0:1, 0:1], (T, T))

    return pl.pallas_call(
        body, name=name, grid=(HEADS,), in_specs=[pl.BlockSpec((None, 2, REL_CLIP_LEFT), lambda h: (h, 0, 0))],
        out_specs=pl.BlockSpec((None, 3, T, T), lambda h: (h, 0, 0, 0)),
        out_shape=jax.ShapeDtypeStruct((HEADS, 3, T, T), F32), compiler_params=_params("parallel"),
    )(_table_rows(table))


def rel_bias_grad(name, dbias):
    T = dbias.shape[-1]
    L = REL_CLIP_LEFT
    assert T == 2 * L

    def body(d_ref, o_ref):
        qq = lax.broadcasted_iota(jnp.int32, (T, T), 0)
        ll = lax.broadcasted_iota(jnp.int32, (T, T), 1)
        wrapped = ll + qq >= T

        def columns(d):
            x = _skew(d_ref[d], True)
            return (jnp.sum(jnp.where(wrapped, 0.0, x), axis=0, keepdims=True),
                    jnp.sum(jnp.where(wrapped, x, 0.0), axis=0, keepdims=True))

        pos0, neg0 = columns(0)
        pos1, neg1 = columns(1)
        clipped = (jnp.sum(neg0[:, :L]) + jnp.sum(pos1[:, :L]) + jnp.sum(neg1) + jnp.sum(d_ref[2]))
        lane = lax.broadcasted_iota(jnp.int32, (1, L), 1)
        low = neg0[:, L:] + pos1[:, L:]
        o_ref[...] = jnp.zeros_like(o_ref)
        o_ref[0:1, :] = jnp.where(lane == 0, low + clipped, low)
        o_ref[1:2, :] = pos0[:, :L]

    rows = pl.pallas_call(
        body, name=name, grid=(HEADS,), in_specs=[pl.BlockSpec((None, 3, T, T), lambda h: (h, 0, 0, 0))],
        out_specs=pl.BlockSpec((None, 8, L), lambda h: (h, 0, 0)), out_shape=jax.ShapeDtypeStruct((HEADS, 8, L), F32),
        compiler_params=_params("parallel"),
    )(dbias)
    return rows[:, :2, :].reshape(HEADS, 2 * L)[:, :REL_TABLE].T


HBM = pl.BlockSpec(memory_space=pl.ANY)


def _place():
    return lax.axis_index("x"), lax.axis_index("y"), lax.axis_index("c")


def all_gather(name, shards):
    n = len(shards)

    def body(*refs):
        x_refs, out_refs = refs[:n], refs[n:2 * n]
        send_sems, recv_sems, local_sems = refs[2 * n:]
        x, y, c = _place()
        me, sibling = (x, y, c), (x, y, 1 - c)
        chips = [(1 - x, y), (x, 1 - y), (1 - x, 1 - y)]

        def block(t, dev):
            return out_refs[t].at[4 * dev[0] + 2 * dev[1] + dev[2]]

        def copy(t, k, dev, to, src=None):
            return pltpu.make_async_remote_copy(
                src_ref=block(t, dev) if src is None else src, dst_ref=block(t, dev),
                send_sem=send_sems.at[t, k], recv_sem=recv_sems.at[t, k], device_id=to, device_id_type=MESH)

        mine = [pltpu.make_async_copy(x_refs[t], block(t, me), local_sems.at[t]) for t in range(n)]
        for cp in mine:
            cp.start()
        first = []
        for t in range(n):
            first.append(copy(t, 0, me, sibling, src=x_refs[t]))
            first += [copy(t, 1 + j, me, (*chip, c), src=x_refs[t]) for j, chip in enumerate(chips)]
        for cp in first:
            cp.start()
        passed = []
        for j, chip in enumerate(chips):
            for t in range(n):
                copy(t, 1 + j, (*chip, c), me).wait_recv()
                cp = copy(t, 4 + j, (*chip, c), sibling)
                cp.start()
                passed.append(cp)
        for t in range(n):
            copy(t, 0, sibling, me).wait_recv()
            for j, chip in enumerate(chips):
                copy(t, 4 + j, (*chip, 1 - c), me).wait_recv()
        for cp in first + passed:
            cp.wait_send()
        for cp in mine:
            cp.wait()

    return pl.pallas_call(
        body, name=name, in_specs=[HBM] * n, out_specs=[HBM] * n,
        out_shape=[jax.ShapeDtypeStruct((N_DEV, *s.shape), s.dtype) for s in shards],
        scratch_shapes=[pltpu.SemaphoreType.DMA((n, 7)), pltpu.SemaphoreType.DMA((n, 7)), pltpu.SemaphoreType.DMA((n,))],
    )(*shards)


def exchange_sibling(name, grads):
    n = len(grads)

    def body(*refs):
        g_refs, out_refs = refs[:n], refs[n:2 * n]
        send_sems, recv_sems = refs[2 * n:]
        x, y, c = _place()
        cps = [pltpu.make_async_remote_copy(
            src_ref=g_refs[t].at[1 - c], dst_ref=out_refs[t], send_sem=send_sems.at[t], recv_sem=recv_sems.at[t],
            device_id=(x, y, 1 - c), device_id_type=MESH) for t in range(n)]
        for cp in cps:
            cp.start()
        for cp in cps:
            cp.wait()

    return pl.pallas_call(
        body, name=name, in_specs=[HBM] * n, out_specs=[HBM] * n,
        out_shape=[jax.ShapeDtypeStruct(g.shape[1:], g.dtype) for g in grads],
        scratch_shapes=[pltpu.SemaphoreType.DMA((n,)), pltpu.SemaphoreType.DMA((n,))],
    )(*grads)


def exchange_chips(name, parts):
    n = len(parts)

    def body(*refs):
        p_refs, out_refs = refs[:n], refs[n:2 * n]
        send_sems, recv_sems = refs[2 * n:]
        x, y, c = _place()
        chips = [(1 - x, y), (x, 1 - y), (1 - x, 1 - y)]
        cps = [pltpu.make_async_remote_copy(
            src_ref=p_refs[t].at[2 * chip[0] + chip[1]], dst_ref=out_refs[t].at[j],
            send_sem=send_sems.at[t, j], recv_sem=recv_sems.at[t, j], device_id=(*chip, c), device_id_type=MESH)
            for t in range(n) for j, chip in enumerate(chips)]
        for cp in cps:
            cp.start()
        for cp in cps:
            cp.wait()

    return pl.pallas_call(
        body, name=name, in_specs=[HBM] * n, out_specs=[HBM] * n,
        out_shape=[jax.ShapeDtypeStruct((3, *p.shape[1:]), p.dtype) for p in parts],
        scratch_shapes=[pltpu.SemaphoreType.DMA((n, 3)), pltpu.SemaphoreType.DMA((n, 3))],
    )(*parts)


def _as_rows(shape):
    return (int(np.prod(shape[:-1])), shape[-1])


ELEMENTWISE_BLOCK = 256 * 1024


def _row_tile(rows, cols):
    return _tile(rows, max(128, ELEMENTWISE_BLOCK // cols // 128 * 128))


def add_sibling(name, grad, recv, core):
    rows, cols = _as_rows(grad.shape[2:])
    tr = _row_tile(rows, cols)

    def body(c_ref, g_ref, r_ref, o_ref):
        o_ref[...] = (g_ref[...].astype(F32) + r_ref[...].astype(F32)).astype(BF16)

    blk = pl.BlockSpec((None, tr, cols), lambda k, i, c_ref: (k, i, 0))
    return pl.pallas_call(
        body, name=name,
        grid_spec=pltpu.PrefetchScalarGridSpec(
            num_scalar_prefetch=1, grid=(4, rows // tr),
            in_specs=[pl.BlockSpec((None, None, tr, cols), lambda k, i, c_ref: (c_ref[0], k, i, 0)), blk],
            out_specs=blk),
        out_shape=jax.ShapeDtypeStruct((4, rows, cols), BF16), compiler_params=_params("parallel", "parallel"),
    )(core, grad.reshape(2, 4, rows, cols), recv.reshape(4, rows, cols)).reshape(recv.shape)


def sum_chips(name, part, recv, chip):
    shape = part.shape[1:]
    rows, cols = _as_rows(shape)
    tr = _row_tile(rows, cols)

    def body(c_ref, p_ref, r_ref, o_ref):
        o_ref[...] = (p_ref[...].astype(F32) + r_ref[0].astype(F32) + r_ref[1].astype(F32) + r_ref[2].astype(F32))

    return pl.pallas_call(
        body, name=name,
        grid_spec=pltpu.PrefetchScalarGridSpec(
            num_scalar_prefetch=1, grid=(rows // tr,),
            in_specs=[pl.BlockSpec((None, tr, cols), lambda i, c_ref: (c_ref[0], i, 0)),
                      pl.BlockSpec((3, tr, cols), lambda i, c_ref: (0, i, 0))],
            out_specs=pl.BlockSpec((tr, cols), lambda i, c_ref: (i, 0))),
        out_shape=jax.ShapeDtypeStruct((rows, cols), F32), compiler_params=_params("parallel"),
    )(chip, part.reshape(4, rows, cols), recv.reshape(3, rows, cols)).reshape(shape)


def sum_devices(name, gathered):
    _, rows, cols = gathered.shape

    def body(g_ref, o_ref):
        acc = g_ref[0]
        for d in range(1, N_DEV):
            acc = acc + g_ref[d]
        o_ref[...] = acc

    return pl.pallas_call(body, name=name, out_shape=jax.ShapeDtypeStruct((rows, cols), F32))(gathered)


def adamw(name, w, g, m, v):
    shape = w.shape
    rows, cols = _as_rows(shape)
    tr = _row_tile(rows, cols) if rows % 8 == 0 else rows
    c1 = 1.0 / (1.0 - ADAM_B1 ** ADAM_STEP)
    c2 = 1.0 / (1.0 - ADAM_B2 ** ADAM_STEP)

    def body(w_ref, g_ref, m_ref, v_ref, d_ref, mo_ref, vo_ref):
        g_ = g_ref[...]
        m_ = ADAM_B1 * m_ref[...] + (1.0 - ADAM_B1) * g_
        v_ = ADAM_B2 * v_ref[...] + (1.0 - ADAM_B2) * (g_ * g_)
        d_ref[...] = -ADAM_LR * ((m_ * c1) / (jnp.sqrt(v_ * c2) + ADAM_EPS) + ADAM_WD * w_ref[...])
        mo_ref[...] = m_
        vo_ref[...] = v_

    blk = pl.BlockSpec((tr, cols), lambda i: (i, 0))
    outs = pl.pallas_call(
        body, name=name, grid=(rows // tr,), in_specs=[blk] * 4, out_specs=[blk] * 3,
        out_shape=[jax.ShapeDtypeStruct((rows, cols), F32)] * 3, compiler_params=_params("parallel"),
    )(*[a.reshape(rows, cols) for a in (w, g, m, v)])
    return [o.reshape(shape) for o in outs]


def _spread_rope(r):
    z = jnp.zeros_like(r[..., :32])
    return jnp.concatenate([r[..., :32], z, r[..., 32:], z], -1)


def _gather_rope(r):
    return jnp.concatenate([r[..., :32], r[..., 64:96]], -1)


def pad_w_uq(w):
    w = w.reshape(w.shape[0], -1, MLA_NOPE + MLA_ROPE)
    return jnp.concatenate([w[..., :MLA_NOPE], _spread_rope(w[..., MLA_NOPE:])], -1).reshape(w.shape[0], -1)


def unpad_w_uq(g):
    g = g.reshape(g.shape[0], -1, 2 * MLA_NOPE)
    return jnp.concatenate([g[..., :MLA_NOPE], _gather_rope(g[..., MLA_NOPE:])], -1).reshape(g.shape[0], -1)


def pad_w_down(w):
    lat = MLA_Q_LORA + MLA_KV_LORA
    return jnp.concatenate([w[:, :lat], _spread_rope(w[:, lat:])], -1)


def unpad_w_down(g):
    lat = MLA_Q_LORA + MLA_KV_LORA
    return jnp.concatenate([g[:, :lat], _gather_rope(g[:, lat:])], -1)


def _heads(arr, width, first=0, off=0, w=None):
    return HeadCols(arr, width, lambda p: first // ATT_G + p, off, w)


def mla_forward(h16, w, gq, gkv, tables):
    cos, sin = tables
    down = mm_nn("mla_down", h16, w["down"], [F32])[0]
    cq, ckv = rms_fwd("mla_rms", down, gq, gkv)
    q = mm_nn("mla_uq", cq, w["uq"], [F32])[0]
    kv = mm_nn("mla_ukv", ckv, w["ukv"], [BF16])[0]
    qr, kp = mla_prep_fwd("mla_prep", q, kv, down, cos, sin)
    scale = (MLA_NOPE + MLA_ROPE) ** -0.5
    o, lse = softmax_attn_fwd("mla_attn", "mla", _heads(qr, 256), _heads(kp, 256), _heads(kv, 256, off=128, w=128), scale)
    m = mm_nn("mla_wo", o, w["wo"], [F32])[0]
    return m, (down, cq, ckv, qr, kp, kv, o, lse)


def mla_backward(du16, h16, saved, w, gq, gkv, tables):
    cos, sin = tables
    down, cq, ckv, qr, kp, kv, o, lse = saved
    scale = (MLA_NOPE + MLA_ROPE) ** -0.5
    g = {"wo": mm_tn("mla_dwo", o, du16, "row", w["wo"].R, w["wo"].C)}
    do = mm_nt("mla_do", du16, w["wo"], BF16)
    dq, dk, dv = softmax_attn_bwd("mla_attn_bwd", "mla", _heads(qr, 256), _heads(kp, 256), _heads(kv, 256, off=128, w=128),
                                  _heads(o, 128), _heads(do, 128), lse, scale)
    dq16, dkv16, dkr = mla_prep_bwd("mla_prep_bwd", dq, dk, dv, cos, sin)
    g["uq"] = mm_tn("mla_duq", cq, dq16, "col", w["uq"].R, w["uq"].C)
    dcq = mm_nt("mla_dcq", dq16, w["uq"], F32)
    g["ukv"] = mm_tn("mla_dukv", ckv, dkv16, "col", w["ukv"].R, w["ukv"].C)
    dckv = mm_nt("mla_dckv", dkv16, w["ukv"], F32)
    ddown, dgq, dgkv = rms_bwd("mla_rms_bwd", down, dcq, dckv, dkr, gq, gkv)
    g["down"] = mm_tn("mla_ddown", h16, ddown, "row", w["down"].R, w["down"].C)
    dh = mm_nt("mla_dh", ddown, w["down"], F32)
    return dh, g, (dgq, dgkv)


def qkv_forward(kind, h16, w, bias=None):
    qkv = mm_nn(kind + "_qkv", h16, w["qkv"], [BF16])[0]
    q, k, v = _heads(qkv, 128), _heads(qkv, 128, HEADS), _heads(qkv, 128, 2 * HEADS)
    scale = HEAD_DIM ** -0.5
    if kind == "sb":
        o, lse = stick_attn_fwd("sb_attn", q, k, v, scale), None
    else:
        o, lse = softmax_attn_fwd("ca_attn", "ca", q, k, v, scale, bias)
    m = mm_nn(kind + "_wo", o, w["wo"], [F32])[0]
    return m, (qkv, o, lse)


def qkv_backward(kind, du16, h16, saved, w, bias=None):
    qkv, o, lse = saved
    q, k, v = _heads(qkv, 128), _heads(qkv, 128, HEADS), _heads(qkv, 128, 2 * HEADS)
    scale = HEAD_DIM ** -0.5
    g = {"wo": mm_tn(kind + "_dwo", o, du16, "row", w["wo"].R, w["wo"].C)}
    do = mm_nt(kind + "_do", du16, w["wo"], BF16)
    dbias = None
    if kind == "sb":
        dq, dk, dv = stick_attn_bwd("sb_attn_bwd", q, k, v, _heads(do, 128), scale)
    else:
        dq, dk, dv, dbias = softmax_attn_bwd("ca_attn_bwd", "ca", q, k, v, _heads(o, 128), _heads(do, 128), lse,
                                             scale, bias)
    dqkv = jnp.concatenate([dq, dk, dv], axis=1).astype(BF16)
    g["qkv"] = mm_tn(kind + "_dqkv", h16, dqkv, "col", w["qkv"].R, w["qkv"].C)
    dh = mm_nt(kind + "_dh", dqkv, w["qkv"], F32)
    return dh, g, dbias


def mlp_forward(h16, w):
    a, z = mm_nn("ffn_in", h16, w["w_in"], [F32, BF16], epilogue=_relu2_epilogue)
    f = mm_nn("ffn_out", z, w["w_out"], [F32])[0]
    return f, (a, z)


def mlp_backward(du16, h16, saved, w):
    a, z = saved
    da = mm_nt("ffn_da", du16, w["w_out"], BF16, epilogue=_mulrelu_epilogue, extra=a)
    g = {"w_out": mm_tn("ffn_dwout", z, du16, "row", w["w_out"].R, w["w_out"].C)}
    dh = mm_nt("ffn_dh", da, w["w_in"], F32)
    g["w_in"] = mm_tn("ffn_dwin", h16, da, "col", w["w_in"].R, w["w_in"].C)
    return dh, g


WEIGHTS = ("ln_mix_g", "ln_mix_b", "ln_ffn_g", "ln_ffn_b", "ffn_w_in", "ffn_w_out", "mla_w_down", "mla_q_norm_g",
           "mla_w_uq", "mla_kv_norm_g", "mla_w_ukv", "mla_w_o", "sb_w_qkv", "sb_w_o", "ca_w_qkv", "ca_rel_bias",
           "ca_w_o")
MIXERS = ("mla", "sb", "ca")
LAYER_WEIGHTS = {
    "mla": (("down", "mla_w_down", "row"), ("uq", "mla_w_uq", "col"), ("ukv", "mla_w_ukv", "col"),
            ("wo", "mla_w_o", "row")),
    "sb": (("qkv", "sb_w_qkv", "col"), ("wo", "sb_w_o", "row")),
    "ca": (("qkv", "ca_w_qkv", "col"), ("wo", "ca_w_o", "row")),
    "ffn": (("w_in", "ffn_w_in", "col"), ("w_out", "ffn_w_out", "row")),
}
PAD = {"mla_w_down": pad_w_down, "mla_w_uq": pad_w_uq}
UNPAD = {"mla_w_down": unpad_w_down, "mla_w_uq": unpad_w_uq}


def _pack_rows(vectors):
    flat = jnp.concatenate([v.reshape(-1) for v in vectors])
    n = flat.shape[0]
    rows = -(-n // 1024) * 8
    offsets = np.cumsum([0] + [int(np.prod(v.shape)) for v in vectors])
    return jnp.pad(flat, (0, rows * 128 - n)).reshape(rows, 128), offsets


def reduce_scatter(tag, grads, core, chip):
    from_sibling = exchange_sibling("rs_sibling_" + tag, grads)
    parts = [add_sibling(f"rs_add_{tag}{t}", g, r, core) for t, (g, r) in enumerate(zip(grads, from_sibling))]
    from_chips = exchange_chips("rs_chips_" + tag, parts)
    return [sum_chips(f"rs_sum_{tag}{t}", p, r, chip) for t, (p, r) in enumerate(zip(parts, from_chips))]


def kernel(x, ln_mix_g, ln_mix_b, ln_ffn_g, ln_ffn_b, ffn_w_in, ffn_w_out, mla_w_down, mla_q_norm_g, mla_w_uq, mla_kv_norm_g, mla_w_ukv, mla_w_o, sb_w_qkv, sb_w_o, ca_w_qkv, ca_rel_bias, ca_w_o, loss_target, m_ln_mix_g, m_ln_mix_b, m_ln_ffn_g, m_ln_ffn_b, m_ffn_w_in, m_ffn_w_out, m_mla_w_down, m_mla_q_norm_g, m_mla_w_uq, m_mla_kv_norm_g, m_mla_w_ukv, m_mla_w_o, m_sb_w_qkv, m_sb_w_o, m_ca_w_qkv, m_ca_rel_bias, m_ca_w_o, v_ln_mix_g, v_ln_mix_b, v_ln_ffn_g, v_ln_ffn_b, v_ffn_w_in, v_ffn_w_out, v_mla_w_down, v_mla_q_norm_g, v_mla_w_uq, v_mla_kv_norm_g, v_mla_w_ukv, v_mla_w_o, v_sb_w_qkv, v_sb_w_o, v_ca_w_qkv, v_ca_rel_bias, v_ca_w_o):
    w = dict(zip(WEIGHTS, (ln_mix_g, ln_mix_b, ln_ffn_g, ln_ffn_b, ffn_w_in, ffn_w_out, mla_w_down, mla_q_norm_g,
                           mla_w_uq, mla_kv_norm_g, mla_w_ukv, mla_w_o, sb_w_qkv, sb_w_o, ca_w_qkv, ca_rel_bias,
                           ca_w_o)))
    mom = dict(zip(WEIGHTS, (m_ln_mix_g, m_ln_mix_b, m_ln_ffn_g, m_ln_ffn_b, m_ffn_w_in, m_ffn_w_out, m_mla_w_down,
                             m_mla_q_norm_g, m_mla_w_uq, m_mla_kv_norm_g, m_mla_w_ukv, m_mla_w_o, m_sb_w_qkv,
                             m_sb_w_o, m_ca_w_qkv, m_ca_rel_bias, m_ca_w_o)))
    var = dict(zip(WEIGHTS, (v_ln_mix_g, v_ln_mix_b, v_ln_ffn_g, v_ln_ffn_b, v_ffn_w_in, v_ffn_w_out, v_mla_w_down,
                             v_mla_q_norm_g, v_mla_w_uq, v_mla_kv_norm_g, v_mla_w_ukv, v_mla_w_o, v_sb_w_qkv,
                             v_sb_w_o, v_ca_w_qkv, v_ca_rel_bias, v_ca_w_o)))
    S, D = x.shape[1], x.shape[2]
    xi, yi, ci = _place()
    core = ci.astype(jnp.int32).reshape(1)
    chip = (2 * xi + yi).astype(jnp.int32).reshape(1)
    me = 4 * xi + 2 * yi + ci
    tables = rope_tables(S)
    n_mla = mla_w_down.shape[0]
    lat = MLA_Q_LORA // N_DEV

    gains = jnp.pad(jnp.stack([mla_q_norm_g.reshape(-1), mla_kv_norm_g.reshape(-1)]), ((0, 6), (0, 128 - n_mla * lat)))
    gains = all_gather("ag_gains", [gains])[0]

    def full_gain(row, slot):
        return gains[:, row, slot * lat:(slot + 1) * lat].reshape(-1)

    def gather_layer(i):
        kind, slot = MIXERS[i % 3], i // 3
        specs = [(key, name, how, slot) for key, name, how in LAYER_WEIGHTS[kind]]
        specs += [(key, name, how, i) for key, name, how in LAYER_WEIGHTS["ffn"]]
        shards = [PAD.get(name, lambda a: a)(w[name][idx]).astype(BF16) for _, name, _, idx in specs]
        gathered = all_gather("ag_" + kind, shards)
        return {key: Weight(how, g) for (key, _, how, _), g in zip(specs, gathered)}, specs

    layers = [gather_layer(i) for i in range(DEPTH)]
    bias = rel_bias_blocks("ca_bias", ca_rel_bias[0], _att_tiles("ca", S)[1])

    h, h16 = x[0], x[0].astype(BF16)
    saved = []
    for i in range(DEPTH):
        kind, slot = MIXERS[i % 3], i // 3
        lw = layers[i][0]
        if kind == "mla":
            mix, s_mix = mla_forward(h16, lw, full_gain(0, slot), full_gain(1, slot), tables)
        else:
            mix, s_mix = qkv_forward(kind, h16, lw, bias if kind == "ca" else None)
        y, y16, xh1, rs1 = ln_fwd("ln_mix", h, mix, ln_mix_g[i], ln_mix_b[i])
        f, s_mlp = mlp_forward(y16, lw)
        y2, y2_16, xh2, rs2 = ln_fwd("ln_ffn", y, f, ln_ffn_g[i], ln_ffn_b[i])
        saved.append((h16, s_mix, xh1, rs1, y16, s_mlp, xh2, rs2))
        h, h16 = y2, y2_16
    sq, dy = loss_fwd_bwd("loss", h, loss_target[0])
    loss = 0.5 / D * lax.psum(sq[0, 0], ("x", "y", "c"))

    ga, gb = dy, None
    grads = {name: [None] * w[name].shape[0] for name in WEIGHTS}
    dbias = None
    for i in reversed(range(DEPTH)):
        kind, slot = MIXERS[i % 3], i // 3
        lw, specs = layers[i]
        h16_in, s_mix, xh1, rs1, y16, s_mlp, xh2, rs2 = saved[i]
        du, du16, grads["ln_ffn_g"][i], grads["ln_ffn_b"][i] = ln_bwd("ln_ffn_bwd", ga, gb, xh2, rs2, ln_ffn_g[i])
        dh_mlp, g_mlp = mlp_backward(du16, y16, s_mlp, lw)
        du, du16, grads["ln_mix_g"][i], grads["ln_mix_b"][i] = ln_bwd("ln_mix_bwd", du, dh_mlp, xh1, rs1, ln_mix_g[i])
        if kind == "mla":
            dh_mix, g_mix, (dgq, dgkv) = mla_backward(du16, h16_in, s_mix, lw, full_gain(0, slot), full_gain(1, slot),
                                                      tables)
            grads["mla_q_norm_g"][slot], grads["mla_kv_norm_g"][slot] = dgq, dgkv
        else:
            dh_mix, g_mix, db = qkv_backward(kind, du16, h16_in, s_mix, lw, bias if kind == "ca" else None)
            dbias = db if kind == "ca" else dbias
        ga, gb = du, dh_mix
        full = {**g_mix, **g_mlp}
        reduced = reduce_scatter(kind, [full[key] for key, _, _, _ in specs], core, chip)
        for (_, name, _, idx), g in zip(specs, reduced):
            grads[name][idx] = UNPAD.get(name, lambda a: a)(g)
    grad_x = axpy("grad_x", ga, gb)[None]
    grads["ca_rel_bias"][0] = rel_bias_grad("ca_bias_grad", dbias)

    small = ("ln_mix_g", "ln_mix_b", "ln_ffn_g", "ln_ffn_b", "ca_rel_bias", "mla_q_norm_g", "mla_kv_norm_g")
    packed, offsets = _pack_rows([g for name in small for g in grads[name]])
    total = sum_devices("sum_small", all_gather("ag_small", [packed])[0]).reshape(-1)
    pos = 0
    for name in small:
        for idx, g in enumerate(grads[name]):
            full = total[offsets[pos]:offsets[pos + 1]]
            pos += 1
            if name in ("mla_q_norm_g", "mla_kv_norm_g"):
                full = lax.dynamic_slice(full, (me * lat,), (lat,))
            grads[name][idx] = full.reshape(w[name].shape[1:])

    g_out, d_out, m_out, v_out = [], [], [], []
    for name in WEIGHTS:
        g = jnp.stack(grads[name])
        delta, new_m, new_v = adamw("adamw_" + name, w[name], g, mom[name], var[name])
        g_out.append(g)
        d_out.append(delta)
        m_out.append(new_m)
        v_out.append(new_v)
    return (loss, grad_x, *g_out, *d_out, *m_out, *v_out)
```

```python
import functools
import math

import numpy as np
import jax
import jax.numpy as jnp
from jax import lax
from jax.experimental import pallas as pl
from jax.experimental.pallas import tpu as pltpu

F32 = jnp.float32
BF16 = jnp.bfloat16
MESH = pl.DeviceIdType.MESH
N_DEV = 8

DEPTH = 4
CHUNK = 64
CHUNK_SHIFT = 6
HEADS = 16
HEAD_DIM = 128
MLA_Q_LORA = 512
MLA_KV_LORA = 512
MLA_NOPE = 128
MLA_ROPE = 64
ROPE_THETA = 10000.0
CA_LEFT_CHUNKS = 8
REL_CLIP_LEFT = 128
REL_TABLE = REL_CLIP_LEFT + CHUNK
LN_EPS = 1e-5
RMS_EPS = 1e-6
ALPHA = (2.0 * DEPTH) ** 0.25
NEG = -1e30
ADAM_LR = 0.001
ADAM_B1 = 0.9
ADAM_B2 = 0.999
ADAM_EPS = 1e-08
ADAM_WD = 0.01
ADAM_STEP = 10

V7X_VMEM_BYTES = 64 * 1024 * 1024
VMEM_LIMIT = V7X_VMEM_BYTES - 8 * 1024 * 1024
ATT_TQ = 512
ATT_TK = 256
ATT_G = 2


def _params(*sem):
    return pltpu.CompilerParams(dimension_semantics=sem if sem else None, vmem_limit_bytes=VMEM_LIMIT)


HBM = pl.BlockSpec(memory_space=pl.ANY)


class Job:
    def __init__(self, name, want, operands, out_shape, n_copies, copies, done, aliases=None):
        self.name, self.want, self.operands, self.out_shape = name, want, list(operands), list(out_shape)
        self.n_copies, self.copies, self.done, self.aliases = n_copies, copies, done, dict(aliases or {})

    def sems(self):
        return [pltpu.SemaphoreType.DMA((self.n_copies,)), pltpu.SemaphoreType.DMA((self.n_copies,))]


class Scheduler:
    def __init__(self):
        self.pending = []

    def post(self, job):
        self.pending.append(job)

    def take(self, name):
        for job in self.pending:
            if job.want is not None and job.want in name:
                self.pending.remove(job)
                return job
        return None

    def flush(self):
        while self.pending:
            job = self.pending.pop(0)
            n_in, n_out = len(job.operands), len(job.out_shape)

            def body(*refs, job=job, n_in=n_in, n_out=n_out):
                cps = job.copies(refs[:n_in], refs[n_in:n_in + n_out], refs[-2], refs[-1])
                for cp in cps:
                    cp.start()
                for cp in cps:
                    cp.wait()

            outs = pl.pallas_call(
                body, name=job.name, in_specs=[HBM] * n_in, out_specs=[HBM] * n_out, out_shape=job.out_shape,
                scratch_shapes=job.sems(), input_output_aliases=job.aliases)(*job.operands)
            job.done(list(outs))


SCHED = Scheduler()


def _call(body, operands, *, name, grid, in_specs, out_specs, out_shape, scratch_shapes=(), semantics):
    job = SCHED.take(name)
    if job is None:
        return list(pl.pallas_call(
            body, name=name, grid=grid, in_specs=list(in_specs), out_specs=list(out_specs), out_shape=list(out_shape),
            scratch_shapes=list(scratch_shapes), compiler_params=_params(*semantics))(*operands))
    n_in, n_out, n_scr = len(operands), len(out_shape), len(scratch_shapes)
    j_in, j_out = len(job.operands), len(job.out_shape)

    def carrying(*refs):
        a, b = n_in, n_in + j_in
        c, d = b + n_out, b + n_out + j_out
        job_refs = (refs[a:b], refs[c:d], refs[-2], refs[-1])
        ids = [pl.program_id(k) for k in range(len(grid))]
        first = functools.reduce(jnp.logical_and, [i == 0 for i in ids])
        last = functools.reduce(jnp.logical_and, [i == g - 1 for i, g in zip(ids, grid)])

        @pl.when(first)
        def _():
            for cp in job.copies(*job_refs):
                cp.start()

        body(*refs[:a], *refs[b:c], *refs[d:d + n_scr])

        @pl.when(last)
        def _():
            for cp in job.copies(*job_refs):
                cp.wait()

    outs = pl.pallas_call(
        carrying, name=name + "_carry", grid=grid, in_specs=list(in_specs) + [HBM] * j_in,
        out_specs=list(out_specs) + [HBM] * j_out, out_shape=list(out_shape) + job.out_shape,
        scratch_shapes=list(scratch_shapes) + job.sems(),
        input_output_aliases={n_in + i: n_out + o for i, o in job.aliases.items()},
        compiler_params=_params(*(["arbitrary"] * len(grid))))(*operands, *job.operands)
    job.done(list(outs[n_out:]))
    return list(outs[:n_out])


def _matmul(name, a, b, *, contract, grid, a_spec, b_spec, o_specs, out_shape, acc_shape,
            epilogue=None, extra=(), extra_specs=()):
    nk = grid[2]
    n_extra = len(extra)
    n_out = len(out_shape)

    def finish(acc, e_refs, o_refs):
        outs = epilogue(acc, *[e[...] for e in e_refs]) if epilogue else (acc,)
        for o_ref, val in zip(o_refs, outs):
            o_ref[...] = val.astype(o_ref.dtype)

    def product(a_ref, b_ref):
        return lax.dot_general(a_ref[...], b_ref[...], (contract, ((), ())), preferred_element_type=F32)

    def body_single(*refs):
        finish(product(refs[0], refs[1]), refs[2:2 + n_extra], refs[2 + n_extra:2 + n_extra + n_out])

    def body(*refs):
        a_ref, b_ref = refs[0], refs[1]
        acc_ref = refs[-1]
        k = pl.program_id(2)

        @pl.when(k == 0)
        def _():
            acc_ref[...] = jnp.zeros_like(acc_ref)

        acc_ref[...] += product(a_ref, b_ref)

        @pl.when(k == nk - 1)
        def _():
            finish(acc_ref[...], refs[2:2 + n_extra], refs[2 + n_extra:2 + n_extra + n_out])

    return _call(
        body_single if nk == 1 else body, [a, b, *extra], name=name, grid=grid,
        in_specs=[a_spec, b_spec, *extra_specs], out_specs=o_specs, out_shape=out_shape,
        scratch_shapes=[] if nk == 1 else [pltpu.VMEM(acc_shape, F32)],
        semantics=("parallel", "parallel", "arbitrary"))


MATMUL_BLOCK_BYTES = 40 * 1024 * 1024
MAX_TK = 2048
MULTI_TK = 512


def _fit_tn(n, tm, tk, nk, out_bytes):
    cands = sorted({n} | {t for t in range(128, n, 128) if n % t == 0}, reverse=True)
    for tn in cands:
        need = 2 * 2 * (tm * tk + tk * tn) + 2 * tm * tn * out_bytes + (tm * tn * 4 if nk > 1 else 0) + tm * tn * 4
        if need <= MATMUL_BLOCK_BYTES:
            return tn
    return cands[-1]


def _itemsize(dtypes):
    return sum(jnp.dtype(d).itemsize for d in dtypes)


def _tile(n, pref):
    if n <= pref:
        return n
    t = pref
    while t >= 128:
        if n % t == 0 and t % 128 == 0:
            return t
        t -= 128
    return n


class Weight:
    def __init__(self, kind, arr):
        self.kind = kind
        self.arr = arr
        self.R, self.C = arr.shape[1], arr.shape[2]

    @property
    def two_d(self):
        return self.arr.reshape(N_DEV * self.R, self.C)


def mm_nn(name, a, w, out_dtypes, epilogue=None):
    M, K = a.shape
    tm = M
    tk = K if K <= MAX_TK else _tile(K, MULTI_TK)
    nk = K // tk
    if w.kind == "row":
        b = w.two_d
        N = w.C
        tn = _fit_tn(N, tm, tk, nk, _itemsize(out_dtypes))
        b_spec = pl.BlockSpec((tk, tn), lambda i, j, k: (k, j))
    else:
        b = w.arr
        N = N_DEV * w.C
        tn = _fit_tn(w.C, tm, tk, nk, _itemsize(out_dtypes))
        per = w.C // tn
        b_spec = pl.BlockSpec((None, tk, tn), lambda i, j, k: (j // per, k, j % per))
    grid = (M // tm, N // tn, nk)
    return _matmul(
        name, a, b, contract=((1,), (0,)), grid=grid,
        a_spec=pl.BlockSpec((tm, tk), lambda i, j, k: (i, k)), b_spec=b_spec,
        o_specs=[pl.BlockSpec((tm, tn), lambda i, j, k: (i, j)) for _ in out_dtypes],
        out_shape=[jax.ShapeDtypeStruct((M, N), d) for d in out_dtypes], acc_shape=(tm, tn), epilogue=epilogue)


def mm_nt(name, dy, w, out_dtype, epilogue=None, extra=None):
    M, N = dy.shape
    tm = M
    out_bytes = jnp.dtype(out_dtype).itemsize + (0 if extra is None else extra.dtype.itemsize)
    if w.kind == "row":
        b = w.two_d
        kin = N_DEV * w.R
        tk = N if N <= MAX_TK else _tile(N, MULTI_TK)
        tn = _fit_tn(kin, tm, tk, N // tk, out_bytes)
        b_spec = pl.BlockSpec((tn, tk), lambda i, j, k: (j, k))
    else:
        b = w.arr
        kin = w.R
        tk = _tile(w.C, MULTI_TK)
        per = w.C // tk
        tn = _fit_tn(kin, tm, tk, N // tk, out_bytes)
        b_spec = pl.BlockSpec((None, tn, tk), lambda i, j, k: (k // per, j, k % per))
    grid = (M // tm, kin // tn, N // tk)
    o_spec = pl.BlockSpec((tm, tn), lambda i, j, k: (i, j))
    return _matmul(
        name, dy, b, contract=((1,), (1,)), grid=grid,
        a_spec=pl.BlockSpec((tm, tk), lambda i, j, k: (i, k)), b_spec=b_spec, o_specs=[o_spec],
        out_shape=[jax.ShapeDtypeStruct((M, kin), out_dtype)], acc_shape=(tm, tn), epilogue=epilogue,
        extra=() if extra is None else (extra,), extra_specs=() if extra is None else (o_spec,))[0]


TRANSPOSE_TILE = 512


def transpose(name, x):
    S, n = x.shape
    ts, tn = _tile(S, TRANSPOSE_TILE), _tile(n, TRANSPOSE_TILE)

    def body(x_ref, o_ref):
        o_ref[...] = x_ref[...].T

    return pl.pallas_call(
        body, name=name, grid=(S // ts, n // tn), in_specs=[pl.BlockSpec((ts, tn), lambda i, j: (i, j))],
        out_specs=pl.BlockSpec((tn, ts), lambda i, j: (j, i)), out_shape=jax.ShapeDtypeStruct((n, S), x.dtype),
        compiler_params=_params("parallel", "parallel"),
    )(x)


def mm_tn(name, x, dy, kind, R, C):
    S, kin = x.shape
    x = transpose(name + "_t", x)
    N = dy.shape[1]
    tk = S if S <= MAX_TK else _tile(S, MULTI_TK)
    nk = S // tk
    if kind == "col":
        tm = kin
        tn = _fit_tn(C, tm, tk, nk, 2)
        per = C // tn
        grid = (1, N // tn, nk)
        o_spec = pl.BlockSpec((None, None, tm, tn), lambda i, j, k: ((j // per) % 2, (j // per) // 2, 0, j % per))
    else:
        tm = R
        tn = _fit_tn(N, tm, tk, nk, 2)
        grid = (N_DEV, N // tn, nk)
        o_spec = pl.BlockSpec((None, None, tm, tn), lambda i, j, k: (i % 2, i // 2, 0, j))
    return _matmul(
        name, x, dy, contract=((1,), (0,)), grid=grid,
        a_spec=pl.BlockSpec((tm, tk), lambda i, j, k: (i, k)),
        b_spec=pl.BlockSpec((tk, tn), lambda i, j, k: (k, j)), o_specs=[o_spec],
        out_shape=[jax.ShapeDtypeStruct((2, 4, R, C), BF16)], acc_shape=(tm, tn))[0]


def _relu2_epilogue(acc):
    r = jnp.maximum(acc, 0.0)
    return acc, r * r


def _mulrelu_epilogue(acc, a):
    return (acc * (2.0 * jnp.maximum(a, 0.0)),)


ROW_TILE = 256


def ln_fwd(name, h, m, g, b):
    S, D = h.shape
    ts = _tile(S, ROW_TILE)

    def body(h_ref, m_ref, g_ref, b_ref, y_ref, y16_ref, xh_ref, rs_ref):
        u = ALPHA * h_ref[...] + m_ref[...]
        mu = jnp.mean(u, axis=-1, keepdims=True)
        d = u - mu
        var = jnp.mean(d * d, axis=-1, keepdims=True)
        rstd = lax.rsqrt(var + LN_EPS)
        xh = d * rstd
        y = xh * g_ref[...] + b_ref[...]
        y_ref[...] = y
        y16_ref[...] = y.astype(BF16)
        xh_ref[...] = xh
        rs_ref[...] = jnp.broadcast_to(rstd, rs_ref.shape)

    row = pl.BlockSpec((ts, D), lambda i: (i, 0))
    vec = pl.BlockSpec((1, D), lambda i: (0, 0))
    return pl.pallas_call(
        body, name=name, grid=(S // ts,), in_specs=[row, row, vec, vec],
        out_specs=[row, row, row, pl.BlockSpec((ts, 128), lambda i: (i, 0))],
        out_shape=[jax.ShapeDtypeStruct((S, D), F32), jax.ShapeDtypeStruct((S, D), BF16),
                   jax.ShapeDtypeStruct((S, D), F32), jax.ShapeDtypeStruct((S, 128), F32)],
        compiler_params=_params("parallel"),
    )(h, m, g.reshape(1, D), b.reshape(1, D))


def ln_bwd(name, ga, gb, xhat, rstd, g):
    S, D = xhat.shape
    ts = _tile(S, ROW_TILE)
    two = gb is not None

    def body(*refs):
        if two:
            ga_ref, gb_ref, xh_ref, rs_ref, g_ref, du_ref, du16_ref, dg_ref, db_ref = refs
            dy = ALPHA * ga_ref[...] + gb_ref[...]
        else:
            ga_ref, xh_ref, rs_ref, g_ref, du_ref, du16_ref, dg_ref, db_ref = refs
            dy = ga_ref[...]
        xh = xh_ref[...]

        @pl.when(pl.program_id(0) == 0)
        def _():
            dg_ref[...] = jnp.zeros_like(dg_ref)
            db_ref[...] = jnp.zeros_like(db_ref)

        dg_ref[...] += jnp.sum(dy * xh, axis=0, keepdims=True)
        db_ref[...] += jnp.sum(dy, axis=0, keepdims=True)
        dxh = dy * g_ref[...]
        m1 = jnp.mean(dxh, axis=-1, keepdims=True)
        m2 = jnp.mean(dxh * xh, axis=-1, keepdims=True)
        du = rs_ref[:, 0:1] * (dxh - m1 - xh * m2)
        du_ref[...] = du
        du16_ref[...] = du.astype(BF16)

    row = pl.BlockSpec((ts, D), lambda i: (i, 0))
    vec = pl.BlockSpec((1, D), lambda i: (0, 0))
    stat = pl.BlockSpec((ts, 128), lambda i: (i, 0))
    ins = [ga, gb, xhat, rstd, g.reshape(1, D)] if two else [ga, xhat, rstd, g.reshape(1, D)]
    in_specs = [row, row, row, stat, vec] if two else [row, row, stat, vec]
    return pl.pallas_call(
        body, name=name, grid=(S // ts,), in_specs=in_specs, out_specs=[row, row, vec, vec],
        out_shape=[jax.ShapeDtypeStruct((S, D), F32), jax.ShapeDtypeStruct((S, D), BF16),
                   jax.ShapeDtypeStruct((1, D), F32), jax.ShapeDtypeStruct((1, D), F32)],
        compiler_params=_params("arbitrary"),
    )(*ins)


def loss_fwd_bwd(name, y, target):
    S, D = y.shape
    ts = _tile(S, ROW_TILE)

    def body(y_ref, t_ref, l_ref, dy_ref):
        @pl.when(pl.program_id(0) == 0)
        def _():
            l_ref[...] = jnp.zeros_like(l_ref)

        e = y_ref[...] - t_ref[...]
        l_ref[...] += jnp.sum(e * e)
        dy_ref[...] = e * (1.0 / D)

    row = pl.BlockSpec((ts, D), lambda i: (i, 0))
    return pl.pallas_call(
        body, name=name, grid=(S // ts,), in_specs=[row, row],
        out_specs=[pl.BlockSpec((1, 128), lambda i: (0, 0)), row],
        out_shape=[jax.ShapeDtypeStruct((1, 128), F32), jax.ShapeDtypeStruct((S, D), F32)],
        compiler_params=_params("arbitrary"),
    )(y, target)


def axpy(name, ga, gb):
    S, D = ga.shape
    ts = _tile(S, ROW_TILE)

    def body(a_ref, b_ref, o_ref):
        o_ref[...] = ALPHA * a_ref[...] + b_ref[...]

    row = pl.BlockSpec((ts, D), lambda i: (i, 0))
    return pl.pallas_call(body, name=name, grid=(S // ts,), in_specs=[row, row], out_specs=row,
                          out_shape=jax.ShapeDtypeStruct((S, D), F32), compiler_params=_params("parallel"))(ga, gb)


def rms_fwd(name, down, gq, gkv):
    S = down.shape[0]
    ts = _tile(S, ROW_TILE)
    L = MLA_Q_LORA

    def body(d_ref, gq_ref, gkv_ref, q_ref, kv_ref):
        for lo, g_ref, o_ref in ((0, gq_ref, q_ref), (L, gkv_ref, kv_ref)):
            x = d_ref[:, lo:lo + L]
            r = lax.rsqrt(jnp.mean(x * x, axis=-1, keepdims=True) + RMS_EPS)
            o_ref[...] = (x * r * g_ref[...]).astype(BF16)

    vec = pl.BlockSpec((1, L), lambda i: (0, 0))
    out = pl.BlockSpec((ts, L), lambda i: (i, 0))
    return pl.pallas_call(
        body, name=name, grid=(S // ts,), in_specs=[pl.BlockSpec((ts, down.shape[1]), lambda i: (i, 0)), vec, vec],
        out_specs=[out, out], out_shape=[jax.ShapeDtypeStruct((S, L), BF16)] * 2, compiler_params=_params("parallel"),
    )(down, gq.reshape(1, L), gkv.reshape(1, L))


def rms_bwd(name, down, dq, dkv, dkr, gq, gkv):
    S, W = down.shape
    ts = _tile(S, ROW_TILE)
    L = MLA_Q_LORA

    def body(d_ref, dq_ref, dkv_ref, dkr_ref, gq_ref, gkv_ref, o_ref, dgq_ref, dgkv_ref):
        @pl.when(pl.program_id(0) == 0)
        def _():
            dgq_ref[...] = jnp.zeros_like(dgq_ref)
            dgkv_ref[...] = jnp.zeros_like(dgkv_ref)

        for lo, dy_ref, g_ref, dg_ref in ((0, dq_ref, gq_ref, dgq_ref), (L, dkv_ref, gkv_ref, dgkv_ref)):
            x = d_ref[:, lo:lo + L]
            dy = dy_ref[...]
            r = lax.rsqrt(jnp.mean(x * x, axis=-1, keepdims=True) + RMS_EPS)
            dg_ref[...] += jnp.sum(dy * x * r, axis=0, keepdims=True)
            dyg = dy * g_ref[...]
            dx = r * dyg - x * (r * r * r) * jnp.mean(dyg * x, axis=-1, keepdims=True)
            o_ref[:, lo:lo + L] = dx.astype(BF16)
        o_ref[:, 2 * L:] = dkr_ref[...].astype(BF16)

    vec = pl.BlockSpec((1, L), lambda i: (0, 0))
    lat = pl.BlockSpec((ts, L), lambda i: (i, 0))
    full = pl.BlockSpec((ts, W), lambda i: (i, 0))
    return pl.pallas_call(
        body, name=name, grid=(S // ts,),
        in_specs=[full, lat, lat, pl.BlockSpec((ts, 128), lambda i: (i, 0)), vec, vec],
        out_specs=[full, vec, vec],
        out_shape=[jax.ShapeDtypeStruct((S, W), BF16), jax.ShapeDtypeStruct((1, L), F32), jax.ShapeDtypeStruct((1, L), F32)],
        compiler_params=_params("arbitrary"),
    )(down, dq, dkv, dkr, gq.reshape(1, L), gkv.reshape(1, L))


def rope_tables(S):
    half = MLA_ROPE // 2
    inv = (np.float32(ROPE_THETA) ** (-np.arange(half, dtype=np.float32) / np.float32(half))).astype(np.float32)
    ang = np.arange(S, dtype=np.float32)[:, None] * inv[None, :]
    cos, sin = np.cos(ang).astype(np.float32), np.sin(ang).astype(np.float32)
    z = np.zeros_like(cos)
    return (jnp.asarray(np.concatenate([cos, z, cos, z], 1)), jnp.asarray(np.concatenate([-sin, z, sin, z], 1)))


def _rot(x, cos, sin):
    return x * cos + pltpu.roll(x, 64, 1) * sin


def mla_prep_fwd(name, q, kv, down, cos, sin):
    S = q.shape[0]
    ts = _tile(S, 512)

    def body(q_ref, kv_ref, kr_ref, c_ref, s_ref, qo_ref, ko_ref):
        c, s = c_ref[...], s_ref[...]
        qo_ref[:, :128] = q_ref[:, :128].astype(BF16)
        qo_ref[:, 128:] = _rot(q_ref[:, 128:], c, s).astype(BF16)
        ko_ref[:, :128] = kv_ref[:, :128]
        ko_ref[:, 128:] = _rot(kr_ref[...], c, s).astype(BF16)

    head = pl.BlockSpec((ts, 256), lambda i, h: (i, h))
    tab = pl.BlockSpec((ts, 128), lambda i, h: (i, 0))
    return pl.pallas_call(
        body, name=name, grid=(S // ts, HEADS),
        in_specs=[head, head, pl.BlockSpec((ts, 128), lambda i, h: (i, 2 * MLA_Q_LORA // 128)), tab, tab],
        out_specs=[head, head], out_shape=[jax.ShapeDtypeStruct(q.shape, BF16)] * 2,
        compiler_params=_params("parallel", "parallel"),
    )(q, kv, down, cos, sin)


def mla_prep_bwd(name, dq, dk, dv, cos, sin):
    S = dq.shape[0]
    ts = _tile(S, 512)

    def body(dq_ref, dk_ref, dv_ref, c_ref, s_ref, qo_ref, kvo_ref, kr_ref):
        c, s = c_ref[...], -s_ref[...]
        qo_ref[:, :128] = dq_ref[:, :128].astype(BF16)
        qo_ref[:, 128:] = _rot(dq_ref[:, 128:], c, s).astype(BF16)
        kvo_ref[:, :128] = dk_ref[:, :128].astype(BF16)
        kvo_ref[:, 128:] = dv_ref[...].astype(BF16)

        @pl.when(pl.program_id(1) == 0)
        def _():
            kr_ref[...] = jnp.zeros_like(kr_ref)

        kr_ref[...] += _rot(dk_ref[:, 128:], c, s)

    head = pl.BlockSpec((ts, 256), lambda i, h: (i, h))
    tab = pl.BlockSpec((ts, 128), lambda i, h: (i, 0))
    return pl.pallas_call(
        body, name=name, grid=(S // ts, HEADS),
        in_specs=[head, head, pl.BlockSpec((ts, 128), lambda i, h: (i, h)), tab, tab],
        out_specs=[head, head, tab],
        out_shape=[jax.ShapeDtypeStruct(dq.shape, BF16), jax.ShapeDtypeStruct(dq.shape, BF16),
                   jax.ShapeDtypeStruct((S, 128), F32)],
        compiler_params=_params("parallel", "arbitrary"),
    )(dq, dk, dv, cos, sin)


def _dot_nt(a, b):
    return lax.dot_general(a, b, (((1,), (1,)), ((), ())), preferred_element_type=F32)


def _dot_tn(a, b):
    return lax.dot_general(a, b, (((0,), (0,)), ((), ())), preferred_element_type=F32)


def _dot(a, b):
    return jnp.dot(a, b, preferred_element_type=F32)


def _positions(i, j, TQ, TK):
    row = i * TQ + lax.broadcasted_iota(jnp.int32, (TQ, TK), 0)
    col = j * TK + lax.broadcasted_iota(jnp.int32, (TQ, TK), 1)
    return row, col


def _softmax_mask(mode, row, col):
    rc, cc = row >> CHUNK_SHIFT, col >> CHUNK_SHIFT
    if mode == "mla":
        return cc <= rc
    return (cc <= rc) & (cc >= rc - CA_LEFT_CHUNKS)


def _key_blocks(mode, i, TQ, TK):
    per = TQ // TK
    if mode == "ca":
        lo = jnp.maximum(i - (CA_LEFT_CHUNKS * CHUNK) // TK, 0)
        return lo, 0, i - lo + 1
    return 0, i * per, per


class HeadCols:
    def __init__(self, arr, width, index, off=0, w=None):
        self.arr, self.width, self.index, self.off = arr, width, index, off
        self.w = width if w is None else w

    def rows(self, T):
        return pl.BlockSpec((T, ATT_G * self.width), lambda p, i: (i, self.index(p)))

    def full(self, S):
        return pl.BlockSpec((S, ATT_G * self.width), lambda p, i: (0, self.index(p)))

    def lanes(self, g):
        lo = g * self.width + self.off
        return slice(lo, lo + self.w)


def _att_tiles(mode, S):
    tk = min(ATT_TK, S)
    return (tk if mode == "ca" else min(ATT_TQ, S)), tk


def _walk(lo, n, per, step, carry, descending=False):
    tail = [lo + n + d for d in range(per)]
    if descending:
        for j in reversed(tail):
            carry = step(j, carry, True)
        return lax.fori_loop(0, n, lambda t, c: step(lo + n - 1 - t, c, False), carry)
    carry = lax.fori_loop(0, n, lambda t, c: step(lo + t, c, False), carry)
    for j in tail:
        carry = step(j, carry, True)
    return carry


def softmax_attn_fwd(name, mode, q, k, v, scale, bias=None):
    S = q.arr.shape[0]
    TQ, TK = _att_tiles(mode, S)
    G, dv = ATT_G, v.w

    def body(*refs):
        if bias is not None:
            q_ref, k_ref, v_ref, b_ref, o_ref, lse_ref = refs
        else:
            q_ref, k_ref, v_ref, o_ref, lse_ref = refs
        i = pl.program_id(1)
        qs = [q_ref[:, q.lanes(g)] for g in range(G)]

        def block(g, j, carry, mask, ks):
            m, l, acc = carry
            s = _dot_nt(qs[g], k_ref[ks, k.lanes(g)]) * scale
            if bias is not None:
                s = s + b_ref[g, jnp.minimum(i - j, 2)]
            if mask is not None:
                s = jnp.where(mask, s, NEG)
            m_new = jnp.maximum(m, jnp.max(s, axis=-1, keepdims=True))
            a = jnp.exp(m - m_new)
            p = jnp.exp(s - m_new)
            if mask is not None:
                p = jnp.where(mask, p, 0.0)
            l = a * l + jnp.sum(p, axis=-1, keepdims=True)
            acc = a * acc + _dot(p.astype(BF16), v_ref[ks, v.lanes(g)])
            return m_new, l, acc

        def step(j, carry, masked):
            ks = pl.ds(pl.multiple_of(j * TK, TK), TK)
            mask = _softmax_mask(mode, *_positions(i, j, TQ, TK)) if masked or mode == "ca" else None
            return tuple(block(g, j, carry[g], mask, ks) for g in range(G))

        init = (jnp.full((TQ, 1), NEG, F32), jnp.zeros((TQ, 1), F32), jnp.zeros((TQ, dv), F32))
        lo, n, per = _key_blocks(mode, i, TQ, TK)
        if mode == "ca":
            out = lax.fori_loop(lo, lo + per, lambda j, c: step(j, c, True), (init,) * G)
        else:
            out = _walk(lo, n, per, step, (init,) * G)
        for g, (m, l, acc) in enumerate(out):
            o_ref[:, g * dv:(g + 1) * dv] = (acc / l).astype(BF16)
            lse_ref[:, g * 128:(g + 1) * 128] = jnp.broadcast_to(m + jnp.log(l), (TQ, 128))

    in_specs = [q.rows(TQ), k.full(S), v.full(S)]
    ins = [q.arr, k.arr, v.arr]
    if bias is not None:
        in_specs.append(pl.BlockSpec((G, 3, TK, TK), lambda p, i: (p, 0, 0, 0)))
        ins.append(bias)
    return _call(
        body, ins, name=name, grid=(HEADS // G, S // TQ), in_specs=in_specs,
        out_specs=[pl.BlockSpec((TQ, G * dv), lambda p, i: (i, p)), pl.BlockSpec((TQ, G * 128), lambda p, i: (i, p))],
        out_shape=[jax.ShapeDtypeStruct((S, HEADS * dv), BF16), jax.ShapeDtypeStruct((S, HEADS * 128), F32)],
        semantics=("parallel", "parallel"))


def softmax_attn_bwd(name, mode, q, k, v, o, do, lse, scale, bias=None):
    S = q.arr.shape[0]
    TQ, TK = _att_tiles(mode, S)
    G, dqk, dv = ATT_G, q.w, v.w

    def body(*refs):
        if bias is not None:
            q_ref, k_ref, v_ref, o_ref, do_ref, lse_ref, b_ref, dq_ref, dk_ref, dv_ref, db_ref = refs
        else:
            q_ref, k_ref, v_ref, o_ref, do_ref, lse_ref, dq_ref, dk_ref, dv_ref = refs
        i = pl.program_id(1)

        @pl.when(i == 0)
        def _():
            dk_ref[...] = jnp.zeros_like(dk_ref)
            dv_ref[...] = jnp.zeros_like(dv_ref)
            if bias is not None:
                db_ref[...] = jnp.zeros_like(db_ref)

        qs = [q_ref[:, q.lanes(g)] for g in range(G)]
        dos = [do_ref[:, do.lanes(g)] for g in range(G)]
        lses = [lse_ref[:, g * 128:g * 128 + 1] for g in range(G)]
        deltas = [jnp.sum(dos[g].astype(F32) * o_ref[:, o.lanes(g)].astype(F32), axis=-1, keepdims=True)
                  for g in range(G)]

        def block(g, j, dq, mask, ks):
            kb, vb = k_ref[ks, k.lanes(g)], v_ref[ks, v.lanes(g)]
            s = _dot_nt(qs[g], kb) * scale
            if bias is not None:
                slot = jnp.minimum(i - j, 2)
                s = s + b_ref[g, slot]
            p = jnp.exp(s - lses[g])
            if mask is not None:
                p = jnp.where(mask, p, 0.0)
            ds = p * (_dot_nt(dos[g], vb) - deltas[g])
            if bias is not None:
                db_ref[g, slot] += ds
            dsb = (ds * scale).astype(BF16)
            dk_ref[ks, g * dqk:(g + 1) * dqk] += _dot_tn(dsb, qs[g])
            dv_ref[ks, g * dv:(g + 1) * dv] += _dot_tn(p.astype(BF16), dos[g])
            return dq + _dot(dsb, kb)

        def step(j, carry, masked):
            ks = pl.ds(pl.multiple_of(j * TK, TK), TK)
            mask = _softmax_mask(mode, *_positions(i, j, TQ, TK)) if masked or mode == "ca" else None
            return tuple(block(g, j, carry[g], mask, ks) for g in range(G))

        init = (jnp.zeros((TQ, dqk), F32),) * G
        lo, n, per = _key_blocks(mode, i, TQ, TK)
        if mode == "ca":
            out = lax.fori_loop(lo, lo + per, lambda j, c: step(j, c, True), init)
        else:
            out = _walk(lo, n, per, step, init)
        for g in range(G):
            dq_ref[:, g * dqk:(g + 1) * dqk] = out[g]

    in_specs = [q.rows(TQ), k.full(S), v.full(S), o.rows(TQ), do.rows(TQ),
                pl.BlockSpec((TQ, G * 128), lambda p, i: (i, p))]
    ins = [q.arr, k.arr, v.arr, o.arr, do.arr, lse]
    out_specs = [pl.BlockSpec((TQ, G * dqk), lambda p, i: (i, p)), pl.BlockSpec((S, G * dqk), lambda p, i: (0, p)),
                 pl.BlockSpec((S, G * dv), lambda p, i: (0, p))]
    out_shape = [jax.ShapeDtypeStruct((S, HEADS * dqk), F32), jax.ShapeDtypeStruct((S, HEADS * dqk), F32),
                 jax.ShapeDtypeStruct((S, HEADS * dv), F32)]
    if bias is not None:
        bspec = pl.BlockSpec((G, 3, TK, TK), lambda p, i: (p, 0, 0, 0))
        in_specs.append(bspec)
        ins.append(bias)
        out_specs.append(bspec)
        out_shape.append(jax.ShapeDtypeStruct(bias.shape, F32))
    return _call(body, ins, name=name, grid=(HEADS // G, S // TQ), in_specs=in_specs, out_specs=out_specs,
                 out_shape=out_shape, semantics=("parallel", "arbitrary"))


def _split2(x):
    hi = x.astype(BF16)
    return hi, (x - hi.astype(F32)).astype(BF16)


def _split3(x):
    hi = x.astype(BF16)
    r = x - hi.astype(F32)
    mid = r.astype(BF16)
    return hi, mid, (r - mid.astype(F32)).astype(BF16)


def _stick_block(qb, kb, strict, scale):
    z = _dot_nt(qb, kb) * scale
    sp = jnp.log(1.0 + jnp.exp(-jnp.abs(z)))
    lb = jnp.minimum(z, 0.0) - sp
    l1m = jnp.minimum(-z, 0.0) - sp
    if strict is not None:
        l1m = jnp.where(strict, l1m, 0.0)
    return z, lb, l1m


def _strict_mask(i, j, TQ, TK):
    row, col = _positions(i, j, TQ, TK)
    return col < row


def _tri(T, inclusive):
    r = lax.broadcasted_iota(jnp.int32, (T, T), 0)
    c = lax.broadcasted_iota(jnp.int32, (T, T), 1)
    return ((r >= c) if inclusive else (r > c)).astype(BF16)


def _suffix(parts, tri):
    out = _dot(parts[0], tri)
    for p in parts[1:]:
        out = out + _dot(p, tri)
    return out


def stick_attn_fwd(name, q, k, v, scale):
    S = q.arr.shape[0]
    TQ, TK = _att_tiles("sb", S)
    G, dv = ATT_G, v.w

    def body(q_ref, k_ref, v_ref, o_ref):
        i = pl.program_id(1)
        qs = [q_ref[:, q.lanes(g)] for g in range(G)]
        tri = _tri(TK, False)

        def block(g, carry, strict, ks):
            right, acc = carry
            z, lb, l1m = _stick_block(qs[g], k_ref[ks, k.lanes(g)], strict, scale)
            a = jnp.exp(lb + _suffix(_split2(l1m), tri) + right)
            if strict is not None:
                a = jnp.where(strict, a, 0.0)
            acc = acc + _dot(a.astype(BF16), v_ref[ks, v.lanes(g)])
            return right + jnp.sum(l1m, axis=-1, keepdims=True), acc

        def step(j, carry, masked):
            ks = pl.ds(pl.multiple_of(j * TK, TK), TK)
            strict = _strict_mask(i, j, TQ, TK) if masked else None
            return tuple(block(g, carry[g], strict, ks) for g in range(G))

        init = (jnp.zeros((TQ, 1), F32), jnp.zeros((TQ, dv), F32))
        lo, n, per = _key_blocks("sb", i, TQ, TK)
        out = _walk(lo, n, per, step, (init,) * G, descending=True)
        for g in range(G):
            o_ref[:, g * dv:(g + 1) * dv] = out[g][1].astype(BF16)

    return _call(
        body, [q.arr, k.arr, v.arr], name=name, grid=(HEADS // G, S // TQ),
        in_specs=[q.rows(TQ), k.full(S), v.full(S)], out_specs=[pl.BlockSpec((TQ, G * dv), lambda p, i: (i, p))],
        out_shape=[jax.ShapeDtypeStruct((S, HEADS * dv), BF16)], semantics=("parallel", "parallel"))[0]


def stick_attn_bwd(name, q, k, v, do, scale):
    S = q.arr.shape[0]
    TQ, TK = _att_tiles("sb", S)
    G, dqk, dv = ATT_G, q.w, v.w

    def body(q_ref, k_ref, v_ref, do_ref, dq_ref, dk_ref, dv_ref):
        i = pl.program_id(1)

        @pl.when(i == 0)
        def _():
            dk_ref[...] = jnp.zeros_like(dk_ref)
            dv_ref[...] = jnp.zeros_like(dv_ref)

        qs = [q_ref[:, q.lanes(g)] for g in range(G)]
        dos = [do_ref[:, do.lanes(g)] for g in range(G)]
        tri = _tri(TK, False)
        tri_inc = _tri(TK, True)

        def block(g, right, strict, ks):
            kb = k_ref[ks, k.lanes(g)]
            z, lb, l1m = _stick_block(qs[g], kb, strict, scale)
            a = jnp.exp(lb + _suffix(_split2(l1m), tri) + right)
            if strict is not None:
                a = jnp.where(strict, a, 0.0)
            return kb, z, l1m, a, a * _dot_nt(dos[g], v_ref[ks, v.lanes(g)])

        def total(j, carry, masked):
            ks = pl.ds(pl.multiple_of(j * TK, TK), TK)
            strict = _strict_mask(i, j, TQ, TK) if masked else None
            out = []
            for g in range(G):
                right, gtot = carry[g]
                _, _, l1m, _, gg = block(g, right, strict, ks)
                out.append((right + jnp.sum(l1m, axis=-1, keepdims=True), gtot + jnp.sum(gg, axis=-1, keepdims=True)))
            return tuple(out)

        zero = jnp.zeros((TQ, 1), F32)
        lo, n, per = _key_blocks("sb", i, TQ, TK)
        gtots = [c[1] for c in _walk(lo, n, per, total, ((zero, zero),) * G, descending=True)]

        def step(j, carry, masked):
            ks = pl.ds(pl.multiple_of(j * TK, TK), TK)
            strict = _strict_mask(i, j, TQ, TK) if masked else None
            out = []
            for g in range(G):
                right, gright, dq = carry[g]
                kb, z, l1m, a, gg = block(g, right, strict, ks)
                c = gtots[g] - (_suffix(_split3(gg), tri_inc) + gright)
                sig = 1.0 / (1.0 + jnp.exp(-z))
                dz = gg * (1.0 - sig) - c * sig
                if strict is not None:
                    dz = jnp.where(strict, dz, 0.0)
                dzb = (dz * scale).astype(BF16)
                dk_ref[ks, g * dqk:(g + 1) * dqk] += _dot_tn(dzb, qs[g])
                dv_ref[ks, g * dv:(g + 1) * dv] += _dot_tn(a.astype(BF16), dos[g])
                out.append((right + jnp.sum(l1m, axis=-1, keepdims=True),
                            gright + jnp.sum(gg, axis=-1, keepdims=True), dq + _dot(dzb, kb)))
            return tuple(out)

        out = _walk(lo, n, per, step, ((zero, zero, jnp.zeros((TQ, dqk), F32)),) * G, descending=True)
        for g in range(G):
            dq_ref[:, g * dqk:(g + 1) * dqk] = out[g][2]

    return _call(
        body, [q.arr, k.arr, v.arr, do.arr], name=name, grid=(HEADS // G, S // TQ),
        in_specs=[q.rows(TQ), k.full(S), v.full(S), do.rows(TQ)],
        out_specs=[pl.BlockSpec((TQ, G * dqk), lambda p, i: (i, p)), pl.BlockSpec((S, G * dqk), lambda p, i: (0, p)),
                   pl.BlockSpec((S, G * dv), lambda p, i: (0, p))],
        out_shape=[jax.ShapeDtypeStruct((S, HEADS * dqk), F32), jax.ShapeDtypeStruct((S, HEADS * dqk), F32),
                   jax.ShapeDtypeStruct((S, HEADS * dv), F32)],
        semantics=("parallel", "arbitrary"))


def _skew(x, back):
    T = x.shape[0]
    rows = lax.broadcasted_iota(jnp.int32, (T, T), 0)
    for b in range(T.bit_length() - 1):
        shift = T - (1 << b) if back else 1 << b
        x = jnp.where(((rows >> b) & 1) == 1, pltpu.roll(x, shift, 1), x)
    return x


def _table_rows(table):
    t = jnp.pad(table.T, ((0, 0), (0, 2 * REL_CLIP_LEFT - REL_TABLE)))
    return t.reshape(table.shape[1], 2, REL_CLIP_LEFT)


def rel_bias_blocks(name, table, T):
    assert T == 2 * REL_CLIP_LEFT, "the base rows below are laid out for blocks of 256"

    def body(t_ref, o_ref):
        low, high = t_ref[0:1, :], t_ref[1:2, :]
        first = jnp.broadcast_to(t_ref[0:1, 0:1], (1, REL_CLIP_LEFT))
        qq = lax.broadcasted_iota(jnp.int32, (T, T), 0)
        kk = lax.broadcasted_iota(jnp.int32, (T, T), 1)

        def rolled(row):
            return _skew(jnp.broadcast_to(row, (T, T)), False)

        far = jnp.concatenate([first, low], axis=1)
        near = jnp.concatenate([high, jnp.zeros_like(high)], axis=1)
        o_ref[0] = jnp.where(kk >= qq, rolled(near), rolled(far))
        o_ref[1] = jnp.where(kk >= qq, rolled(far), jnp.broadcast_to(t_ref[0:1, 0:1], (T, T)))
        o_ref[2] = jnp.broadcast_to(t_ref[0:1, 0:1], (T, T))

    return pl.pallas_call(
        body, name=name, grid=(HEADS,), in_specs=[pl.BlockSpec((None, 2, REL_CLIP_LEFT), lambda h: (h, 0, 0))],
        out_specs=pl.BlockSpec((None, 3, T, T), lambda h: (h, 0, 0, 0)),
        out_shape=jax.ShapeDtypeStruct((HEADS, 3, T, T), F32), compiler_params=_params("parallel"),
    )(_table_rows(table))


def rel_bias_grad(name, dbias):
    T = dbias.shape[-1]
    L = REL_CLIP_LEFT
    assert T == 2 * L

    def body(d_ref, o_ref):
        qq = lax.broadcasted_iota(jnp.int32, (T, T), 0)
        ll = lax.broadcasted_iota(jnp.int32, (T, T), 1)
        wrapped = ll + qq >= T

        def columns(d):
            x = _skew(d_ref[d], True)
            return (jnp.sum(jnp.where(wrapped, 0.0, x), axis=0, keepdims=True),
                    jnp.sum(jnp.where(wrapped, x, 0.0), axis=0, keepdims=True))

        pos0, neg0 = columns(0)
        pos1, neg1 = columns(1)
        clipped = (jnp.sum(neg0[:, :L]) + jnp.sum(pos1[:, :L]) + jnp.sum(neg1) + jnp.sum(d_ref[2]))
        lane = lax.broadcasted_iota(jnp.int32, (1, L), 1)
        low = neg0[:, L:] + pos1[:, L:]
        o_ref[...] = jnp.zeros_like(o_ref)
        o_ref[0:1, :] = jnp.where(lane == 0, low + clipped, low)
        o_ref[1:2, :] = pos0[:, :L]

    rows = pl.pallas_call(
        body, name=name, grid=(HEADS,), in_specs=[pl.BlockSpec((None, 3, T, T), lambda h: (h, 0, 0, 0))],
        out_specs=pl.BlockSpec((None, 8, L), lambda h: (h, 0, 0)), out_shape=jax.ShapeDtypeStruct((HEADS, 8, L), F32),
        compiler_params=_params("parallel"),
    )(dbias)
    return rows[:, :2, :].reshape(HEADS, 2 * L)[:, :REL_TABLE].T


HBM = pl.BlockSpec(memory_space=pl.ANY)


def _place():
    return lax.axis_index("x"), lax.axis_index("y"), lax.axis_index("c")


def all_gather(name, shards):
    n = len(shards)

    def body(*refs):
        x_refs, out_refs = refs[:n], refs[n:2 * n]
        send_sems, recv_sems, local_sems = refs[2 * n:]
        x, y, c = _place()
        me, sibling = (x, y, c), (x, y, 1 - c)
        chips = [(1 - x, y), (x, 1 - y), (1 - x, 1 - y)]

        def block(t, dev):
            return out_refs[t].at[4 * dev[0] + 2 * dev[1] + dev[2]]

        def copy(t, k, dev, to, src=None):
            return pltpu.make_async_remote_copy(
                src_ref=block(t, dev) if src is None else src, dst_ref=block(t, dev),
                send_sem=send_sems.at[t, k], recv_sem=recv_sems.at[t, k], device_id=to, device_id_type=MESH)

        mine = [pltpu.make_async_copy(x_refs[t], block(t, me), local_sems.at[t]) for t in range(n)]
        for cp in mine:
            cp.start()
        first = []
        for t in range(n):
            first.append(copy(t, 0, me, sibling, src=x_refs[t]))
            first += [copy(t, 1 + j, me, (*chip, c), src=x_refs[t]) for j, chip in enumerate(chips)]
        for cp in first:
            cp.start()
        passed = []
        for j, chip in enumerate(chips):
            for t in range(n):
                copy(t, 1 + j, (*chip, c), me).wait_recv()
                cp = copy(t, 4 + j, (*chip, c), sibling)
                cp.start()
                passed.append(cp)
        for t in range(n):
            copy(t, 0, sibling, me).wait_recv()
            for j, chip in enumerate(chips):
                copy(t, 4 + j, (*chip, 1 - c), me).wait_recv()
        for cp in first + passed:
            cp.wait_send()
        for cp in mine:
            cp.wait()

    return pl.pallas_call(
        body, name=name, in_specs=[HBM] * n, out_specs=[HBM] * n,
        out_shape=[jax.ShapeDtypeStruct((N_DEV, *s.shape), s.dtype) for s in shards],
        scratch_shapes=[pltpu.SemaphoreType.DMA((n, 7)), pltpu.SemaphoreType.DMA((n, 7)), pltpu.SemaphoreType.DMA((n,))],
    )(*shards)


def _remote(src, dst, send_sems, recv_sems, k, to):
    return pltpu.make_async_remote_copy(src_ref=src, dst_ref=dst, send_sem=send_sems.at[k], recv_sem=recv_sems.at[k],
                                        device_id=to, device_id_type=MESH)


class Future:
    def __init__(self):
        self.value = None

    def get(self):
        if self.value is None:
            SCHED.flush()
        return self.value


def gather_jobs(name, shards, want_chips, want_sibling):
    n = len(shards)
    result = Future()

    def to_chips(in_refs, out_refs, send_sems, recv_sems):
        x, y, c = _place()
        me = 4 * x + 2 * y + c
        cps = []
        for t in range(n):
            cps.append(pltpu.make_async_copy(in_refs[t], out_refs[t].at[me], send_sems.at[4 * t]))
            for j, chip in enumerate([(1 - x, y), (x, 1 - y), (1 - x, 1 - y)]):
                cps.append(_remote(in_refs[t], out_refs[t].at[me], send_sems, recv_sems, 4 * t + 1 + j, (*chip, c)))
        return cps

    def to_sibling(in_refs, out_refs, send_sems, recv_sems):
        x, y, c = _place()
        return [_remote(in_refs[t].at[2 * chip + c], out_refs[t].at[2 * chip + c], send_sems, recv_sems, 4 * t + chip,
                        (x, y, 1 - c)) for t in range(n) for chip in range(4)]

    lands = [jax.ShapeDtypeStruct((N_DEV, *s.shape), s.dtype) for s in shards]

    def second(outs):
        SCHED.post(Job(name + "_sibling", want_sibling, outs, lands, 4 * n, to_sibling,
                       lambda final: setattr(result, "value", final), aliases={t: t for t in range(n)}))

    SCHED.post(Job(name + "_chips", want_chips, shards, lands, 4 * n, to_chips, second))
    return result


def scatter_jobs(name, grads, core, chip, want_sibling, wants_chips):
    n = len(grads)
    result = Future()
    sums = [None] * n

    def to_sibling(in_refs, out_refs, send_sems, recv_sems):
        x, y, c = _place()
        return [_remote(in_refs[t].at[1 - c], out_refs[t], send_sems, recv_sems, t, (x, y, 1 - c)) for t in range(n)]

    def after_sibling(received):
        parts = [add_sibling(f"{name}_add{t}", grads[t], received[t], core) for t in range(n)]
        for group, want in wants_chips:
            def to_chips(in_refs, out_refs, send_sems, recv_sems, m=len(group)):
                x, y, c = _place()
                return [_remote(in_refs[t].at[2 * cx + cy], out_refs[t].at[j], send_sems, recv_sems, 3 * t + j,
                                (cx, cy, c))
                        for t in range(m) for j, (cx, cy) in enumerate([(1 - x, y), (x, 1 - y), (1 - x, 1 - y)])]

            def after_chips(received, group=group):
                for t, r in zip(group, received):
                    sums[t] = sum_chips(f"{name}_sum{t}", parts[t], r, chip)
                if all(s is not None for s in sums):
                    result.value = sums

            mine = [parts[t] for t in group]
            SCHED.post(Job(f"{name}_chips{group[0]}", want, mine,
                           [jax.ShapeDtypeStruct((3, *p.shape[1:]), p.dtype) for p in mine], 3 * len(mine), to_chips,
                           after_chips))

    SCHED.post(Job(name + "_sibling", want_sibling, grads, [jax.ShapeDtypeStruct(g.shape[1:], g.dtype) for g in grads],
                   n, to_sibling, after_sibling))
    return result


def exchange_sibling(name, grads):
    n = len(grads)

    def body(*refs):
        g_refs, out_refs = refs[:n], refs[n:2 * n]
        send_sems, recv_sems = refs[2 * n:]
        x, y, c = _place()
        cps = [pltpu.make_async_remote_copy(
            src_ref=g_refs[t].at[1 - c], dst_ref=out_refs[t], send_sem=send_sems.at[t], recv_sem=recv_sems.at[t],
            device_id=(x, y, 1 - c), device_id_type=MESH) for t in range(n)]
        for cp in cps:
            cp.start()
        for cp in cps:
            cp.wait()

    return pl.pallas_call(
        body, name=name, in_specs=[HBM] * n, out_specs=[HBM] * n,
        out_shape=[jax.ShapeDtypeStruct(g.shape[1:], g.dtype) for g in grads],
        scratch_shapes=[pltpu.SemaphoreType.DMA((n,)), pltpu.SemaphoreType.DMA((n,))],
    )(*grads)


def exchange_chips(name, parts):
    n = len(parts)

    def body(*refs):
        p_refs, out_refs = refs[:n], refs[n:2 * n]
        send_sems, recv_sems = refs[2 * n:]
        x, y, c = _place()
        chips = [(1 - x, y), (x, 1 - y), (1 - x, 1 - y)]
        cps = [pltpu.make_async_remote_copy(
            src_ref=p_refs[t].at[2 * chip[0] + chip[1]], dst_ref=out_refs[t].at[j],
            send_sem=send_sems.at[t, j], recv_sem=recv_sems.at[t, j], device_id=(*chip, c), device_id_type=MESH)
            for t in range(n) for j, chip in enumerate(chips)]
        for cp in cps:
            cp.start()
        for cp in cps:
            cp.wait()

    return pl.pallas_call(
        body, name=name, in_specs=[HBM] * n, out_specs=[HBM] * n,
        out_shape=[jax.ShapeDtypeStruct((3, *p.shape[1:]), p.dtype) for p in parts],
        scratch_shapes=[pltpu.SemaphoreType.DMA((n, 3)), pltpu.SemaphoreType.DMA((n, 3))],
    )(*parts)


def _as_rows(shape):
    return (int(np.prod(shape[:-1])), shape[-1])


ELEMENTWISE_BLOCK = 256 * 1024


def _row_tile(rows, cols):
    return _tile(rows, max(128, ELEMENTWISE_BLOCK // cols // 128 * 128))


def add_sibling(name, grad, recv, core):
    rows, cols = _as_rows(grad.shape[2:])
    tr = _row_tile(rows, cols)

    def body(c_ref, g_ref, r_ref, o_ref):
        o_ref[...] = (g_ref[...].astype(F32) + r_ref[...].astype(F32)).astype(BF16)

    blk = pl.BlockSpec((None, tr, cols), lambda k, i, c_ref: (k, i, 0))
    return pl.pallas_call(
        body, name=name,
        grid_spec=pltpu.PrefetchScalarGridSpec(
            num_scalar_prefetch=1, grid=(4, rows // tr),
            in_specs=[pl.BlockSpec((None, None, tr, cols), lambda k, i, c_ref: (c_ref[0], k, i, 0)), blk],
            out_specs=blk),
        out_shape=jax.ShapeDtypeStruct((4, rows, cols), BF16), compiler_params=_params("parallel", "parallel"),
    )(core, grad.reshape(2, 4, rows, cols), recv.reshape(4, rows, cols)).reshape(recv.shape)


def sum_chips(name, part, recv, chip):
    shape = part.shape[1:]
    rows, cols = _as_rows(shape)
    tr = _row_tile(rows, cols)

    def body(c_ref, p_ref, r_ref, o_ref):
        o_ref[...] = (p_ref[...].astype(F32) + r_ref[0].astype(F32) + r_ref[1].astype(F32) + r_ref[2].astype(F32))

    return pl.pallas_call(
        body, name=name,
        grid_spec=pltpu.PrefetchScalarGridSpec(
            num_scalar_prefetch=1, grid=(rows // tr,),
            in_specs=[pl.BlockSpec((None, tr, cols), lambda i, c_ref: (c_ref[0], i, 0)),
                      pl.BlockSpec((3, tr, cols), lambda i, c_ref: (0, i, 0))],
            out_specs=pl.BlockSpec((tr, cols), lambda i, c_ref: (i, 0))),
        out_shape=jax.ShapeDtypeStruct((rows, cols), F32), compiler_params=_params("parallel"),
    )(chip, part.reshape(4, rows, cols), recv.reshape(3, rows, cols)).reshape(shape)


def sum_devices(name, gathered):
    _, rows, cols = gathered.shape

    def body(g_ref, o_ref):
        acc = g_ref[0]
        for d in range(1, N_DEV):
            acc = acc + g_ref[d]
        o_ref[...] = acc

    return pl.pallas_call(body, name=name, out_shape=jax.ShapeDtypeStruct((rows, cols), F32))(gathered)


def adamw(name, w, g, m, v):
    shape = w.shape
    rows, cols = _as_rows(shape)
    tr = _row_tile(rows, cols) if rows % 8 == 0 else rows
    c1 = 1.0 / (1.0 - ADAM_B1 ** ADAM_STEP)
    c2 = 1.0 / (1.0 - ADAM_B2 ** ADAM_STEP)

    def body(w_ref, g_ref, m_ref, v_ref, d_ref, mo_ref, vo_ref):
        g_ = g_ref[...]
        m_ = ADAM_B1 * m_ref[...] + (1.0 - ADAM_B1) * g_
        v_ = ADAM_B2 * v_ref[...] + (1.0 - ADAM_B2) * (g_ * g_)
        d_ref[...] = -ADAM_LR * ((m_ * c1) / (jnp.sqrt(v_ * c2) + ADAM_EPS) + ADAM_WD * w_ref[...])
        mo_ref[...] = m_
        vo_ref[...] = v_

    blk = pl.BlockSpec((tr, cols), lambda i: (i, 0))
    outs = pl.pallas_call(
        body, name=name, grid=(rows // tr,), in_specs=[blk] * 4, out_specs=[blk] * 3,
        out_shape=[jax.ShapeDtypeStruct((rows, cols), F32)] * 3, compiler_params=_params("parallel"),
    )(*[a.reshape(rows, cols) for a in (w, g, m, v)])
    return [o.reshape(shape) for o in outs]


def _spread_rope(r):
    z = jnp.zeros_like(r[..., :32])
    return jnp.concatenate([r[..., :32], z, r[..., 32:], z], -1)


def _gather_rope(r):
    return jnp.concatenate([r[..., :32], r[..., 64:96]], -1)


def pad_w_uq(w):
    w = w.reshape(w.shape[0], -1, MLA_NOPE + MLA_ROPE)
    return jnp.concatenate([w[..., :MLA_NOPE], _spread_rope(w[..., MLA_NOPE:])], -1).reshape(w.shape[0], -1)


def unpad_w_uq(g):
    g = g.reshape(g.shape[0], -1, 2 * MLA_NOPE)
    return jnp.concatenate([g[..., :MLA_NOPE], _gather_rope(g[..., MLA_NOPE:])], -1).reshape(g.shape[0], -1)


def pad_w_down(w):
    lat = MLA_Q_LORA + MLA_KV_LORA
    return jnp.concatenate([w[:, :lat], _spread_rope(w[:, lat:])], -1)


def unpad_w_down(g):
    lat = MLA_Q_LORA + MLA_KV_LORA
    return jnp.concatenate([g[:, :lat], _gather_rope(g[:, lat:])], -1)


def _heads(arr, width, first=0, off=0, w=None):
    return HeadCols(arr, width, lambda p: first // ATT_G + p, off, w)


def mla_forward(h16, w, gq, gkv, tables):
    cos, sin = tables
    down = mm_nn("mla_down", h16, w["down"], [F32])[0]
    cq, ckv = rms_fwd("mla_rms", down, gq, gkv)
    q = mm_nn("mla_uq", cq, w["uq"], [F32])[0]
    kv = mm_nn("mla_ukv", ckv, w["ukv"], [BF16])[0]
    qr, kp = mla_prep_fwd("mla_prep", q, kv, down, cos, sin)
    scale = (MLA_NOPE + MLA_ROPE) ** -0.5
    o, lse = softmax_attn_fwd("mla_attn", "mla", _heads(qr, 256), _heads(kp, 256), _heads(kv, 256, off=128, w=128), scale)
    m = mm_nn("mla_wo", o, w["wo"], [F32])[0]
    return m, (down, cq, ckv, qr, kp, kv, o, lse)


def mla_backward(du16, h16, saved, w, gq, gkv, tables):
    cos, sin = tables
    down, cq, ckv, qr, kp, kv, o, lse = saved
    scale = (MLA_NOPE + MLA_ROPE) ** -0.5
    g = {"wo": mm_tn("mla_dwo", o, du16, "row", w["wo"].R, w["wo"].C)}
    do = mm_nt("mla_do", du16, w["wo"], BF16)
    dq, dk, dv = softmax_attn_bwd("mla_attn_bwd", "mla", _heads(qr, 256), _heads(kp, 256), _heads(kv, 256, off=128, w=128),
                                  _heads(o, 128), _heads(do, 128), lse, scale)
    dq16, dkv16, dkr = mla_prep_bwd("mla_prep_bwd", dq, dk, dv, cos, sin)
    g["uq"] = mm_tn("mla_duq", cq, dq16, "col", w["uq"].R, w["uq"].C)
    dcq = mm_nt("mla_dcq", dq16, w["uq"], F32)
    g["ukv"] = mm_tn("mla_dukv", ckv, dkv16, "col", w["ukv"].R, w["ukv"].C)
    dckv = mm_nt("mla_dckv", dkv16, w["ukv"], F32)
    ddown, dgq, dgkv = rms_bwd("mla_rms_bwd", down, dcq, dckv, dkr, gq, gkv)
    g["down"] = mm_tn("mla_ddown", h16, ddown, "row", w["down"].R, w["down"].C)
    dh = mm_nt("mla_dh", ddown, w["down"], F32)
    return dh, g, (dgq, dgkv)


def qkv_forward(kind, h16, w, bias=None):
    qkv = mm_nn(kind + "_qkv", h16, w["qkv"], [BF16])[0]
    q, k, v = _heads(qkv, 128), _heads(qkv, 128, HEADS), _heads(qkv, 128, 2 * HEADS)
    scale = HEAD_DIM ** -0.5
    if kind == "sb":
        o, lse = stick_attn_fwd("sb_attn", q, k, v, scale), None
    else:
        o, lse = softmax_attn_fwd("ca_attn", "ca", q, k, v, scale, bias)
    m = mm_nn(kind + "_wo", o, w["wo"], [F32])[0]
    return m, (qkv, o, lse)


def qkv_backward(kind, du16, h16, saved, w, bias=None):
    qkv, o, lse = saved
    q, k, v = _heads(qkv, 128), _heads(qkv, 128, HEADS), _heads(qkv, 128, 2 * HEADS)
    scale = HEAD_DIM ** -0.5
    g = {"wo": mm_tn(kind + "_dwo", o, du16, "row", w["wo"].R, w["wo"].C)}
    do = mm_nt(kind + "_do", du16, w["wo"], BF16)
    dbias = None
    if kind == "sb":
        dq, dk, dv = stick_attn_bwd("sb_attn_bwd", q, k, v, _heads(do, 128), scale)
    else:
        dq, dk, dv, dbias = softmax_attn_bwd("ca_attn_bwd", "ca", q, k, v, _heads(o, 128), _heads(do, 128), lse,
                                             scale, bias)
    dqkv = jnp.concatenate([dq, dk, dv], axis=1).astype(BF16)
    g["qkv"] = mm_tn(kind + "_dqkv", h16, dqkv, "col", w["qkv"].R, w["qkv"].C)
    dh = mm_nt(kind + "_dh", dqkv, w["qkv"], F32)
    return dh, g, dbias


def mlp_forward(h16, w):
    a, z = mm_nn("ffn_in", h16, w["w_in"], [F32, BF16], epilogue=_relu2_epilogue)
    f = mm_nn("ffn_out", z, w["w_out"], [F32])[0]
    return f, (a, z)


def mlp_backward(du16, h16, saved, w):
    a, z = saved
    da = mm_nt("ffn_da", du16, w["w_out"], BF16, epilogue=_mulrelu_epilogue, extra=a)
    g = {"w_out": mm_tn("ffn_dwout", z, du16, "row", w["w_out"].R, w["w_out"].C)}
    dh = mm_nt("ffn_dh", da, w["w_in"], F32)
    g["w_in"] = mm_tn("ffn_dwin", h16, da, "col", w["w_in"].R, w["w_in"].C)
    return dh, g


WEIGHTS = ("ln_mix_g", "ln_mix_b", "ln_ffn_g", "ln_ffn_b", "ffn_w_in", "ffn_w_out", "mla_w_down", "mla_q_norm_g",
           "mla_w_uq", "mla_kv_norm_g", "mla_w_ukv", "mla_w_o", "sb_w_qkv", "sb_w_o", "ca_w_qkv", "ca_rel_bias",
           "ca_w_o")
MIXERS = ("mla", "sb", "ca")
LAYER_WEIGHTS = {
    "mla": (("down", "mla_w_down", "row"), ("uq", "mla_w_uq", "col"), ("ukv", "mla_w_ukv", "col"),
            ("wo", "mla_w_o", "row")),
    "sb": (("qkv", "sb_w_qkv", "col"), ("wo", "sb_w_o", "row")),
    "ca": (("qkv", "ca_w_qkv", "col"), ("wo", "ca_w_o", "row")),
    "ffn": (("w_in", "ffn_w_in", "col"), ("w_out", "ffn_w_out", "row")),
}
PAD = {"mla_w_down": pad_w_down, "mla_w_uq": pad_w_uq}
UNPAD = {"mla_w_down": unpad_w_down, "mla_w_uq": unpad_w_uq}


def _pack_rows(vectors):
    flat = jnp.concatenate([v.reshape(-1) for v in vectors])
    n = flat.shape[0]
    rows = -(-n // 1024) * 8
    offsets = np.cumsum([0] + [int(np.prod(v.shape)) for v in vectors])
    return jnp.pad(flat, (0, rows * 128 - n)).reshape(rows, 128), offsets


def _part(i, part):
    group, idx = (MIXERS[i % 3], i // 3) if part == "mix" else ("ffn", i)
    return [(key, name, how, idx) for key, name, how in LAYER_WEIGHTS[group]]


def kernel(x, ln_mix_g, ln_mix_b, ln_ffn_g, ln_ffn_b, ffn_w_in, ffn_w_out, mla_w_down, mla_q_norm_g, mla_w_uq, mla_kv_norm_g, mla_w_ukv, mla_w_o, sb_w_qkv, sb_w_o, ca_w_qkv, ca_rel_bias, ca_w_o, loss_target, m_ln_mix_g, m_ln_mix_b, m_ln_ffn_g, m_ln_ffn_b, m_ffn_w_in, m_ffn_w_out, m_mla_w_down, m_mla_q_norm_g, m_mla_w_uq, m_mla_kv_norm_g, m_mla_w_ukv, m_mla_w_o, m_sb_w_qkv, m_sb_w_o, m_ca_w_qkv, m_ca_rel_bias, m_ca_w_o, v_ln_mix_g, v_ln_mix_b, v_ln_ffn_g, v_ln_ffn_b, v_ffn_w_in, v_ffn_w_out, v_mla_w_down, v_mla_q_norm_g, v_mla_w_uq, v_mla_kv_norm_g, v_mla_w_ukv, v_mla_w_o, v_sb_w_qkv, v_sb_w_o, v_ca_w_qkv, v_ca_rel_bias, v_ca_w_o):
    w = dict(zip(WEIGHTS, (ln_mix_g, ln_mix_b, ln_ffn_g, ln_ffn_b, ffn_w_in, ffn_w_out, mla_w_down, mla_q_norm_g,
                           mla_w_uq, mla_kv_norm_g, mla_w_ukv, mla_w_o, sb_w_qkv, sb_w_o, ca_w_qkv, ca_rel_bias,
                           ca_w_o)))
    mom = dict(zip(WEIGHTS, (m_ln_mix_g, m_ln_mix_b, m_ln_ffn_g, m_ln_ffn_b, m_ffn_w_in, m_ffn_w_out, m_mla_w_down,
                             m_mla_q_norm_g, m_mla_w_uq, m_mla_kv_norm_g, m_mla_w_ukv, m_mla_w_o, m_sb_w_qkv,
                             m_sb_w_o, m_ca_w_qkv, m_ca_rel_bias, m_ca_w_o)))
    var = dict(zip(WEIGHTS, (v_ln_mix_g, v_ln_mix_b, v_ln_ffn_g, v_ln_ffn_b, v_ffn_w_in, v_ffn_w_out, v_mla_w_down,
                             v_mla_q_norm_g, v_mla_w_uq, v_mla_kv_norm_g, v_mla_w_ukv, v_mla_w_o, v_sb_w_qkv,
                             v_sb_w_o, v_ca_w_qkv, v_ca_rel_bias, v_ca_w_o)))
    S, D = x.shape[1], x.shape[2]
    xi, yi, ci = _place()
    core = ci.astype(jnp.int32).reshape(1)
    chip = (2 * xi + yi).astype(jnp.int32).reshape(1)
    me = 4 * xi + 2 * yi + ci
    tables = rope_tables(S)
    n_mla = mla_w_down.shape[0]
    lat = MLA_Q_LORA // N_DEV

    gains = jnp.pad(jnp.stack([mla_q_norm_g.reshape(-1), mla_kv_norm_g.reshape(-1)]), ((0, 6), (0, 128 - n_mla * lat)))
    gains = all_gather("ag_gains", [gains])[0]

    def full_gain(row, slot):
        return gains[:, row, slot * lat:(slot + 1) * lat].reshape(-1)

    def post_gather(i, part):
        kind = MIXERS[i % 3]
        specs = _part(i, part)
        shards = [PAD.get(name, lambda a: a)(w[name][idx]).astype(BF16) for _, name, _, idx in specs]
        if part == "ffn":
            wants = (kind + "_attn", kind + "_wo")
        else:
            wants = ("ffn_in", "ffn_out") if i > 0 else (None, None)
        return specs, gather_jobs(f"ag_{part}{i}", shards, *wants)

    def gathered(specs, future):
        return {key: Weight(how, g) for (key, _, how, _), g in zip(specs, future.get())}

    def post_scatter(i, part, g):
        kind = MIXERS[i % 3]
        specs = _part(i, part)
        n = len(specs)
        if part == "ffn":
            wants = (kind + "_dwo", [(list(range(n)), kind + "_attn_bwd")])
        elif i > 0:
            wants = ("ffn_da", [(list(range(n - 1)), "ffn_dwout"), ([n - 1], "ffn_dh")])
        else:
            wants = (None, [(list(range(n)), None)])
        return specs, scatter_jobs(f"rs_{part}{i}", [g[key] for key, _, _, _ in specs], core, chip, *wants)

    SCHED.pending.clear()
    bias = rel_bias_blocks("ca_bias", ca_rel_bias[0], _att_tiles("ca", S)[1])

    h, h16 = x[0], x[0].astype(BF16)
    saved, layers = [], []
    next_mix = post_gather(0, "mix")
    for i in range(DEPTH):
        kind, slot = MIXERS[i % 3], i // 3
        lw = gathered(*next_mix)
        ffn = post_gather(i, "ffn")
        if kind == "mla":
            mix, s_mix = mla_forward(h16, lw, full_gain(0, slot), full_gain(1, slot), tables)
        else:
            mix, s_mix = qkv_forward(kind, h16, lw, bias if kind == "ca" else None)
        y, y16, xh1, rs1 = ln_fwd("ln_mix", h, mix, ln_mix_g[i], ln_mix_b[i])
        lw.update(gathered(*ffn))
        if i + 1 < DEPTH:
            next_mix = post_gather(i + 1, "mix")
        f, s_mlp = mlp_forward(y16, lw)
        y2, y2_16, xh2, rs2 = ln_fwd("ln_ffn", y, f, ln_ffn_g[i], ln_ffn_b[i])
        saved.append((h16, s_mix, xh1, rs1, y16, s_mlp, xh2, rs2))
        layers.append(lw)
        h, h16 = y2, y2_16
    sq, dy = loss_fwd_bwd("loss", h, loss_target[0])
    loss = 0.5 / D * lax.psum(sq[0, 0], ("x", "y", "c"))

    ga, gb = dy, None
    grads = {name: [None] * w[name].shape[0] for name in WEIGHTS}
    dbias = None
    scattered = []
    for i in reversed(range(DEPTH)):
        kind, slot = MIXERS[i % 3], i // 3
        lw = layers[i]
        h16_in, s_mix, xh1, rs1, y16, s_mlp, xh2, rs2 = saved[i]
        du, du16, grads["ln_ffn_g"][i], grads["ln_ffn_b"][i] = ln_bwd("ln_ffn_bwd", ga, gb, xh2, rs2, ln_ffn_g[i])
        dh_mlp, g_mlp = mlp_backward(du16, y16, s_mlp, lw)
        scattered.append(post_scatter(i, "ffn", g_mlp))
        du, du16, grads["ln_mix_g"][i], grads["ln_mix_b"][i] = ln_bwd("ln_mix_bwd", du, dh_mlp, xh1, rs1, ln_mix_g[i])
        if kind == "mla":
            dh_mix, g_mix, (dgq, dgkv) = mla_backward(du16, h16_in, s_mix, lw, full_gain(0, slot), full_gain(1, slot),
                                                      tables)
            grads["mla_q_norm_g"][slot], grads["mla_kv_norm_g"][slot] = dgq, dgkv
        else:
            dh_mix, g_mix, db = qkv_backward(kind, du16, h16_in, s_mix, lw, bias if kind == "ca" else None)
            dbias = db if kind == "ca" else dbias
        scattered.append(post_scatter(i, "mix", g_mix))
        ga, gb = du, dh_mix
    grad_x = axpy("grad_x", ga, gb)[None]
    SCHED.flush()
    for specs, future in scattered:
        for (_, name, _, idx), g in zip(specs, future.get()):
            grads[name][idx] = UNPAD.get(name, lambda a: a)(g)
    grads["ca_rel_bias"][0] = rel_bias_grad("ca_bias_grad", dbias)

    small = ("ln_mix_g", "ln_mix_b", "ln_ffn_g", "ln_ffn_b", "ca_rel_bias", "mla_q_norm_g", "mla_kv_norm_g")
    packed, offsets = _pack_rows([g for name in small for g in grads[name]])
    total = sum_devices("sum_small", all_gather("ag_small", [packed])[0]).reshape(-1)
    pos = 0
    for name in small:
        for idx, g in enumerate(grads[name]):
            full = total[offsets[pos]:offsets[pos + 1]]
            pos += 1
            if name in ("mla_q_norm_g", "mla_kv_norm_g"):
                full = lax.dynamic_slice(full, (me * lat,), (lat,))
            grads[name][idx] = full.reshape(w[name].shape[1:])

    g_out, d_out, m_out, v_out = [], [], [], []
    for name in WEIGHTS:
        g = jnp.stack(grads[name])
        delta, new_m, new_v = adamw("adamw_" + name, w[name], g, mom[name], var[name])
        g_out.append(g)
        d_out.append(delta)
        m_out.append(new_m)
        v_out.append(new_v)
    return (loss, grad_x, *g_out, *d_out, *m_out, *v_out)
```

```python
import functools
import math

import numpy as np
import jax
import jax.numpy as jnp
from jax import lax
from jax.experimental import pallas as pl
from jax.experimental.pallas import tpu as pltpu

F32 = jnp.float32
BF16 = jnp.bfloat16
MESH = pl.DeviceIdType.MESH
N_DEV = 8

DEPTH = 4
CHUNK = 64
CHUNK_SHIFT = 6
HEADS = 16
HEAD_DIM = 128
MLA_Q_LORA = 512
MLA_KV_LORA = 512
MLA_NOPE = 128
MLA_ROPE = 64
ROPE_THETA = 10000.0
CA_LEFT_CHUNKS = 8
REL_CLIP_LEFT = 128
REL_TABLE = REL_CLIP_LEFT + CHUNK
LN_EPS = 1e-5
RMS_EPS = 1e-6
ALPHA = (2.0 * DEPTH) ** 0.25
NEG = -1e30
ADAM_LR = 0.001
ADAM_B1 = 0.9
ADAM_B2 = 0.999
ADAM_EPS = 1e-08
ADAM_WD = 0.01
ADAM_STEP = 10

V7X_VMEM_BYTES = 64 * 1024 * 1024
VMEM_LIMIT = V7X_VMEM_BYTES - 8 * 1024 * 1024
ATT_TQ = 512
ATT_TK = 256
ATT_G = 2


def _params(*sem):
    return pltpu.CompilerParams(dimension_semantics=sem if sem else None, vmem_limit_bytes=VMEM_LIMIT)


HBM = pl.BlockSpec(memory_space=pl.ANY)


class Job:
    def __init__(self, name, want, operands, out_shape, n_copies, copies, done, aliases=None):
        self.name, self.want, self.operands, self.out_shape = name, want, list(operands), list(out_shape)
        self.n_copies, self.copies, self.done, self.aliases = n_copies, copies, done, dict(aliases or {})

    def sems(self):
        return [pltpu.SemaphoreType.DMA((self.n_copies,)), pltpu.SemaphoreType.DMA((self.n_copies,))]


class Scheduler:
    def __init__(self):
        self.pending = []

    def post(self, job):
        self.pending.append(job)

    def take(self, name):
        mine = [job for job in self.pending if job.want is not None and job.want in name]
        self.pending = [job for job in self.pending if job not in mine]
        return mine

    def flush(self):
        while self.pending:
            job = self.pending.pop(0)
            n_in, n_out = len(job.operands), len(job.out_shape)

            def body(*refs, job=job, n_in=n_in, n_out=n_out):
                cps = job.copies(refs[:n_in], refs[n_in:n_in + n_out], refs[-2], refs[-1])
                for cp in cps:
                    cp.start()
                for cp in cps:
                    cp.wait()

            outs = pl.pallas_call(
                body, name=job.name, in_specs=[HBM] * n_in, out_specs=[HBM] * n_out, out_shape=job.out_shape,
                scratch_shapes=job.sems(), input_output_aliases=job.aliases)(*job.operands)
            job.done(list(outs))


SCHED = Scheduler()


def _call(body, operands, *, name, grid, in_specs, out_specs, out_shape, scratch_shapes=(), semantics):
    jobs = SCHED.take(name)
    if not jobs:
        return list(pl.pallas_call(
            body, name=name, grid=grid, in_specs=list(in_specs), out_specs=list(out_specs), out_shape=list(out_shape),
            scratch_shapes=list(scratch_shapes), compiler_params=_params(*semantics))(*operands))
    n_in, n_out, n_scr = len(operands), len(out_shape), len(scratch_shapes)
    j_in = np.cumsum([0] + [len(job.operands) for job in jobs])
    j_out = np.cumsum([0] + [len(job.out_shape) for job in jobs])
    a, b = n_in, n_in + int(j_in[-1])
    c, d = b + n_out, b + n_out + int(j_out[-1])

    def carrying(*refs):
        def copies():
            sems = refs[d + n_scr:]
            return [cp for k, job in enumerate(jobs)
                    for cp in job.copies(refs[a + j_in[k]:a + j_in[k + 1]], refs[c + j_out[k]:c + j_out[k + 1]],
                                         sems[2 * k], sems[2 * k + 1])]

        ids = [pl.program_id(k) for k in range(len(grid))]
        first = functools.reduce(jnp.logical_and, [i == 0 for i in ids])
        last = functools.reduce(jnp.logical_and, [i == g - 1 for i, g in zip(ids, grid)])

        @pl.when(first)
        def _():
            for cp in copies():
                cp.start()

        body(*refs[:a], *refs[b:c], *refs[d:d + n_scr])

        @pl.when(last)
        def _():
            for cp in copies():
                cp.wait()

    aliases = {n_in + int(j_in[k]) + i: n_out + int(j_out[k]) + o for k, job in enumerate(jobs)
               for i, o in job.aliases.items()}
    outs = pl.pallas_call(
        carrying, name=name + "_carry", grid=grid, in_specs=list(in_specs) + [HBM] * int(j_in[-1]),
        out_specs=list(out_specs) + [HBM] * int(j_out[-1]),
        out_shape=list(out_shape) + [s for job in jobs for s in job.out_shape],
        scratch_shapes=list(scratch_shapes) + [s for job in jobs for s in job.sems()],
        input_output_aliases=aliases,
        compiler_params=_params(*(["arbitrary"] * len(grid))))(*operands, *[o for job in jobs for o in job.operands])
    for k, job in enumerate(jobs):
        job.done(list(outs[n_out + int(j_out[k]):n_out + int(j_out[k + 1])]))
    return list(outs[:n_out])


def _matmul(name, a, b, *, contract, grid, a_spec, b_spec, o_specs, out_shape, acc_shape,
            epilogue=None, extra=(), extra_specs=()):
    nk = grid[2]
    n_extra = len(extra)
    n_out = len(out_shape)

    def finish(acc, e_refs, o_refs):
        outs = epilogue(acc, *[e[...] for e in e_refs]) if epilogue else (acc,)
        for o_ref, val in zip(o_refs, outs):
            o_ref[...] = val.astype(o_ref.dtype)

    def product(a_ref, b_ref):
        return lax.dot_general(a_ref[...], b_ref[...], (contract, ((), ())), preferred_element_type=F32)

    def body_single(*refs):
        finish(product(refs[0], refs[1]), refs[2:2 + n_extra], refs[2 + n_extra:2 + n_extra + n_out])

    def body(*refs):
        a_ref, b_ref = refs[0], refs[1]
        acc_ref = refs[-1]
        k = pl.program_id(2)

        @pl.when(k == 0)
        def _():
            acc_ref[...] = jnp.zeros_like(acc_ref)

        acc_ref[...] += product(a_ref, b_ref)

        @pl.when(k == nk - 1)
        def _():
            finish(acc_ref[...], refs[2:2 + n_extra], refs[2 + n_extra:2 + n_extra + n_out])

    return _call(
        body_single if nk == 1 else body, [a, b, *extra], name=name, grid=grid,
        in_specs=[a_spec, b_spec, *extra_specs], out_specs=o_specs, out_shape=out_shape,
        scratch_shapes=[] if nk == 1 else [pltpu.VMEM(acc_shape, F32)],
        semantics=("parallel", "parallel", "arbitrary"))


MATMUL_BLOCK_BYTES = 40 * 1024 * 1024
MAX_TK = 2048
MULTI_TK = 512


def _fit_tn(n, tm, tk, nk, out_bytes):
    cands = sorted({n} | {t for t in range(128, n, 128) if n % t == 0}, reverse=True)
    for tn in cands:
        need = 2 * 2 * (tm * tk + tk * tn) + 2 * tm * tn * out_bytes + (tm * tn * 4 if nk > 1 else 0) + tm * tn * 4
        if need <= MATMUL_BLOCK_BYTES:
            return tn
    return cands[-1]


def _itemsize(dtypes):
    return sum(jnp.dtype(d).itemsize for d in dtypes)


def _tile(n, pref):
    if n <= pref:
        return n
    t = pref
    while t >= 128:
        if n % t == 0 and t % 128 == 0:
            return t
        t -= 128
    return n


class Weight:
    def __init__(self, kind, arr):
        self.kind = kind
        self.arr = arr
        self.R, self.C = arr.shape[1], arr.shape[2]

    @property
    def two_d(self):
        return self.arr.reshape(N_DEV * self.R, self.C)


def mm_nn(name, a, w, out_dtypes, epilogue=None, transposed=()):
    M, K = a.shape
    tm = M
    tk = K if K <= MAX_TK else _tile(K, MULTI_TK)
    nk = K // tk
    if w.kind == "row":
        b = w.two_d
        N = w.C
        tn = _fit_tn(N, tm, tk, nk, _itemsize(out_dtypes))
        b_spec = pl.BlockSpec((tk, tn), lambda i, j, k: (k, j))
    else:
        b = w.arr
        N = N_DEV * w.C
        tn = _fit_tn(w.C, tm, tk, nk, _itemsize(out_dtypes))
        per = w.C // tn
        b_spec = pl.BlockSpec((None, tk, tn), lambda i, j, k: (j // per, k, j % per))
    grid = (M // tm, N // tn, nk)
    flip = [t < len(transposed) and transposed[t] for t in range(len(out_dtypes))]
    return _matmul(
        name, a, b, contract=((1,), (0,)), grid=grid,
        a_spec=pl.BlockSpec((tm, tk), lambda i, j, k: (i, k)), b_spec=b_spec,
        o_specs=[pl.BlockSpec((tn, tm), lambda i, j, k: (j, i)) if f else pl.BlockSpec((tm, tn), lambda i, j, k: (i, j))
                 for f in flip],
        out_shape=[jax.ShapeDtypeStruct((N, M) if f else (M, N), d) for f, d in zip(flip, out_dtypes)],
        acc_shape=(tm, tn), epilogue=epilogue)


def mm_nt(name, dy, w, out_dtype, epilogue=None, extra=None):
    M, N = dy.shape
    tm = M
    out_bytes = jnp.dtype(out_dtype).itemsize + (0 if extra is None else extra.dtype.itemsize)
    if w.kind == "row":
        b = w.two_d
        kin = N_DEV * w.R
        tk = N if N <= MAX_TK else _tile(N, MULTI_TK)
        tn = _fit_tn(kin, tm, tk, N // tk, out_bytes)
        b_spec = pl.BlockSpec((tn, tk), lambda i, j, k: (j, k))
    else:
        b = w.arr
        kin = w.R
        tk = _tile(w.C, MULTI_TK)
        per = w.C // tk
        tn = _fit_tn(kin, tm, tk, N // tk, out_bytes)
        b_spec = pl.BlockSpec((None, tn, tk), lambda i, j, k: (k // per, j, k % per))
    grid = (M // tm, kin // tn, N // tk)
    o_spec = pl.BlockSpec((tm, tn), lambda i, j, k: (i, j))
    return _matmul(
        name, dy, b, contract=((1,), (1,)), grid=grid,
        a_spec=pl.BlockSpec((tm, tk), lambda i, j, k: (i, k)), b_spec=b_spec, o_specs=[o_spec],
        out_shape=[jax.ShapeDtypeStruct((M, kin), out_dtype)], acc_shape=(tm, tn), epilogue=epilogue,
        extra=() if extra is None else (extra,), extra_specs=() if extra is None else (o_spec,))[0]


TRANSPOSE_TILE = 512


def transpose(name, x):
    S, n = x.shape
    ts, tn = _tile(S, TRANSPOSE_TILE), _tile(n, TRANSPOSE_TILE)

    def body(x_ref, o_ref):
        o_ref[...] = x_ref[...].T

    return pl.pallas_call(
        body, name=name, grid=(S // ts, n // tn), in_specs=[pl.BlockSpec((ts, tn), lambda i, j: (i, j))],
        out_specs=pl.BlockSpec((tn, ts), lambda i, j: (j, i)), out_shape=jax.ShapeDtypeStruct((n, S), x.dtype),
        compiler_params=_params("parallel", "parallel"),
    )(x)


def mm_tn(name, x, dy, kind, R, C, transposed=False):
    if not transposed:
        x = transpose(name + "_t", x)
    kin, S = x.shape
    N = dy.shape[1]
    tk = S if S <= MAX_TK else _tile(S, MULTI_TK)
    nk = S // tk
    if kind == "col":
        tm = kin
        tn = _fit_tn(C, tm, tk, nk, 2)
        per = C // tn
        grid = (1, N // tn, nk)
        o_spec = pl.BlockSpec((None, None, tm, tn), lambda i, j, k: ((j // per) % 2, (j // per) // 2, 0, j % per))
    else:
        tm = R
        tn = _fit_tn(N, tm, tk, nk, 2)
        grid = (N_DEV, N // tn, nk)
        o_spec = pl.BlockSpec((None, None, tm, tn), lambda i, j, k: (i % 2, i // 2, 0, j))
    return _matmul(
        name, x, dy, contract=((1,), (0,)), grid=grid,
        a_spec=pl.BlockSpec((tm, tk), lambda i, j, k: (i, k)),
        b_spec=pl.BlockSpec((tk, tn), lambda i, j, k: (k, j)), o_specs=[o_spec],
        out_shape=[jax.ShapeDtypeStruct((2, 4, R, C), BF16)], acc_shape=(tm, tn))[0]


def _relu2_epilogue(acc):
    r = jnp.maximum(acc, 0.0)
    z = (r * r).astype(BF16)
    return acc, z, z.T


def _mulrelu_epilogue(acc, a):
    return (acc * (2.0 * jnp.maximum(a, 0.0)),)


ROW_TILE = 256


def ln_fwd(name, h, m, g, b):
    S, D = h.shape
    ts = _tile(S, ROW_TILE)

    def body(h_ref, m_ref, g_ref, b_ref, y_ref, y16_ref, yt_ref, xh_ref, rs_ref):
        u = ALPHA * h_ref[...] + m_ref[...]
        mu = jnp.mean(u, axis=-1, keepdims=True)
        d = u - mu
        var = jnp.mean(d * d, axis=-1, keepdims=True)
        rstd = lax.rsqrt(var + LN_EPS)
        xh = d * rstd
        y = xh * g_ref[...] + b_ref[...]
        y16 = y.astype(BF16)
        y_ref[...] = y
        y16_ref[...] = y16
        yt_ref[...] = y16.T
        xh_ref[...] = xh
        rs_ref[...] = jnp.broadcast_to(rstd, rs_ref.shape)

    row = pl.BlockSpec((ts, D), lambda i: (i, 0))
    vec = pl.BlockSpec((1, D), lambda i: (0, 0))
    return pl.pallas_call(
        body, name=name, grid=(S // ts,), in_specs=[row, row, vec, vec],
        out_specs=[row, row, pl.BlockSpec((D, ts), lambda i: (0, i)), row, pl.BlockSpec((ts, 128), lambda i: (i, 0))],
        out_shape=[jax.ShapeDtypeStruct((S, D), F32), jax.ShapeDtypeStruct((S, D), BF16),
                   jax.ShapeDtypeStruct((D, S), BF16), jax.ShapeDtypeStruct((S, D), F32),
                   jax.ShapeDtypeStruct((S, 128), F32)],
        compiler_params=_params("parallel"),
    )(h, m, g.reshape(1, D), b.reshape(1, D))


def ln_bwd(name, ga, gb, xhat, rstd, g):
    S, D = xhat.shape
    ts = _tile(S, ROW_TILE)
    two = gb is not None

    def body(*refs):
        if two:
            ga_ref, gb_ref, xh_ref, rs_ref, g_ref, du_ref, du16_ref, dg_ref, db_ref = refs
            dy = ALPHA * ga_ref[...] + gb_ref[...]
        else:
            ga_ref, xh_ref, rs_ref, g_ref, du_ref, du16_ref, dg_ref, db_ref = refs
            dy = ga_ref[...]
        xh = xh_ref[...]

        @pl.when(pl.program_id(0) == 0)
        def _():
            dg_ref[...] = jnp.zeros_like(dg_ref)
            db_ref[...] = jnp.zeros_like(db_ref)

        dg_ref[...] += jnp.sum(dy * xh, axis=0, keepdims=True)
        db_ref[...] += jnp.sum(dy, axis=0, keepdims=True)
        dxh = dy * g_ref[...]
        m1 = jnp.mean(dxh, axis=-1, keepdims=True)
        m2 = jnp.mean(dxh * xh, axis=-1, keepdims=True)
        du = rs_ref[:, 0:1] * (dxh - m1 - xh * m2)
        du_ref[...] = du
        du16_ref[...] = du.astype(BF16)

    row = pl.BlockSpec((ts, D), lambda i: (i, 0))
    vec = pl.BlockSpec((1, D), lambda i: (0, 0))
    stat = pl.BlockSpec((ts, 128), lambda i: (i, 0))
    ins = [ga, gb, xhat, rstd, g.reshape(1, D)] if two else [ga, xhat, rstd, g.reshape(1, D)]
    in_specs = [row, row, row, stat, vec] if two else [row, row, stat, vec]
    return pl.pallas_call(
        body, name=name, grid=(S // ts,), in_specs=in_specs, out_specs=[row, row, vec, vec],
        out_shape=[jax.ShapeDtypeStruct((S, D), F32), jax.ShapeDtypeStruct((S, D), BF16),
                   jax.ShapeDtypeStruct((1, D), F32), jax.ShapeDtypeStruct((1, D), F32)],
        compiler_params=_params("arbitrary"),
    )(*ins)


def loss_fwd_bwd(name, y, target):
    S, D = y.shape
    ts = _tile(S, ROW_TILE)

    def body(y_ref, t_ref, l_ref, dy_ref):
        @pl.when(pl.program_id(0) == 0)
        def _():
            l_ref[...] = jnp.zeros_like(l_ref)

        e = y_ref[...] - t_ref[...]
        l_ref[...] += jnp.sum(e * e)
        dy_ref[...] = e * (1.0 / D)

    row = pl.BlockSpec((ts, D), lambda i: (i, 0))
    return pl.pallas_call(
        body, name=name, grid=(S // ts,), in_specs=[row, row],
        out_specs=[pl.BlockSpec((1, 128), lambda i: (0, 0)), row],
        out_shape=[jax.ShapeDtypeStruct((1, 128), F32), jax.ShapeDtypeStruct((S, D), F32)],
        compiler_params=_params("arbitrary"),
    )(y, target)


def axpy(name, ga, gb):
    S, D = ga.shape
    ts = _tile(S, ROW_TILE)

    def body(a_ref, b_ref, o_ref):
        o_ref[...] = ALPHA * a_ref[...] + b_ref[...]

    row = pl.BlockSpec((ts, D), lambda i: (i, 0))
    return pl.pallas_call(body, name=name, grid=(S // ts,), in_specs=[row, row], out_specs=row,
                          out_shape=jax.ShapeDtypeStruct((S, D), F32), compiler_params=_params("parallel"))(ga, gb)


def rms_fwd(name, down, gq, gkv):
    S = down.shape[0]
    ts = _tile(S, ROW_TILE)
    L = MLA_Q_LORA

    def body(d_ref, gq_ref, gkv_ref, q_ref, kv_ref):
        for lo, g_ref, o_ref in ((0, gq_ref, q_ref), (L, gkv_ref, kv_ref)):
            x = d_ref[:, lo:lo + L]
            r = lax.rsqrt(jnp.mean(x * x, axis=-1, keepdims=True) + RMS_EPS)
            o_ref[...] = (x * r * g_ref[...]).astype(BF16)

    vec = pl.BlockSpec((1, L), lambda i: (0, 0))
    out = pl.BlockSpec((ts, L), lambda i: (i, 0))
    return pl.pallas_call(
        body, name=name, grid=(S // ts,), in_specs=[pl.BlockSpec((ts, down.shape[1]), lambda i: (i, 0)), vec, vec],
        out_specs=[out, out], out_shape=[jax.ShapeDtypeStruct((S, L), BF16)] * 2, compiler_params=_params("parallel"),
    )(down, gq.reshape(1, L), gkv.reshape(1, L))


def rms_bwd(name, down, dq, dkv, dkr, gq, gkv):
    S, W = down.shape
    ts = _tile(S, ROW_TILE)
    L = MLA_Q_LORA

    def body(d_ref, dq_ref, dkv_ref, dkr_ref, gq_ref, gkv_ref, o_ref, dgq_ref, dgkv_ref):
        @pl.when(pl.program_id(0) == 0)
        def _():
            dgq_ref[...] = jnp.zeros_like(dgq_ref)
            dgkv_ref[...] = jnp.zeros_like(dgkv_ref)

        for lo, dy_ref, g_ref, dg_ref in ((0, dq_ref, gq_ref, dgq_ref), (L, dkv_ref, gkv_ref, dgkv_ref)):
            x = d_ref[:, lo:lo + L]
            dy = dy_ref[...]
            r = lax.rsqrt(jnp.mean(x * x, axis=-1, keepdims=True) + RMS_EPS)
            dg_ref[...] += jnp.sum(dy * x * r, axis=0, keepdims=True)
            dyg = dy * g_ref[...]
            dx = r * dyg - x * (r * r * r) * jnp.mean(dyg * x, axis=-1, keepdims=True)
            o_ref[:, lo:lo + L] = dx.astype(BF16)
        o_ref[:, 2 * L:] = dkr_ref[...].astype(BF16)

    vec = pl.BlockSpec((1, L), lambda i: (0, 0))
    lat = pl.BlockSpec((ts, L), lambda i: (i, 0))
    full = pl.BlockSpec((ts, W), lambda i: (i, 0))
    return pl.pallas_call(
        body, name=name, grid=(S // ts,),
        in_specs=[full, lat, lat, pl.BlockSpec((ts, 128), lambda i: (i, 0)), vec, vec],
        out_specs=[full, vec, vec],
        out_shape=[jax.ShapeDtypeStruct((S, W), BF16), jax.ShapeDtypeStruct((1, L), F32), jax.ShapeDtypeStruct((1, L), F32)],
        compiler_params=_params("arbitrary"),
    )(down, dq, dkv, dkr, gq.reshape(1, L), gkv.reshape(1, L))


def rope_tables(S):
    half = MLA_ROPE // 2
    inv = (np.float32(ROPE_THETA) ** (-np.arange(half, dtype=np.float32) / np.float32(half))).astype(np.float32)
    ang = np.arange(S, dtype=np.float32)[:, None] * inv[None, :]
    cos, sin = np.cos(ang).astype(np.float32), np.sin(ang).astype(np.float32)
    z = np.zeros_like(cos)
    return (jnp.asarray(np.concatenate([cos, z, cos, z], 1)), jnp.asarray(np.concatenate([-sin, z, sin, z], 1)))


def _rot(x, cos, sin):
    return x * cos + pltpu.roll(x, 64, 1) * sin


def mla_prep_fwd(name, q, kv, down, cos, sin):
    S = q.shape[0]
    ts = _tile(S, ROW_TILE)

    def body(q_ref, kv_ref, kr_ref, c_ref, s_ref, qo_ref, ko_ref):
        c, s = c_ref[...], s_ref[...]
        key = _rot(kr_ref[...], c, s).astype(BF16)
        for h in range(HEADS):
            lo = 256 * h
            qo_ref[:, lo:lo + 128] = q_ref[:, lo:lo + 128].astype(BF16)
            qo_ref[:, lo + 128:lo + 256] = _rot(q_ref[:, lo + 128:lo + 256], c, s).astype(BF16)
            ko_ref[:, lo:lo + 128] = kv_ref[:, lo:lo + 128]
            ko_ref[:, lo + 128:lo + 256] = key

    heads = pl.BlockSpec((ts, HEADS * 256), lambda i: (i, 0))
    tab = pl.BlockSpec((ts, 128), lambda i: (i, 0))
    return pl.pallas_call(
        body, name=name, grid=(S // ts,),
        in_specs=[heads, heads, pl.BlockSpec((ts, 128), lambda i: (i, 2 * MLA_Q_LORA // 128)), tab, tab],
        out_specs=[heads, heads], out_shape=[jax.ShapeDtypeStruct(q.shape, BF16)] * 2,
        compiler_params=_params("parallel"),
    )(q, kv, down, cos, sin)


def mla_prep_bwd(name, dq, dk, dv, cos, sin):
    S = dq.shape[0]
    ts = _tile(S, ROW_TILE)

    def body(dq_ref, dk_ref, dv_ref, c_ref, s_ref, qo_ref, kvo_ref, kr_ref):
        c, s = c_ref[...], -s_ref[...]
        key = jnp.zeros((ts, 128), F32)
        for h in range(HEADS):
            lo = 256 * h
            qo_ref[:, lo:lo + 128] = dq_ref[:, lo:lo + 128].astype(BF16)
            qo_ref[:, lo + 128:lo + 256] = _rot(dq_ref[:, lo + 128:lo + 256], c, s).astype(BF16)
            kvo_ref[:, lo:lo + 128] = dk_ref[:, lo:lo + 128].astype(BF16)
            kvo_ref[:, lo + 128:lo + 256] = dv_ref[:, 128 * h:128 * h + 128].astype(BF16)
            key = key + dk_ref[:, lo + 128:lo + 256]
        kr_ref[...] = _rot(key, c, s)

    heads = pl.BlockSpec((ts, HEADS * 256), lambda i: (i, 0))
    tab = pl.BlockSpec((ts, 128), lambda i: (i, 0))
    return pl.pallas_call(
        body, name=name, grid=(S // ts,),
        in_specs=[heads, heads, pl.BlockSpec((ts, HEADS * 128), lambda i: (i, 0)), tab, tab],
        out_specs=[heads, heads, tab],
        out_shape=[jax.ShapeDtypeStruct(dq.shape, BF16), jax.ShapeDtypeStruct(dq.shape, BF16),
                   jax.ShapeDtypeStruct((S, 128), F32)],
        compiler_params=_params("parallel"),
    )(dq, dk, dv, cos, sin)


def _dot_nt(a, b):
    return lax.dot_general(a, b, (((1,), (1,)), ((), ())), preferred_element_type=F32)


def _dot_tn(a, b):
    return lax.dot_general(a, b, (((0,), (0,)), ((), ())), preferred_element_type=F32)


def _dot(a, b):
    return jnp.dot(a, b, preferred_element_type=F32)


def _positions(i, j, TQ, TK):
    row = i * TQ + lax.broadcasted_iota(jnp.int32, (TQ, TK), 0)
    col = j * TK + lax.broadcasted_iota(jnp.int32, (TQ, TK), 1)
    return row, col


def _softmax_mask(mode, row, col):
    rc, cc = row >> CHUNK_SHIFT, col >> CHUNK_SHIFT
    if mode == "mla":
        return cc <= rc
    return (cc <= rc) & (cc >= rc - CA_LEFT_CHUNKS)


def _key_blocks(mode, i, TQ, TK):
    per = TQ // TK
    if mode == "ca":
        lo = jnp.maximum(i - (CA_LEFT_CHUNKS * CHUNK) // TK, 0)
        return lo, 0, i - lo + 1
    return 0, i * per, per


class HeadCols:
    def __init__(self, arr, width, index, off=0, w=None):
        self.arr, self.width, self.index, self.off = arr, width, index, off
        self.w = width if w is None else w

    def rows(self, T):
        return pl.BlockSpec((T, ATT_G * self.width), lambda p, i: (i, self.index(p)))

    def full(self, S):
        return pl.BlockSpec((S, ATT_G * self.width), lambda p, i: (0, self.index(p)))

    def lanes(self, g):
        lo = g * self.width + self.off
        return slice(lo, lo + self.w)


def _att_tiles(mode, S):
    tk = min(ATT_TK, S)
    return (tk if mode == "ca" else min(ATT_TQ, S)), tk


def _walk(lo, n, per, step, carry, descending=False):
    tail = [lo + n + d for d in range(per)]
    if descending:
        for j in reversed(tail):
            carry = step(j, carry, True)
        return lax.fori_loop(0, n, lambda t, c: step(lo + n - 1 - t, c, False), carry)
    carry = lax.fori_loop(0, n, lambda t, c: step(lo + t, c, False), carry)
    for j in tail:
        carry = step(j, carry, True)
    return carry


def softmax_attn_fwd(name, mode, q, k, v, scale, bias=None):
    S = q.arr.shape[0]
    TQ, TK = _att_tiles(mode, S)
    G, dv = ATT_G, v.w

    def body(*refs):
        if bias is not None:
            q_ref, k_ref, v_ref, b_ref, o_ref, lse_ref = refs
        else:
            q_ref, k_ref, v_ref, o_ref, lse_ref = refs
        i = pl.program_id(1)
        qs = [q_ref[:, q.lanes(g)] for g in range(G)]

        def block(g, j, carry, mask, ks):
            m, l, acc = carry
            s = _dot_nt(qs[g], k_ref[ks, k.lanes(g)]) * scale
            if bias is not None:
                s = s + b_ref[g, jnp.minimum(i - j, 2)]
            if mask is not None:
                s = jnp.where(mask, s, NEG)
            m_new = jnp.maximum(m, jnp.max(s, axis=-1, keepdims=True))
            a = jnp.exp(m - m_new)
            p = jnp.exp(s - m_new)
            if mask is not None:
                p = jnp.where(mask, p, 0.0)
            l = a * l + jnp.sum(p, axis=-1, keepdims=True)
            acc = a * acc + _dot(p.astype(BF16), v_ref[ks, v.lanes(g)])
            return m_new, l, acc

        def step(j, carry, masked):
            ks = pl.ds(pl.multiple_of(j * TK, TK), TK)
            mask = _softmax_mask(mode, *_positions(i, j, TQ, TK)) if masked or mode == "ca" else None
            return tuple(block(g, j, carry[g], mask, ks) for g in range(G))

        init = (jnp.full((TQ, 1), NEG, F32), jnp.zeros((TQ, 1), F32), jnp.zeros((TQ, dv), F32))
        lo, n, per = _key_blocks(mode, i, TQ, TK)
        if mode == "ca":
            out = lax.fori_loop(lo, lo + per, lambda j, c: step(j, c, True), (init,) * G)
        else:
            out = _walk(lo, n, per, step, (init,) * G)
        for g, (m, l, acc) in enumerate(out):
            o_ref[:, g * dv:(g + 1) * dv] = (acc / l).astype(BF16)
            lse_ref[:, g * 128:(g + 1) * 128] = jnp.broadcast_to(m + jnp.log(l), (TQ, 128))

    in_specs = [q.rows(TQ), k.full(S), v.full(S)]
    ins = [q.arr, k.arr, v.arr]
    if bias is not None:
        in_specs.append(pl.BlockSpec((G, 3, TK, TK), lambda p, i: (p, 0, 0, 0)))
        ins.append(bias)
    return _call(
        body, ins, name=name, grid=(HEADS // G, S // TQ), in_specs=in_specs,
        out_specs=[pl.BlockSpec((TQ, G * dv), lambda p, i: (i, p)), pl.BlockSpec((TQ, G * 128), lambda p, i: (i, p))],
        out_shape=[jax.ShapeDtypeStruct((S, HEADS * dv), BF16), jax.ShapeDtypeStruct((S, HEADS * 128), F32)],
        semantics=("parallel", "parallel"))


def softmax_attn_bwd(name, mode, q, k, v, o, do, lse, scale, bias=None):
    S = q.arr.shape[0]
    TQ, TK = _att_tiles(mode, S)
    G, dqk, dv = ATT_G, q.w, v.w

    def body(*refs):
        if bias is not None:
            q_ref, k_ref, v_ref, o_ref, do_ref, lse_ref, b_ref, dq_ref, dk_ref, dv_ref, db_ref = refs
        else:
            q_ref, k_ref, v_ref, o_ref, do_ref, lse_ref, dq_ref, dk_ref, dv_ref = refs
        i = pl.program_id(1)

        @pl.when(i == 0)
        def _():
            dk_ref[...] = jnp.zeros_like(dk_ref)
            dv_ref[...] = jnp.zeros_like(dv_ref)
            if bias is not None:
                db_ref[...] = jnp.zeros_like(db_ref)

        qs = [q_ref[:, q.lanes(g)] for g in range(G)]
        dos = [do_ref[:, do.lanes(g)] for g in range(G)]
        lses = [lse_ref[:, g * 128:g * 128 + 1] for g in range(G)]
        deltas = [jnp.sum(dos[g].astype(F32) * o_ref[:, o.lanes(g)].astype(F32), axis=-1, keepdims=True)
                  for g in range(G)]

        def block(g, j, dq, mask, ks):
            kb, vb = k_ref[ks, k.lanes(g)], v_ref[ks, v.lanes(g)]
            s = _dot_nt(qs[g], kb) * scale
            if bias is not None:
                slot = jnp.minimum(i - j, 2)
                s = s + b_ref[g, slot]
            p = jnp.exp(s - lses[g])
            if mask is not None:
                p = jnp.where(mask, p, 0.0)
            ds = p * (_dot_nt(dos[g], vb) - deltas[g])
            if bias is not None:
                db_ref[g, slot] += ds
            dsb = (ds * scale).astype(BF16)
            dk_ref[ks, g * dqk:(g + 1) * dqk] += _dot_tn(dsb, qs[g])
            dv_ref[ks, g * dv:(g + 1) * dv] += _dot_tn(p.astype(BF16), dos[g])
            return dq + _dot(dsb, kb)

        def step(j, carry, masked):
            ks = pl.ds(pl.multiple_of(j * TK, TK), TK)
            mask = _softmax_mask(mode, *_positions(i, j, TQ, TK)) if masked or mode == "ca" else None
            return tuple(block(g, j, carry[g], mask, ks) for g in range(G))

        init = (jnp.zeros((TQ, dqk), F32),) * G
        lo, n, per = _key_blocks(mode, i, TQ, TK)
        if mode == "ca":
            out = lax.fori_loop(lo, lo + per, lambda j, c: step(j, c, True), init)
        else:
            out = _walk(lo, n, per, step, init)
        for g in range(G):
            dq_ref[:, g * dqk:(g + 1) * dqk] = out[g]

    in_specs = [q.rows(TQ), k.full(S), v.full(S), o.rows(TQ), do.rows(TQ),
                pl.BlockSpec((TQ, G * 128), lambda p, i: (i, p))]
    ins = [q.arr, k.arr, v.arr, o.arr, do.arr, lse]
    out_specs = [pl.BlockSpec((TQ, G * dqk), lambda p, i: (i, p)), pl.BlockSpec((S, G * dqk), lambda p, i: (0, p)),
                 pl.BlockSpec((S, G * dv), lambda p, i: (0, p))]
    out_shape = [jax.ShapeDtypeStruct((S, HEADS * dqk), F32), jax.ShapeDtypeStruct((S, HEADS * dqk), F32),
                 jax.ShapeDtypeStruct((S, HEADS * dv), F32)]
    if bias is not None:
        bspec = pl.BlockSpec((G, 3, TK, TK), lambda p, i: (p, 0, 0, 0))
        in_specs.append(bspec)
        ins.append(bias)
        out_specs.append(bspec)
        out_shape.append(jax.ShapeDtypeStruct(bias.shape, F32))
    return _call(body, ins, name=name, grid=(HEADS // G, S // TQ), in_specs=in_specs, out_specs=out_specs,
                 out_shape=out_shape, semantics=("parallel", "arbitrary"))


def _split2(x):
    hi = x.astype(BF16)
    return hi, (x - hi.astype(F32)).astype(BF16)


def _split3(x):
    hi = x.astype(BF16)
    r = x - hi.astype(F32)
    mid = r.astype(BF16)
    return hi, mid, (r - mid.astype(F32)).astype(BF16)


def _stick_block(qb, kb, strict, scale):
    z = _dot_nt(qb, kb) * scale
    sp = jnp.log(1.0 + jnp.exp(-jnp.abs(z)))
    lb = jnp.minimum(z, 0.0) - sp
    l1m = jnp.minimum(-z, 0.0) - sp
    if strict is not None:
        l1m = jnp.where(strict, l1m, 0.0)
    return z, lb, l1m


def _strict_mask(i, j, TQ, TK):
    row, col = _positions(i, j, TQ, TK)
    return col < row


def _tri(T, inclusive):
    r = lax.broadcasted_iota(jnp.int32, (T, T), 0)
    c = lax.broadcasted_iota(jnp.int32, (T, T), 1)
    return ((r >= c) if inclusive else (r > c)).astype(BF16)


def _suffix(parts, tri):
    out = _dot(parts[0], tri)
    for p in parts[1:]:
        out = out + _dot(p, tri)
    return out


def stick_attn_fwd(name, q, k, v, scale):
    S = q.arr.shape[0]
    TQ, TK = _att_tiles("sb", S)
    G, dv = ATT_G, v.w

    def body(q_ref, k_ref, v_ref, o_ref):
        i = pl.program_id(1)
        qs = [q_ref[:, q.lanes(g)] for g in range(G)]
        tri = _tri(TK, False)

        def block(g, carry, strict, ks):
            right, acc = carry
            z, lb, l1m = _stick_block(qs[g], k_ref[ks, k.lanes(g)], strict, scale)
            a = jnp.exp(lb + _suffix(_split2(l1m), tri) + right)
            if strict is not None:
                a = jnp.where(strict, a, 0.0)
            acc = acc + _dot(a.astype(BF16), v_ref[ks, v.lanes(g)])
            return right + jnp.sum(l1m, axis=-1, keepdims=True), acc

        def step(j, carry, masked):
            ks = pl.ds(pl.multiple_of(j * TK, TK), TK)
            strict = _strict_mask(i, j, TQ, TK) if masked else None
            return tuple(block(g, carry[g], strict, ks) for g in range(G))

        init = (jnp.zeros((TQ, 1), F32), jnp.zeros((TQ, dv), F32))
        lo, n, per = _key_blocks("sb", i, TQ, TK)
        out = _walk(lo, n, per, step, (init,) * G, descending=True)
        for g in range(G):
            o_ref[:, g * dv:(g + 1) * dv] = out[g][1].astype(BF16)

    return _call(
        body, [q.arr, k.arr, v.arr], name=name, grid=(HEADS // G, S // TQ),
        in_specs=[q.rows(TQ), k.full(S), v.full(S)], out_specs=[pl.BlockSpec((TQ, G * dv), lambda p, i: (i, p))],
        out_shape=[jax.ShapeDtypeStruct((S, HEADS * dv), BF16)], semantics=("parallel", "parallel"))[0]


def stick_attn_bwd(name, q, k, v, do, scale):
    S = q.arr.shape[0]
    TQ, TK = _att_tiles("sb", S)
    G, dqk, dv = ATT_G, q.w, v.w

    def body(q_ref, k_ref, v_ref, do_ref, dq_ref, dk_ref, dv_ref):
        i = pl.program_id(1)

        @pl.when(i == 0)
        def _():
            dk_ref[...] = jnp.zeros_like(dk_ref)
            dv_ref[...] = jnp.zeros_like(dv_ref)

        qs = [q_ref[:, q.lanes(g)] for g in range(G)]
        dos = [do_ref[:, do.lanes(g)] for g in range(G)]
        tri = _tri(TK, False)
        tri_inc = _tri(TK, True)

        def block(g, right, strict, ks):
            kb = k_ref[ks, k.lanes(g)]
            z, lb, l1m = _stick_block(qs[g], kb, strict, scale)
            a = jnp.exp(lb + _suffix(_split2(l1m), tri) + right)
            if strict is not None:
                a = jnp.where(strict, a, 0.0)
            return kb, z, l1m, a, a * _dot_nt(dos[g], v_ref[ks, v.lanes(g)])

        def total(j, carry, masked):
            ks = pl.ds(pl.multiple_of(j * TK, TK), TK)
            strict = _strict_mask(i, j, TQ, TK) if masked else None
            out = []
            for g in range(G):
                right, gtot = carry[g]
                _, _, l1m, _, gg = block(g, right, strict, ks)
                out.append((right + jnp.sum(l1m, axis=-1, keepdims=True), gtot + jnp.sum(gg, axis=-1, keepdims=True)))
            return tuple(out)

        zero = jnp.zeros((TQ, 1), F32)
        lo, n, per = _key_blocks("sb", i, TQ, TK)
        gtots = [c[1] for c in _walk(lo, n, per, total, ((zero, zero),) * G, descending=True)]

        def step(j, carry, masked):
            ks = pl.ds(pl.multiple_of(j * TK, TK), TK)
            strict = _strict_mask(i, j, TQ, TK) if masked else None
            out = []
            for g in range(G):
                right, gright, dq = carry[g]
                kb, z, l1m, a, gg = block(g, right, strict, ks)
                c = gtots[g] - (_suffix(_split3(gg), tri_inc) + gright)
                sig = 1.0 / (1.0 + jnp.exp(-z))
                dz = gg * (1.0 - sig) - c * sig
                if strict is not None:
                    dz = jnp.where(strict, dz, 0.0)
                dzb = (dz * scale).astype(BF16)
                dk_ref[ks, g * dqk:(g + 1) * dqk] += _dot_tn(dzb, qs[g])
                dv_ref[ks, g * dv:(g + 1) * dv] += _dot_tn(a.astype(BF16), dos[g])
                out.append((right + jnp.sum(l1m, axis=-1, keepdims=True),
                            gright + jnp.sum(gg, axis=-1, keepdims=True), dq + _dot(dzb, kb)))
            return tuple(out)

        out = _walk(lo, n, per, step, ((zero, zero, jnp.zeros((TQ, dqk), F32)),) * G, descending=True)
        for g in range(G):
            dq_ref[:, g * dqk:(g + 1) * dqk] = out[g][2]

    return _call(
        body, [q.arr, k.arr, v.arr, do.arr], name=name, grid=(HEADS // G, S // TQ),
        in_specs=[q.rows(TQ), k.full(S), v.full(S), do.rows(TQ)],
        out_specs=[pl.BlockSpec((TQ, G * dqk), lambda p, i: (i, p)), pl.BlockSpec((S, G * dqk), lambda p, i: (0, p)),
                   pl.BlockSpec((S, G * dv), lambda p, i: (0, p))],
        out_shape=[jax.ShapeDtypeStruct((S, HEADS * dqk), F32), jax.ShapeDtypeStruct((S, HEADS * dqk), F32),
                   jax.ShapeDtypeStruct((S, HEADS * dv), F32)],
        semantics=("parallel", "arbitrary"))


def _skew(x, back):
    T = x.shape[0]
    rows = lax.broadcasted_iota(jnp.int32, (T, T), 0)
    for b in range(T.bit_length() - 1):
        shift = T - (1 << b) if back else 1 << b
        x = jnp.where(((rows >> b) & 1) == 1, pltpu.roll(x, shift, 1), x)
    return x


def _table_rows(table):
    t = jnp.pad(table.T, ((0, 0), (0, 2 * REL_CLIP_LEFT - REL_TABLE)))
    return t.reshape(table.shape[1], 2, REL_CLIP_LEFT)


def rel_bias_blocks(name, table, T):
    assert T == 2 * REL_CLIP_LEFT, "the base rows below are laid out for blocks of 256"

    def body(t_ref, o_ref):
        low, high = t_ref[0:1, :], t_ref[1:2, :]
        first = jnp.broadcast_to(t_ref[0:1, 0:1], (1, REL_CLIP_LEFT))
        qq = lax.broadcasted_iota(jnp.int32, (T, T), 0)
        kk = lax.broadcasted_iota(jnp.int32, (T, T), 1)

        def rolled(row):
            return _skew(jnp.broadcast_to(row, (T, T)), False)

        far = jnp.concatenate([first, low], axis=1)
        near = jnp.concatenate([high, jnp.zeros_like(high)], axis=1)
        o_ref[0] = jnp.where(kk >= qq, rolled(near), rolled(far))
        o_ref[1] = jnp.where(kk >= qq, rolled(far), jnp.broadcast_to(t_ref[0:1, 0:1], (T, T)))
        o_ref[2] = jnp.broadcast_to(t_ref[0:1, 0:1], (T, T))

    return pl.pallas_call(
        body, name=name, grid=(HEADS,), in_specs=[pl.BlockSpec((None, 2, REL_CLIP_LEFT), lambda h: (h, 0, 0))],
        out_specs=pl.BlockSpec((None, 3, T, T), lambda h: (h, 0, 0, 0)),
        out_shape=jax.ShapeDtypeStruct((HEADS, 3, T, T), F32), compiler_params=_params("parallel"),
    )(_table_rows(table))


def rel_bias_grad(name, dbias):
    T = dbias.shape[-1]
    L = REL_CLIP_LEFT
    assert T == 2 * L

    def body(d_ref, o_ref):
        qq = lax.broadcasted_iota(jnp.int32, (T, T), 0)
        ll = lax.broadcasted_iota(jnp.int32, (T, T), 1)
        wrapped = ll + qq >= T

        def columns(d):
            x = _skew(d_ref[d], True)
            return (jnp.sum(jnp.where(wrapped, 0.0, x), axis=0, keepdims=True),
                    jnp.sum(jnp.where(wrapped, x, 0.0), axis=0, keepdims=True))

        pos0, neg0 = columns(0)
        pos1, neg1 = columns(1)
        clipped = (jnp.sum(neg0[:, :L]) + jnp.sum(pos1[:, :L]) + jnp.sum(neg1) + jnp.sum(d_ref[2]))
        lane = lax.broadcasted_iota(jnp.int32, (1, L), 1)
        low = neg0[:, L:] + pos1[:, L:]
        o_ref[...] = jnp.zeros_like(o_ref)
        o_ref[0:1, :] = jnp.where(lane == 0, low + clipped, low)
        o_ref[1:2, :] = pos0[:, :L]

    rows = pl.pallas_call(
        body, name=name, grid=(HEADS,), in_specs=[pl.BlockSpec((None, 3, T, T), lambda h: (h, 0, 0, 0))],
        out_specs=pl.BlockSpec((None, 8, L), lambda h: (h, 0, 0)), out_shape=jax.ShapeDtypeStruct((HEADS, 8, L), F32),
        compiler_params=_params("parallel"),
    )(dbias)
    return rows[:, :2, :].reshape(HEADS, 2 * L)[:, :REL_TABLE].T


HBM = pl.BlockSpec(memory_space=pl.ANY)


def _place():
    return lax.axis_index("x"), lax.axis_index("y"), lax.axis_index("c")


def all_gather(name, shards):
    n = len(shards)

    def body(*refs):
        x_refs, out_refs = refs[:n], refs[n:2 * n]
        send_sems, recv_sems, local_sems = refs[2 * n:]
        x, y, c = _place()
        me, sibling = (x, y, c), (x, y, 1 - c)
        chips = [(1 - x, y), (x, 1 - y), (1 - x, 1 - y)]

        def block(t, dev):
            return out_refs[t].at[4 * dev[0] + 2 * dev[1] + dev[2]]

        def copy(t, k, dev, to, src=None):
            return pltpu.make_async_remote_copy(
                src_ref=block(t, dev) if src is None else src, dst_ref=block(t, dev),
                send_sem=send_sems.at[t, k], recv_sem=recv_sems.at[t, k], device_id=to, device_id_type=MESH)

        mine = [pltpu.make_async_copy(x_refs[t], block(t, me), local_sems.at[t]) for t in range(n)]
        for cp in mine:
            cp.start()
        first = []
        for t in range(n):
            first.append(copy(t, 0, me, sibling, src=x_refs[t]))
            first += [copy(t, 1 + j, me, (*chip, c), src=x_refs[t]) for j, chip in enumerate(chips)]
        for cp in first:
            cp.start()
        passed = []
        for j, chip in enumerate(chips):
            for t in range(n):
                copy(t, 1 + j, (*chip, c), me).wait_recv()
                cp = copy(t, 4 + j, (*chip, c), sibling)
                cp.start()
                passed.append(cp)
        for t in range(n):
            copy(t, 0, sibling, me).wait_recv()
            for j, chip in enumerate(chips):
                copy(t, 4 + j, (*chip, 1 - c), me).wait_recv()
        for cp in first + passed:
            cp.wait_send()
        for cp in mine:
            cp.wait()

    return pl.pallas_call(
        body, name=name, in_specs=[HBM] * n, out_specs=[HBM] * n,
        out_shape=[jax.ShapeDtypeStruct((N_DEV, *s.shape), s.dtype) for s in shards],
        scratch_shapes=[pltpu.SemaphoreType.DMA((n, 7)), pltpu.SemaphoreType.DMA((n, 7)), pltpu.SemaphoreType.DMA((n,))],
    )(*shards)


def _remote(src, dst, send_sems, recv_sems, k, to):
    return pltpu.make_async_remote_copy(src_ref=src, dst_ref=dst, send_sem=send_sems.at[k], recv_sem=recv_sems.at[k],
                                        device_id=to, device_id_type=MESH)


class Future:
    def __init__(self):
        self.value = None

    def get(self):
        if self.value is None:
            SCHED.flush()
        return self.value


def gather_jobs(name, shards, want_chips, want_sibling):
    n = len(shards)
    result = Future()

    def to_chips(in_refs, out_refs, send_sems, recv_sems):
        x, y, c = _place()
        me = 4 * x + 2 * y + c
        cps = []
        for t in range(n):
            cps.append(pltpu.make_async_copy(in_refs[t], out_refs[t].at[me], send_sems.at[4 * t]))
            for j, chip in enumerate([(1 - x, y), (x, 1 - y), (1 - x, 1 - y)]):
                cps.append(_remote(in_refs[t], out_refs[t].at[me], send_sems, recv_sems, 4 * t + 1 + j, (*chip, c)))
        return cps

    def to_sibling(in_refs, out_refs, send_sems, recv_sems):
        x, y, c = _place()
        return [_remote(in_refs[t].at[2 * chip + c], out_refs[t].at[2 * chip + c], send_sems, recv_sems, 4 * t + chip,
                        (x, y, 1 - c)) for t in range(n) for chip in range(4)]

    lands = [jax.ShapeDtypeStruct((N_DEV, *s.shape), s.dtype) for s in shards]

    def second(outs):
        SCHED.post(Job(name + "_sibling", want_sibling, outs, lands, 4 * n, to_sibling,
                       lambda final: setattr(result, "value", final), aliases={t: t for t in range(n)}))

    SCHED.post(Job(name + "_chips", want_chips, shards, lands, 4 * n, to_chips, second))
    return result


def scatter_jobs(name, grads, core, chip, want_sibling, wants_chips):
    n = len(grads)
    result = Future()
    sums = [None] * n

    def to_sibling(in_refs, out_refs, send_sems, recv_sems):
        x, y, c = _place()
        return [_remote(in_refs[t].at[1 - c], out_refs[t], send_sems, recv_sems, t, (x, y, 1 - c)) for t in range(n)]

    def after_sibling(received):
        parts = [add_sibling(f"{name}_add{t}", grads[t], received[t], core) for t in range(n)]
        for group, want in wants_chips:
            def to_chips(in_refs, out_refs, send_sems, recv_sems, m=len(group)):
                x, y, c = _place()
                return [_remote(in_refs[t].at[2 * cx + cy], out_refs[t].at[j], send_sems, recv_sems, 3 * t + j,
                                (cx, cy, c))
                        for t in range(m) for j, (cx, cy) in enumerate([(1 - x, y), (x, 1 - y), (1 - x, 1 - y)])]

            def after_chips(received, group=group):
                for t, r in zip(group, received):
                    sums[t] = sum_chips(f"{name}_sum{t}", parts[t], r, chip)
                if all(s is not None for s in sums):
                    result.value = sums

            mine = [parts[t] for t in group]
            SCHED.post(Job(f"{name}_chips{group[0]}", want, mine,
                           [jax.ShapeDtypeStruct((3, *p.shape[1:]), p.dtype) for p in mine], 3 * len(mine), to_chips,
                           after_chips))

    SCHED.post(Job(name + "_sibling", want_sibling, grads, [jax.ShapeDtypeStruct(g.shape[1:], g.dtype) for g in grads],
                   n, to_sibling, after_sibling))
    return result


def exchange_sibling(name, grads):
    n = len(grads)

    def body(*refs):
        g_refs, out_refs = refs[:n], refs[n:2 * n]
        send_sems, recv_sems = refs[2 * n:]
        x, y, c = _place()
        cps = [pltpu.make_async_remote_copy(
            src_ref=g_refs[t].at[1 - c], dst_ref=out_refs[t], send_sem=send_sems.at[t], recv_sem=recv_sems.at[t],
            device_id=(x, y, 1 - c), device_id_type=MESH) for t in range(n)]
        for cp in cps:
            cp.start()
        for cp in cps:
            cp.wait()

    return pl.pallas_call(
        body, name=name, in_specs=[HBM] * n, out_specs=[HBM] * n,
        out_shape=[jax.ShapeDtypeStruct(g.shape[1:], g.dtype) for g in grads],
        scratch_shapes=[pltpu.SemaphoreType.DMA((n,)), pltpu.SemaphoreType.DMA((n,))],
    )(*grads)


def exchange_chips(name, parts):
    n = len(parts)

    def body(*refs):
        p_refs, out_refs = refs[:n], refs[n:2 * n]
        send_sems, recv_sems = refs[2 * n:]
        x, y, c = _place()
        chips = [(1 - x, y), (x, 1 - y), (1 - x, 1 - y)]
        cps = [pltpu.make_async_remote_copy(
            src_ref=p_refs[t].at[2 * chip[0] + chip[1]], dst_ref=out_refs[t].at[j],
            send_sem=send_sems.at[t, j], recv_sem=recv_sems.at[t, j], device_id=(*chip, c), device_id_type=MESH)
            for t in range(n) for j, chip in enumerate(chips)]
        for cp in cps:
            cp.start()
        for cp in cps:
            cp.wait()

    return pl.pallas_call(
        body, name=name, in_specs=[HBM] * n, out_specs=[HBM] * n,
        out_shape=[jax.ShapeDtypeStruct((3, *p.shape[1:]), p.dtype) for p in parts],
        scratch_shapes=[pltpu.SemaphoreType.DMA((n, 3)), pltpu.SemaphoreType.DMA((n, 3))],
    )(*parts)


def _as_rows(shape):
    return (int(np.prod(shape[:-1])), shape[-1])


ELEMENTWISE_BLOCK = 256 * 1024


def _row_tile(rows, cols):
    return _tile(rows, max(128, ELEMENTWISE_BLOCK // cols // 128 * 128))


def add_sibling(name, grad, recv, core):
    rows, cols = _as_rows(grad.shape[2:])
    tr = _row_tile(rows, cols)

    def body(c_ref, g_ref, r_ref, o_ref):
        o_ref[...] = (g_ref[...].astype(F32) + r_ref[...].astype(F32)).astype(BF16)

    blk = pl.BlockSpec((None, tr, cols), lambda k, i, c_ref: (k, i, 0))
    return pl.pallas_call(
        body, name=name,
        grid_spec=pltpu.PrefetchScalarGridSpec(
            num_scalar_prefetch=1, grid=(4, rows // tr),
            in_specs=[pl.BlockSpec((None, None, tr, cols), lambda k, i, c_ref: (c_ref[0], k, i, 0)), blk],
            out_specs=blk),
        out_shape=jax.ShapeDtypeStruct((4, rows, cols), BF16), compiler_params=_params("parallel", "parallel"),
    )(core, grad.reshape(2, 4, rows, cols), recv.reshape(4, rows, cols)).reshape(recv.shape)


def sum_chips(name, part, recv, chip):
    shape = part.shape[1:]
    rows, cols = _as_rows(shape)
    tr = _row_tile(rows, cols)

    def body(c_ref, p_ref, r_ref, o_ref):
        o_ref[...] = (p_ref[...].astype(F32) + r_ref[0].astype(F32) + r_ref[1].astype(F32) + r_ref[2].astype(F32))

    return pl.pallas_call(
        body, name=name,
        grid_spec=pltpu.PrefetchScalarGridSpec(
            num_scalar_prefetch=1, grid=(rows // tr,),
            in_specs=[pl.BlockSpec((None, tr, cols), lambda i, c_ref: (c_ref[0], i, 0)),
                      pl.BlockSpec((3, tr, cols), lambda i, c_ref: (0, i, 0))],
            out_specs=pl.BlockSpec((tr, cols), lambda i, c_ref: (i, 0))),
        out_shape=jax.ShapeDtypeStruct((rows, cols), F32), compiler_params=_params("parallel"),
    )(chip, part.reshape(4, rows, cols), recv.reshape(3, rows, cols)).reshape(shape)


def sum_devices(name, gathered):
    _, rows, cols = gathered.shape

    def body(g_ref, o_ref):
        acc = g_ref[0]
        for d in range(1, N_DEV):
            acc = acc + g_ref[d]
        o_ref[...] = acc

    return pl.pallas_call(body, name=name, out_shape=jax.ShapeDtypeStruct((rows, cols), F32))(gathered)


def adamw(name, w, g, m, v):
    shape = w.shape
    rows, cols = _as_rows(shape)
    tr = _row_tile(rows, cols) if rows % 8 == 0 else rows
    c1 = 1.0 / (1.0 - ADAM_B1 ** ADAM_STEP)
    c2 = 1.0 / (1.0 - ADAM_B2 ** ADAM_STEP)

    def body(w_ref, g_ref, m_ref, v_ref, d_ref, mo_ref, vo_ref):
        g_ = g_ref[...]
        m_ = ADAM_B1 * m_ref[...] + (1.0 - ADAM_B1) * g_
        v_ = ADAM_B2 * v_ref[...] + (1.0 - ADAM_B2) * (g_ * g_)
        d_ref[...] = -ADAM_LR * ((m_ * c1) / (jnp.sqrt(v_ * c2) + ADAM_EPS) + ADAM_WD * w_ref[...])
        mo_ref[...] = m_
        vo_ref[...] = v_

    blk = pl.BlockSpec((tr, cols), lambda i: (i, 0))
    outs = pl.pallas_call(
        body, name=name, grid=(rows // tr,), in_specs=[blk] * 4, out_specs=[blk] * 3,
        out_shape=[jax.ShapeDtypeStruct((rows, cols), F32)] * 3, compiler_params=_params("parallel"),
    )(*[a.reshape(rows, cols) for a in (w, g, m, v)])
    return [o.reshape(shape) for o in outs]


def _spread_rope(r):
    z = jnp.zeros_like(r[..., :32])
    return jnp.concatenate([r[..., :32], z, r[..., 32:], z], -1)


def _gather_rope(r):
    return jnp.concatenate([r[..., :32], r[..., 64:96]], -1)


def pad_w_uq(w):
    w = w.reshape(w.shape[0], -1, MLA_NOPE + MLA_ROPE)
    return jnp.concatenate([w[..., :MLA_NOPE], _spread_rope(w[..., MLA_NOPE:])], -1).reshape(w.shape[0], -1)


def unpad_w_uq(g):
    g = g.reshape(g.shape[0], -1, 2 * MLA_NOPE)
    return jnp.concatenate([g[..., :MLA_NOPE], _gather_rope(g[..., MLA_NOPE:])], -1).reshape(g.shape[0], -1)


def pad_w_down(w):
    lat = MLA_Q_LORA + MLA_KV_LORA
    return jnp.concatenate([w[:, :lat], _spread_rope(w[:, lat:])], -1)


def unpad_w_down(g):
    lat = MLA_Q_LORA + MLA_KV_LORA
    return jnp.concatenate([g[:, :lat], _gather_rope(g[:, lat:])], -1)


def _heads(arr, width, first=0, off=0, w=None):
    return HeadCols(arr, width, lambda p: first // ATT_G + p, off, w)


def mla_forward(h16, w, gq, gkv, tables):
    cos, sin = tables
    down = mm_nn("mla_down", h16, w["down"], [F32])[0]
    cq, ckv = rms_fwd("mla_rms", down, gq, gkv)
    q = mm_nn("mla_uq", cq, w["uq"], [F32])[0]
    kv = mm_nn("mla_ukv", ckv, w["ukv"], [BF16])[0]
    qr, kp = mla_prep_fwd("mla_prep", q, kv, down, cos, sin)
    scale = (MLA_NOPE + MLA_ROPE) ** -0.5
    o, lse = softmax_attn_fwd("mla_attn", "mla", _heads(qr, 256), _heads(kp, 256), _heads(kv, 256, off=128, w=128), scale)
    m = mm_nn("mla_wo", o, w["wo"], [F32])[0]
    return m, (down, cq, ckv, qr, kp, kv, o, lse)


def mla_backward(du16, h16t, saved, w, gq, gkv, tables):
    cos, sin = tables
    down, cq, ckv, qr, kp, kv, o, lse = saved
    scale = (MLA_NOPE + MLA_ROPE) ** -0.5
    g = {"wo": mm_tn("mla_dwo", o, du16, "row", w["wo"].R, w["wo"].C)}
    do = mm_nt("mla_do", du16, w["wo"], BF16)
    dq, dk, dv = softmax_attn_bwd("mla_attn_bwd", "mla", _heads(qr, 256), _heads(kp, 256), _heads(kv, 256, off=128, w=128),
                                  _heads(o, 128), _heads(do, 128), lse, scale)
    dq16, dkv16, dkr = mla_prep_bwd("mla_prep_bwd", dq, dk, dv, cos, sin)
    g["uq"] = mm_tn("mla_duq", cq, dq16, "col", w["uq"].R, w["uq"].C)
    dcq = mm_nt("mla_dcq", dq16, w["uq"], F32)
    g["ukv"] = mm_tn("mla_dukv", ckv, dkv16, "col", w["ukv"].R, w["ukv"].C)
    dckv = mm_nt("mla_dckv", dkv16, w["ukv"], F32)
    ddown, dgq, dgkv = rms_bwd("mla_rms_bwd", down, dcq, dckv, dkr, gq, gkv)
    g["down"] = mm_tn("mla_ddown", h16t, ddown, "row", w["down"].R, w["down"].C, transposed=True)
    dh = mm_nt("mla_dh", ddown, w["down"], F32)
    return dh, g, (dgq, dgkv)


def qkv_forward(kind, h16, w, bias=None):
    qkv = mm_nn(kind + "_qkv", h16, w["qkv"], [BF16])[0]
    q, k, v = _heads(qkv, 128), _heads(qkv, 128, HEADS), _heads(qkv, 128, 2 * HEADS)
    scale = HEAD_DIM ** -0.5
    if kind == "sb":
        o, lse = stick_attn_fwd("sb_attn", q, k, v, scale), None
    else:
        o, lse = softmax_attn_fwd("ca_attn", "ca", q, k, v, scale, bias)
    m = mm_nn(kind + "_wo", o, w["wo"], [F32])[0]
    return m, (qkv, o, lse)


def qkv_backward(kind, du16, h16t, saved, w, bias=None):
    qkv, o, lse = saved
    q, k, v = _heads(qkv, 128), _heads(qkv, 128, HEADS), _heads(qkv, 128, 2 * HEADS)
    scale = HEAD_DIM ** -0.5
    g = {"wo": mm_tn(kind + "_dwo", o, du16, "row", w["wo"].R, w["wo"].C)}
    do = mm_nt(kind + "_do", du16, w["wo"], BF16)
    dbias = None
    if kind == "sb":
        dq, dk, dv = stick_attn_bwd("sb_attn_bwd", q, k, v, _heads(do, 128), scale)
    else:
        dq, dk, dv, dbias = softmax_attn_bwd("ca_attn_bwd", "ca", q, k, v, _heads(o, 128), _heads(do, 128), lse,
                                             scale, bias)
    dqkv = jnp.concatenate([dq, dk, dv], axis=1).astype(BF16)
    g["qkv"] = mm_tn(kind + "_dqkv", h16t, dqkv, "col", w["qkv"].R, w["qkv"].C, transposed=True)
    dh = mm_nt(kind + "_dh", dqkv, w["qkv"], F32)
    return dh, g, dbias


def mlp_forward(h16, w):
    a, z, zt = mm_nn("ffn_in", h16, w["w_in"], [F32, BF16, BF16], epilogue=_relu2_epilogue,
                     transposed=(False, False, True))
    f = mm_nn("ffn_out", z, w["w_out"], [F32])[0]
    return f, (a, zt)


def mlp_backward(du16, h16t, saved, w):
    a, zt = saved
    da = mm_nt("ffn_da", du16, w["w_out"], BF16, epilogue=_mulrelu_epilogue, extra=a)
    g = {"w_out": mm_tn("ffn_dwout", zt, du16, "row", w["w_out"].R, w["w_out"].C, transposed=True)}
    dh = mm_nt("ffn_dh", da, w["w_in"], F32)
    g["w_in"] = mm_tn("ffn_dwin", h16t, da, "col", w["w_in"].R, w["w_in"].C, transposed=True)
    return dh, g


WEIGHTS = ("ln_mix_g", "ln_mix_b", "ln_ffn_g", "ln_ffn_b", "ffn_w_in", "ffn_w_out", "mla_w_down", "mla_q_norm_g",
           "mla_w_uq", "mla_kv_norm_g", "mla_w_ukv", "mla_w_o", "sb_w_qkv", "sb_w_o", "ca_w_qkv", "ca_rel_bias",
           "ca_w_o")
MIXERS = ("mla", "sb", "ca")
LAYER_WEIGHTS = {
    "mla": (("down", "mla_w_down", "row"), ("uq", "mla_w_uq", "col"), ("ukv", "mla_w_ukv", "col"),
            ("wo", "mla_w_o", "row")),
    "sb": (("qkv", "sb_w_qkv", "col"), ("wo", "sb_w_o", "row")),
    "ca": (("qkv", "ca_w_qkv", "col"), ("wo", "ca_w_o", "row")),
    "ffn": (("w_in", "ffn_w_in", "col"), ("w_out", "ffn_w_out", "row")),
}
PAD = {"mla_w_down": pad_w_down, "mla_w_uq": pad_w_uq}
UNPAD = {"mla_w_down": unpad_w_down, "mla_w_uq": unpad_w_uq}


def _pack_rows(vectors):
    flat = jnp.concatenate([v.reshape(-1) for v in vectors])
    n = flat.shape[0]
    rows = -(-n // 1024) * 8
    offsets = np.cumsum([0] + [int(np.prod(v.shape)) for v in vectors])
    return jnp.pad(flat, (0, rows * 128 - n)).reshape(rows, 128), offsets


def _part(i, part):
    group, idx = (MIXERS[i % 3], i // 3) if part == "mix" else ("ffn", i)
    return [(key, name, how, idx) for key, name, how in LAYER_WEIGHTS[group]]


def kernel(x, ln_mix_g, ln_mix_b, ln_ffn_g, ln_ffn_b, ffn_w_in, ffn_w_out, mla_w_down, mla_q_norm_g, mla_w_uq, mla_kv_norm_g, mla_w_ukv, mla_w_o, sb_w_qkv, sb_w_o, ca_w_qkv, ca_rel_bias, ca_w_o, loss_target, m_ln_mix_g, m_ln_mix_b, m_ln_ffn_g, m_ln_ffn_b, m_ffn_w_in, m_ffn_w_out, m_mla_w_down, m_mla_q_norm_g, m_mla_w_uq, m_mla_kv_norm_g, m_mla_w_ukv, m_mla_w_o, m_sb_w_qkv, m_sb_w_o, m_ca_w_qkv, m_ca_rel_bias, m_ca_w_o, v_ln_mix_g, v_ln_mix_b, v_ln_ffn_g, v_ln_ffn_b, v_ffn_w_in, v_ffn_w_out, v_mla_w_down, v_mla_q_norm_g, v_mla_w_uq, v_mla_kv_norm_g, v_mla_w_ukv, v_mla_w_o, v_sb_w_qkv, v_sb_w_o, v_ca_w_qkv, v_ca_rel_bias, v_ca_w_o):
    w = dict(zip(WEIGHTS, (ln_mix_g, ln_mix_b, ln_ffn_g, ln_ffn_b, ffn_w_in, ffn_w_out, mla_w_down, mla_q_norm_g,
                           mla_w_uq, mla_kv_norm_g, mla_w_ukv, mla_w_o, sb_w_qkv, sb_w_o, ca_w_qkv, ca_rel_bias,
                           ca_w_o)))
    mom = dict(zip(WEIGHTS, (m_ln_mix_g, m_ln_mix_b, m_ln_ffn_g, m_ln_ffn_b, m_ffn_w_in, m_ffn_w_out, m_mla_w_down,
                             m_mla_q_norm_g, m_mla_w_uq, m_mla_kv_norm_g, m_mla_w_ukv, m_mla_w_o, m_sb_w_qkv,
                             m_sb_w_o, m_ca_w_qkv, m_ca_rel_bias, m_ca_w_o)))
    var = dict(zip(WEIGHTS, (v_ln_mix_g, v_ln_mix_b, v_ln_ffn_g, v_ln_ffn_b, v_ffn_w_in, v_ffn_w_out, v_mla_w_down,
                             v_mla_q_norm_g, v_mla_w_uq, v_mla_kv_norm_g, v_mla_w_ukv, v_mla_w_o, v_sb_w_qkv,
                             v_sb_w_o, v_ca_w_qkv, v_ca_rel_bias, v_ca_w_o)))
    S, D = x.shape[1], x.shape[2]
    xi, yi, ci = _place()
    core = ci.astype(jnp.int32).reshape(1)
    chip = (2 * xi + yi).astype(jnp.int32).reshape(1)
    me = 4 * xi + 2 * yi + ci
    tables = rope_tables(S)
    n_mla = mla_w_down.shape[0]
    lat = MLA_Q_LORA // N_DEV

    gains = jnp.pad(jnp.stack([mla_q_norm_g.reshape(-1), mla_kv_norm_g.reshape(-1)]), ((0, 6), (0, 128 - n_mla * lat)))
    gains = all_gather("ag_gains", [gains])[0]

    def full_gain(row, slot):
        return gains[:, row, slot * lat:(slot + 1) * lat].reshape(-1)

    first_matmul = {"mla": "mla_down", "sb": "sb_qkv", "ca": "ca_qkv"}

    def post_gather(i, part):
        kind = MIXERS[i % 3]
        specs = _part(i, part)
        shards = [PAD.get(name, lambda a: a)(w[name][idx]).astype(BF16) for _, name, _, idx in specs]
        if part == "mix":
            wants = ("ffn_in", "ffn_out") if i > 0 else (None, None)
            return [(specs, gather_jobs(f"ag_mix{i}", shards, *wants))]
        w_in = ("ffn_out", first_matmul[kind]) if i > 0 else ("mla_down", "mla_uq")
        return [(specs[:1], gather_jobs(f"ag_w_in{i}", shards[:1], *w_in)),
                (specs[1:], gather_jobs(f"ag_w_out{i}", shards[1:], kind + "_attn", kind + "_wo"))]

    def gathered(posted):
        return {key: Weight(how, g) for specs, future in posted for (key, _, how, _), g in zip(specs, future.get())}

    def post_scatter(i, part, g):
        kind = MIXERS[i % 3]
        specs = _part(i, part)
        n = len(specs)
        if part == "ffn":
            wants = (kind + "_dwo", [(list(range(n)), kind + "_attn_bwd")])
        elif i > 0:
            wants = ("ffn_da", [(list(range(n - 1)), "ffn_dwout"), ([n - 1], "ffn_dh")])
        else:
            wants = (None, [(list(range(n)), None)])
        return specs, scatter_jobs(f"rs_{part}{i}", [g[key] for key, _, _, _ in specs], core, chip, *wants)

    SCHED.pending.clear()
    bias = rel_bias_blocks("ca_bias", ca_rel_bias[0], _att_tiles("ca", S)[1])

    h, h16 = x[0], x[0].astype(BF16)
    h16t = transpose("x_t", h16)
    saved, layers = [], []
    mix_w, ffn_w = post_gather(0, "mix"), None
    for i in range(DEPTH):
        kind, slot = MIXERS[i % 3], i // 3
        lw = gathered(mix_w)
        if i == 0:
            ffn_w = post_gather(0, "ffn")
        if kind == "mla":
            mix, s_mix = mla_forward(h16, lw, full_gain(0, slot), full_gain(1, slot), tables)
        else:
            mix, s_mix = qkv_forward(kind, h16, lw, bias if kind == "ca" else None)
        y, y16, y16t, xh1, rs1 = ln_fwd("ln_mix", h, mix, ln_mix_g[i], ln_mix_b[i])
        lw.update(gathered(ffn_w))
        if i + 1 < DEPTH:
            mix_w, ffn_w = post_gather(i + 1, "mix"), post_gather(i + 1, "ffn")
        f, s_mlp = mlp_forward(y16, lw)
        y2, y2_16, y2_16t, xh2, rs2 = ln_fwd("ln_ffn", y, f, ln_ffn_g[i], ln_ffn_b[i])
        saved.append((h16t, s_mix, xh1, rs1, y16t, s_mlp, xh2, rs2))
        layers.append(lw)
        h, h16, h16t = y2, y2_16, y2_16t
    sq, dy = loss_fwd_bwd("loss", h, loss_target[0])
    loss = 0.5 / D * lax.psum(sq[0, 0], ("x", "y", "c"))

    ga, gb = dy, None
    grads = {name: [None] * w[name].shape[0] for name in WEIGHTS}
    dbias = None
    scattered = []
    for i in reversed(range(DEPTH)):
        kind, slot = MIXERS[i % 3], i // 3
        lw = layers[i]
        h16_in, s_mix, xh1, rs1, y16, s_mlp, xh2, rs2 = saved[i]
        du, du16, grads["ln_ffn_g"][i], grads["ln_ffn_b"][i] = ln_bwd("ln_ffn_bwd", ga, gb, xh2, rs2, ln_ffn_g[i])
        dh_mlp, g_mlp = mlp_backward(du16, y16, s_mlp, lw)
        scattered.append(post_scatter(i, "ffn", g_mlp))
        du, du16, grads["ln_mix_g"][i], grads["ln_mix_b"][i] = ln_bwd("ln_mix_bwd", du, dh_mlp, xh1, rs1, ln_mix_g[i])
        if kind == "mla":
            dh_mix, g_mix, (dgq, dgkv) = mla_backward(du16, h16_in, s_mix, lw, full_gain(0, slot), full_gain(1, slot),
                                                      tables)
            grads["mla_q_norm_g"][slot], grads["mla_kv_norm_g"][slot] = dgq, dgkv
        else:
            dh_mix, g_mix, db = qkv_backward(kind, du16, h16_in, s_mix, lw, bias if kind == "ca" else None)
            dbias = db if kind == "ca" else dbias
        scattered.append(post_scatter(i, "mix", g_mix))
        ga, gb = du, dh_mix
    grad_x = axpy("grad_x", ga, gb)[None]
    SCHED.flush()
    for specs, future in scattered:
        for (_, name, _, idx), g in zip(specs, future.get()):
            grads[name][idx] = UNPAD.get(name, lambda a: a)(g)
    grads["ca_rel_bias"][0] = rel_bias_grad("ca_bias_grad", dbias)

    small = ("ln_mix_g", "ln_mix_b", "ln_ffn_g", "ln_ffn_b", "ca_rel_bias", "mla_q_norm_g", "mla_kv_norm_g")
    packed, offsets = _pack_rows([g for name in small for g in grads[name]])
    total = sum_devices("sum_small", all_gather("ag_small", [packed])[0]).reshape(-1)
    pos = 0
    for name in small:
        for idx, g in enumerate(grads[name]):
            full = total[offsets[pos]:offsets[pos + 1]]
            pos += 1
            if name in ("mla_q_norm_g", "mla_kv_norm_g"):
                full = lax.dynamic_slice(full, (me * lat,), (lat,))
            grads[name][idx] = full.reshape(w[name].shape[1:])

    g_out, d_out, m_out, v_out = [], [], [], []
    for name in WEIGHTS:
        g = jnp.stack(grads[name])
        delta, new_m, new_v = adamw("adamw_" + name, w[name], g, mom[name], var[name])
        g_out.append(g)
        d_out.append(delta)
        m_out.append(new_m)
        v_out.append(new_v)
    return (loss, grad_x, *g_out, *d_out, *m_out, *v_out)
```

```python
import functools
import math

import numpy as np
import jax
import jax.numpy as jnp
from jax import lax
from jax.experimental import pallas as pl
from jax.experimental.pallas import tpu as pltpu

F32 = jnp.float32
BF16 = jnp.bfloat16
MESH = pl.DeviceIdType.MESH
N_DEV = 8

DEPTH = 4
CHUNK = 64
CHUNK_SHIFT = 6
HEADS = 16
HEAD_DIM = 128
MLA_Q_LORA = 512
MLA_KV_LORA = 512
MLA_NOPE = 128
MLA_ROPE = 64
ROPE_THETA = 10000.0
CA_LEFT_CHUNKS = 8
REL_CLIP_LEFT = 128
REL_TABLE = REL_CLIP_LEFT + CHUNK
LN_EPS = 1e-5
RMS_EPS = 1e-6
ALPHA = (2.0 * DEPTH) ** 0.25
NEG = -1e30
ADAM_LR = 0.001
ADAM_B1 = 0.9
ADAM_B2 = 0.999
ADAM_EPS = 1e-08
ADAM_WD = 0.01
ADAM_STEP = 10

V7X_VMEM_BYTES = 64 * 1024 * 1024
VMEM_LIMIT = V7X_VMEM_BYTES - 8 * 1024 * 1024
ATT_TQ = 512
ATT_TK = 256
ATT_G = 2


def _params(*sem):
    return pltpu.CompilerParams(dimension_semantics=sem if sem else None, vmem_limit_bytes=VMEM_LIMIT)


HBM = pl.BlockSpec(memory_space=pl.ANY)


class Job:
    def __init__(self, name, want, operands, out_shape, n_copies, copies, done, aliases=None):
        self.name, self.want, self.operands, self.out_shape = name, want, list(operands), list(out_shape)
        self.n_copies, self.copies, self.done, self.aliases = n_copies, copies, done, dict(aliases or {})

    def sems(self):
        return [pltpu.SemaphoreType.DMA((self.n_copies,)), pltpu.SemaphoreType.DMA((self.n_copies,))]


class Scheduler:
    def __init__(self):
        self.pending = []

    def post(self, job):
        self.pending.append(job)

    def take(self, name):
        mine = [job for job in self.pending if job.want is not None and job.want in name]
        self.pending = [job for job in self.pending if job not in mine]
        return mine

    def flush(self):
        while self.pending:
            job = self.pending.pop(0)
            n_in, n_out = len(job.operands), len(job.out_shape)

            def body(*refs, job=job, n_in=n_in, n_out=n_out):
                cps = job.copies(refs[:n_in], refs[n_in:n_in + n_out], refs[-2], refs[-1])
                for cp in cps:
                    cp.start()
                for cp in cps:
                    cp.wait()

            outs = pl.pallas_call(
                body, name=job.name, in_specs=[HBM] * n_in, out_specs=[HBM] * n_out, out_shape=job.out_shape,
                scratch_shapes=job.sems(), input_output_aliases=job.aliases)(*job.operands)
            job.done(list(outs))


SCHED = Scheduler()


def _call(body, operands, *, name, grid, in_specs, out_specs, out_shape, scratch_shapes=(), semantics):
    jobs = SCHED.take(name)
    if not jobs:
        return list(pl.pallas_call(
            body, name=name, grid=grid, in_specs=list(in_specs), out_specs=list(out_specs), out_shape=list(out_shape),
            scratch_shapes=list(scratch_shapes), compiler_params=_params(*semantics))(*operands))
    n_in, n_out, n_scr = len(operands), len(out_shape), len(scratch_shapes)
    j_in = np.cumsum([0] + [len(job.operands) for job in jobs])
    j_out = np.cumsum([0] + [len(job.out_shape) for job in jobs])
    a, b = n_in, n_in + int(j_in[-1])
    c, d = b + n_out, b + n_out + int(j_out[-1])

    def carrying(*refs):
        def copies():
            sems = refs[d + n_scr:]
            return [cp for k, job in enumerate(jobs)
                    for cp in job.copies(refs[a + j_in[k]:a + j_in[k + 1]], refs[c + j_out[k]:c + j_out[k + 1]],
                                         sems[2 * k], sems[2 * k + 1])]

        ids = [pl.program_id(k) for k in range(len(grid))]
        first = functools.reduce(jnp.logical_and, [i == 0 for i in ids])
        last = functools.reduce(jnp.logical_and, [i == g - 1 for i, g in zip(ids, grid)])

        @pl.when(first)
        def _():
            for cp in copies():
                cp.start()

        body(*refs[:a], *refs[b:c], *refs[d:d + n_scr])

        @pl.when(last)
        def _():
            for cp in copies():
                cp.wait()

    aliases = {n_in + int(j_in[k]) + i: n_out + int(j_out[k]) + o for k, job in enumerate(jobs)
               for i, o in job.aliases.items()}
    outs = pl.pallas_call(
        carrying, name=name + "_carry", grid=grid, in_specs=list(in_specs) + [HBM] * int(j_in[-1]),
        out_specs=list(out_specs) + [HBM] * int(j_out[-1]),
        out_shape=list(out_shape) + [s for job in jobs for s in job.out_shape],
        scratch_shapes=list(scratch_shapes) + [s for job in jobs for s in job.sems()],
        input_output_aliases=aliases,
        compiler_params=_params(*(["arbitrary"] * len(grid))))(*operands, *[o for job in jobs for o in job.operands])
    for k, job in enumerate(jobs):
        job.done(list(outs[n_out + int(j_out[k]):n_out + int(j_out[k + 1])]))
    return list(outs[:n_out])


def _matmul(name, a, b, *, contract, grid, a_spec, b_spec, o_specs, out_shape, acc_shape,
            epilogue=None, extra=(), extra_specs=()):
    nk = grid[2]
    n_extra = len(extra)
    n_out = len(out_shape)

    def finish(acc, e_refs, o_refs):
        outs = epilogue(acc, *[e[...] for e in e_refs]) if epilogue else (acc,)
        for o_ref, val in zip(o_refs, outs):
            o_ref[...] = val.astype(o_ref.dtype)

    def product(a_ref, b_ref):
        return lax.dot_general(a_ref[...], b_ref[...], (contract, ((), ())), preferred_element_type=F32)

    def body_single(*refs):
        finish(product(refs[0], refs[1]), refs[2:2 + n_extra], refs[2 + n_extra:2 + n_extra + n_out])

    def body(*refs):
        a_ref, b_ref = refs[0], refs[1]
        acc_ref = refs[-1]
        k = pl.program_id(2)

        @pl.when(k == 0)
        def _():
            acc_ref[...] = jnp.zeros_like(acc_ref)

        acc_ref[...] += product(a_ref, b_ref)

        @pl.when(k == nk - 1)
        def _():
            finish(acc_ref[...], refs[2:2 + n_extra], refs[2 + n_extra:2 + n_extra + n_out])

    return _call(
        body_single if nk == 1 else body, [a, b, *extra], name=name, grid=grid,
        in_specs=[a_spec, b_spec, *extra_specs], out_specs=o_specs, out_shape=out_shape,
        scratch_shapes=[] if nk == 1 else [pltpu.VMEM(acc_shape, F32)],
        semantics=("parallel", "parallel", "arbitrary"))


MATMUL_BLOCK_BYTES = 40 * 1024 * 1024
MAX_TK = 2048
MULTI_TK = 512


def _fit_tn(n, tm, tk, nk, out_bytes):
    cands = sorted({n} | {t for t in range(128, n, 128) if n % t == 0}, reverse=True)
    for tn in cands:
        need = 2 * 2 * (tm * tk + tk * tn) + 2 * tm * tn * out_bytes + (tm * tn * 4 if nk > 1 else 0) + tm * tn * 4
        if need <= MATMUL_BLOCK_BYTES:
            return tn
    return cands[-1]


def _itemsize(dtypes):
    return sum(jnp.dtype(d).itemsize for d in dtypes)


def _tile(n, pref):
    if n <= pref:
        return n
    t = pref
    while t >= 128:
        if n % t == 0 and t % 128 == 0:
            return t
        t -= 128
    return n


class Weight:
    def __init__(self, kind, arr):
        self.kind = kind
        self.arr = arr
        self.R, self.C = arr.shape[1], arr.shape[2]

    @property
    def two_d(self):
        return self.arr.reshape(N_DEV * self.R, self.C)


def mm_nn(name, a, w, out_dtypes, epilogue=None, transposed=()):
    M, K = a.shape
    tm = M
    tk = K if K <= MAX_TK else _tile(K, MULTI_TK)
    nk = K // tk
    if w.kind == "row":
        b = w.two_d
        N = w.C
        tn = _fit_tn(N, tm, tk, nk, _itemsize(out_dtypes))
        b_spec = pl.BlockSpec((tk, tn), lambda i, j, k: (k, j))
    else:
        b = w.arr
        N = N_DEV * w.C
        tn = _fit_tn(w.C, tm, tk, nk, _itemsize(out_dtypes))
        per = w.C // tn
        b_spec = pl.BlockSpec((None, tk, tn), lambda i, j, k: (j // per, k, j % per))
    grid = (M // tm, N // tn, nk)
    flip = [t < len(transposed) and transposed[t] for t in range(len(out_dtypes))]
    return _matmul(
        name, a, b, contract=((1,), (0,)), grid=grid,
        a_spec=pl.BlockSpec((tm, tk), lambda i, j, k: (i, k)), b_spec=b_spec,
        o_specs=[pl.BlockSpec((tn, tm), lambda i, j, k: (j, i)) if f else pl.BlockSpec((tm, tn), lambda i, j, k: (i, j))
                 for f in flip],
        out_shape=[jax.ShapeDtypeStruct((N, M) if f else (M, N), d) for f, d in zip(flip, out_dtypes)],
        acc_shape=(tm, tn), epilogue=epilogue)


def mm_nt(name, dy, w, out_dtype, epilogue=None, extra=None):
    M, N = dy.shape
    tm = M
    out_bytes = jnp.dtype(out_dtype).itemsize + (0 if extra is None else extra.dtype.itemsize)
    if w.kind == "row":
        b = w.two_d
        kin = N_DEV * w.R
        tk = N if N <= MAX_TK else _tile(N, MULTI_TK)
        tn = _fit_tn(kin, tm, tk, N // tk, out_bytes)
        b_spec = pl.BlockSpec((tn, tk), lambda i, j, k: (j, k))
    else:
        b = w.arr
        kin = w.R
        tk = _tile(w.C, MULTI_TK)
        per = w.C // tk
        tn = _fit_tn(kin, tm, tk, N // tk, out_bytes)
        b_spec = pl.BlockSpec((None, tn, tk), lambda i, j, k: (k // per, j, k % per))
    grid = (M // tm, kin // tn, N // tk)
    o_spec = pl.BlockSpec((tm, tn), lambda i, j, k: (i, j))
    return _matmul(
        name, dy, b, contract=((1,), (1,)), grid=grid,
        a_spec=pl.BlockSpec((tm, tk), lambda i, j, k: (i, k)), b_spec=b_spec, o_specs=[o_spec],
        out_shape=[jax.ShapeDtypeStruct((M, kin), out_dtype)], acc_shape=(tm, tn), epilogue=epilogue,
        extra=() if extra is None else (extra,), extra_specs=() if extra is None else (o_spec,))[0]


TRANSPOSE_TILE = 512


def transpose(name, x):
    S, n = x.shape
    ts, tn = _tile(S, TRANSPOSE_TILE), _tile(n, TRANSPOSE_TILE)

    def body(x_ref, o_ref):
        o_ref[...] = x_ref[...].T

    return pl.pallas_call(
        body, name=name, grid=(S // ts, n // tn), in_specs=[pl.BlockSpec((ts, tn), lambda i, j: (i, j))],
        out_specs=pl.BlockSpec((tn, ts), lambda i, j: (j, i)), out_shape=jax.ShapeDtypeStruct((n, S), x.dtype),
        compiler_params=_params("parallel", "parallel"),
    )(x)


def mm_tn(name, x, dy, kind, R, C, transposed=False):
    if not transposed:
        x = transpose(name + "_t", x)
    kin, S = x.shape
    N = dy.shape[1]
    tk = S if S <= MAX_TK else _tile(S, MULTI_TK)
    nk = S // tk
    if kind == "col":
        tm = kin
        tn = _fit_tn(C, tm, tk, nk, 2)
        per = C // tn
        grid = (1, N // tn, nk)
        o_spec = pl.BlockSpec((None, None, tm, tn), lambda i, j, k: ((j // per) % 2, (j // per) // 2, 0, j % per))
    else:
        tm = R
        tn = _fit_tn(N, tm, tk, nk, 2)
        grid = (N_DEV, N // tn, nk)
        o_spec = pl.BlockSpec((None, None, tm, tn), lambda i, j, k: (i % 2, i // 2, 0, j))
    return _matmul(
        name, x, dy, contract=((1,), (0,)), grid=grid,
        a_spec=pl.BlockSpec((tm, tk), lambda i, j, k: (i, k)),
        b_spec=pl.BlockSpec((tk, tn), lambda i, j, k: (k, j)), o_specs=[o_spec],
        out_shape=[jax.ShapeDtypeStruct((2, 4, R, C), BF16)], acc_shape=(tm, tn))[0]


def _relu2_epilogue(acc):
    r = jnp.maximum(acc, 0.0)
    z = (r * r).astype(BF16)
    return acc, z, z.T


def _mulrelu_epilogue(acc, a):
    return (acc * (2.0 * jnp.maximum(a, 0.0)),)


ROW_TILE = 256


def ln_fwd(name, h, m, g, b):
    S, D = h.shape
    ts = _tile(S, ROW_TILE)

    def body(h_ref, m_ref, g_ref, b_ref, y_ref, y16_ref, yt_ref, xh_ref, rs_ref):
        u = ALPHA * h_ref[...] + m_ref[...]
        mu = jnp.mean(u, axis=-1, keepdims=True)
        d = u - mu
        var = jnp.mean(d * d, axis=-1, keepdims=True)
        rstd = lax.rsqrt(var + LN_EPS)
        xh = d * rstd
        y = xh * g_ref[...] + b_ref[...]
        y16 = y.astype(BF16)
        y_ref[...] = y
        y16_ref[...] = y16
        yt_ref[...] = y16.T
        xh_ref[...] = xh
        rs_ref[...] = jnp.broadcast_to(rstd, rs_ref.shape)

    row = pl.BlockSpec((ts, D), lambda i: (i, 0))
    vec = pl.BlockSpec((1, D), lambda i: (0, 0))
    return pl.pallas_call(
        body, name=name, grid=(S // ts,), in_specs=[row, row, vec, vec],
        out_specs=[row, row, pl.BlockSpec((D, ts), lambda i: (0, i)), row, pl.BlockSpec((ts, 128), lambda i: (i, 0))],
        out_shape=[jax.ShapeDtypeStruct((S, D), F32), jax.ShapeDtypeStruct((S, D), BF16),
                   jax.ShapeDtypeStruct((D, S), BF16), jax.ShapeDtypeStruct((S, D), F32),
                   jax.ShapeDtypeStruct((S, 128), F32)],
        compiler_params=_params("parallel"),
    )(h, m, g.reshape(1, D), b.reshape(1, D))


def ln_bwd(name, ga, gb, xhat, rstd, g):
    S, D = xhat.shape
    ts = _tile(S, ROW_TILE)
    two = gb is not None

    def body(*refs):
        if two:
            ga_ref, gb_ref, xh_ref, rs_ref, g_ref, du_ref, du16_ref, dg_ref, db_ref = refs
            dy = ALPHA * ga_ref[...] + gb_ref[...]
        else:
            ga_ref, xh_ref, rs_ref, g_ref, du_ref, du16_ref, dg_ref, db_ref = refs
            dy = ga_ref[...]
        xh = xh_ref[...]

        @pl.when(pl.program_id(0) == 0)
        def _():
            dg_ref[...] = jnp.zeros_like(dg_ref)
            db_ref[...] = jnp.zeros_like(db_ref)

        dg_ref[...] += jnp.sum(dy * xh, axis=0, keepdims=True)
        db_ref[...] += jnp.sum(dy, axis=0, keepdims=True)
        dxh = dy * g_ref[...]
        m1 = jnp.mean(dxh, axis=-1, keepdims=True)
        m2 = jnp.mean(dxh * xh, axis=-1, keepdims=True)
        du = rs_ref[:, 0:1] * (dxh - m1 - xh * m2)
        du_ref[...] = du
        du16_ref[...] = du.astype(BF16)

    row = pl.BlockSpec((ts, D), lambda i: (i, 0))
    vec = pl.BlockSpec((1, D), lambda i: (0, 0))
    stat = pl.BlockSpec((ts, 128), lambda i: (i, 0))
    ins = [ga, gb, xhat, rstd, g.reshape(1, D)] if two else [ga, xhat, rstd, g.reshape(1, D)]
    in_specs = [row, row, row, stat, vec] if two else [row, row, stat, vec]
    return pl.pallas_call(
        body, name=name, grid=(S // ts,), in_specs=in_specs, out_specs=[row, row, vec, vec],
        out_shape=[jax.ShapeDtypeStruct((S, D), F32), jax.ShapeDtypeStruct((S, D), BF16),
                   jax.ShapeDtypeStruct((1, D), F32), jax.ShapeDtypeStruct((1, D), F32)],
        compiler_params=_params("arbitrary"),
    )(*ins)


def loss_fwd_bwd(name, y, target):
    S, D = y.shape
    ts = _tile(S, ROW_TILE)

    def body(y_ref, t_ref, l_ref, dy_ref):
        @pl.when(pl.program_id(0) == 0)
        def _():
            l_ref[...] = jnp.zeros_like(l_ref)

        e = y_ref[...] - t_ref[...]
        l_ref[...] += jnp.sum(e * e)
        dy_ref[...] = e * (1.0 / D)

    row = pl.BlockSpec((ts, D), lambda i: (i, 0))
    return pl.pallas_call(
        body, name=name, grid=(S // ts,), in_specs=[row, row],
        out_specs=[pl.BlockSpec((1, 128), lambda i: (0, 0)), row],
        out_shape=[jax.ShapeDtypeStruct((1, 128), F32), jax.ShapeDtypeStruct((S, D), F32)],
        compiler_params=_params("arbitrary"),
    )(y, target)


def axpy(name, ga, gb):
    S, D = ga.shape
    ts = _tile(S, ROW_TILE)

    def body(a_ref, b_ref, o_ref):
        o_ref[...] = ALPHA * a_ref[...] + b_ref[...]

    row = pl.BlockSpec((ts, D), lambda i: (i, 0))
    return pl.pallas_call(body, name=name, grid=(S // ts,), in_specs=[row, row], out_specs=row,
                          out_shape=jax.ShapeDtypeStruct((S, D), F32), compiler_params=_params("parallel"))(ga, gb)


def rms_fwd(name, down, gq, gkv):
    S = down.shape[0]
    ts = _tile(S, ROW_TILE)
    L = MLA_Q_LORA

    def body(d_ref, gq_ref, gkv_ref, q_ref, kv_ref):
        for lo, g_ref, o_ref in ((0, gq_ref, q_ref), (L, gkv_ref, kv_ref)):
            x = d_ref[:, lo:lo + L]
            r = lax.rsqrt(jnp.mean(x * x, axis=-1, keepdims=True) + RMS_EPS)
            o_ref[...] = (x * r * g_ref[...]).astype(BF16)

    vec = pl.BlockSpec((1, L), lambda i: (0, 0))
    out = pl.BlockSpec((ts, L), lambda i: (i, 0))
    return pl.pallas_call(
        body, name=name, grid=(S // ts,), in_specs=[pl.BlockSpec((ts, down.shape[1]), lambda i: (i, 0)), vec, vec],
        out_specs=[out, out], out_shape=[jax.ShapeDtypeStruct((S, L), BF16)] * 2, compiler_params=_params("parallel"),
    )(down, gq.reshape(1, L), gkv.reshape(1, L))


def rms_bwd(name, down, dq, dkv, dkr, gq, gkv):
    S, W = down.shape
    ts = _tile(S, ROW_TILE)
    L = MLA_Q_LORA

    def body(d_ref, dq_ref, dkv_ref, dkr_ref, gq_ref, gkv_ref, o_ref, dgq_ref, dgkv_ref):
        @pl.when(pl.program_id(0) == 0)
        def _():
            dgq_ref[...] = jnp.zeros_like(dgq_ref)
            dgkv_ref[...] = jnp.zeros_like(dgkv_ref)

        for lo, dy_ref, g_ref, dg_ref in ((0, dq_ref, gq_ref, dgq_ref), (L, dkv_ref, gkv_ref, dgkv_ref)):
            x = d_ref[:, lo:lo + L]
            dy = dy_ref[...]
            r = lax.rsqrt(jnp.mean(x * x, axis=-1, keepdims=True) + RMS_EPS)
            dg_ref[...] += jnp.sum(dy * x * r, axis=0, keepdims=True)
            dyg = dy * g_ref[...]
            dx = r * dyg - x * (r * r * r) * jnp.mean(dyg * x, axis=-1, keepdims=True)
            o_ref[:, lo:lo + L] = dx.astype(BF16)
        o_ref[:, 2 * L:] = dkr_ref[...].astype(BF16)

    vec = pl.BlockSpec((1, L), lambda i: (0, 0))
    lat = pl.BlockSpec((ts, L), lambda i: (i, 0))
    full = pl.BlockSpec((ts, W), lambda i: (i, 0))
    return pl.pallas_call(
        body, name=name, grid=(S // ts,),
        in_specs=[full, lat, lat, pl.BlockSpec((ts, 128), lambda i: (i, 0)), vec, vec],
        out_specs=[full, vec, vec],
        out_shape=[jax.ShapeDtypeStruct((S, W), BF16), jax.ShapeDtypeStruct((1, L), F32), jax.ShapeDtypeStruct((1, L), F32)],
        compiler_params=_params("arbitrary"),
    )(down, dq, dkv, dkr, gq.reshape(1, L), gkv.reshape(1, L))


def rope_tables(S):
    half = MLA_ROPE // 2
    inv = (np.float32(ROPE_THETA) ** (-np.arange(half, dtype=np.float32) / np.float32(half))).astype(np.float32)
    ang = np.arange(S, dtype=np.float32)[:, None] * inv[None, :]
    cos, sin = np.cos(ang).astype(np.float32), np.sin(ang).astype(np.float32)
    z = np.zeros_like(cos)
    return (jnp.asarray(np.concatenate([cos, z, cos, z], 1)), jnp.asarray(np.concatenate([-sin, z, sin, z], 1)))


def _rot(x, cos, sin):
    return x * cos + pltpu.roll(x, 64, 1) * sin


def mla_prep_fwd(name, q, kv, down, cos, sin):
    S = q.shape[0]
    ts = _tile(S, ROW_TILE)

    def body(q_ref, kv_ref, kr_ref, c_ref, s_ref, qo_ref, ko_ref):
        c, s = c_ref[...], s_ref[...]
        key = _rot(kr_ref[...], c, s).astype(BF16)
        for h in range(HEADS):
            lo = 256 * h
            qo_ref[:, lo:lo + 128] = q_ref[:, lo:lo + 128].astype(BF16)
            qo_ref[:, lo + 128:lo + 256] = _rot(q_ref[:, lo + 128:lo + 256], c, s).astype(BF16)
            ko_ref[:, lo:lo + 128] = kv_ref[:, lo:lo + 128]
            ko_ref[:, lo + 128:lo + 256] = key

    heads = pl.BlockSpec((ts, HEADS * 256), lambda i: (i, 0))
    tab = pl.BlockSpec((ts, 128), lambda i: (i, 0))
    return pl.pallas_call(
        body, name=name, grid=(S // ts,),
        in_specs=[heads, heads, pl.BlockSpec((ts, 128), lambda i: (i, 2 * MLA_Q_LORA // 128)), tab, tab],
        out_specs=[heads, heads], out_shape=[jax.ShapeDtypeStruct(q.shape, BF16)] * 2,
        compiler_params=_params("parallel"),
    )(q, kv, down, cos, sin)


def mla_prep_bwd(name, dq, dk, dv, cos, sin):
    S = dq.shape[0]
    ts = _tile(S, ROW_TILE)

    def body(dq_ref, dk_ref, dv_ref, c_ref, s_ref, qo_ref, kvo_ref, kr_ref):
        c, s = c_ref[...], -s_ref[...]
        key = jnp.zeros((ts, 128), F32)
        for h in range(HEADS):
            lo = 256 * h
            qo_ref[:, lo:lo + 128] = dq_ref[:, lo:lo + 128].astype(BF16)
            qo_ref[:, lo + 128:lo + 256] = _rot(dq_ref[:, lo + 128:lo + 256], c, s).astype(BF16)
            kvo_ref[:, lo:lo + 128] = dk_ref[:, lo:lo + 128].astype(BF16)
            kvo_ref[:, lo + 128:lo + 256] = dv_ref[:, 128 * h:128 * h + 128].astype(BF16)
            key = key + dk_ref[:, lo + 128:lo + 256]
        kr_ref[...] = _rot(key, c, s)

    heads = pl.BlockSpec((ts, HEADS * 256), lambda i: (i, 0))
    tab = pl.BlockSpec((ts, 128), lambda i: (i, 0))
    return pl.pallas_call(
        body, name=name, grid=(S // ts,),
        in_specs=[heads, heads, pl.BlockSpec((ts, HEADS * 128), lambda i: (i, 0)), tab, tab],
        out_specs=[heads, heads, tab],
        out_shape=[jax.ShapeDtypeStruct(dq.shape, BF16), jax.ShapeDtypeStruct(dq.shape, BF16),
                   jax.ShapeDtypeStruct((S, 128), F32)],
        compiler_params=_params("parallel"),
    )(dq, dk, dv, cos, sin)


def _dot_nt(a, b):
    return lax.dot_general(a, b, (((1,), (1,)), ((), ())), preferred_element_type=F32)


def _dot_tn(a, b):
    return lax.dot_general(a, b, (((0,), (0,)), ((), ())), preferred_element_type=F32)


def _dot(a, b):
    return jnp.dot(a, b, preferred_element_type=F32)


def _positions(i, j, TQ, TK):
    row = i * TQ + lax.broadcasted_iota(jnp.int32, (TQ, TK), 0)
    col = j * TK + lax.broadcasted_iota(jnp.int32, (TQ, TK), 1)
    return row, col


def _softmax_mask(mode, row, col):
    rc, cc = row >> CHUNK_SHIFT, col >> CHUNK_SHIFT
    if mode == "mla":
        return cc <= rc
    return (cc <= rc) & (cc >= rc - CA_LEFT_CHUNKS)


def _key_blocks(mode, i, TQ, TK):
    per = TQ // TK
    if mode == "ca":
        lo = jnp.maximum(i - (CA_LEFT_CHUNKS * CHUNK) // TK, 0)
        return lo, 0, i - lo + 1
    return 0, i * per, per


class HeadCols:
    def __init__(self, arr, width, index, off=0, w=None):
        self.arr, self.width, self.index, self.off = arr, width, index, off
        self.w = width if w is None else w

    def rows(self, T):
        return pl.BlockSpec((T, ATT_G * self.width), lambda p, i: (i, self.index(p)))

    def full(self, S):
        return pl.BlockSpec((S, ATT_G * self.width), lambda p, i: (0, self.index(p)))

    def lanes(self, g):
        lo = g * self.width + self.off
        return slice(lo, lo + self.w)


def _att_tiles(mode, S):
    tk = min(ATT_TK, S)
    return (tk if mode == "ca" else min(ATT_TQ, S)), tk


def _walk(lo, n, per, step, carry, descending=False):
    tail = [lo + n + d for d in range(per)]
    if descending:
        for j in reversed(tail):
            carry = step(j, carry, True)
        return lax.fori_loop(0, n, lambda t, c: step(lo + n - 1 - t, c, False), carry)
    carry = lax.fori_loop(0, n, lambda t, c: step(lo + t, c, False), carry)
    for j in tail:
        carry = step(j, carry, True)
    return carry


def softmax_attn_fwd(name, mode, q, k, v, scale, bias=None):
    S = q.arr.shape[0]
    TQ, TK = _att_tiles(mode, S)
    G, dv = ATT_G, v.w

    def body(*refs):
        if bias is not None:
            q_ref, k_ref, v_ref, b_ref, o_ref, lse_ref = refs
        else:
            q_ref, k_ref, v_ref, o_ref, lse_ref = refs
        i = pl.program_id(1)
        qs = [q_ref[:, q.lanes(g)] for g in range(G)]

        def block(g, j, carry, mask, ks):
            m, l, acc = carry
            s = _dot_nt(qs[g], k_ref[ks, k.lanes(g)]) * scale
            if bias is not None:
                s = s + b_ref[g, jnp.minimum(i - j, 2)]
            if mask is not None:
                s = jnp.where(mask, s, NEG)
            m_new = jnp.maximum(m, jnp.max(s, axis=-1, keepdims=True))
            a = jnp.exp(m - m_new)
            p = jnp.exp(s - m_new)
            if mask is not None:
                p = jnp.where(mask, p, 0.0)
            l = a * l + jnp.sum(p, axis=-1, keepdims=True)
            acc = a * acc + _dot(p.astype(BF16), v_ref[ks, v.lanes(g)])
            return m_new, l, acc

        def step(j, carry, masked):
            ks = pl.ds(pl.multiple_of(j * TK, TK), TK)
            mask = _softmax_mask(mode, *_positions(i, j, TQ, TK)) if masked or mode == "ca" else None
            return tuple(block(g, j, carry[g], mask, ks) for g in range(G))

        init = (jnp.full((TQ, 1), NEG, F32), jnp.zeros((TQ, 1), F32), jnp.zeros((TQ, dv), F32))
        lo, n, per = _key_blocks(mode, i, TQ, TK)
        if mode == "ca":
            out = lax.fori_loop(lo, lo + per, lambda j, c: step(j, c, True), (init,) * G)
        else:
            out = _walk(lo, n, per, step, (init,) * G)
        for g, (m, l, acc) in enumerate(out):
            o_ref[:, g * dv:(g + 1) * dv] = (acc / l).astype(BF16)
            lse_ref[:, g * 128:(g + 1) * 128] = jnp.broadcast_to(m + jnp.log(l), (TQ, 128))

    in_specs = [q.rows(TQ), k.full(S), v.full(S)]
    ins = [q.arr, k.arr, v.arr]
    if bias is not None:
        in_specs.append(pl.BlockSpec((G, 3, TK, TK), lambda p, i: (p, 0, 0, 0)))
        ins.append(bias)
    return _call(
        body, ins, name=name, grid=(HEADS // G, S // TQ), in_specs=in_specs,
        out_specs=[pl.BlockSpec((TQ, G * dv), lambda p, i: (i, p)), pl.BlockSpec((TQ, G * 128), lambda p, i: (i, p))],
        out_shape=[jax.ShapeDtypeStruct((S, HEADS * dv), BF16), jax.ShapeDtypeStruct((S, HEADS * 128), F32)],
        semantics=("parallel", "parallel"))


def softmax_attn_bwd(name, mode, q, k, v, o, do, lse, scale, bias=None):
    S = q.arr.shape[0]
    TQ, TK = _att_tiles(mode, S)
    G, dqk, dv = ATT_G, q.w, v.w

    def body(*refs):
        if bias is not None:
            q_ref, k_ref, v_ref, o_ref, do_ref, lse_ref, b_ref, dq_ref, dk_ref, dv_ref, db_ref = refs
        else:
            q_ref, k_ref, v_ref, o_ref, do_ref, lse_ref, dq_ref, dk_ref, dv_ref = refs
        i = pl.program_id(1)

        @pl.when(i == 0)
        def _():
            dk_ref[...] = jnp.zeros_like(dk_ref)
            dv_ref[...] = jnp.zeros_like(dv_ref)
            if bias is not None:
                db_ref[...] = jnp.zeros_like(db_ref)

        qs = [q_ref[:, q.lanes(g)] for g in range(G)]
        dos = [do_ref[:, do.lanes(g)] for g in range(G)]
        lses = [lse_ref[:, g * 128:g * 128 + 1] for g in range(G)]
        deltas = [jnp.sum(dos[g].astype(F32) * o_ref[:, o.lanes(g)].astype(F32), axis=-1, keepdims=True)
                  for g in range(G)]

        def block(g, j, dq, mask, ks):
            kb, vb = k_ref[ks, k.lanes(g)], v_ref[ks, v.lanes(g)]
            s = _dot_nt(qs[g], kb) * scale
            if bias is not None:
                slot = jnp.minimum(i - j, 2)
                s = s + b_ref[g, slot]
            p = jnp.exp(s - lses[g])
            if mask is not None:
                p = jnp.where(mask, p, 0.0)
            ds = p * (_dot_nt(dos[g], vb) - deltas[g])
            if bias is not None:
                db_ref[g, slot] += ds
            dsb = (ds * scale).astype(BF16)
            dk_ref[ks, g * dqk:(g + 1) * dqk] += _dot_tn(dsb, qs[g])
            dv_ref[ks, g * dv:(g + 1) * dv] += _dot_tn(p.astype(BF16), dos[g])
            return dq + _dot(dsb, kb)

        def step(j, carry, masked):
            ks = pl.ds(pl.multiple_of(j * TK, TK), TK)
            mask = _softmax_mask(mode, *_positions(i, j, TQ, TK)) if masked or mode == "ca" else None
            return tuple(block(g, j, carry[g], mask, ks) for g in range(G))

        init = (jnp.zeros((TQ, dqk), F32),) * G
        lo, n, per = _key_blocks(mode, i, TQ, TK)
        if mode == "ca":
            out = lax.fori_loop(lo, lo + per, lambda j, c: step(j, c, True), init)
        else:
            out = _walk(lo, n, per, step, init)
        for g in range(G):
            dq_ref[:, g * dqk:(g + 1) * dqk] = out[g]

    in_specs = [q.rows(TQ), k.full(S), v.full(S), o.rows(TQ), do.rows(TQ),
                pl.BlockSpec((TQ, G * 128), lambda p, i: (i, p))]
    ins = [q.arr, k.arr, v.arr, o.arr, do.arr, lse]
    out_specs = [pl.BlockSpec((TQ, G * dqk), lambda p, i: (i, p)), pl.BlockSpec((S, G * dqk), lambda p, i: (0, p)),
                 pl.BlockSpec((S, G * dv), lambda p, i: (0, p))]
    out_shape = [jax.ShapeDtypeStruct((S, HEADS * dqk), F32), jax.ShapeDtypeStruct((S, HEADS * dqk), F32),
                 jax.ShapeDtypeStruct((S, HEADS * dv), F32)]
    if bias is not None:
        bspec = pl.BlockSpec((G, 3, TK, TK), lambda p, i: (p, 0, 0, 0))
        in_specs.append(bspec)
        ins.append(bias)
        out_specs.append(bspec)
        out_shape.append(jax.ShapeDtypeStruct(bias.shape, F32))
    return _call(body, ins, name=name, grid=(HEADS // G, S // TQ), in_specs=in_specs, out_specs=out_specs,
                 out_shape=out_shape, semantics=("parallel", "arbitrary"))


def _split2(x):
    hi = x.astype(BF16)
    return hi, (x - hi.astype(F32)).astype(BF16)


def _split3(x):
    hi = x.astype(BF16)
    r = x - hi.astype(F32)
    mid = r.astype(BF16)
    return hi, mid, (r - mid.astype(F32)).astype(BF16)


def _stick_block(qb, kb, strict, scale):
    z = _dot_nt(qb, kb) * scale
    sp = jnp.log(1.0 + jnp.exp(-jnp.abs(z)))
    lb = jnp.minimum(z, 0.0) - sp
    l1m = jnp.minimum(-z, 0.0) - sp
    if strict is not None:
        l1m = jnp.where(strict, l1m, 0.0)
    return z, lb, l1m


def _strict_mask(i, j, TQ, TK):
    row, col = _positions(i, j, TQ, TK)
    return col < row


def _tri(T, inclusive):
    r = lax.broadcasted_iota(jnp.int32, (T, T), 0)
    c = lax.broadcasted_iota(jnp.int32, (T, T), 1)
    return ((r >= c) if inclusive else (r > c)).astype(BF16)


def _tri_prefix(T, inclusive):
    r = lax.broadcasted_iota(jnp.int32, (T, T), 0)
    c = lax.broadcasted_iota(jnp.int32, (T, T), 1)
    return ((r <= c) if inclusive else (r < c)).astype(BF16)


def _suffix(parts, tri):
    out = _dot(parts[0], tri)
    for p in parts[1:]:
        out = out + _dot(p, tri)
    return out


def stick_attn_fwd(name, q, k, v, scale):
    S = q.arr.shape[0]
    TQ, TK = _att_tiles("sb", S)
    G, dv = ATT_G, v.w

    def body(q_ref, k_ref, v_ref, o_ref, tot_ref):
        i = pl.program_id(1)
        qs = [q_ref[:, q.lanes(g)] for g in range(G)]
        tri = _tri(TK, False)

        def block(g, carry, strict, ks):
            right, acc = carry
            z, lb, l1m = _stick_block(qs[g], k_ref[ks, k.lanes(g)], strict, scale)
            a = jnp.exp(lb + _suffix(_split2(l1m), tri) + right)
            if strict is not None:
                a = jnp.where(strict, a, 0.0)
            acc = acc + _dot(a.astype(BF16), v_ref[ks, v.lanes(g)])
            return right + jnp.sum(l1m, axis=-1, keepdims=True), acc

        def step(j, carry, masked):
            ks = pl.ds(pl.multiple_of(j * TK, TK), TK)
            strict = _strict_mask(i, j, TQ, TK) if masked else None
            return tuple(block(g, carry[g], strict, ks) for g in range(G))

        init = (jnp.zeros((TQ, 1), F32), jnp.zeros((TQ, dv), F32))
        lo, n, per = _key_blocks("sb", i, TQ, TK)
        out = _walk(lo, n, per, step, (init,) * G, descending=True)
        for g in range(G):
            o_ref[:, g * dv:(g + 1) * dv] = out[g][1].astype(BF16)
            tot_ref[:, g * 128:(g + 1) * 128] = jnp.broadcast_to(out[g][0], (TQ, 128))

    return _call(
        body, [q.arr, k.arr, v.arr], name=name, grid=(HEADS // G, S // TQ),
        in_specs=[q.rows(TQ), k.full(S), v.full(S)],
        out_specs=[pl.BlockSpec((TQ, G * dv), lambda p, i: (i, p)), pl.BlockSpec((TQ, G * 128), lambda p, i: (i, p))],
        out_shape=[jax.ShapeDtypeStruct((S, HEADS * dv), BF16), jax.ShapeDtypeStruct((S, HEADS * 128), F32)],
        semantics=("parallel", "parallel"))


def stick_attn_bwd(name, q, k, v, do, total, scale):
    S = q.arr.shape[0]
    TQ, TK = _att_tiles("sb", S)
    G, dqk, dv = ATT_G, q.w, v.w

    def body(q_ref, k_ref, v_ref, do_ref, tot_ref, dq_ref, dk_ref, dv_ref):
        i = pl.program_id(1)

        @pl.when(i == 0)
        def _():
            dk_ref[...] = jnp.zeros_like(dk_ref)
            dv_ref[...] = jnp.zeros_like(dv_ref)

        qs = [q_ref[:, q.lanes(g)] for g in range(G)]
        dos = [do_ref[:, do.lanes(g)] for g in range(G)]
        tots = [tot_ref[:, g * 128:g * 128 + 1] for g in range(G)]
        upto = _tri_prefix(TK, True)
        before = _tri_prefix(TK, False)

        def step(j, carry, masked):
            ks = pl.ds(pl.multiple_of(j * TK, TK), TK)
            strict = _strict_mask(i, j, TQ, TK) if masked else None
            out = []
            for g in range(G):
                left, gleft, dq = carry[g]
                kb = k_ref[ks, k.lanes(g)]
                z, lb, l1m = _stick_block(qs[g], kb, strict, scale)
                a = jnp.exp(lb + (tots[g] - (left + _suffix(_split3(l1m), upto))))
                if strict is not None:
                    a = jnp.where(strict, a, 0.0)
                gg = a * _dot_nt(dos[g], v_ref[ks, v.lanes(g)])
                c = gleft + _suffix(_split3(gg), before)
                sig = 1.0 / (1.0 + jnp.exp(-z))
                dz = gg * (1.0 - sig) - c * sig
                if strict is not None:
                    dz = jnp.where(strict, dz, 0.0)
                dzb = (dz * scale).astype(BF16)
                dk_ref[ks, g * dqk:(g + 1) * dqk] += _dot_tn(dzb, qs[g])
                dv_ref[ks, g * dv:(g + 1) * dv] += _dot_tn(a.astype(BF16), dos[g])
                out.append((left + jnp.sum(l1m, axis=-1, keepdims=True),
                            gleft + jnp.sum(gg, axis=-1, keepdims=True), dq + _dot(dzb, kb)))
            return tuple(out)

        zero = jnp.zeros((TQ, 1), F32)
        lo, n, per = _key_blocks("sb", i, TQ, TK)
        out = _walk(lo, n, per, step, ((zero, zero, jnp.zeros((TQ, dqk), F32)),) * G)
        for g in range(G):
            dq_ref[:, g * dqk:(g + 1) * dqk] = out[g][2]

    return _call(
        body, [q.arr, k.arr, v.arr, do.arr, total], name=name, grid=(HEADS // G, S // TQ),
        in_specs=[q.rows(TQ), k.full(S), v.full(S), do.rows(TQ), pl.BlockSpec((TQ, G * 128), lambda p, i: (i, p))],
        out_specs=[pl.BlockSpec((TQ, G * dqk), lambda p, i: (i, p)), pl.BlockSpec((S, G * dqk), lambda p, i: (0, p)),
                   pl.BlockSpec((S, G * dv), lambda p, i: (0, p))],
        out_shape=[jax.ShapeDtypeStruct((S, HEADS * dqk), F32), jax.ShapeDtypeStruct((S, HEADS * dqk), F32),
                   jax.ShapeDtypeStruct((S, HEADS * dv), F32)],
        semantics=("parallel", "arbitrary"))


def _skew(x, back):
    T = x.shape[0]
    rows = lax.broadcasted_iota(jnp.int32, (T, T), 0)
    for b in range(T.bit_length() - 1):
        shift = T - (1 << b) if back else 1 << b
        x = jnp.where(((rows >> b) & 1) == 1, pltpu.roll(x, shift, 1), x)
    return x


def _table_rows(table):
    t = jnp.pad(table.T, ((0, 0), (0, 2 * REL_CLIP_LEFT - REL_TABLE)))
    return t.reshape(table.shape[1], 2, REL_CLIP_LEFT)


def rel_bias_blocks(name, table, T):
    assert T == 2 * REL_CLIP_LEFT, "the base rows below are laid out for blocks of 256"

    def body(t_ref, o_ref):
        low, high = t_ref[0:1, :], t_ref[1:2, :]
        first = jnp.broadcast_to(t_ref[0:1, 0:1], (1, REL_CLIP_LEFT))
        qq = lax.broadcasted_iota(jnp.int32, (T, T), 0)
        kk = lax.broadcasted_iota(jnp.int32, (T, T), 1)

        def rolled(row):
            return _skew(jnp.broadcast_to(row, (T, T)), False)

        far = jnp.concatenate([first, low], axis=1)
        near = jnp.concatenate([high, jnp.zeros_like(high)], axis=1)
        o_ref[0] = jnp.where(kk >= qq, rolled(near), rolled(far))
        o_ref[1] = jnp.where(kk >= qq, rolled(far), jnp.broadcast_to(t_ref[0:1, 0:1], (T, T)))
        o_ref[2] = jnp.broadcast_to(t_ref[0:1, 0:1], (T, T))

    return pl.pallas_call(
        body, name=name, grid=(HEADS,), in_specs=[pl.BlockSpec((None, 2, REL_CLIP_LEFT), lambda h: (h, 0, 0))],
        out_specs=pl.BlockSpec((None, 3, T, T), lambda h: (h, 0, 0, 0)),
        out_shape=jax.ShapeDtypeStruct((HEADS, 3, T, T), F32), compiler_params=_params("parallel"),
    )(_table_rows(table))


def rel_bias_grad(name, dbias):
    T = dbias.shape[-1]
    L = REL_CLIP_LEFT
    assert T == 2 * L

    def body(d_ref, o_ref):
        qq = lax.broadcasted_iota(jnp.int32, (T, T), 0)
        ll = lax.broadcasted_iota(jnp.int32, (T, T), 1)
        wrapped = ll + qq >= T

        def columns(d):
            x = _skew(d_ref[d], True)
            return (jnp.sum(jnp.where(wrapped, 0.0, x), axis=0, keepdims=True),
                    jnp.sum(jnp.where(wrapped, x, 0.0), axis=0, keepdims=True))

        pos0, neg0 = columns(0)
        pos1, neg1 = columns(1)
        clipped = (jnp.sum(neg0[:, :L]) + jnp.sum(pos1[:, :L]) + jnp.sum(neg1) + jnp.sum(d_ref[2]))
        lane = lax.broadcasted_iota(jnp.int32, (1, L), 1)
        low = neg0[:, L:] + pos1[:, L:]
        o_ref[...] = jnp.zeros_like(o_ref)
        o_ref[0:1, :] = jnp.where(lane == 0, low + clipped, low)
        o_ref[1:2, :] = pos0[:, :L]

    rows = pl.pallas_call(
        body, name=name, grid=(HEADS,), in_specs=[pl.BlockSpec((None, 3, T, T), lambda h: (h, 0, 0, 0))],
        out_specs=pl.BlockSpec((None, 8, L), lambda h: (h, 0, 0)), out_shape=jax.ShapeDtypeStruct((HEADS, 8, L), F32),
        compiler_params=_params("parallel"),
    )(dbias)
    return rows[:, :2, :].reshape(HEADS, 2 * L)[:, :REL_TABLE].T


HBM = pl.BlockSpec(memory_space=pl.ANY)


def _place():
    return lax.axis_index("x"), lax.axis_index("y"), lax.axis_index("c")


def all_gather(name, shards):
    n = len(shards)

    def body(*refs):
        x_refs, out_refs = refs[:n], refs[n:2 * n]
        send_sems, recv_sems, local_sems = refs[2 * n:]
        x, y, c = _place()
        me, sibling = (x, y, c), (x, y, 1 - c)
        chips = [(1 - x, y), (x, 1 - y), (1 - x, 1 - y)]

        def block(t, dev):
            return out_refs[t].at[4 * dev[0] + 2 * dev[1] + dev[2]]

        def copy(t, k, dev, to, src=None):
            return pltpu.make_async_remote_copy(
                src_ref=block(t, dev) if src is None else src, dst_ref=block(t, dev),
                send_sem=send_sems.at[t, k], recv_sem=recv_sems.at[t, k], device_id=to, device_id_type=MESH)

        mine = [pltpu.make_async_copy(x_refs[t], block(t, me), local_sems.at[t]) for t in range(n)]
        for cp in mine:
            cp.start()
        first = []
        for t in range(n):
            first.append(copy(t, 0, me, sibling, src=x_refs[t]))
            first += [copy(t, 1 + j, me, (*chip, c), src=x_refs[t]) for j, chip in enumerate(chips)]
        for cp in first:
            cp.start()
        passed = []
        for j, chip in enumerate(chips):
            for t in range(n):
                copy(t, 1 + j, (*chip, c), me).wait_recv()
                cp = copy(t, 4 + j, (*chip, c), sibling)
                cp.start()
                passed.append(cp)
        for t in range(n):
            copy(t, 0, sibling, me).wait_recv()
            for j, chip in enumerate(chips):
                copy(t, 4 + j, (*chip, 1 - c), me).wait_recv()
        for cp in first + passed:
            cp.wait_send()
        for cp in mine:
            cp.wait()

    return pl.pallas_call(
        body, name=name, in_specs=[HBM] * n, out_specs=[HBM] * n,
        out_shape=[jax.ShapeDtypeStruct((N_DEV, *s.shape), s.dtype) for s in shards],
        scratch_shapes=[pltpu.SemaphoreType.DMA((n, 7)), pltpu.SemaphoreType.DMA((n, 7)), pltpu.SemaphoreType.DMA((n,))],
    )(*shards)


def _remote(src, dst, send_sems, recv_sems, k, to):
    return pltpu.make_async_remote_copy(src_ref=src, dst_ref=dst, send_sem=send_sems.at[k], recv_sem=recv_sems.at[k],
                                        device_id=to, device_id_type=MESH)


class Future:
    def __init__(self):
        self.value = None

    def get(self):
        if self.value is None:
            SCHED.flush()
        return self.value


def gather_jobs(name, shards, want_chips, want_sibling):
    n = len(shards)
    result = Future()

    def to_chips(in_refs, out_refs, send_sems, recv_sems):
        x, y, c = _place()
        me = 4 * x + 2 * y + c
        cps = []
        for t in range(n):
            cps.append(pltpu.make_async_copy(in_refs[t], out_refs[t].at[me], send_sems.at[4 * t]))
            for j, chip in enumerate([(1 - x, y), (x, 1 - y), (1 - x, 1 - y)]):
                cps.append(_remote(in_refs[t], out_refs[t].at[me], send_sems, recv_sems, 4 * t + 1 + j, (*chip, c)))
        return cps

    def to_sibling(in_refs, out_refs, send_sems, recv_sems):
        x, y, c = _place()
        return [_remote(in_refs[t].at[2 * chip + c], out_refs[t].at[2 * chip + c], send_sems, recv_sems, 4 * t + chip,
                        (x, y, 1 - c)) for t in range(n) for chip in range(4)]

    lands = [jax.ShapeDtypeStruct((N_DEV, *s.shape), s.dtype) for s in shards]

    def second(outs):
        SCHED.post(Job(name + "_sibling", want_sibling, outs, lands, 4 * n, to_sibling,
                       lambda final: setattr(result, "value", final), aliases={t: t for t in range(n)}))

    SCHED.post(Job(name + "_chips", want_chips, shards, lands, 4 * n, to_chips, second))
    return result


def scatter_jobs(name, grads, core, chip, want_sibling, wants_chips):
    n = len(grads)
    result = Future()
    sums = [None] * n

    def to_sibling(in_refs, out_refs, send_sems, recv_sems):
        x, y, c = _place()
        return [_remote(in_refs[t].at[1 - c], out_refs[t], send_sems, recv_sems, t, (x, y, 1 - c)) for t in range(n)]

    def after_sibling(received):
        parts = [add_sibling(f"{name}_add{t}", grads[t], received[t], core) for t in range(n)]
        for group, want in wants_chips:
            def to_chips(in_refs, out_refs, send_sems, recv_sems, m=len(group)):
                x, y, c = _place()
                return [_remote(in_refs[t].at[2 * cx + cy], out_refs[t].at[j], send_sems, recv_sems, 3 * t + j,
                                (cx, cy, c))
                        for t in range(m) for j, (cx, cy) in enumerate([(1 - x, y), (x, 1 - y), (1 - x, 1 - y)])]

            def after_chips(received, group=group):
                for t, r in zip(group, received):
                    sums[t] = sum_chips(f"{name}_sum{t}", parts[t], r, chip)
                if all(s is not None for s in sums):
                    result.value = sums

            mine = [parts[t] for t in group]
            SCHED.post(Job(f"{name}_chips{group[0]}", want, mine,
                           [jax.ShapeDtypeStruct((3, *p.shape[1:]), p.dtype) for p in mine], 3 * len(mine), to_chips,
                           after_chips))

    SCHED.post(Job(name + "_sibling", want_sibling, grads, [jax.ShapeDtypeStruct(g.shape[1:], g.dtype) for g in grads],
                   n, to_sibling, after_sibling))
    return result


def exchange_sibling(name, grads):
    n = len(grads)

    def body(*refs):
        g_refs, out_refs = refs[:n], refs[n:2 * n]
        send_sems, recv_sems = refs[2 * n:]
        x, y, c = _place()
        cps = [pltpu.make_async_remote_copy(
            src_ref=g_refs[t].at[1 - c], dst_ref=out_refs[t], send_sem=send_sems.at[t], recv_sem=recv_sems.at[t],
            device_id=(x, y, 1 - c), device_id_type=MESH) for t in range(n)]
        for cp in cps:
            cp.start()
        for cp in cps:
            cp.wait()

    return pl.pallas_call(
        body, name=name, in_specs=[HBM] * n, out_specs=[HBM] * n,
        out_shape=[jax.ShapeDtypeStruct(g.shape[1:], g.dtype) for g in grads],
        scratch_shapes=[pltpu.SemaphoreType.DMA((n,)), pltpu.SemaphoreType.DMA((n,))],
    )(*grads)


def exchange_chips(name, parts):
    n = len(parts)

    def body(*refs):
        p_refs, out_refs = refs[:n], refs[n:2 * n]
        send_sems, recv_sems = refs[2 * n:]
        x, y, c = _place()
        chips = [(1 - x, y), (x, 1 - y), (1 - x, 1 - y)]
        cps = [pltpu.make_async_remote_copy(
            src_ref=p_refs[t].at[2 * chip[0] + chip[1]], dst_ref=out_refs[t].at[j],
            send_sem=send_sems.at[t, j], recv_sem=recv_sems.at[t, j], device_id=(*chip, c), device_id_type=MESH)
            for t in range(n) for j, chip in enumerate(chips)]
        for cp in cps:
            cp.start()
        for cp in cps:
            cp.wait()

    return pl.pallas_call(
        body, name=name, in_specs=[HBM] * n, out_specs=[HBM] * n,
        out_shape=[jax.ShapeDtypeStruct((3, *p.shape[1:]), p.dtype) for p in parts],
        scratch_shapes=[pltpu.SemaphoreType.DMA((n, 3)), pltpu.SemaphoreType.DMA((n, 3))],
    )(*parts)


def _as_rows(shape):
    return (int(np.prod(shape[:-1])), shape[-1])


ELEMENTWISE_BLOCK = 256 * 1024


def _row_tile(rows, cols):
    return _tile(rows, max(128, ELEMENTWISE_BLOCK // cols // 128 * 128))


def add_sibling(name, grad, recv, core):
    rows, cols = _as_rows(grad.shape[2:])
    tr = _row_tile(rows, cols)

    def body(c_ref, g_ref, r_ref, o_ref):
        o_ref[...] = (g_ref[...].astype(F32) + r_ref[...].astype(F32)).astype(BF16)

    blk = pl.BlockSpec((None, tr, cols), lambda k, i, c_ref: (k, i, 0))
    return pl.pallas_call(
        body, name=name,
        grid_spec=pltpu.PrefetchScalarGridSpec(
            num_scalar_prefetch=1, grid=(4, rows // tr),
            in_specs=[pl.BlockSpec((None, None, tr, cols), lambda k, i, c_ref: (c_ref[0], k, i, 0)), blk],
            out_specs=blk),
        out_shape=jax.ShapeDtypeStruct((4, rows, cols), BF16), compiler_params=_params("parallel", "parallel"),
    )(core, grad.reshape(2, 4, rows, cols), recv.reshape(4, rows, cols)).reshape(recv.shape)


def sum_chips(name, part, recv, chip):
    shape = part.shape[1:]
    rows, cols = _as_rows(shape)
    tr = _row_tile(rows, cols)

    def body(c_ref, p_ref, r_ref, o_ref):
        o_ref[...] = (p_ref[...].astype(F32) + r_ref[0].astype(F32) + r_ref[1].astype(F32) + r_ref[2].astype(F32))

    return pl.pallas_call(
        body, name=name,
        grid_spec=pltpu.PrefetchScalarGridSpec(
            num_scalar_prefetch=1, grid=(rows // tr,),
            in_specs=[pl.BlockSpec((None, tr, cols), lambda i, c_ref: (c_ref[0], i, 0)),
                      pl.BlockSpec((3, tr, cols), lambda i, c_ref: (0, i, 0))],
            out_specs=pl.BlockSpec((tr, cols), lambda i, c_ref: (i, 0))),
        out_shape=jax.ShapeDtypeStruct((rows, cols), F32), compiler_params=_params("parallel"),
    )(chip, part.reshape(4, rows, cols), recv.reshape(3, rows, cols)).reshape(shape)


def sum_devices(name, gathered):
    _, rows, cols = gathered.shape

    def body(g_ref, o_ref):
        acc = g_ref[0]
        for d in range(1, N_DEV):
            acc = acc + g_ref[d]
        o_ref[...] = acc

    return pl.pallas_call(body, name=name, out_shape=jax.ShapeDtypeStruct((rows, cols), F32))(gathered)


def adamw(name, w, g, m, v):
    shape = w.shape
    rows, cols = _as_rows(shape)
    tr = _row_tile(rows, cols) if rows % 8 == 0 else rows
    c1 = 1.0 / (1.0 - ADAM_B1 ** ADAM_STEP)
    c2 = 1.0 / (1.0 - ADAM_B2 ** ADAM_STEP)

    def body(w_ref, g_ref, m_ref, v_ref, d_ref, mo_ref, vo_ref):
        g_ = g_ref[...]
        m_ = ADAM_B1 * m_ref[...] + (1.0 - ADAM_B1) * g_
        v_ = ADAM_B2 * v_ref[...] + (1.0 - ADAM_B2) * (g_ * g_)
        d_ref[...] = -ADAM_LR * ((m_ * c1) / (jnp.sqrt(v_ * c2) + ADAM_EPS) + ADAM_WD * w_ref[...])
        mo_ref[...] = m_
        vo_ref[...] = v_

    blk = pl.BlockSpec((tr, cols), lambda i: (i, 0))
    outs = pl.pallas_call(
        body, name=name, grid=(rows // tr,), in_specs=[blk] * 4, out_specs=[blk] * 3,
        out_shape=[jax.ShapeDtypeStruct((rows, cols), F32)] * 3, compiler_params=_params("parallel"),
    )(*[a.reshape(rows, cols) for a in (w, g, m, v)])
    return [o.reshape(shape) for o in outs]


def _spread_rope(r):
    z = jnp.zeros_like(r[..., :32])
    return jnp.concatenate([r[..., :32], z, r[..., 32:], z], -1)


def _gather_rope(r):
    return jnp.concatenate([r[..., :32], r[..., 64:96]], -1)


def pad_w_uq(w):
    w = w.reshape(w.shape[0], -1, MLA_NOPE + MLA_ROPE)
    return jnp.concatenate([w[..., :MLA_NOPE], _spread_rope(w[..., MLA_NOPE:])], -1).reshape(w.shape[0], -1)


def unpad_w_uq(g):
    g = g.reshape(g.shape[0], -1, 2 * MLA_NOPE)
    return jnp.concatenate([g[..., :MLA_NOPE], _gather_rope(g[..., MLA_NOPE:])], -1).reshape(g.shape[0], -1)


def pad_w_down(w):
    lat = MLA_Q_LORA + MLA_KV_LORA
    return jnp.concatenate([w[:, :lat], _spread_rope(w[:, lat:])], -1)


def unpad_w_down(g):
    lat = MLA_Q_LORA + MLA_KV_LORA
    return jnp.concatenate([g[:, :lat], _gather_rope(g[:, lat:])], -1)


def _heads(arr, width, first=0, off=0, w=None):
    return HeadCols(arr, width, lambda p: first // ATT_G + p, off, w)


def mla_forward(h16, w, gq, gkv, tables):
    cos, sin = tables
    down = mm_nn("mla_down", h16, w["down"], [F32])[0]
    cq, ckv = rms_fwd("mla_rms", down, gq, gkv)
    q = mm_nn("mla_uq", cq, w["uq"], [F32])[0]
    kv = mm_nn("mla_ukv", ckv, w["ukv"], [BF16])[0]
    qr, kp = mla_prep_fwd("mla_prep", q, kv, down, cos, sin)
    scale = (MLA_NOPE + MLA_ROPE) ** -0.5
    o, lse = softmax_attn_fwd("mla_attn", "mla", _heads(qr, 256), _heads(kp, 256), _heads(kv, 256, off=128, w=128), scale)
    m = mm_nn("mla_wo", o, w["wo"], [F32])[0]
    return m, (down, cq, ckv, qr, kp, kv, o, lse)


def mla_backward(du16, h16t, saved, w, gq, gkv, tables):
    cos, sin = tables
    down, cq, ckv, qr, kp, kv, o, lse = saved
    scale = (MLA_NOPE + MLA_ROPE) ** -0.5
    g = {"wo": mm_tn("mla_dwo", o, du16, "row", w["wo"].R, w["wo"].C)}
    do = mm_nt("mla_do", du16, w["wo"], BF16)
    dq, dk, dv = softmax_attn_bwd("mla_attn_bwd", "mla", _heads(qr, 256), _heads(kp, 256), _heads(kv, 256, off=128, w=128),
                                  _heads(o, 128), _heads(do, 128), lse, scale)
    dq16, dkv16, dkr = mla_prep_bwd("mla_prep_bwd", dq, dk, dv, cos, sin)
    g["uq"] = mm_tn("mla_duq", cq, dq16, "col", w["uq"].R, w["uq"].C)
    dcq = mm_nt("mla_dcq", dq16, w["uq"], F32)
    g["ukv"] = mm_tn("mla_dukv", ckv, dkv16, "col", w["ukv"].R, w["ukv"].C)
    dckv = mm_nt("mla_dckv", dkv16, w["ukv"], F32)
    ddown, dgq, dgkv = rms_bwd("mla_rms_bwd", down, dcq, dckv, dkr, gq, gkv)
    g["down"] = mm_tn("mla_ddown", h16t, ddown, "row", w["down"].R, w["down"].C, transposed=True)
    dh = mm_nt("mla_dh", ddown, w["down"], F32)
    return dh, g, (dgq, dgkv)


def qkv_forward(kind, h16, w, bias=None):
    qkv = mm_nn(kind + "_qkv", h16, w["qkv"], [BF16])[0]
    q, k, v = _heads(qkv, 128), _heads(qkv, 128, HEADS), _heads(qkv, 128, 2 * HEADS)
    scale = HEAD_DIM ** -0.5
    if kind == "sb":
        o, lse = stick_attn_fwd("sb_attn", q, k, v, scale)
    else:
        o, lse = softmax_attn_fwd("ca_attn", "ca", q, k, v, scale, bias)
    m = mm_nn(kind + "_wo", o, w["wo"], [F32])[0]
    return m, (qkv, o, lse)


def qkv_backward(kind, du16, h16t, saved, w, bias=None):
    qkv, o, lse = saved
    q, k, v = _heads(qkv, 128), _heads(qkv, 128, HEADS), _heads(qkv, 128, 2 * HEADS)
    scale = HEAD_DIM ** -0.5
    g = {"wo": mm_tn(kind + "_dwo", o, du16, "row", w["wo"].R, w["wo"].C)}
    do = mm_nt(kind + "_do", du16, w["wo"], BF16)
    dbias = None
    if kind == "sb":
        dq, dk, dv = stick_attn_bwd("sb_attn_bwd", q, k, v, _heads(do, 128), lse, scale)
    else:
        dq, dk, dv, dbias = softmax_attn_bwd("ca_attn_bwd", "ca", q, k, v, _heads(o, 128), _heads(do, 128), lse,
                                             scale, bias)
    dqkv = jnp.concatenate([dq, dk, dv], axis=1).astype(BF16)
    g["qkv"] = mm_tn(kind + "_dqkv", h16t, dqkv, "col", w["qkv"].R, w["qkv"].C, transposed=True)
    dh = mm_nt(kind + "_dh", dqkv, w["qkv"], F32)
    return dh, g, dbias


def mlp_forward(h16, w):
    a, z, zt = mm_nn("ffn_in", h16, w["w_in"], [F32, BF16, BF16], epilogue=_relu2_epilogue,
                     transposed=(False, False, True))
    f = mm_nn("ffn_out", z, w["w_out"], [F32])[0]
    return f, (a, zt)


def mlp_backward(du16, h16t, saved, w):
    a, zt = saved
    da = mm_nt("ffn_da", du16, w["w_out"], BF16, epilogue=_mulrelu_epilogue, extra=a)
    g = {"w_out": mm_tn("ffn_dwout", zt, du16, "row", w["w_out"].R, w["w_out"].C, transposed=True)}
    dh = mm_nt("ffn_dh", da, w["w_in"], F32)
    g["w_in"] = mm_tn("ffn_dwin", h16t, da, "col", w["w_in"].R, w["w_in"].C, transposed=True)
    return dh, g


WEIGHTS = ("ln_mix_g", "ln_mix_b", "ln_ffn_g", "ln_ffn_b", "ffn_w_in", "ffn_w_out", "mla_w_down", "mla_q_norm_g",
           "mla_w_uq", "mla_kv_norm_g", "mla_w_ukv", "mla_w_o", "sb_w_qkv", "sb_w_o", "ca_w_qkv", "ca_rel_bias",
           "ca_w_o")
MIXERS = ("mla", "sb", "ca")
LAYER_WEIGHTS = {
    "mla": (("down", "mla_w_down", "row"), ("uq", "mla_w_uq", "col"), ("ukv", "mla_w_ukv", "col"),
            ("wo", "mla_w_o", "row")),
    "sb": (("qkv", "sb_w_qkv", "col"), ("wo", "sb_w_o", "row")),
    "ca": (("qkv", "ca_w_qkv", "col"), ("wo", "ca_w_o", "row")),
    "ffn": (("w_in", "ffn_w_in", "col"), ("w_out", "ffn_w_out", "row")),
}
PAD = {"mla_w_down": pad_w_down, "mla_w_uq": pad_w_uq}
UNPAD = {"mla_w_down": unpad_w_down, "mla_w_uq": unpad_w_uq}


def _pack_rows(vectors):
    flat = jnp.concatenate([v.reshape(-1) for v in vectors])
    n = flat.shape[0]
    rows = -(-n // 1024) * 8
    offsets = np.cumsum([0] + [int(np.prod(v.shape)) for v in vectors])
    return jnp.pad(flat, (0, rows * 128 - n)).reshape(rows, 128), offsets


def _part(i, part):
    group, idx = (MIXERS[i % 3], i // 3) if part == "mix" else ("ffn", i)
    return [(key, name, how, idx) for key, name, how in LAYER_WEIGHTS[group]]


def kernel(x, ln_mix_g, ln_mix_b, ln_ffn_g, ln_ffn_b, ffn_w_in, ffn_w_out, mla_w_down, mla_q_norm_g, mla_w_uq, mla_kv_norm_g, mla_w_ukv, mla_w_o, sb_w_qkv, sb_w_o, ca_w_qkv, ca_rel_bias, ca_w_o, loss_target, m_ln_mix_g, m_ln_mix_b, m_ln_ffn_g, m_ln_ffn_b, m_ffn_w_in, m_ffn_w_out, m_mla_w_down, m_mla_q_norm_g, m_mla_w_uq, m_mla_kv_norm_g, m_mla_w_ukv, m_mla_w_o, m_sb_w_qkv, m_sb_w_o, m_ca_w_qkv, m_ca_rel_bias, m_ca_w_o, v_ln_mix_g, v_ln_mix_b, v_ln_ffn_g, v_ln_ffn_b, v_ffn_w_in, v_ffn_w_out, v_mla_w_down, v_mla_q_norm_g, v_mla_w_uq, v_mla_kv_norm_g, v_mla_w_ukv, v_mla_w_o, v_sb_w_qkv, v_sb_w_o, v_ca_w_qkv, v_ca_rel_bias, v_ca_w_o):
    w = dict(zip(WEIGHTS, (ln_mix_g, ln_mix_b, ln_ffn_g, ln_ffn_b, ffn_w_in, ffn_w_out, mla_w_down, mla_q_norm_g,
                           mla_w_uq, mla_kv_norm_g, mla_w_ukv, mla_w_o, sb_w_qkv, sb_w_o, ca_w_qkv, ca_rel_bias,
                           ca_w_o)))
    mom = dict(zip(WEIGHTS, (m_ln_mix_g, m_ln_mix_b, m_ln_ffn_g, m_ln_ffn_b, m_ffn_w_in, m_ffn_w_out, m_mla_w_down,
                             m_mla_q_norm_g, m_mla_w_uq, m_mla_kv_norm_g, m_mla_w_ukv, m_mla_w_o, m_sb_w_qkv,
                             m_sb_w_o, m_ca_w_qkv, m_ca_rel_bias, m_ca_w_o)))
    var = dict(zip(WEIGHTS, (v_ln_mix_g, v_ln_mix_b, v_ln_ffn_g, v_ln_ffn_b, v_ffn_w_in, v_ffn_w_out, v_mla_w_down,
                             v_mla_q_norm_g, v_mla_w_uq, v_mla_kv_norm_g, v_mla_w_ukv, v_mla_w_o, v_sb_w_qkv,
                             v_sb_w_o, v_ca_w_qkv, v_ca_rel_bias, v_ca_w_o)))
    S, D = x.shape[1], x.shape[2]
    xi, yi, ci = _place()
    core = ci.astype(jnp.int32).reshape(1)
    chip = (2 * xi + yi).astype(jnp.int32).reshape(1)
    me = 4 * xi + 2 * yi + ci
    tables = rope_tables(S)
    n_mla = mla_w_down.shape[0]
    lat = MLA_Q_LORA // N_DEV

    gains = jnp.pad(jnp.stack([mla_q_norm_g.reshape(-1), mla_kv_norm_g.reshape(-1)]), ((0, 6), (0, 128 - n_mla * lat)))
    gains = all_gather("ag_gains", [gains])[0]

    def full_gain(row, slot):
        return gains[:, row, slot * lat:(slot + 1) * lat].reshape(-1)

    first_matmul = {"mla": "mla_down", "sb": "sb_qkv", "ca": "ca_qkv"}

    def post_gather(i, part):
        kind = MIXERS[i % 3]
        specs = _part(i, part)
        shards = [PAD.get(name, lambda a: a)(w[name][idx]).astype(BF16) for _, name, _, idx in specs]
        if part == "mix":
            wants = ("ffn_in", "ffn_out") if i > 0 else (None, None)
            return [(specs, gather_jobs(f"ag_mix{i}", shards, *wants))]
        w_in = ("ffn_out", first_matmul[kind]) if i > 0 else ("mla_down", "mla_uq")
        return [(specs[:1], gather_jobs(f"ag_w_in{i}", shards[:1], *w_in)),
                (specs[1:], gather_jobs(f"ag_w_out{i}", shards[1:], kind + "_attn", kind + "_wo"))]

    def gathered(posted):
        return {key: Weight(how, g) for specs, future in posted for (key, _, how, _), g in zip(specs, future.get())}

    long_attention = "sb_attn_bwd"

    def post_scatter(i, part, g):
        kind = MIXERS[i % 3]
        specs = _part(i, part)
        n = len(specs)
        grads_of = lambda sp: [g[key] for key, _, _, _ in sp]
        if part == "ffn":
            w_in_rides = long_attention if MIXERS[(i - 1) % 3] == "sb" and i > 0 else kind + "_attn_bwd"
            return [(specs[:1], scatter_jobs(f"rs_w_in{i}", grads_of(specs[:1]), core, chip, kind + "_dwo",
                                             [([0], w_in_rides)])),
                    (specs[1:], scatter_jobs(f"rs_w_out{i}", grads_of(specs[1:]), core, chip, kind + "_dwo",
                                             [([0], kind + "_attn_bwd")]))]
        if i == 0:
            wants = (None, [(list(range(n)), None)])
        elif MIXERS[(i - 1) % 3] == "sb":
            wants = ("ffn_da", [(list(range(n)), long_attention)])
        else:
            wants = ("ffn_da", [(list(range(n - 1)), "ffn_dwout"), ([n - 1], "ffn_dh")])
        return [(specs, scatter_jobs(f"rs_mix{i}", grads_of(specs), core, chip, *wants))]

    SCHED.pending.clear()
    bias = rel_bias_blocks("ca_bias", ca_rel_bias[0], _att_tiles("ca", S)[1])

    h, h16 = x[0], x[0].astype(BF16)
    h16t = transpose("x_t", h16)
    saved, layers = [], []
    mix_w, ffn_w = post_gather(0, "mix"), None
    for i in range(DEPTH):
        kind, slot = MIXERS[i % 3], i // 3
        lw = gathered(mix_w)
        if i == 0:
            ffn_w = post_gather(0, "ffn")
        if kind == "mla":
            mix, s_mix = mla_forward(h16, lw, full_gain(0, slot), full_gain(1, slot), tables)
        else:
            mix, s_mix = qkv_forward(kind, h16, lw, bias if kind == "ca" else None)
        y, y16, y16t, xh1, rs1 = ln_fwd("ln_mix", h, mix, ln_mix_g[i], ln_mix_b[i])
        lw.update(gathered(ffn_w))
        if i + 1 < DEPTH:
            mix_w, ffn_w = post_gather(i + 1, "mix"), post_gather(i + 1, "ffn")
        f, s_mlp = mlp_forward(y16, lw)
        y2, y2_16, y2_16t, xh2, rs2 = ln_fwd("ln_ffn", y, f, ln_ffn_g[i], ln_ffn_b[i])
        saved.append((h16t, s_mix, xh1, rs1, y16t, s_mlp, xh2, rs2))
        layers.append(lw)
        h, h16, h16t = y2, y2_16, y2_16t
    sq, dy = loss_fwd_bwd("loss", h, loss_target[0])
    loss = 0.5 / D * lax.psum(sq[0, 0], ("x", "y", "c"))

    ga, gb = dy, None
    grads = {name: [None] * w[name].shape[0] for name in WEIGHTS}
    dbias = None
    scattered = []
    for i in reversed(range(DEPTH)):
        kind, slot = MIXERS[i % 3], i // 3
        lw = layers[i]
        h16_in, s_mix, xh1, rs1, y16, s_mlp, xh2, rs2 = saved[i]
        du, du16, grads["ln_ffn_g"][i], grads["ln_ffn_b"][i] = ln_bwd("ln_ffn_bwd", ga, gb, xh2, rs2, ln_ffn_g[i])
        dh_mlp, g_mlp = mlp_backward(du16, y16, s_mlp, lw)
        scattered += post_scatter(i, "ffn", g_mlp)
        du, du16, grads["ln_mix_g"][i], grads["ln_mix_b"][i] = ln_bwd("ln_mix_bwd", du, dh_mlp, xh1, rs1, ln_mix_g[i])
        if kind == "mla":
            dh_mix, g_mix, (dgq, dgkv) = mla_backward(du16, h16_in, s_mix, lw, full_gain(0, slot), full_gain(1, slot),
                                                      tables)
            grads["mla_q_norm_g"][slot], grads["mla_kv_norm_g"][slot] = dgq, dgkv
        else:
            dh_mix, g_mix, db = qkv_backward(kind, du16, h16_in, s_mix, lw, bias if kind == "ca" else None)
            dbias = db if kind == "ca" else dbias
        scattered += post_scatter(i, "mix", g_mix)
        ga, gb = du, dh_mix
    grad_x = axpy("grad_x", ga, gb)[None]
    SCHED.flush()
    for specs, future in scattered:
        for (_, name, _, idx), g in zip(specs, future.get()):
            grads[name][idx] = UNPAD.get(name, lambda a: a)(g)
    grads["ca_rel_bias"][0] = rel_bias_grad("ca_bias_grad", dbias)

    small = ("ln_mix_g", "ln_mix_b", "ln_ffn_g", "ln_ffn_b", "ca_rel_bias", "mla_q_norm_g", "mla_kv_norm_g")
    packed, offsets = _pack_rows([g for name in small for g in grads[name]])
    total = sum_devices("sum_small", all_gather("ag_small", [packed])[0]).reshape(-1)
    pos = 0
    for name in small:
        for idx, g in enumerate(grads[name]):
            full = total[offsets[pos]:offsets[pos + 1]]
            pos += 1
            if name in ("mla_q_norm_g", "mla_kv_norm_g"):
                full = lax.dynamic_slice(full, (me * lat,), (lat,))
            grads[name][idx] = full.reshape(w[name].shape[1:])

    g_out, d_out, m_out, v_out = [], [], [], []
    for name in WEIGHTS:
        g = jnp.stack(grads[name])
        delta, new_m, new_v = adamw("adamw_" + name, w[name], g, mom[name], var[name])
        g_out.append(g)
        d_out.append(delta)
        m_out.append(new_m)
        v_out.append(new_v)
    return (loss, grad_x, *g_out, *d_out, *m_out, *v_out)
```

```python
import functools

import numpy as np
import jax
import jax.numpy as jnp
from jax import lax
from jax.experimental import pallas as pl
from jax.experimental.pallas import tpu as pltpu

F32 = jnp.float32
BF16 = jnp.bfloat16
MESH = pl.DeviceIdType.MESH
N_DEV = 8

DEPTH = 4
CHUNK = 64
CHUNK_SHIFT = 6
HEADS = 16
HEAD_DIM = 128
MLA_Q_LORA = 512
MLA_KV_LORA = 512
MLA_NOPE = 128
MLA_ROPE = 64
ROPE_THETA = 10000.0
CA_LEFT_CHUNKS = 8
REL_CLIP_LEFT = 128
REL_TABLE = REL_CLIP_LEFT + CHUNK
LN_EPS = 1e-5
RMS_EPS = 1e-6
ALPHA = (2.0 * DEPTH) ** 0.25
NEG = -1e30
ADAM_LR = 0.001
ADAM_B1 = 0.9
ADAM_B2 = 0.999
ADAM_EPS = 1e-08
ADAM_WD = 0.01
ADAM_STEP = 10

V7X_VMEM_BYTES = 64 * 1024 * 1024
VMEM_LIMIT = V7X_VMEM_BYTES - 8 * 1024 * 1024
ATT_TQ = 512
ATT_TK = 256
ATT_G = 2


def _params(*sem):
    return pltpu.CompilerParams(dimension_semantics=sem if sem else None, vmem_limit_bytes=VMEM_LIMIT)


HBM = pl.BlockSpec(memory_space=pl.ANY)


class Job:
    def __init__(self, name, want, operands, out_shape, n_copies, copies, done, aliases=None):
        self.name, self.want, self.operands, self.out_shape = name, want, list(operands), list(out_shape)
        self.n_copies, self.copies, self.done, self.aliases = n_copies, copies, done, dict(aliases or {})

    def sems(self):
        return [pltpu.SemaphoreType.DMA((self.n_copies,)), pltpu.SemaphoreType.DMA((self.n_copies,))]


class Scheduler:
    def __init__(self):
        self.pending = []

    def post(self, job):
        self.pending.append(job)

    def take(self, name):
        mine = [job for job in self.pending if job.want is not None and job.want in name]
        self.pending = [job for job in self.pending if job not in mine]
        return mine

    def flush(self):
        while self.pending:
            job = self.pending.pop(0)
            n_in, n_out = len(job.operands), len(job.out_shape)

            def body(*refs, job=job, n_in=n_in, n_out=n_out):
                cps = job.copies(refs[:n_in], refs[n_in:n_in + n_out], refs[-2], refs[-1])
                for cp in cps:
                    cp.start()
                for cp in cps:
                    cp.wait()

            outs = pl.pallas_call(
                body, name=job.name, in_specs=[HBM] * n_in, out_specs=[HBM] * n_out, out_shape=job.out_shape,
                scratch_shapes=job.sems(), input_output_aliases=job.aliases)(*job.operands)
            job.done(list(outs))


SCHED = Scheduler()


def _call(body, operands, *, name, grid, in_specs, out_specs, out_shape, scratch_shapes=(), semantics):
    jobs = SCHED.take(name)
    if not jobs:
        return list(pl.pallas_call(
            body, name=name, grid=grid, in_specs=list(in_specs), out_specs=list(out_specs), out_shape=list(out_shape),
            scratch_shapes=list(scratch_shapes), compiler_params=_params(*semantics))(*operands))
    n_in, n_out, n_scr = len(operands), len(out_shape), len(scratch_shapes)
    j_in = np.cumsum([0] + [len(job.operands) for job in jobs])
    j_out = np.cumsum([0] + [len(job.out_shape) for job in jobs])
    a, b = n_in, n_in + int(j_in[-1])
    c, d = b + n_out, b + n_out + int(j_out[-1])

    def carrying(*refs):
        def copies():
            sems = refs[d + n_scr:]
            return [cp for k, job in enumerate(jobs)
                    for cp in job.copies(refs[a + j_in[k]:a + j_in[k + 1]], refs[c + j_out[k]:c + j_out[k + 1]],
                                         sems[2 * k], sems[2 * k + 1])]

        ids = [pl.program_id(k) for k in range(len(grid))]
        first = functools.reduce(jnp.logical_and, [i == 0 for i in ids])
        last = functools.reduce(jnp.logical_and, [i == g - 1 for i, g in zip(ids, grid)])

        @pl.when(first)
        def _():
            for cp in copies():
                cp.start()

        body(*refs[:a], *refs[b:c], *refs[d:d + n_scr])

        @pl.when(last)
        def _():
            for cp in copies():
                cp.wait()

    aliases = {n_in + int(j_in[k]) + i: n_out + int(j_out[k]) + o for k, job in enumerate(jobs)
               for i, o in job.aliases.items()}
    outs = pl.pallas_call(
        carrying, name=name + "_carry", grid=grid, in_specs=list(in_specs) + [HBM] * int(j_in[-1]),
        out_specs=list(out_specs) + [HBM] * int(j_out[-1]),
        out_shape=list(out_shape) + [s for job in jobs for s in job.out_shape],
        scratch_shapes=list(scratch_shapes) + [s for job in jobs for s in job.sems()],
        input_output_aliases=aliases,
        compiler_params=_params(*(["arbitrary"] * len(grid))))(*operands, *[o for job in jobs for o in job.operands])
    for k, job in enumerate(jobs):
        job.done(list(outs[n_out + int(j_out[k]):n_out + int(j_out[k + 1])]))
    return list(outs[:n_out])


def _matmul(name, a, b, *, contract, grid, a_spec, b_spec, o_specs, out_shape, acc_shape,
            epilogue=None, extra=(), extra_specs=()):
    nk = grid[2]
    n_extra = len(extra)
    n_out = len(out_shape)

    def finish(acc, e_refs, o_refs):
        outs = epilogue(acc, *[e[...] for e in e_refs]) if epilogue else (acc,)
        for o_ref, val in zip(o_refs, outs):
            o_ref[...] = val.astype(o_ref.dtype)

    def product(a_ref, b_ref):
        return lax.dot_general(a_ref[...], b_ref[...], (contract, ((), ())), preferred_element_type=F32)

    def body_single(*refs):
        finish(product(refs[0], refs[1]), refs[2:2 + n_extra], refs[2 + n_extra:2 + n_extra + n_out])

    def body(*refs):
        a_ref, b_ref = refs[0], refs[1]
        acc_ref = refs[-1]
        k = pl.program_id(2)

        @pl.when(k == 0)
        def _():
            acc_ref[...] = jnp.zeros_like(acc_ref)

        acc_ref[...] += product(a_ref, b_ref)

        @pl.when(k == nk - 1)
        def _():
            finish(acc_ref[...], refs[2:2 + n_extra], refs[2 + n_extra:2 + n_extra + n_out])

    return _call(
        body_single if nk == 1 else body, [a, b, *extra], name=name, grid=grid,
        in_specs=[a_spec, b_spec, *extra_specs], out_specs=o_specs, out_shape=out_shape,
        scratch_shapes=[] if nk == 1 else [pltpu.VMEM(acc_shape, F32)],
        semantics=("parallel", "parallel", "arbitrary"))


MATMUL_BLOCK_BYTES = 40 * 1024 * 1024
MAX_TK = 2048
MULTI_TK = 512


def _fit_tn(n, tm, tk, nk, out_bytes):
    cands = sorted({n} | {t for t in range(128, n, 128) if n % t == 0}, reverse=True)
    for tn in cands:
        need = 2 * 2 * (tm * tk + tk * tn) + 2 * tm * tn * out_bytes + (tm * tn * 4 if nk > 1 else 0) + tm * tn * 4
        if need <= MATMUL_BLOCK_BYTES:
            return tn
    return cands[-1]


def _itemsize(dtypes):
    return sum(jnp.dtype(d).itemsize for d in dtypes)


def _tile(n, pref):
    if n <= pref:
        return n
    t = pref
    while t >= 128:
        if n % t == 0 and t % 128 == 0:
            return t
        t -= 128
    return n


class Weight:
    def __init__(self, kind, arr):
        self.kind = kind
        self.arr = arr
        self.R, self.C = arr.shape[1], arr.shape[2]

    @property
    def two_d(self):
        return self.arr.reshape(N_DEV * self.R, self.C)


def mm_nn(name, a, w, out_dtypes, epilogue=None, transposed=()):
    M, K = a.shape
    tm = M
    tk = K if K <= MAX_TK else _tile(K, MULTI_TK)
    nk = K // tk
    if w.kind == "row":
        b = w.two_d
        N = w.C
        tn = _fit_tn(N, tm, tk, nk, _itemsize(out_dtypes))
        b_spec = pl.BlockSpec((tk, tn), lambda i, j, k: (k, j))
    else:
        b = w.arr
        N = N_DEV * w.C
        tn = _fit_tn(w.C, tm, tk, nk, _itemsize(out_dtypes))
        per = w.C // tn
        b_spec = pl.BlockSpec((None, tk, tn), lambda i, j, k: (j // per, k, j % per))
    grid = (M // tm, N // tn, nk)
    flip = [t < len(transposed) and transposed[t] for t in range(len(out_dtypes))]
    return _matmul(
        name, a, b, contract=((1,), (0,)), grid=grid,
        a_spec=pl.BlockSpec((tm, tk), lambda i, j, k: (i, k)), b_spec=b_spec,
        o_specs=[pl.BlockSpec((tn, tm), lambda i, j, k: (j, i)) if f else pl.BlockSpec((tm, tn), lambda i, j, k: (i, j))
                 for f in flip],
        out_shape=[jax.ShapeDtypeStruct((N, M) if f else (M, N), d) for f, d in zip(flip, out_dtypes)],
        acc_shape=(tm, tn), epilogue=epilogue)


def mm_nt(name, dy, w, out_dtype, epilogue=None, extra=None):
    M, N = dy.shape
    tm = M
    out_bytes = jnp.dtype(out_dtype).itemsize + (0 if extra is None else extra.dtype.itemsize)
    if w.kind == "row":
        b = w.two_d
        kin = N_DEV * w.R
        tk = N if N <= MAX_TK else _tile(N, MULTI_TK)
        tn = _fit_tn(kin, tm, tk, N // tk, out_bytes)
        b_spec = pl.BlockSpec((tn, tk), lambda i, j, k: (j, k))
    else:
        b = w.arr
        kin = w.R
        tk = _tile(w.C, MULTI_TK)
        per = w.C // tk
        tn = _fit_tn(kin, tm, tk, N // tk, out_bytes)
        b_spec = pl.BlockSpec((None, tn, tk), lambda i, j, k: (k // per, j, k % per))
    grid = (M // tm, kin // tn, N // tk)
    o_spec = pl.BlockSpec((tm, tn), lambda i, j, k: (i, j))
    return _matmul(
        name, dy, b, contract=((1,), (1,)), grid=grid,
        a_spec=pl.BlockSpec((tm, tk), lambda i, j, k: (i, k)), b_spec=b_spec, o_specs=[o_spec],
        out_shape=[jax.ShapeDtypeStruct((M, kin), out_dtype)], acc_shape=(tm, tn), epilogue=epilogue,
        extra=() if extra is None else (extra,), extra_specs=() if extra is None else (o_spec,))[0]


TRANSPOSE_TILE = 512


def transpose(name, x):
    S, n = x.shape
    ts, tn = _tile(S, TRANSPOSE_TILE), _tile(n, TRANSPOSE_TILE)

    def body(x_ref, o_ref):
        o_ref[...] = x_ref[...].T

    return pl.pallas_call(
        body, name=name, grid=(S // ts, n // tn), in_specs=[pl.BlockSpec((ts, tn), lambda i, j: (i, j))],
        out_specs=pl.BlockSpec((tn, ts), lambda i, j: (j, i)), out_shape=jax.ShapeDtypeStruct((n, S), x.dtype),
        compiler_params=_params("parallel", "parallel"),
    )(x)


def mm_tn(name, x, dy, kind, R, C, transposed=False):
    if not transposed:
        x = transpose(name + "_t", x)
    kin, S = x.shape
    N = dy.shape[1]
    tk = S if S <= MAX_TK else _tile(S, MULTI_TK)
    nk = S // tk
    if kind == "col":
        tm = kin
        tn = _fit_tn(C, tm, tk, nk, 2)
        per = C // tn
        grid = (1, N // tn, nk)
        o_spec = pl.BlockSpec((None, None, tm, tn), lambda i, j, k: ((j // per) % 2, (j // per) // 2, 0, j % per))
    else:
        tm = R
        tn = _fit_tn(N, tm, tk, nk, 2)
        grid = (N_DEV, N // tn, nk)
        o_spec = pl.BlockSpec((None, None, tm, tn), lambda i, j, k: (i % 2, i // 2, 0, j))
    return _matmul(
        name, x, dy, contract=((1,), (0,)), grid=grid,
        a_spec=pl.BlockSpec((tm, tk), lambda i, j, k: (i, k)),
        b_spec=pl.BlockSpec((tk, tn), lambda i, j, k: (k, j)), o_specs=[o_spec],
        out_shape=[jax.ShapeDtypeStruct((2, 4, R, C), BF16)], acc_shape=(tm, tn))[0]


def _relu2_epilogue(acc):
    r = jnp.maximum(acc, 0.0)
    z = (r * r).astype(BF16)
    return acc, z, z.T


def _mulrelu_epilogue(acc, a):
    return (acc * (2.0 * jnp.maximum(a, 0.0)),)


ROW_TILE = 256


def ln_fwd(name, h, m, g, b):
    S, D = h.shape
    ts = _tile(S, ROW_TILE)

    def body(h_ref, m_ref, g_ref, b_ref, y_ref, y16_ref, yt_ref, xh_ref, rs_ref):
        u = ALPHA * h_ref[...] + m_ref[...]
        mu = jnp.mean(u, axis=-1, keepdims=True)
        d = u - mu
        var = jnp.mean(d * d, axis=-1, keepdims=True)
        rstd = lax.rsqrt(var + LN_EPS)
        xh = d * rstd
        y = xh * g_ref[...] + b_ref[...]
        y16 = y.astype(BF16)
        y_ref[...] = y
        y16_ref[...] = y16
        yt_ref[...] = y16.T
        xh_ref[...] = xh
        rs_ref[...] = jnp.broadcast_to(rstd, rs_ref.shape)

    row = pl.BlockSpec((ts, D), lambda i: (i, 0))
    vec = pl.BlockSpec((1, D), lambda i: (0, 0))
    return pl.pallas_call(
        body, name=name, grid=(S // ts,), in_specs=[row, row, vec, vec],
        out_specs=[row, row, pl.BlockSpec((D, ts), lambda i: (0, i)), row, pl.BlockSpec((ts, 128), lambda i: (i, 0))],
        out_shape=[jax.ShapeDtypeStruct((S, D), F32), jax.ShapeDtypeStruct((S, D), BF16),
                   jax.ShapeDtypeStruct((D, S), BF16), jax.ShapeDtypeStruct((S, D), F32),
                   jax.ShapeDtypeStruct((S, 128), F32)],
        compiler_params=_params("parallel"),
    )(h, m, g.reshape(1, D), b.reshape(1, D))


def ln_bwd(name, ga, gb, xhat, rstd, g):
    S, D = xhat.shape
    ts = _tile(S, ROW_TILE)
    two = gb is not None

    def body(*refs):
        if two:
            ga_ref, gb_ref, xh_ref, rs_ref, g_ref, du_ref, du16_ref, dg_ref, db_ref = refs
            dy = ALPHA * ga_ref[...] + gb_ref[...]
        else:
            ga_ref, xh_ref, rs_ref, g_ref, du_ref, du16_ref, dg_ref, db_ref = refs
            dy = ga_ref[...]
        xh = xh_ref[...]

        @pl.when(pl.program_id(0) == 0)
        def _():
            dg_ref[...] = jnp.zeros_like(dg_ref)
            db_ref[...] = jnp.zeros_like(db_ref)

        dg_ref[...] += jnp.sum(dy * xh, axis=0, keepdims=True)
        db_ref[...] += jnp.sum(dy, axis=0, keepdims=True)
        dxh = dy * g_ref[...]
        m1 = jnp.mean(dxh, axis=-1, keepdims=True)
        m2 = jnp.mean(dxh * xh, axis=-1, keepdims=True)
        du = rs_ref[:, 0:1] * (dxh - m1 - xh * m2)
        du_ref[...] = du
        du16_ref[...] = du.astype(BF16)

    row = pl.BlockSpec((ts, D), lambda i: (i, 0))
    vec = pl.BlockSpec((1, D), lambda i: (0, 0))
    stat = pl.BlockSpec((ts, 128), lambda i: (i, 0))
    ins = [ga, gb, xhat, rstd, g.reshape(1, D)] if two else [ga, xhat, rstd, g.reshape(1, D)]
    in_specs = [row, row, row, stat, vec] if two else [row, row, stat, vec]
    return pl.pallas_call(
        body, name=name, grid=(S // ts,), in_specs=in_specs, out_specs=[row, row, vec, vec],
        out_shape=[jax.ShapeDtypeStruct((S, D), F32), jax.ShapeDtypeStruct((S, D), BF16),
                   jax.ShapeDtypeStruct((1, D), F32), jax.ShapeDtypeStruct((1, D), F32)],
        compiler_params=_params("arbitrary"),
    )(*ins)


def loss_fwd_bwd(name, y, target):
    S, D = y.shape
    ts = _tile(S, ROW_TILE)

    def body(y_ref, t_ref, l_ref, dy_ref):
        @pl.when(pl.program_id(0) == 0)
        def _():
            l_ref[...] = jnp.zeros_like(l_ref)

        e = y_ref[...] - t_ref[...]
        l_ref[...] += jnp.sum(e * e)
        dy_ref[...] = e * (1.0 / D)

    row = pl.BlockSpec((ts, D), lambda i: (i, 0))
    return pl.pallas_call(
        body, name=name, grid=(S // ts,), in_specs=[row, row],
        out_specs=[pl.BlockSpec((1, 128), lambda i: (0, 0)), row],
        out_shape=[jax.ShapeDtypeStruct((1, 128), F32), jax.ShapeDtypeStruct((S, D), F32)],
        compiler_params=_params("arbitrary"),
    )(y, target)


def axpy(name, ga, gb):
    S, D = ga.shape
    ts = _tile(S, ROW_TILE)

    def body(a_ref, b_ref, o_ref):
        o_ref[...] = ALPHA * a_ref[...] + b_ref[...]

    row = pl.BlockSpec((ts, D), lambda i: (i, 0))
    return pl.pallas_call(body, name=name, grid=(S // ts,), in_specs=[row, row], out_specs=row,
                          out_shape=jax.ShapeDtypeStruct((S, D), F32), compiler_params=_params("parallel"))(ga, gb)


def rms_fwd(name, down, gq, gkv):
    S = down.shape[0]
    ts = _tile(S, ROW_TILE)
    L = MLA_Q_LORA

    def body(d_ref, gq_ref, gkv_ref, q_ref, kv_ref):
        for lo, g_ref, o_ref in ((0, gq_ref, q_ref), (L, gkv_ref, kv_ref)):
            x = d_ref[:, lo:lo + L]
            r = lax.rsqrt(jnp.mean(x * x, axis=-1, keepdims=True) + RMS_EPS)
            o_ref[...] = (x * r * g_ref[...]).astype(BF16)

    vec = pl.BlockSpec((1, L), lambda i: (0, 0))
    out = pl.BlockSpec((ts, L), lambda i: (i, 0))
    return pl.pallas_call(
        body, name=name, grid=(S // ts,), in_specs=[pl.BlockSpec((ts, down.shape[1]), lambda i: (i, 0)), vec, vec],
        out_specs=[out, out], out_shape=[jax.ShapeDtypeStruct((S, L), BF16)] * 2, compiler_params=_params("parallel"),
    )(down, gq.reshape(1, L), gkv.reshape(1, L))


def rms_bwd(name, down, dq, dkv, dkr, gq, gkv):
    S, W = down.shape
    ts = _tile(S, ROW_TILE)
    L = MLA_Q_LORA

    def body(d_ref, dq_ref, dkv_ref, dkr_ref, gq_ref, gkv_ref, o_ref, dgq_ref, dgkv_ref):
        @pl.when(pl.program_id(0) == 0)
        def _():
            dgq_ref[...] = jnp.zeros_like(dgq_ref)
            dgkv_ref[...] = jnp.zeros_like(dgkv_ref)

        for lo, dy_ref, g_ref, dg_ref in ((0, dq_ref, gq_ref, dgq_ref), (L, dkv_ref, gkv_ref, dgkv_ref)):
            x = d_ref[:, lo:lo + L]
            dy = dy_ref[...]
            r = lax.rsqrt(jnp.mean(x * x, axis=-1, keepdims=True) + RMS_EPS)
            dg_ref[...] += jnp.sum(dy * x * r, axis=0, keepdims=True)
            dyg = dy * g_ref[...]
            dx = r * dyg - x * (r * r * r) * jnp.mean(dyg * x, axis=-1, keepdims=True)
            o_ref[:, lo:lo + L] = dx.astype(BF16)
        o_ref[:, 2 * L:] = dkr_ref[...].astype(BF16)

    vec = pl.BlockSpec((1, L), lambda i: (0, 0))
    lat = pl.BlockSpec((ts, L), lambda i: (i, 0))
    full = pl.BlockSpec((ts, W), lambda i: (i, 0))
    return pl.pallas_call(
        body, name=name, grid=(S // ts,),
        in_specs=[full, lat, lat, pl.BlockSpec((ts, 128), lambda i: (i, 0)), vec, vec],
        out_specs=[full, vec, vec],
        out_shape=[jax.ShapeDtypeStruct((S, W), BF16), jax.ShapeDtypeStruct((1, L), F32), jax.ShapeDtypeStruct((1, L), F32)],
        compiler_params=_params("arbitrary"),
    )(down, dq, dkv, dkr, gq.reshape(1, L), gkv.reshape(1, L))


def rope_tables(S):
    half = MLA_ROPE // 2
    inv = (np.float32(ROPE_THETA) ** (-np.arange(half, dtype=np.float32) / np.float32(half))).astype(np.float32)
    ang = np.arange(S, dtype=np.float32)[:, None] * inv[None, :]
    cos, sin = np.cos(ang).astype(np.float32), np.sin(ang).astype(np.float32)
    z = np.zeros_like(cos)
    return (jnp.asarray(np.concatenate([cos, z, cos, z], 1)), jnp.asarray(np.concatenate([-sin, z, sin, z], 1)))


def _rot(x, cos, sin):
    return x * cos + pltpu.roll(x, 64, 1) * sin


def mla_prep_fwd(name, q, kv, down, cos, sin):
    S = q.shape[0]
    ts = _tile(S, ROW_TILE)

    def body(q_ref, kv_ref, kr_ref, c_ref, s_ref, qo_ref, ko_ref):
        c, s = c_ref[...], s_ref[...]
        key = _rot(kr_ref[...], c, s).astype(BF16)
        for h in range(HEADS):
            lo = 256 * h
            qo_ref[:, lo:lo + 128] = q_ref[:, lo:lo + 128].astype(BF16)
            qo_ref[:, lo + 128:lo + 256] = _rot(q_ref[:, lo + 128:lo + 256], c, s).astype(BF16)
            ko_ref[:, lo:lo + 128] = kv_ref[:, lo:lo + 128]
            ko_ref[:, lo + 128:lo + 256] = key

    heads = pl.BlockSpec((ts, HEADS * 256), lambda i: (i, 0))
    tab = pl.BlockSpec((ts, 128), lambda i: (i, 0))
    return pl.pallas_call(
        body, name=name, grid=(S // ts,),
        in_specs=[heads, heads, pl.BlockSpec((ts, 128), lambda i: (i, 2 * MLA_Q_LORA // 128)), tab, tab],
        out_specs=[heads, heads], out_shape=[jax.ShapeDtypeStruct(q.shape, BF16)] * 2,
        compiler_params=_params("parallel"),
    )(q, kv, down, cos, sin)


def mla_prep_bwd(name, dq, dk, dv, cos, sin):
    S = dq.shape[0]
    ts = _tile(S, ROW_TILE)

    def body(dq_ref, dk_ref, dv_ref, c_ref, s_ref, qo_ref, kvo_ref, kr_ref):
        c, s = c_ref[...], -s_ref[...]
        key = jnp.zeros((ts, 128), F32)
        for h in range(HEADS):
            lo = 256 * h
            qo_ref[:, lo:lo + 128] = dq_ref[:, lo:lo + 128].astype(BF16)
            qo_ref[:, lo + 128:lo + 256] = _rot(dq_ref[:, lo + 128:lo + 256], c, s).astype(BF16)
            kvo_ref[:, lo:lo + 128] = dk_ref[:, lo:lo + 128].astype(BF16)
            kvo_ref[:, lo + 128:lo + 256] = dv_ref[:, 128 * h:128 * h + 128].astype(BF16)
            key = key + dk_ref[:, lo + 128:lo + 256]
        kr_ref[...] = _rot(key, c, s)

    heads = pl.BlockSpec((ts, HEADS * 256), lambda i: (i, 0))
    tab = pl.BlockSpec((ts, 128), lambda i: (i, 0))
    return pl.pallas_call(
        body, name=name, grid=(S // ts,),
        in_specs=[heads, heads, pl.BlockSpec((ts, HEADS * 128), lambda i: (i, 0)), tab, tab],
        out_specs=[heads, heads, tab],
        out_shape=[jax.ShapeDtypeStruct(dq.shape, BF16), jax.ShapeDtypeStruct(dq.shape, BF16),
                   jax.ShapeDtypeStruct((S, 128), F32)],
        compiler_params=_params("parallel"),
    )(dq, dk, dv, cos, sin)


def _dot_nt(a, b):
    return lax.dot_general(a, b, (((1,), (1,)), ((), ())), preferred_element_type=F32)


def _dot_tn(a, b):
    return lax.dot_general(a, b, (((0,), (0,)), ((), ())), preferred_element_type=F32)


def _dot(a, b):
    return jnp.dot(a, b, preferred_element_type=F32)


def _positions(i, j, TQ, TK):
    row = i * TQ + lax.broadcasted_iota(jnp.int32, (TQ, TK), 0)
    col = j * TK + lax.broadcasted_iota(jnp.int32, (TQ, TK), 1)
    return row, col


def _softmax_mask(mode, row, col):
    rc, cc = row >> CHUNK_SHIFT, col >> CHUNK_SHIFT
    if mode == "mla":
        return cc <= rc
    return (cc <= rc) & (cc >= rc - CA_LEFT_CHUNKS)


def _key_blocks(mode, i, TQ, TK):
    per = TQ // TK
    if mode == "ca":
        lo = jnp.maximum(i - (CA_LEFT_CHUNKS * CHUNK) // TK, 0)
        return lo, 0, i - lo + 1
    return 0, i * per, per


class HeadCols:
    def __init__(self, arr, width, index, off=0, w=None):
        self.arr, self.width, self.index, self.off = arr, width, index, off
        self.w = width if w is None else w

    def rows(self, T):
        return pl.BlockSpec((T, ATT_G * self.width), lambda p, i: (i, self.index(p)))

    def full(self, S):
        return pl.BlockSpec((S, ATT_G * self.width), lambda p, i: (0, self.index(p)))

    def lanes(self, g):
        lo = g * self.width + self.off
        return slice(lo, lo + self.w)


def _att_tiles(mode, S):
    tk = min(ATT_TK, S)
    return (tk if mode == "ca" else min(ATT_TQ, S)), tk


def _walk(lo, n, per, step, carry, descending=False):
    tail = [lo + n + d for d in range(per)]
    if descending:
        for j in reversed(tail):
            carry = step(j, carry, True)
        return lax.fori_loop(0, n, lambda t, c: step(lo + n - 1 - t, c, False), carry)
    carry = lax.fori_loop(0, n, lambda t, c: step(lo + t, c, False), carry)
    for j in tail:
        carry = step(j, carry, True)
    return carry


def softmax_attn_fwd(name, mode, q, k, v, scale, bias=None):
    S = q.arr.shape[0]
    TQ, TK = _att_tiles(mode, S)
    G, dv = ATT_G, v.w

    def body(*refs):
        if bias is not None:
            q_ref, k_ref, v_ref, b_ref, o_ref, lse_ref = refs
        else:
            q_ref, k_ref, v_ref, o_ref, lse_ref = refs
        i = pl.program_id(1)
        qs = [q_ref[:, q.lanes(g)] for g in range(G)]

        def block(g, j, carry, mask, ks):
            m, l, acc = carry
            s = _dot_nt(qs[g], k_ref[ks, k.lanes(g)]) * scale
            if bias is not None:
                s = s + b_ref[g, jnp.minimum(i - j, 2)]
            if mask is not None:
                s = jnp.where(mask, s, NEG)
            m_new = jnp.maximum(m, jnp.max(s, axis=-1, keepdims=True))
            a = jnp.exp(m - m_new)
            p = jnp.exp(s - m_new)
            if mask is not None:
                p = jnp.where(mask, p, 0.0)
            l = a * l + jnp.sum(p, axis=-1, keepdims=True)
            acc = a * acc + _dot(p.astype(BF16), v_ref[ks, v.lanes(g)])
            return m_new, l, acc

        def step(j, carry, masked):
            ks = pl.ds(pl.multiple_of(j * TK, TK), TK)
            mask = _softmax_mask(mode, *_positions(i, j, TQ, TK)) if masked or mode == "ca" else None
            return tuple(block(g, j, carry[g], mask, ks) for g in range(G))

        init = (jnp.full((TQ, 1), NEG, F32), jnp.zeros((TQ, 1), F32), jnp.zeros((TQ, dv), F32))
        lo, n, per = _key_blocks(mode, i, TQ, TK)
        if mode == "ca":
            out = lax.fori_loop(lo, lo + per, lambda j, c: step(j, c, True), (init,) * G)
        else:
            out = _walk(lo, n, per, step, (init,) * G)
        for g, (m, l, acc) in enumerate(out):
            o_ref[:, g * dv:(g + 1) * dv] = (acc / l).astype(BF16)
            lse_ref[:, g * 128:(g + 1) * 128] = jnp.broadcast_to(m + jnp.log(l), (TQ, 128))

    in_specs = [q.rows(TQ), k.full(S), v.full(S)]
    ins = [q.arr, k.arr, v.arr]
    if bias is not None:
        in_specs.append(pl.BlockSpec((G, 3, TK, TK), lambda p, i: (p, 0, 0, 0)))
        ins.append(bias)
    return _call(
        body, ins, name=name, grid=(HEADS // G, S // TQ), in_specs=in_specs,
        out_specs=[pl.BlockSpec((TQ, G * dv), lambda p, i: (i, p)), pl.BlockSpec((TQ, G * 128), lambda p, i: (i, p))],
        out_shape=[jax.ShapeDtypeStruct((S, HEADS * dv), BF16), jax.ShapeDtypeStruct((S, HEADS * 128), F32)],
        semantics=("parallel", "parallel"))


def softmax_attn_bwd(name, mode, q, k, v, o, do, lse, scale, bias=None):
    S = q.arr.shape[0]
    TQ, TK = _att_tiles(mode, S)
    G, dqk, dv = ATT_G, q.w, v.w

    def body(*refs):
        if bias is not None:
            q_ref, k_ref, v_ref, o_ref, do_ref, lse_ref, b_ref, dq_ref, dk_ref, dv_ref, db_ref = refs
        else:
            q_ref, k_ref, v_ref, o_ref, do_ref, lse_ref, dq_ref, dk_ref, dv_ref = refs
        i = pl.program_id(1)

        @pl.when(i == 0)
        def _():
            dk_ref[...] = jnp.zeros_like(dk_ref)
            dv_ref[...] = jnp.zeros_like(dv_ref)
            if bias is not None:
                db_ref[...] = jnp.zeros_like(db_ref)

        qs = [q_ref[:, q.lanes(g)] for g in range(G)]
        dos = [do_ref[:, do.lanes(g)] for g in range(G)]
        lses = [lse_ref[:, g * 128:g * 128 + 1] for g in range(G)]
        deltas = [jnp.sum(dos[g].astype(F32) * o_ref[:, o.lanes(g)].astype(F32), axis=-1, keepdims=True)
                  for g in range(G)]

        def block(g, j, dq, mask, ks):
            kb, vb = k_ref[ks, k.lanes(g)], v_ref[ks, v.lanes(g)]
            s = _dot_nt(qs[g], kb) * scale
            if bias is not None:
                slot = jnp.minimum(i - j, 2)
                s = s + b_ref[g, slot]
            p = jnp.exp(s - lses[g])
            if mask is not None:
                p = jnp.where(mask, p, 0.0)
            ds = p * (_dot_nt(dos[g], vb) - deltas[g])
            if bias is not None:
                db_ref[g, slot] += ds
            dsb = (ds * scale).astype(BF16)
            dk_ref[ks, g * dqk:(g + 1) * dqk] += _dot_tn(dsb, qs[g])
            dv_ref[ks, g * dv:(g + 1) * dv] += _dot_tn(p.astype(BF16), dos[g])
            return dq + _dot(dsb, kb)

        def step(j, carry, masked):
            ks = pl.ds(pl.multiple_of(j * TK, TK), TK)
            mask = _softmax_mask(mode, *_positions(i, j, TQ, TK)) if masked or mode == "ca" else None
            return tuple(block(g, j, carry[g], mask, ks) for g in range(G))

        init = (jnp.zeros((TQ, dqk), F32),) * G
        lo, n, per = _key_blocks(mode, i, TQ, TK)
        if mode == "ca":
            out = lax.fori_loop(lo, lo + per, lambda j, c: step(j, c, True), init)
        else:
            out = _walk(lo, n, per, step, init)
        for g in range(G):
            dq_ref[:, g * dqk:(g + 1) * dqk] = out[g]

    in_specs = [q.rows(TQ), k.full(S), v.full(S), o.rows(TQ), do.rows(TQ),
                pl.BlockSpec((TQ, G * 128), lambda p, i: (i, p))]
    ins = [q.arr, k.arr, v.arr, o.arr, do.arr, lse]
    out_specs = [pl.BlockSpec((TQ, G * dqk), lambda p, i: (i, p)), pl.BlockSpec((S, G * dqk), lambda p, i: (0, p)),
                 pl.BlockSpec((S, G * dv), lambda p, i: (0, p))]
    out_shape = [jax.ShapeDtypeStruct((S, HEADS * dqk), F32), jax.ShapeDtypeStruct((S, HEADS * dqk), F32),
                 jax.ShapeDtypeStruct((S, HEADS * dv), F32)]
    if bias is not None:
        bspec = pl.BlockSpec((G, 3, TK, TK), lambda p, i: (p, 0, 0, 0))
        in_specs.append(bspec)
        ins.append(bias)
        out_specs.append(bspec)
        out_shape.append(jax.ShapeDtypeStruct(bias.shape, F32))
    return _call(body, ins, name=name, grid=(HEADS // G, S // TQ), in_specs=in_specs, out_specs=out_specs,
                 out_shape=out_shape, semantics=("parallel", "arbitrary"))


def _split2(x):
    hi = x.astype(BF16)
    return hi, (x - hi.astype(F32)).astype(BF16)


def _split3(x):
    hi = x.astype(BF16)
    r = x - hi.astype(F32)
    mid = r.astype(BF16)
    return hi, mid, (r - mid.astype(F32)).astype(BF16)


def _stick_block(qb, kb, strict, scale):
    z = _dot_nt(qb, kb) * scale
    sp = jnp.log(1.0 + jnp.exp(-jnp.abs(z)))
    lb = jnp.minimum(z, 0.0) - sp
    l1m = jnp.minimum(-z, 0.0) - sp
    if strict is not None:
        l1m = jnp.where(strict, l1m, 0.0)
    return z, lb, l1m


def _strict_mask(i, j, TQ, TK):
    row, col = _positions(i, j, TQ, TK)
    return col < row


def _tri(T, inclusive):
    r = lax.broadcasted_iota(jnp.int32, (T, T), 0)
    c = lax.broadcasted_iota(jnp.int32, (T, T), 1)
    return ((r >= c) if inclusive else (r > c)).astype(BF16)


def _tri_prefix(T, inclusive):
    r = lax.broadcasted_iota(jnp.int32, (T, T), 0)
    c = lax.broadcasted_iota(jnp.int32, (T, T), 1)
    return ((r <= c) if inclusive else (r < c)).astype(BF16)


def _suffix(parts, tri):
    out = _dot(parts[0], tri)
    for p in parts[1:]:
        out = out + _dot(p, tri)
    return out


def stick_attn_fwd(name, q, k, v, scale):
    S = q.arr.shape[0]
    TQ, TK = _att_tiles("sb", S)
    G, dv = ATT_G, v.w

    def body(q_ref, k_ref, v_ref, o_ref, tot_ref):
        i = pl.program_id(1)
        qs = [q_ref[:, q.lanes(g)] for g in range(G)]
        tri = _tri(TK, False)

        def block(g, carry, strict, ks):
            right, acc = carry
            z, lb, l1m = _stick_block(qs[g], k_ref[ks, k.lanes(g)], strict, scale)
            a = jnp.exp(lb + _suffix(_split2(l1m), tri) + right)
            if strict is not None:
                a = jnp.where(strict, a, 0.0)
            acc = acc + _dot(a.astype(BF16), v_ref[ks, v.lanes(g)])
            return right + jnp.sum(l1m, axis=-1, keepdims=True), acc

        def step(j, carry, masked):
            ks = pl.ds(pl.multiple_of(j * TK, TK), TK)
            strict = _strict_mask(i, j, TQ, TK) if masked else None
            return tuple(block(g, carry[g], strict, ks) for g in range(G))

        init = (jnp.zeros((TQ, 1), F32), jnp.zeros((TQ, dv), F32))
        lo, n, per = _key_blocks("sb", i, TQ, TK)
        out = _walk(lo, n, per, step, (init,) * G, descending=True)
        for g in range(G):
            o_ref[:, g * dv:(g + 1) * dv] = out[g][1].astype(BF16)
            tot_ref[:, g * 128:(g + 1) * 128] = jnp.broadcast_to(out[g][0], (TQ, 128))

    return _call(
        body, [q.arr, k.arr, v.arr], name=name, grid=(HEADS // G, S // TQ),
        in_specs=[q.rows(TQ), k.full(S), v.full(S)],
        out_specs=[pl.BlockSpec((TQ, G * dv), lambda p, i: (i, p)), pl.BlockSpec((TQ, G * 128), lambda p, i: (i, p))],
        out_shape=[jax.ShapeDtypeStruct((S, HEADS * dv), BF16), jax.ShapeDtypeStruct((S, HEADS * 128), F32)],
        semantics=("parallel", "parallel"))


def stick_attn_bwd(name, q, k, v, do, total, scale):
    S = q.arr.shape[0]
    TQ, TK = _att_tiles("sb", S)
    G, dqk, dv = ATT_G, q.w, v.w

    def body(q_ref, k_ref, v_ref, do_ref, tot_ref, dq_ref, dk_ref, dv_ref):
        i = pl.program_id(1)

        @pl.when(i == 0)
        def _():
            dk_ref[...] = jnp.zeros_like(dk_ref)
            dv_ref[...] = jnp.zeros_like(dv_ref)

        qs = [q_ref[:, q.lanes(g)] for g in range(G)]
        dos = [do_ref[:, do.lanes(g)] for g in range(G)]
        tots = [tot_ref[:, g * 128:g * 128 + 1] for g in range(G)]
        upto = _tri_prefix(TK, True)
        before = _tri_prefix(TK, False)

        def step(j, carry, masked):
            ks = pl.ds(pl.multiple_of(j * TK, TK), TK)
            strict = _strict_mask(i, j, TQ, TK) if masked else None
            out = []
            for g in range(G):
                left, gleft, dq = carry[g]
                kb = k_ref[ks, k.lanes(g)]
                z, lb, l1m = _stick_block(qs[g], kb, strict, scale)
                a = jnp.exp(lb + (tots[g] - (left + _suffix(_split3(l1m), upto))))
                if strict is not None:
                    a = jnp.where(strict, a, 0.0)
                gg = a * _dot_nt(dos[g], v_ref[ks, v.lanes(g)])
                c = gleft + _suffix(_split3(gg), before)
                sig = 1.0 / (1.0 + jnp.exp(-z))
                dz = gg * (1.0 - sig) - c * sig
                if strict is not None:
                    dz = jnp.where(strict, dz, 0.0)
                dzb = (dz * scale).astype(BF16)
                dk_ref[ks, g * dqk:(g + 1) * dqk] += _dot_tn(dzb, qs[g])
                dv_ref[ks, g * dv:(g + 1) * dv] += _dot_tn(a.astype(BF16), dos[g])
                out.append((left + jnp.sum(l1m, axis=-1, keepdims=True),
                            gleft + jnp.sum(gg, axis=-1, keepdims=True), dq + _dot(dzb, kb)))
            return tuple(out)

        zero = jnp.zeros((TQ, 1), F32)
        lo, n, per = _key_blocks("sb", i, TQ, TK)
        out = _walk(lo, n, per, step, ((zero, zero, jnp.zeros((TQ, dqk), F32)),) * G)
        for g in range(G):
            dq_ref[:, g * dqk:(g + 1) * dqk] = out[g][2]

    return _call(
        body, [q.arr, k.arr, v.arr, do.arr, total], name=name, grid=(HEADS // G, S // TQ),
        in_specs=[q.rows(TQ), k.full(S), v.full(S), do.rows(TQ), pl.BlockSpec((TQ, G * 128), lambda p, i: (i, p))],
        out_specs=[pl.BlockSpec((TQ, G * dqk), lambda p, i: (i, p)), pl.BlockSpec((S, G * dqk), lambda p, i: (0, p)),
                   pl.BlockSpec((S, G * dv), lambda p, i: (0, p))],
        out_shape=[jax.ShapeDtypeStruct((S, HEADS * dqk), F32), jax.ShapeDtypeStruct((S, HEADS * dqk), F32),
                   jax.ShapeDtypeStruct((S, HEADS * dv), F32)],
        semantics=("parallel", "arbitrary"))


def _skew(x, back):
    T = x.shape[0]
    rows = lax.broadcasted_iota(jnp.int32, (T, T), 0)
    for b in range(T.bit_length() - 1):
        shift = T - (1 << b) if back else 1 << b
        x = jnp.where(((rows >> b) & 1) == 1, pltpu.roll(x, shift, 1), x)
    return x


def _table_rows(table):
    t = jnp.pad(table.T, ((0, 0), (0, 2 * REL_CLIP_LEFT - REL_TABLE)))
    return t.reshape(table.shape[1], 2, REL_CLIP_LEFT)


def rel_bias_blocks(name, table, T):
    assert T == 2 * REL_CLIP_LEFT, "the base rows below are laid out for blocks of 256"

    def body(t_ref, o_ref):
        low, high = t_ref[0:1, :], t_ref[1:2, :]
        first = jnp.broadcast_to(t_ref[0:1, 0:1], (1, REL_CLIP_LEFT))
        qq = lax.broadcasted_iota(jnp.int32, (T, T), 0)
        kk = lax.broadcasted_iota(jnp.int32, (T, T), 1)

        def rolled(row):
            return _skew(jnp.broadcast_to(row, (T, T)), False)

        far = jnp.concatenate([first, low], axis=1)
        near = jnp.concatenate([high, jnp.zeros_like(high)], axis=1)
        o_ref[0] = jnp.where(kk >= qq, rolled(near), rolled(far))
        o_ref[1] = jnp.where(kk >= qq, rolled(far), jnp.broadcast_to(t_ref[0:1, 0:1], (T, T)))
        o_ref[2] = jnp.broadcast_to(t_ref[0:1, 0:1], (T, T))

    return pl.pallas_call(
        body, name=name, grid=(HEADS,), in_specs=[pl.BlockSpec((None, 2, REL_CLIP_LEFT), lambda h: (h, 0, 0))],
        out_specs=pl.BlockSpec((None, 3, T, T), lambda h: (h, 0, 0, 0)),
        out_shape=jax.ShapeDtypeStruct((HEADS, 3, T, T), F32), compiler_params=_params("parallel"),
    )(_table_rows(table))


def rel_bias_grad(name, dbias):
    T = dbias.shape[-1]
    L = REL_CLIP_LEFT
    assert T == 2 * L

    def body(d_ref, o_ref):
        qq = lax.broadcasted_iota(jnp.int32, (T, T), 0)
        ll = lax.broadcasted_iota(jnp.int32, (T, T), 1)
        wrapped = ll + qq >= T

        def columns(d):
            x = _skew(d_ref[d], True)
            return (jnp.sum(jnp.where(wrapped, 0.0, x), axis=0, keepdims=True),
                    jnp.sum(jnp.where(wrapped, x, 0.0), axis=0, keepdims=True))

        pos0, neg0 = columns(0)
        pos1, neg1 = columns(1)
        clipped = (jnp.sum(neg0[:, :L]) + jnp.sum(pos1[:, :L]) + jnp.sum(neg1) + jnp.sum(d_ref[2]))
        lane = lax.broadcasted_iota(jnp.int32, (1, L), 1)
        low = neg0[:, L:] + pos1[:, L:]
        o_ref[...] = jnp.zeros_like(o_ref)
        o_ref[0:1, :] = jnp.where(lane == 0, low + clipped, low)
        o_ref[1:2, :] = pos0[:, :L]

    rows = pl.pallas_call(
        body, name=name, grid=(HEADS,), in_specs=[pl.BlockSpec((None, 3, T, T), lambda h: (h, 0, 0, 0))],
        out_specs=pl.BlockSpec((None, 8, L), lambda h: (h, 0, 0)), out_shape=jax.ShapeDtypeStruct((HEADS, 8, L), F32),
        compiler_params=_params("parallel"),
    )(dbias)
    return rows[:, :2, :].reshape(HEADS, 2 * L)[:, :REL_TABLE].T


def _place():
    return lax.axis_index("x"), lax.axis_index("y"), lax.axis_index("c")


def all_gather(name, shards):
    n = len(shards)

    def body(*refs):
        x_refs, out_refs = refs[:n], refs[n:2 * n]
        send_sems, recv_sems, local_sems = refs[2 * n:]
        x, y, c = _place()
        me, sibling = (x, y, c), (x, y, 1 - c)
        chips = [(1 - x, y), (x, 1 - y), (1 - x, 1 - y)]

        def block(t, dev):
            return out_refs[t].at[4 * dev[0] + 2 * dev[1] + dev[2]]

        def copy(t, k, dev, to, src=None):
            return pltpu.make_async_remote_copy(
                src_ref=block(t, dev) if src is None else src, dst_ref=block(t, dev),
                send_sem=send_sems.at[t, k], recv_sem=recv_sems.at[t, k], device_id=to, device_id_type=MESH)

        mine = [pltpu.make_async_copy(x_refs[t], block(t, me), local_sems.at[t]) for t in range(n)]
        for cp in mine:
            cp.start()
        first = []
        for t in range(n):
            first.append(copy(t, 0, me, sibling, src=x_refs[t]))
            first += [copy(t, 1 + j, me, (*chip, c), src=x_refs[t]) for j, chip in enumerate(chips)]
        for cp in first:
            cp.start()
        passed = []
        for j, chip in enumerate(chips):
            for t in range(n):
                copy(t, 1 + j, (*chip, c), me).wait_recv()
                cp = copy(t, 4 + j, (*chip, c), sibling)
                cp.start()
                passed.append(cp)
        for t in range(n):
            copy(t, 0, sibling, me).wait_recv()
            for j, chip in enumerate(chips):
                copy(t, 4 + j, (*chip, 1 - c), me).wait_recv()
        for cp in first + passed:
            cp.wait_send()
        for cp in mine:
            cp.wait()

    return pl.pallas_call(
        body, name=name, in_specs=[HBM] * n, out_specs=[HBM] * n,
        out_shape=[jax.ShapeDtypeStruct((N_DEV, *s.shape), s.dtype) for s in shards],
        scratch_shapes=[pltpu.SemaphoreType.DMA((n, 7)), pltpu.SemaphoreType.DMA((n, 7)), pltpu.SemaphoreType.DMA((n,))],
    )(*shards)


def _remote(src, dst, send_sems, recv_sems, k, to):
    return pltpu.make_async_remote_copy(src_ref=src, dst_ref=dst, send_sem=send_sems.at[k], recv_sem=recv_sems.at[k],
                                        device_id=to, device_id_type=MESH)


class Future:
    def __init__(self):
        self.value = None

    def get(self):
        if self.value is None:
            SCHED.flush()
        return self.value


def gather_jobs(name, shards, wants_chips, want_sibling):
    n, shares = len(shards), len(wants_chips)
    result = Future()
    lands = [jax.ShapeDtypeStruct((N_DEV, *s.shape), s.dtype) for s in shards]

    def to_chips(share):
        def copies(in_refs, out_refs, send_sems, recv_sems):
            x, y, c = _place()
            me = 4 * x + 2 * y + c
            cps = []
            for t in range(n):
                rows = shards[t].shape[0] // shares
                mine = pl.ds(share * rows, rows)
                src, dst = in_refs[t].at[mine], out_refs[t].at[me, mine]
                cps.append(pltpu.make_async_copy(src, dst, send_sems.at[4 * t]))
                for j, chip in enumerate([(1 - x, y), (x, 1 - y), (1 - x, 1 - y)]):
                    cps.append(_remote(src, dst, send_sems, recv_sems, 4 * t + 1 + j, (*chip, c)))
            return cps
        return copies

    def to_sibling(in_refs, out_refs, send_sems, recv_sems):
        x, y, c = _place()
        return [_remote(in_refs[t].at[2 * chip + c], out_refs[t].at[2 * chip + c], send_sems, recv_sems, 4 * t + chip,
                        (x, y, 1 - c)) for t in range(n) for chip in range(4)]

    def post(share, landed):
        if share == shares:
            SCHED.post(Job(name + "_sibling", want_sibling, landed, lands, 4 * n, to_sibling,
                           lambda final: setattr(result, "value", final), aliases={t: t for t in range(n)}))
        else:
            SCHED.post(Job(f"{name}_chips{share}", wants_chips[share], list(shards) + (landed or []), lands, 4 * n,
                           to_chips(share), lambda outs: post(share + 1, outs),
                           aliases={n + t: t for t in range(n)} if landed else None))

    post(0, None)
    return result


def scatter_jobs(name, grads, core, chip, want_sibling, wants_chips):
    n = len(grads)
    result = Future()
    sums = [None] * n

    def to_sibling(in_refs, out_refs, send_sems, recv_sems):
        x, y, c = _place()
        return [_remote(in_refs[t].at[1 - c], out_refs[t], send_sems, recv_sems, t, (x, y, 1 - c)) for t in range(n)]

    def after_sibling(received):
        parts = [add_sibling(f"{name}_add{t}", grads[t], received[t], core) for t in range(n)]
        for group, want in wants_chips:
            def to_chips(in_refs, out_refs, send_sems, recv_sems, m=len(group)):
                x, y, c = _place()
                return [_remote(in_refs[t].at[2 * cx + cy], out_refs[t].at[j], send_sems, recv_sems, 3 * t + j,
                                (cx, cy, c))
                        for t in range(m) for j, (cx, cy) in enumerate([(1 - x, y), (x, 1 - y), (1 - x, 1 - y)])]

            def after_chips(received, group=group):
                for t, r in zip(group, received):
                    sums[t] = sum_chips(f"{name}_sum{t}", parts[t], r, chip)
                if all(s is not None for s in sums):
                    result.value = sums

            mine = [parts[t] for t in group]
            SCHED.post(Job(f"{name}_chips{group[0]}", want, mine,
                           [jax.ShapeDtypeStruct((3, *p.shape[1:]), p.dtype) for p in mine], 3 * len(mine), to_chips,
                           after_chips))

    SCHED.post(Job(name + "_sibling", want_sibling, grads, [jax.ShapeDtypeStruct(g.shape[1:], g.dtype) for g in grads],
                   n, to_sibling, after_sibling))
    return result


def _as_rows(shape):
    return (int(np.prod(shape[:-1])), shape[-1])


ELEMENTWISE_BLOCK = 256 * 1024


def _row_tile(rows, cols):
    return _tile(rows, max(128, ELEMENTWISE_BLOCK // cols // 128 * 128))


def add_sibling(name, grad, recv, core):
    rows, cols = _as_rows(grad.shape[2:])
    tr = _row_tile(rows, cols)

    def body(c_ref, g_ref, r_ref, o_ref):
        o_ref[...] = (g_ref[...].astype(F32) + r_ref[...].astype(F32)).astype(BF16)

    blk = pl.BlockSpec((None, tr, cols), lambda k, i, c_ref: (k, i, 0))
    return pl.pallas_call(
        body, name=name,
        grid_spec=pltpu.PrefetchScalarGridSpec(
            num_scalar_prefetch=1, grid=(4, rows // tr),
            in_specs=[pl.BlockSpec((None, None, tr, cols), lambda k, i, c_ref: (c_ref[0], k, i, 0)), blk],
            out_specs=blk),
        out_shape=jax.ShapeDtypeStruct((4, rows, cols), BF16), compiler_params=_params("parallel", "parallel"),
    )(core, grad.reshape(2, 4, rows, cols), recv.reshape(4, rows, cols)).reshape(recv.shape)


def sum_chips(name, part, recv, chip):
    shape = part.shape[1:]
    rows, cols = _as_rows(shape)
    tr = _row_tile(rows, cols)

    def body(c_ref, p_ref, r_ref, o_ref):
        o_ref[...] = (p_ref[...].astype(F32) + r_ref[0].astype(F32) + r_ref[1].astype(F32) + r_ref[2].astype(F32))

    return pl.pallas_call(
        body, name=name,
        grid_spec=pltpu.PrefetchScalarGridSpec(
            num_scalar_prefetch=1, grid=(rows // tr,),
            in_specs=[pl.BlockSpec((None, tr, cols), lambda i, c_ref: (c_ref[0], i, 0)),
                      pl.BlockSpec((3, tr, cols), lambda i, c_ref: (0, i, 0))],
            out_specs=pl.BlockSpec((tr, cols), lambda i, c_ref: (i, 0))),
        out_shape=jax.ShapeDtypeStruct((rows, cols), F32), compiler_params=_params("parallel"),
    )(chip, part.reshape(4, rows, cols), recv.reshape(3, rows, cols)).reshape(shape)


def sum_devices(name, gathered):
    _, rows, cols = gathered.shape

    def body(g_ref, o_ref):
        acc = g_ref[0]
        for d in range(1, N_DEV):
            acc = acc + g_ref[d]
        o_ref[...] = acc

    return pl.pallas_call(body, name=name, out_shape=jax.ShapeDtypeStruct((rows, cols), F32))(gathered)


def adamw(name, w, g, m, v):
    shape = w.shape
    rows, cols = _as_rows(shape)
    tr = _row_tile(rows, cols) if rows % 8 == 0 else rows
    c1 = 1.0 / (1.0 - ADAM_B1 ** ADAM_STEP)
    c2 = 1.0 / (1.0 - ADAM_B2 ** ADAM_STEP)

    def body(w_ref, g_ref, m_ref, v_ref, d_ref, mo_ref, vo_ref):
        g_ = g_ref[...]
        m_ = ADAM_B1 * m_ref[...] + (1.0 - ADAM_B1) * g_
        v_ = ADAM_B2 * v_ref[...] + (1.0 - ADAM_B2) * (g_ * g_)
        d_ref[...] = -ADAM_LR * ((m_ * c1) / (jnp.sqrt(v_ * c2) + ADAM_EPS) + ADAM_WD * w_ref[...])
        mo_ref[...] = m_
        vo_ref[...] = v_

    blk = pl.BlockSpec((tr, cols), lambda i: (i, 0))
    outs = pl.pallas_call(
        body, name=name, grid=(rows // tr,), in_specs=[blk] * 4, out_specs=[blk] * 3,
        out_shape=[jax.ShapeDtypeStruct((rows, cols), F32)] * 3, compiler_params=_params("parallel"),
    )(*[a.reshape(rows, cols) for a in (w, g, m, v)])
    return [o.reshape(shape) for o in outs]


def _spread_rope(r):
    z = jnp.zeros_like(r[..., :32])
    return jnp.concatenate([r[..., :32], z, r[..., 32:], z], -1)


def _gather_rope(r):
    return jnp.concatenate([r[..., :32], r[..., 64:96]], -1)


def pad_w_uq(w):
    w = w.reshape(w.shape[0], -1, MLA_NOPE + MLA_ROPE)
    return jnp.concatenate([w[..., :MLA_NOPE], _spread_rope(w[..., MLA_NOPE:])], -1).reshape(w.shape[0], -1)


def unpad_w_uq(g):
    g = g.reshape(g.shape[0], -1, 2 * MLA_NOPE)
    return jnp.concatenate([g[..., :MLA_NOPE], _gather_rope(g[..., MLA_NOPE:])], -1).reshape(g.shape[0], -1)


def pad_w_down(w):
    lat = MLA_Q_LORA + MLA_KV_LORA
    return jnp.concatenate([w[:, :lat], _spread_rope(w[:, lat:])], -1)


def unpad_w_down(g):
    lat = MLA_Q_LORA + MLA_KV_LORA
    return jnp.concatenate([g[:, :lat], _gather_rope(g[:, lat:])], -1)


def _heads(arr, width, first=0, off=0, w=None):
    return HeadCols(arr, width, lambda p: first // ATT_G + p, off, w)


def mla_forward(h16, w, gq, gkv, tables):
    cos, sin = tables
    down = mm_nn("mla_down", h16, w["down"], [F32])[0]
    cq, ckv = rms_fwd("mla_rms", down, gq, gkv)
    q = mm_nn("mla_uq", cq, w["uq"], [F32])[0]
    kv = mm_nn("mla_ukv", ckv, w["ukv"], [BF16])[0]
    qr, kp = mla_prep_fwd("mla_prep", q, kv, down, cos, sin)
    scale = (MLA_NOPE + MLA_ROPE) ** -0.5
    o, lse = softmax_attn_fwd("mla_attn", "mla", _heads(qr, 256), _heads(kp, 256), _heads(kv, 256, off=128, w=128), scale)
    m = mm_nn("mla_wo", o, w["wo"], [F32])[0]
    return m, (down, cq, ckv, qr, kp, kv, o, lse)


def mla_backward(du16, h16t, saved, w, gq, gkv, tables):
    cos, sin = tables
    down, cq, ckv, qr, kp, kv, o, lse = saved
    scale = (MLA_NOPE + MLA_ROPE) ** -0.5
    g = {"wo": mm_tn("mla_dwo", o, du16, "row", w["wo"].R, w["wo"].C)}
    do = mm_nt("mla_do", du16, w["wo"], BF16)
    dq, dk, dv = softmax_attn_bwd("mla_attn_bwd", "mla", _heads(qr, 256), _heads(kp, 256), _heads(kv, 256, off=128, w=128),
                                  _heads(o, 128), _heads(do, 128), lse, scale)
    dq16, dkv16, dkr = mla_prep_bwd("mla_prep_bwd", dq, dk, dv, cos, sin)
    g["uq"] = mm_tn("mla_duq", cq, dq16, "col", w["uq"].R, w["uq"].C)
    dcq = mm_nt("mla_dcq", dq16, w["uq"], F32)
    g["ukv"] = mm_tn("mla_dukv", ckv, dkv16, "col", w["ukv"].R, w["ukv"].C)
    dckv = mm_nt("mla_dckv", dkv16, w["ukv"], F32)
    ddown, dgq, dgkv = rms_bwd("mla_rms_bwd", down, dcq, dckv, dkr, gq, gkv)
    g["down"] = mm_tn("mla_ddown", h16t, ddown, "row", w["down"].R, w["down"].C, transposed=True)
    dh = mm_nt("mla_dh", ddown, w["down"], F32)
    return dh, g, (dgq, dgkv)


def qkv_forward(kind, h16, w, bias=None):
    qkv = mm_nn(kind + "_qkv", h16, w["qkv"], [BF16])[0]
    q, k, v = _heads(qkv, 128), _heads(qkv, 128, HEADS), _heads(qkv, 128, 2 * HEADS)
    scale = HEAD_DIM ** -0.5
    if kind == "sb":
        o, lse = stick_attn_fwd("sb_attn", q, k, v, scale)
    else:
        o, lse = softmax_attn_fwd("ca_attn", "ca", q, k, v, scale, bias)
    m = mm_nn(kind + "_wo", o, w["wo"], [F32])[0]
    return m, (qkv, o, lse)


def qkv_backward(kind, du16, h16t, saved, w, bias=None):
    qkv, o, lse = saved
    q, k, v = _heads(qkv, 128), _heads(qkv, 128, HEADS), _heads(qkv, 128, 2 * HEADS)
    scale = HEAD_DIM ** -0.5
    g = {"wo": mm_tn(kind + "_dwo", o, du16, "row", w["wo"].R, w["wo"].C)}
    do = mm_nt(kind + "_do", du16, w["wo"], BF16)
    dbias = None
    if kind == "sb":
        dq, dk, dv = stick_attn_bwd("sb_attn_bwd", q, k, v, _heads(do, 128), lse, scale)
    else:
        dq, dk, dv, dbias = softmax_attn_bwd("ca_attn_bwd", "ca", q, k, v, _heads(o, 128), _heads(do, 128), lse,
                                             scale, bias)
    dqkv = jnp.concatenate([dq, dk, dv], axis=1).astype(BF16)
    g["qkv"] = mm_tn(kind + "_dqkv", h16t, dqkv, "col", w["qkv"].R, w["qkv"].C, transposed=True)
    dh = mm_nt(kind + "_dh", dqkv, w["qkv"], F32)
    return dh, g, dbias


def mlp_forward(h16, w):
    a, z, zt = mm_nn("ffn_in", h16, w["w_in"], [F32, BF16, BF16], epilogue=_relu2_epilogue,
                     transposed=(False, False, True))
    f = mm_nn("ffn_out", z, w["w_out"], [F32])[0]
    return f, (a, zt)


def mlp_backward(du16, h16t, saved, w):
    a, zt = saved
    da = mm_nt("ffn_da", du16, w["w_out"], BF16, epilogue=_mulrelu_epilogue, extra=a)
    g = {"w_out": mm_tn("ffn_dwout", zt, du16, "row", w["w_out"].R, w["w_out"].C, transposed=True)}
    dh = mm_nt("ffn_dh", da, w["w_in"], F32)
    g["w_in"] = mm_tn("ffn_dwin", h16t, da, "col", w["w_in"].R, w["w_in"].C, transposed=True)
    return dh, g


WEIGHTS = ("ln_mix_g", "ln_mix_b", "ln_ffn_g", "ln_ffn_b", "ffn_w_in", "ffn_w_out", "mla_w_down", "mla_q_norm_g",
           "mla_w_uq", "mla_kv_norm_g", "mla_w_ukv", "mla_w_o", "sb_w_qkv", "sb_w_o", "ca_w_qkv", "ca_rel_bias",
           "ca_w_o")
MIXERS = ("mla", "sb", "ca")
LAYER_WEIGHTS = {
    "mla": (("down", "mla_w_down", "row"), ("uq", "mla_w_uq", "col"), ("ukv", "mla_w_ukv", "col"),
            ("wo", "mla_w_o", "row")),
    "sb": (("qkv", "sb_w_qkv", "col"), ("wo", "sb_w_o", "row")),
    "ca": (("qkv", "ca_w_qkv", "col"), ("wo", "ca_w_o", "row")),
    "ffn": (("w_in", "ffn_w_in", "col"), ("w_out", "ffn_w_out", "row")),
}
PAD = {"mla_w_down": pad_w_down, "mla_w_uq": pad_w_uq}
UNPAD = {"mla_w_down": unpad_w_down, "mla_w_uq": unpad_w_uq}


def _pack_rows(vectors):
    flat = jnp.concatenate([v.reshape(-1) for v in vectors])
    n = flat.shape[0]
    rows = -(-n // 1024) * 8
    offsets = np.cumsum([0] + [int(np.prod(v.shape)) for v in vectors])
    return jnp.pad(flat, (0, rows * 128 - n)).reshape(rows, 128), offsets


def _part(i, part):
    group, idx = (MIXERS[i % 3], i // 3) if part == "mix" else ("ffn", i)
    return [(key, name, how, idx) for key, name, how in LAYER_WEIGHTS[group]]


def kernel(x, ln_mix_g, ln_mix_b, ln_ffn_g, ln_ffn_b, ffn_w_in, ffn_w_out, mla_w_down, mla_q_norm_g, mla_w_uq, mla_kv_norm_g, mla_w_ukv, mla_w_o, sb_w_qkv, sb_w_o, ca_w_qkv, ca_rel_bias, ca_w_o, loss_target, m_ln_mix_g, m_ln_mix_b, m_ln_ffn_g, m_ln_ffn_b, m_ffn_w_in, m_ffn_w_out, m_mla_w_down, m_mla_q_norm_g, m_mla_w_uq, m_mla_kv_norm_g, m_mla_w_ukv, m_mla_w_o, m_sb_w_qkv, m_sb_w_o, m_ca_w_qkv, m_ca_rel_bias, m_ca_w_o, v_ln_mix_g, v_ln_mix_b, v_ln_ffn_g, v_ln_ffn_b, v_ffn_w_in, v_ffn_w_out, v_mla_w_down, v_mla_q_norm_g, v_mla_w_uq, v_mla_kv_norm_g, v_mla_w_ukv, v_mla_w_o, v_sb_w_qkv, v_sb_w_o, v_ca_w_qkv, v_ca_rel_bias, v_ca_w_o):
    w = dict(zip(WEIGHTS, (ln_mix_g, ln_mix_b, ln_ffn_g, ln_ffn_b, ffn_w_in, ffn_w_out, mla_w_down, mla_q_norm_g,
                           mla_w_uq, mla_kv_norm_g, mla_w_ukv, mla_w_o, sb_w_qkv, sb_w_o, ca_w_qkv, ca_rel_bias,
                           ca_w_o)))
    mom = dict(zip(WEIGHTS, (m_ln_mix_g, m_ln_mix_b, m_ln_ffn_g, m_ln_ffn_b, m_ffn_w_in, m_ffn_w_out, m_mla_w_down,
                             m_mla_q_norm_g, m_mla_w_uq, m_mla_kv_norm_g, m_mla_w_ukv, m_mla_w_o, m_sb_w_qkv,
                             m_sb_w_o, m_ca_w_qkv, m_ca_rel_bias, m_ca_w_o)))
    var = dict(zip(WEIGHTS, (v_ln_mix_g, v_ln_mix_b, v_ln_ffn_g, v_ln_ffn_b, v_ffn_w_in, v_ffn_w_out, v_mla_w_down,
                             v_mla_q_norm_g, v_mla_w_uq, v_mla_kv_norm_g, v_mla_w_ukv, v_mla_w_o, v_sb_w_qkv,
                             v_sb_w_o, v_ca_w_qkv, v_ca_rel_bias, v_ca_w_o)))
    S, D = x.shape[1], x.shape[2]
    xi, yi, ci = _place()
    core = ci.astype(jnp.int32).reshape(1)
    chip = (2 * xi + yi).astype(jnp.int32).reshape(1)
    me = 4 * xi + 2 * yi + ci
    tables = rope_tables(S)
    n_mla = mla_w_down.shape[0]
    lat = MLA_Q_LORA // N_DEV

    gains = jnp.pad(jnp.stack([mla_q_norm_g.reshape(-1), mla_kv_norm_g.reshape(-1)]), ((0, 6), (0, 128 - n_mla * lat)))
    gains = all_gather("ag_gains", [gains])[0]

    def full_gain(row, slot):
        return gains[:, row, slot * lat:(slot + 1) * lat].reshape(-1)

    first_matmul = {"mla": "mla_down", "sb": "sb_qkv", "ca": "ca_qkv"}

    def post_gather(i, part):
        kind = MIXERS[i % 3]
        specs = _part(i, part)
        shards = [PAD.get(name, lambda a: a)(w[name][idx]).astype(BF16) for _, name, _, idx in specs]
        if part == "mix":
            wants = (["ffn_in"], "ffn_out") if i > 0 else ([None], None)
            return [(specs, gather_jobs(f"ag_mix{i}", shards, *wants))]
        w_in = (["ffn_out", first_matmul[kind]], kind + "_attn") if i > 0 else (["mla_down", "mla_uq"], "mla_ukv")
        return [(specs[:1], gather_jobs(f"ag_w_in{i}", shards[:1], *w_in)),
                (specs[1:], gather_jobs(f"ag_w_out{i}", shards[1:], [kind + "_attn"], kind + "_wo"))]

    def gathered(posted):
        return {key: Weight(how, g) for specs, future in posted for (key, _, how, _), g in zip(specs, future.get())}

    long_attention = "sb_attn_bwd"

    def post_scatter(i, part, g):
        kind = MIXERS[i % 3]
        specs = _part(i, part)
        n = len(specs)
        grads_of = lambda sp: [g[key] for key, _, _, _ in sp]
        if part == "ffn":
            w_in_rides = long_attention if MIXERS[(i - 1) % 3] == "sb" and i > 0 else kind + "_attn_bwd"
            return [(specs[:1], scatter_jobs(f"rs_w_in{i}", grads_of(specs[:1]), core, chip, kind + "_dwo",
                                             [([0], w_in_rides)])),
                    (specs[1:], scatter_jobs(f"rs_w_out{i}", grads_of(specs[1:]), core, chip, kind + "_dwo",
                                             [([0], kind + "_attn_bwd")]))]
        if i == 0:
            wants = (None, [(list(range(n)), None)])
        else:
            wants = ("ffn_da", [(list(range(n - 1)), "ffn_dwout"), ([n - 1], "ffn_dh")])
        return [(specs, scatter_jobs(f"rs_mix{i}", grads_of(specs), core, chip, *wants))]

    SCHED.pending.clear()
    bias = rel_bias_blocks("ca_bias", ca_rel_bias[0], _att_tiles("ca", S)[1])

    h, h16 = x[0], x[0].astype(BF16)
    h16t = transpose("x_t", h16)
    saved, layers = [], []
    mix_w, ffn_w = post_gather(0, "mix"), None
    for i in range(DEPTH):
        kind, slot = MIXERS[i % 3], i // 3
        lw = gathered(mix_w)
        if i == 0:
            ffn_w = post_gather(0, "ffn")
        if kind == "mla":
            mix, s_mix = mla_forward(h16, lw, full_gain(0, slot), full_gain(1, slot), tables)
        else:
            mix, s_mix = qkv_forward(kind, h16, lw, bias if kind == "ca" else None)
        y, y16, y16t, xh1, rs1 = ln_fwd("ln_mix", h, mix, ln_mix_g[i], ln_mix_b[i])
        lw.update(gathered(ffn_w))
        if i + 1 < DEPTH:
            mix_w, ffn_w = post_gather(i + 1, "mix"), post_gather(i + 1, "ffn")
        f, s_mlp = mlp_forward(y16, lw)
        y2, y2_16, y2_16t, xh2, rs2 = ln_fwd("ln_ffn", y, f, ln_ffn_g[i], ln_ffn_b[i])
        saved.append((h16t, s_mix, xh1, rs1, y16t, s_mlp, xh2, rs2))
        layers.append(lw)
        h, h16, h16t = y2, y2_16, y2_16t
    sq, dy = loss_fwd_bwd("loss", h, loss_target[0])
    loss = 0.5 / D * lax.psum(sq[0, 0], ("x", "y", "c"))

    ga, gb = dy, None
    grads = {name: [None] * w[name].shape[0] for name in WEIGHTS}
    dbias = None
    scattered = []
    for i in reversed(range(DEPTH)):
        kind, slot = MIXERS[i % 3], i // 3
        lw = layers[i]
        h16_in, s_mix, xh1, rs1, y16, s_mlp, xh2, rs2 = saved[i]
        du, du16, grads["ln_ffn_g"][i], grads["ln_ffn_b"][i] = ln_bwd("ln_ffn_bwd", ga, gb, xh2, rs2, ln_ffn_g[i])
        dh_mlp, g_mlp = mlp_backward(du16, y16, s_mlp, lw)
        scattered += post_scatter(i, "ffn", g_mlp)
        du, du16, grads["ln_mix_g"][i], grads["ln_mix_b"][i] = ln_bwd("ln_mix_bwd", du, dh_mlp, xh1, rs1, ln_mix_g[i])
        if kind == "mla":
            dh_mix, g_mix, (dgq, dgkv) = mla_backward(du16, h16_in, s_mix, lw, full_gain(0, slot), full_gain(1, slot),
                                                      tables)
            grads["mla_q_norm_g"][slot], grads["mla_kv_norm_g"][slot] = dgq, dgkv
        else:
            dh_mix, g_mix, db = qkv_backward(kind, du16, h16_in, s_mix, lw, bias if kind == "ca" else None)
            dbias = db if kind == "ca" else dbias
        scattered += post_scatter(i, "mix", g_mix)
        ga, gb = du, dh_mix
    grad_x = axpy("grad_x", ga, gb)[None]
    SCHED.flush()
    for specs, future in scattered:
        for (_, name, _, idx), g in zip(specs, future.get()):
            grads[name][idx] = UNPAD.get(name, lambda a: a)(g)
    grads["ca_rel_bias"][0] = rel_bias_grad("ca_bias_grad", dbias)

    small = ("ln_mix_g", "ln_mix_b", "ln_ffn_g", "ln_ffn_b", "ca_rel_bias", "mla_q_norm_g", "mla_kv_norm_g")
    packed, offsets = _pack_rows([g for name in small for g in grads[name]])
    total = sum_devices("sum_small", all_gather("ag_small", [packed])[0]).reshape(-1)
    pos = 0
    for name in small:
        for idx, g in enumerate(grads[name]):
            full = total[offsets[pos]:offsets[pos + 1]]
            pos += 1
            if name in ("mla_q_norm_g", "mla_kv_norm_g"):
                full = lax.dynamic_slice(full, (me * lat,), (lat,))
            grads[name][idx] = full.reshape(w[name].shape[1:])

    g_out, d_out, m_out, v_out = [], [], [], []
    for name in WEIGHTS:
        g = jnp.stack(grads[name])
        delta, new_m, new_v = adamw("adamw_" + name, w[name], g, mom[name], var[name])
        g_out.append(g)
        d_out.append(delta)
        m_out.append(new_m)
        v_out.append(new_v)
    return (loss, grad_x, *g_out, *d_out, *m_out, *v_out)
```

```python
import functools

import numpy as np
import jax
import jax.numpy as jnp
from jax import lax
from jax.experimental import pallas as pl
from jax.experimental.pallas import tpu as pltpu

F32 = jnp.float32
BF16 = jnp.bfloat16
MESH = pl.DeviceIdType.MESH
N_DEV = 8

DEPTH = 4
CHUNK = 64
CHUNK_SHIFT = 6
HEADS = 16
HEAD_DIM = 128
MLA_Q_LORA = 512
MLA_KV_LORA = 512
MLA_NOPE = 128
MLA_ROPE = 64
ROPE_THETA = 10000.0
CA_LEFT_CHUNKS = 8
REL_CLIP_LEFT = 128
REL_TABLE = REL_CLIP_LEFT + CHUNK
LN_EPS = 1e-5
RMS_EPS = 1e-6
ALPHA = (2.0 * DEPTH) ** 0.25
NEG = -1e30
ADAM_LR = 0.001
ADAM_B1 = 0.9
ADAM_B2 = 0.999
ADAM_EPS = 1e-08
ADAM_WD = 0.01
ADAM_STEP = 10

V7X_VMEM_BYTES = 64 * 1024 * 1024
VMEM_LIMIT = V7X_VMEM_BYTES - 8 * 1024 * 1024
ATT_TQ = 512
ATT_TK = 256
ATT_G = 2


def _params(*sem):
    return pltpu.CompilerParams(dimension_semantics=sem if sem else None, vmem_limit_bytes=VMEM_LIMIT)


HBM = pl.BlockSpec(memory_space=pl.ANY)


class Job:
    def __init__(self, name, want, operands, out_shape, n_copies, copies, done, aliases=None):
        self.name, self.want, self.operands, self.out_shape = name, want, list(operands), list(out_shape)
        self.n_copies, self.copies, self.done, self.aliases = n_copies, copies, done, dict(aliases or {})

    def sems(self):
        return [pltpu.SemaphoreType.DMA((self.n_copies,)), pltpu.SemaphoreType.DMA((self.n_copies,))]


class Scheduler:
    def __init__(self):
        self.pending = []

    def post(self, job):
        self.pending.append(job)

    def take(self, name):
        mine = [job for job in self.pending if job.want is not None and job.want in name]
        self.pending = [job for job in self.pending if job not in mine]
        return mine

    def flush(self):
        while self.pending:
            job = self.pending.pop(0)
            n_in, n_out = len(job.operands), len(job.out_shape)

            def body(*refs, job=job, n_in=n_in, n_out=n_out):
                cps = job.copies(refs[:n_in], refs[n_in:n_in + n_out], refs[-2], refs[-1])
                for cp in cps:
                    cp.start()
                for cp in cps:
                    cp.wait()

            outs = pl.pallas_call(
                body, name=job.name, in_specs=[HBM] * n_in, out_specs=[HBM] * n_out, out_shape=job.out_shape,
                scratch_shapes=job.sems(), input_output_aliases=job.aliases)(*job.operands)
            job.done(list(outs))


SCHED = Scheduler()


def _call(body, operands, *, name, grid, in_specs, out_specs, out_shape, scratch_shapes=(), semantics):
    jobs = SCHED.take(name)
    if not jobs:
        return list(pl.pallas_call(
            body, name=name, grid=grid, in_specs=list(in_specs), out_specs=list(out_specs), out_shape=list(out_shape),
            scratch_shapes=list(scratch_shapes), compiler_params=_params(*semantics))(*operands))
    n_in, n_out, n_scr = len(operands), len(out_shape), len(scratch_shapes)
    j_in = np.cumsum([0] + [len(job.operands) for job in jobs])
    j_out = np.cumsum([0] + [len(job.out_shape) for job in jobs])
    a, b = n_in, n_in + int(j_in[-1])
    c, d = b + n_out, b + n_out + int(j_out[-1])

    def carrying(*refs):
        def copies():
            sems = refs[d + n_scr:]
            return [cp for k, job in enumerate(jobs)
                    for cp in job.copies(refs[a + j_in[k]:a + j_in[k + 1]], refs[c + j_out[k]:c + j_out[k + 1]],
                                         sems[2 * k], sems[2 * k + 1])]

        ids = [pl.program_id(k) for k in range(len(grid))]
        first = functools.reduce(jnp.logical_and, [i == 0 for i in ids])
        last = functools.reduce(jnp.logical_and, [i == g - 1 for i, g in zip(ids, grid)])

        @pl.when(first)
        def _():
            for cp in copies():
                cp.start()

        body(*refs[:a], *refs[b:c], *refs[d:d + n_scr])

        @pl.when(last)
        def _():
            for cp in copies():
                cp.wait()

    aliases = {n_in + int(j_in[k]) + i: n_out + int(j_out[k]) + o for k, job in enumerate(jobs)
               for i, o in job.aliases.items()}
    outs = pl.pallas_call(
        carrying, name=name + "_carry", grid=grid, in_specs=list(in_specs) + [HBM] * int(j_in[-1]),
        out_specs=list(out_specs) + [HBM] * int(j_out[-1]),
        out_shape=list(out_shape) + [s for job in jobs for s in job.out_shape],
        scratch_shapes=list(scratch_shapes) + [s for job in jobs for s in job.sems()],
        input_output_aliases=aliases,
        compiler_params=_params(*(["arbitrary"] * len(grid))))(*operands, *[o for job in jobs for o in job.operands])
    for k, job in enumerate(jobs):
        job.done(list(outs[n_out + int(j_out[k]):n_out + int(j_out[k + 1])]))
    return list(outs[:n_out])


def _matmul(name, a, b, *, contract, grid, a_spec, b_spec, o_specs, out_shape, acc_shape,
            epilogue=None, extra=(), extra_specs=()):
    nk = grid[2]
    n_extra = len(extra)
    n_out = len(out_shape)

    def finish(acc, e_refs, o_refs):
        outs = epilogue(acc, *[e[...] for e in e_refs]) if epilogue else (acc,)
        for o_ref, val in zip(o_refs, outs):
            o_ref[...] = val.astype(o_ref.dtype)

    def product(a_ref, b_ref):
        return lax.dot_general(a_ref[...], b_ref[...], (contract, ((), ())), preferred_element_type=F32)

    def body_single(*refs):
        finish(product(refs[0], refs[1]), refs[2:2 + n_extra], refs[2 + n_extra:2 + n_extra + n_out])

    def body(*refs):
        a_ref, b_ref = refs[0], refs[1]
        acc_ref = refs[-1]
        k = pl.program_id(2)

        @pl.when(k == 0)
        def _():
            acc_ref[...] = jnp.zeros_like(acc_ref)

        acc_ref[...] += product(a_ref, b_ref)

        @pl.when(k == nk - 1)
        def _():
            finish(acc_ref[...], refs[2:2 + n_extra], refs[2 + n_extra:2 + n_extra + n_out])

    return _call(
        body_single if nk == 1 else body, [a, b, *extra], name=name, grid=grid,
        in_specs=[a_spec, b_spec, *extra_specs], out_specs=o_specs, out_shape=out_shape,
        scratch_shapes=[] if nk == 1 else [pltpu.VMEM(acc_shape, F32)],
        semantics=("parallel", "parallel", "arbitrary"))


MATMUL_BLOCK_BYTES = 40 * 1024 * 1024
MAX_TK = 2048
MULTI_TK = 512


def _fit_tn(n, tm, tk, nk, out_bytes):
    cands = sorted({n} | {t for t in range(128, n, 128) if n % t == 0}, reverse=True)
    for tn in cands:
        need = 2 * 2 * (tm * tk + tk * tn) + 2 * tm * tn * out_bytes + (tm * tn * 4 if nk > 1 else 0) + tm * tn * 4
        if need <= MATMUL_BLOCK_BYTES:
            return tn
    return cands[-1]


def _itemsize(dtypes):
    return sum(jnp.dtype(d).itemsize for d in dtypes)


def _tile(n, pref):
    if n <= pref:
        return n
    t = pref
    while t >= 128:
        if n % t == 0 and t % 128 == 0:
            return t
        t -= 128
    return n


class Weight:
    def __init__(self, kind, arr):
        self.kind = kind
        self.arr = arr
        self.R, self.C = arr.shape[1], arr.shape[2]

    @property
    def two_d(self):
        return self.arr.reshape(N_DEV * self.R, self.C)


def mm_nn(name, a, w, out_dtypes, epilogue=None, transposed=()):
    M, K = a.shape
    tm = M
    tk = K if K <= MAX_TK else _tile(K, MULTI_TK)
    nk = K // tk
    if w.kind == "row":
        b = w.two_d
        N = w.C
        tn = _fit_tn(N, tm, tk, nk, _itemsize(out_dtypes))
        b_spec = pl.BlockSpec((tk, tn), lambda i, j, k: (k, j))
    else:
        b = w.arr
        N = N_DEV * w.C
        tn = _fit_tn(w.C, tm, tk, nk, _itemsize(out_dtypes))
        per = w.C // tn
        b_spec = pl.BlockSpec((None, tk, tn), lambda i, j, k: (j // per, k, j % per))
    grid = (M // tm, N // tn, nk)
    flip = [t < len(transposed) and transposed[t] for t in range(len(out_dtypes))]
    return _matmul(
        name, a, b, contract=((1,), (0,)), grid=grid,
        a_spec=pl.BlockSpec((tm, tk), lambda i, j, k: (i, k)), b_spec=b_spec,
        o_specs=[pl.BlockSpec((tn, tm), lambda i, j, k: (j, i)) if f else pl.BlockSpec((tm, tn), lambda i, j, k: (i, j))
                 for f in flip],
        out_shape=[jax.ShapeDtypeStruct((N, M) if f else (M, N), d) for f, d in zip(flip, out_dtypes)],
        acc_shape=(tm, tn), epilogue=epilogue)


def mm_nt(name, dy, w, out_dtype, epilogue=None, extra=None):
    M, N = dy.shape
    tm = M
    out_bytes = jnp.dtype(out_dtype).itemsize + (0 if extra is None else extra.dtype.itemsize)
    if w.kind == "row":
        b = w.two_d
        kin = N_DEV * w.R
        tk = N if N <= MAX_TK else _tile(N, MULTI_TK)
        tn = _fit_tn(kin, tm, tk, N // tk, out_bytes)
        b_spec = pl.BlockSpec((tn, tk), lambda i, j, k: (j, k))
    else:
        b = w.arr
        kin = w.R
        tk = _tile(w.C, MULTI_TK)
        per = w.C // tk
        tn = _fit_tn(kin, tm, tk, N // tk, out_bytes)
        b_spec = pl.BlockSpec((None, tn, tk), lambda i, j, k: (k // per, j, k % per))
    grid = (M // tm, kin // tn, N // tk)
    o_spec = pl.BlockSpec((tm, tn), lambda i, j, k: (i, j))
    return _matmul(
        name, dy, b, contract=((1,), (1,)), grid=grid,
        a_spec=pl.BlockSpec((tm, tk), lambda i, j, k: (i, k)), b_spec=b_spec, o_specs=[o_spec],
        out_shape=[jax.ShapeDtypeStruct((M, kin), out_dtype)], acc_shape=(tm, tn), epilogue=epilogue,
        extra=() if extra is None else (extra,), extra_specs=() if extra is None else (o_spec,))[0]


TRANSPOSE_TILE = 512


def transpose(name, x):
    S, n = x.shape
    ts, tn = _tile(S, TRANSPOSE_TILE), _tile(n, TRANSPOSE_TILE)

    def body(x_ref, o_ref):
        o_ref[...] = x_ref[...].T

    return pl.pallas_call(
        body, name=name, grid=(S // ts, n // tn), in_specs=[pl.BlockSpec((ts, tn), lambda i, j: (i, j))],
        out_specs=pl.BlockSpec((tn, ts), lambda i, j: (j, i)), out_shape=jax.ShapeDtypeStruct((n, S), x.dtype),
        compiler_params=_params("parallel", "parallel"),
    )(x)


def mm_tn(name, x, dy, kind, R, C, transposed=False):
    if not transposed:
        x = transpose(name + "_t", x)
    kin, S = x.shape
    N = dy.shape[1]
    tk = S if S <= MAX_TK else _tile(S, MULTI_TK)
    nk = S // tk
    if kind == "col":
        tm = kin
        tn = _fit_tn(C, tm, tk, nk, 2)
        per = C // tn
        grid = (1, N // tn, nk)
        o_spec = pl.BlockSpec((None, None, tm, tn), lambda i, j, k: ((j // per) % 2, (j // per) // 2, 0, j % per))
    else:
        tm = R
        tn = _fit_tn(N, tm, tk, nk, 2)
        grid = (N_DEV, N // tn, nk)
        o_spec = pl.BlockSpec((None, None, tm, tn), lambda i, j, k: (i % 2, i // 2, 0, j))
    return _matmul(
        name, x, dy, contract=((1,), (0,)), grid=grid,
        a_spec=pl.BlockSpec((tm, tk), lambda i, j, k: (i, k)),
        b_spec=pl.BlockSpec((tk, tn), lambda i, j, k: (k, j)), o_specs=[o_spec],
        out_shape=[jax.ShapeDtypeStruct((2, 4, R, C), BF16)], acc_shape=(tm, tn))[0]


def _relu2_epilogue(acc):
    r = jnp.maximum(acc, 0.0)
    z = (r * r).astype(BF16)
    return acc, z, z.T


def _mulrelu_epilogue(acc, a):
    return (acc * (2.0 * jnp.maximum(a, 0.0)),)


ROW_TILE = 256


def ln_fwd(name, h, m, g, b):
    S, D = h.shape
    ts = _tile(S, ROW_TILE)

    def body(h_ref, m_ref, g_ref, b_ref, y_ref, y16_ref, yt_ref, xh_ref, rs_ref):
        u = ALPHA * h_ref[...] + m_ref[...]
        mu = jnp.mean(u, axis=-1, keepdims=True)
        d = u - mu
        var = jnp.mean(d * d, axis=-1, keepdims=True)
        rstd = lax.rsqrt(var + LN_EPS)
        xh = d * rstd
        y = xh * g_ref[...] + b_ref[...]
        y16 = y.astype(BF16)
        y_ref[...] = y
        y16_ref[...] = y16
        yt_ref[...] = y16.T
        xh_ref[...] = xh
        rs_ref[...] = jnp.broadcast_to(rstd, rs_ref.shape)

    row = pl.BlockSpec((ts, D), lambda i: (i, 0))
    vec = pl.BlockSpec((1, D), lambda i: (0, 0))
    return pl.pallas_call(
        body, name=name, grid=(S // ts,), in_specs=[row, row, vec, vec],
        out_specs=[row, row, pl.BlockSpec((D, ts), lambda i: (0, i)), row, pl.BlockSpec((ts, 128), lambda i: (i, 0))],
        out_shape=[jax.ShapeDtypeStruct((S, D), F32), jax.ShapeDtypeStruct((S, D), BF16),
                   jax.ShapeDtypeStruct((D, S), BF16), jax.ShapeDtypeStruct((S, D), F32),
                   jax.ShapeDtypeStruct((S, 128), F32)],
        compiler_params=_params("parallel"),
    )(h, m, g.reshape(1, D), b.reshape(1, D))


def ln_bwd(name, ga, gb, xhat, rstd, g):
    S, D = xhat.shape
    ts = _tile(S, ROW_TILE)
    two = gb is not None

    def body(*refs):
        if two:
            ga_ref, gb_ref, xh_ref, rs_ref, g_ref, du_ref, du16_ref, dg_ref, db_ref = refs
            dy = ALPHA * ga_ref[...] + gb_ref[...]
        else:
            ga_ref, xh_ref, rs_ref, g_ref, du_ref, du16_ref, dg_ref, db_ref = refs
            dy = ga_ref[...]
        xh = xh_ref[...]

        @pl.when(pl.program_id(0) == 0)
        def _():
            dg_ref[...] = jnp.zeros_like(dg_ref)
            db_ref[...] = jnp.zeros_like(db_ref)

        dg_ref[...] += jnp.sum(dy * xh, axis=0, keepdims=True)
        db_ref[...] += jnp.sum(dy, axis=0, keepdims=True)
        dxh = dy * g_ref[...]
        m1 = jnp.mean(dxh, axis=-1, keepdims=True)
        m2 = jnp.mean(dxh * xh, axis=-1, keepdims=True)
        du = rs_ref[:, 0:1] * (dxh - m1 - xh * m2)
        du_ref[...] = du
        du16_ref[...] = du.astype(BF16)

    row = pl.BlockSpec((ts, D), lambda i: (i, 0))
    vec = pl.BlockSpec((1, D), lambda i: (0, 0))
    stat = pl.BlockSpec((ts, 128), lambda i: (i, 0))
    ins = [ga, gb, xhat, rstd, g.reshape(1, D)] if two else [ga, xhat, rstd, g.reshape(1, D)]
    in_specs = [row, row, row, stat, vec] if two else [row, row, stat, vec]
    return pl.pallas_call(
        body, name=name, grid=(S // ts,), in_specs=in_specs, out_specs=[row, row, vec, vec],
        out_shape=[jax.ShapeDtypeStruct((S, D), F32), jax.ShapeDtypeStruct((S, D), BF16),
                   jax.ShapeDtypeStruct((1, D), F32), jax.ShapeDtypeStruct((1, D), F32)],
        compiler_params=_params("arbitrary"),
    )(*ins)


def loss_fwd_bwd(name, y, target):
    S, D = y.shape
    ts = _tile(S, ROW_TILE)

    def body(y_ref, t_ref, l_ref, dy_ref):
        @pl.when(pl.program_id(0) == 0)
        def _():
            l_ref[...] = jnp.zeros_like(l_ref)

        e = y_ref[...] - t_ref[...]
        l_ref[...] += jnp.sum(e * e)
        dy_ref[...] = e * (1.0 / D)

    row = pl.BlockSpec((ts, D), lambda i: (i, 0))
    return pl.pallas_call(
        body, name=name, grid=(S // ts,), in_specs=[row, row],
        out_specs=[pl.BlockSpec((1, 128), lambda i: (0, 0)), row],
        out_shape=[jax.ShapeDtypeStruct((1, 128), F32), jax.ShapeDtypeStruct((S, D), F32)],
        compiler_params=_params("arbitrary"),
    )(y, target)


def axpy(name, ga, gb):
    S, D = ga.shape
    ts = _tile(S, ROW_TILE)

    def body(a_ref, b_ref, o_ref):
        o_ref[...] = ALPHA * a_ref[...] + b_ref[...]

    row = pl.BlockSpec((ts, D), lambda i: (i, 0))
    return pl.pallas_call(body, name=name, grid=(S // ts,), in_specs=[row, row], out_specs=row,
                          out_shape=jax.ShapeDtypeStruct((S, D), F32), compiler_params=_params("parallel"))(ga, gb)


def rms_fwd(name, down, gq, gkv):
    S = down.shape[0]
    ts = _tile(S, ROW_TILE)
    L = MLA_Q_LORA

    def body(d_ref, gq_ref, gkv_ref, q_ref, kv_ref):
        for lo, g_ref, o_ref in ((0, gq_ref, q_ref), (L, gkv_ref, kv_ref)):
            x = d_ref[:, lo:lo + L]
            r = lax.rsqrt(jnp.mean(x * x, axis=-1, keepdims=True) + RMS_EPS)
            o_ref[...] = (x * r * g_ref[...]).astype(BF16)

    vec = pl.BlockSpec((1, L), lambda i: (0, 0))
    out = pl.BlockSpec((ts, L), lambda i: (i, 0))
    return pl.pallas_call(
        body, name=name, grid=(S // ts,), in_specs=[pl.BlockSpec((ts, down.shape[1]), lambda i: (i, 0)), vec, vec],
        out_specs=[out, out], out_shape=[jax.ShapeDtypeStruct((S, L), BF16)] * 2, compiler_params=_params("parallel"),
    )(down, gq.reshape(1, L), gkv.reshape(1, L))


def rms_bwd(name, down, dq, dkv, dkr, gq, gkv):
    S, W = down.shape
    ts = _tile(S, ROW_TILE)
    L = MLA_Q_LORA

    def body(d_ref, dq_ref, dkv_ref, dkr_ref, gq_ref, gkv_ref, o_ref, dgq_ref, dgkv_ref):
        @pl.when(pl.program_id(0) == 0)
        def _():
            dgq_ref[...] = jnp.zeros_like(dgq_ref)
            dgkv_ref[...] = jnp.zeros_like(dgkv_ref)

        for lo, dy_ref, g_ref, dg_ref in ((0, dq_ref, gq_ref, dgq_ref), (L, dkv_ref, gkv_ref, dgkv_ref)):
            x = d_ref[:, lo:lo + L]
            dy = dy_ref[...]
            r = lax.rsqrt(jnp.mean(x * x, axis=-1, keepdims=True) + RMS_EPS)
            dg_ref[...] += jnp.sum(dy * x * r, axis=0, keepdims=True)
            dyg = dy * g_ref[...]
            dx = r * dyg - x * (r * r * r) * jnp.mean(dyg * x, axis=-1, keepdims=True)
            o_ref[:, lo:lo + L] = dx.astype(BF16)
        o_ref[:, 2 * L:] = dkr_ref[...].astype(BF16)

    vec = pl.BlockSpec((1, L), lambda i: (0, 0))
    lat = pl.BlockSpec((ts, L), lambda i: (i, 0))
    full = pl.BlockSpec((ts, W), lambda i: (i, 0))
    return pl.pallas_call(
        body, name=name, grid=(S // ts,),
        in_specs=[full, lat, lat, pl.BlockSpec((ts, 128), lambda i: (i, 0)), vec, vec],
        out_specs=[full, vec, vec],
        out_shape=[jax.ShapeDtypeStruct((S, W), BF16), jax.ShapeDtypeStruct((1, L), F32), jax.ShapeDtypeStruct((1, L), F32)],
        compiler_params=_params("arbitrary"),
    )(down, dq, dkv, dkr, gq.reshape(1, L), gkv.reshape(1, L))


def rope_tables(S):
    half = MLA_ROPE // 2
    inv = (np.float32(ROPE_THETA) ** (-np.arange(half, dtype=np.float32) / np.float32(half))).astype(np.float32)
    ang = np.arange(S, dtype=np.float32)[:, None] * inv[None, :]
    cos, sin = np.cos(ang).astype(np.float32), np.sin(ang).astype(np.float32)
    z = np.zeros_like(cos)
    return (jnp.asarray(np.concatenate([cos, z, cos, z], 1)), jnp.asarray(np.concatenate([-sin, z, sin, z], 1)))


def _rot(x, cos, sin):
    return x * cos + pltpu.roll(x, 64, 1) * sin


def mla_prep_fwd(name, q, kv, down, cos, sin):
    S = q.shape[0]
    ts = _tile(S, ROW_TILE)

    def body(q_ref, kv_ref, kr_ref, c_ref, s_ref, qo_ref, ko_ref):
        c, s = c_ref[...], s_ref[...]
        key = _rot(kr_ref[...], c, s).astype(BF16)
        for h in range(HEADS):
            lo = 256 * h
            qo_ref[:, lo:lo + 128] = q_ref[:, lo:lo + 128].astype(BF16)
            qo_ref[:, lo + 128:lo + 256] = _rot(q_ref[:, lo + 128:lo + 256], c, s).astype(BF16)
            ko_ref[:, lo:lo + 128] = kv_ref[:, lo:lo + 128]
            ko_ref[:, lo + 128:lo + 256] = key

    heads = pl.BlockSpec((ts, HEADS * 256), lambda i: (i, 0))
    tab = pl.BlockSpec((ts, 128), lambda i: (i, 0))
    return pl.pallas_call(
        body, name=name, grid=(S // ts,),
        in_specs=[heads, heads, pl.BlockSpec((ts, 128), lambda i: (i, 2 * MLA_Q_LORA // 128)), tab, tab],
        out_specs=[heads, heads], out_shape=[jax.ShapeDtypeStruct(q.shape, BF16)] * 2,
        compiler_params=_params("parallel"),
    )(q, kv, down, cos, sin)


def mla_prep_bwd(name, dq, dk, dv, cos, sin):
    S = dq.shape[0]
    ts = _tile(S, ROW_TILE)

    def body(dq_ref, dk_ref, dv_ref, c_ref, s_ref, qo_ref, kvo_ref, kr_ref):
        c, s = c_ref[...], -s_ref[...]
        key = jnp.zeros((ts, 128), F32)
        for h in range(HEADS):
            lo = 256 * h
            qo_ref[:, lo:lo + 128] = dq_ref[:, lo:lo + 128].astype(BF16)
            qo_ref[:, lo + 128:lo + 256] = _rot(dq_ref[:, lo + 128:lo + 256], c, s).astype(BF16)
            kvo_ref[:, lo:lo + 128] = dk_ref[:, lo:lo + 128].astype(BF16)
            kvo_ref[:, lo + 128:lo + 256] = dv_ref[:, 128 * h:128 * h + 128].astype(BF16)
            key = key + dk_ref[:, lo + 128:lo + 256]
        kr_ref[...] = _rot(key, c, s)

    heads = pl.BlockSpec((ts, HEADS * 256), lambda i: (i, 0))
    tab = pl.BlockSpec((ts, 128), lambda i: (i, 0))
    return pl.pallas_call(
        body, name=name, grid=(S // ts,),
        in_specs=[heads, heads, pl.BlockSpec((ts, HEADS * 128), lambda i: (i, 0)), tab, tab],
        out_specs=[heads, heads, tab],
        out_shape=[jax.ShapeDtypeStruct(dq.shape, BF16), jax.ShapeDtypeStruct(dq.shape, BF16),
                   jax.ShapeDtypeStruct((S, 128), F32)],
        compiler_params=_params("parallel"),
    )(dq, dk, dv, cos, sin)


def _dot_nt(a, b):
    return lax.dot_general(a, b, (((1,), (1,)), ((), ())), preferred_element_type=F32)


def _dot_tn(a, b):
    return lax.dot_general(a, b, (((0,), (0,)), ((), ())), preferred_element_type=F32)


def _dot(a, b):
    return jnp.dot(a, b, preferred_element_type=F32)


def _positions(i, j, TQ, TK):
    row = i * TQ + lax.broadcasted_iota(jnp.int32, (TQ, TK), 0)
    col = j * TK + lax.broadcasted_iota(jnp.int32, (TQ, TK), 1)
    return row, col


def _softmax_mask(mode, row, col):
    rc, cc = row >> CHUNK_SHIFT, col >> CHUNK_SHIFT
    if mode == "mla":
        return cc <= rc
    return (cc <= rc) & (cc >= rc - CA_LEFT_CHUNKS)


def _key_blocks(mode, i, TQ, TK):
    per = TQ // TK
    if mode == "ca":
        lo = jnp.maximum(i - (CA_LEFT_CHUNKS * CHUNK) // TK, 0)
        return lo, 0, i - lo + 1
    return 0, i * per, per


class HeadCols:
    def __init__(self, arr, width, index, off=0, w=None):
        self.arr, self.width, self.index, self.off = arr, width, index, off
        self.w = width if w is None else w

    def rows(self, T):
        return pl.BlockSpec((T, ATT_G * self.width), lambda p, i: (i, self.index(p)))

    def full(self, S):
        return pl.BlockSpec((S, ATT_G * self.width), lambda p, i: (0, self.index(p)))

    def lanes(self, g):
        lo = g * self.width + self.off
        return slice(lo, lo + self.w)


def _att_tiles(mode, S):
    tk = min(ATT_TK, S)
    return (tk if mode == "ca" else min(ATT_TQ, S)), tk


def _walk(lo, n, per, step, carry, descending=False):
    tail = [lo + n + d for d in range(per)]
    if descending:
        for j in reversed(tail):
            carry = step(j, carry, True)
        return lax.fori_loop(0, n, lambda t, c: step(lo + n - 1 - t, c, False), carry)
    carry = lax.fori_loop(0, n, lambda t, c: step(lo + t, c, False), carry)
    for j in tail:
        carry = step(j, carry, True)
    return carry


def softmax_attn_fwd(name, mode, q, k, v, scale, bias=None):
    S = q.arr.shape[0]
    TQ, TK = _att_tiles(mode, S)
    G, dv = ATT_G, v.w

    def body(*refs):
        if bias is not None:
            q_ref, k_ref, v_ref, b_ref, o_ref, lse_ref = refs
        else:
            q_ref, k_ref, v_ref, o_ref, lse_ref = refs
        i = pl.program_id(1)
        qs = [q_ref[:, q.lanes(g)] for g in range(G)]

        def block(g, j, carry, mask, ks):
            m, l, acc = carry
            s = _dot_nt(qs[g], k_ref[ks, k.lanes(g)]) * scale
            if bias is not None:
                s = s + b_ref[g, jnp.minimum(i - j, 2)]
            if mask is not None:
                s = jnp.where(mask, s, NEG)
            m_new = jnp.maximum(m, jnp.max(s, axis=-1, keepdims=True))
            a = jnp.exp(m - m_new)
            p = jnp.exp(s - m_new)
            if mask is not None:
                p = jnp.where(mask, p, 0.0)
            l = a * l + jnp.sum(p, axis=-1, keepdims=True)
            acc = a * acc + _dot(p.astype(BF16), v_ref[ks, v.lanes(g)])
            return m_new, l, acc

        def step(j, carry, masked):
            ks = pl.ds(pl.multiple_of(j * TK, TK), TK)
            mask = _softmax_mask(mode, *_positions(i, j, TQ, TK)) if masked or mode == "ca" else None
            return tuple(block(g, j, carry[g], mask, ks) for g in range(G))

        init = (jnp.full((TQ, 1), NEG, F32), jnp.zeros((TQ, 1), F32), jnp.zeros((TQ, dv), F32))
        lo, n, per = _key_blocks(mode, i, TQ, TK)
        if mode == "ca":
            out = lax.fori_loop(lo, lo + per, lambda j, c: step(j, c, True), (init,) * G)
        else:
            out = _walk(lo, n, per, step, (init,) * G)
        for g, (m, l, acc) in enumerate(out):
            o_ref[:, g * dv:(g + 1) * dv] = (acc / l).astype(BF16)
            lse_ref[:, g * 128:(g + 1) * 128] = jnp.broadcast_to(m + jnp.log(l), (TQ, 128))

    in_specs = [q.rows(TQ), k.full(S), v.full(S)]
    ins = [q.arr, k.arr, v.arr]
    if bias is not None:
        in_specs.append(pl.BlockSpec((G, 3, TK, TK), lambda p, i: (p, 0, 0, 0)))
        ins.append(bias)
    return _call(
        body, ins, name=name, grid=(HEADS // G, S // TQ), in_specs=in_specs,
        out_specs=[pl.BlockSpec((TQ, G * dv), lambda p, i: (i, p)), pl.BlockSpec((TQ, G * 128), lambda p, i: (i, p))],
        out_shape=[jax.ShapeDtypeStruct((S, HEADS * dv), BF16), jax.ShapeDtypeStruct((S, HEADS * 128), F32)],
        semantics=("parallel", "parallel"))


def softmax_attn_bwd(name, mode, q, k, v, o, do, lse, scale, bias=None):
    S = q.arr.shape[0]
    TQ, TK = _att_tiles(mode, S)
    G, dqk, dv = ATT_G, q.w, v.w

    def body(*refs):
        if bias is not None:
            q_ref, k_ref, v_ref, o_ref, do_ref, lse_ref, b_ref, dq_ref, dk_ref, dv_ref, db_ref = refs
        else:
            q_ref, k_ref, v_ref, o_ref, do_ref, lse_ref, dq_ref, dk_ref, dv_ref = refs
        i = pl.program_id(1)

        @pl.when(i == 0)
        def _():
            dk_ref[...] = jnp.zeros_like(dk_ref)
            dv_ref[...] = jnp.zeros_like(dv_ref)
            if bias is not None:
                db_ref[...] = jnp.zeros_like(db_ref)

        qs = [q_ref[:, q.lanes(g)] for g in range(G)]
        dos = [do_ref[:, do.lanes(g)] for g in range(G)]
        lses = [lse_ref[:, g * 128:g * 128 + 1] for g in range(G)]
        deltas = [jnp.sum(dos[g].astype(F32) * o_ref[:, o.lanes(g)].astype(F32), axis=-1, keepdims=True)
                  for g in range(G)]

        def block(g, j, dq, mask, ks):
            kb, vb = k_ref[ks, k.lanes(g)], v_ref[ks, v.lanes(g)]
            s = _dot_nt(qs[g], kb) * scale
            if bias is not None:
                slot = jnp.minimum(i - j, 2)
                s = s + b_ref[g, slot]
            p = jnp.exp(s - lses[g])
            if mask is not None:
                p = jnp.where(mask, p, 0.0)
            ds = p * (_dot_nt(dos[g], vb) - deltas[g])
            if bias is not None:
                db_ref[g, slot] += ds
            dsb = (ds * scale).astype(BF16)
            dk_ref[ks, g * dqk:(g + 1) * dqk] += _dot_tn(dsb, qs[g])
            dv_ref[ks, g * dv:(g + 1) * dv] += _dot_tn(p.astype(BF16), dos[g])
            return dq + _dot(dsb, kb)

        def step(j, carry, masked):
            ks = pl.ds(pl.multiple_of(j * TK, TK), TK)
            mask = _softmax_mask(mode, *_positions(i, j, TQ, TK)) if masked or mode == "ca" else None
            return tuple(block(g, j, carry[g], mask, ks) for g in range(G))

        init = (jnp.zeros((TQ, dqk), F32),) * G
        lo, n, per = _key_blocks(mode, i, TQ, TK)
        if mode == "ca":
            out = lax.fori_loop(lo, lo + per, lambda j, c: step(j, c, True), init)
        else:
            out = _walk(lo, n, per, step, init)
        for g in range(G):
            dq_ref[:, g * dqk:(g + 1) * dqk] = out[g]

    in_specs = [q.rows(TQ), k.full(S), v.full(S), o.rows(TQ), do.rows(TQ),
                pl.BlockSpec((TQ, G * 128), lambda p, i: (i, p))]
    ins = [q.arr, k.arr, v.arr, o.arr, do.arr, lse]
    out_specs = [pl.BlockSpec((TQ, G * dqk), lambda p, i: (i, p)), pl.BlockSpec((S, G * dqk), lambda p, i: (0, p)),
                 pl.BlockSpec((S, G * dv), lambda p, i: (0, p))]
    out_shape = [jax.ShapeDtypeStruct((S, HEADS * dqk), F32), jax.ShapeDtypeStruct((S, HEADS * dqk), F32),
                 jax.ShapeDtypeStruct((S, HEADS * dv), F32)]
    if bias is not None:
        bspec = pl.BlockSpec((G, 3, TK, TK), lambda p, i: (p, 0, 0, 0))
        in_specs.append(bspec)
        ins.append(bias)
        out_specs.append(bspec)
        out_shape.append(jax.ShapeDtypeStruct(bias.shape, F32))
    return _call(body, ins, name=name, grid=(HEADS // G, S // TQ), in_specs=in_specs, out_specs=out_specs,
                 out_shape=out_shape, semantics=("parallel", "arbitrary"))


def _split2(x):
    hi = x.astype(BF16)
    return hi, (x - hi.astype(F32)).astype(BF16)


def _split3(x):
    hi = x.astype(BF16)
    r = x - hi.astype(F32)
    mid = r.astype(BF16)
    return hi, mid, (r - mid.astype(F32)).astype(BF16)


def _stick_block(qb, kb, strict, scale):
    z = _dot_nt(qb, kb) * scale
    sp = jnp.log(1.0 + jnp.exp(-jnp.abs(z)))
    lb = jnp.minimum(z, 0.0) - sp
    l1m = jnp.minimum(-z, 0.0) - sp
    if strict is not None:
        l1m = jnp.where(strict, l1m, 0.0)
    return z, lb, l1m


def _strict_mask(i, j, TQ, TK):
    row, col = _positions(i, j, TQ, TK)
    return col < row


def _tri(T, inclusive):
    r = lax.broadcasted_iota(jnp.int32, (T, T), 0)
    c = lax.broadcasted_iota(jnp.int32, (T, T), 1)
    return ((r >= c) if inclusive else (r > c)).astype(BF16)


def _tri_prefix(T, inclusive):
    r = lax.broadcasted_iota(jnp.int32, (T, T), 0)
    c = lax.broadcasted_iota(jnp.int32, (T, T), 1)
    return ((r <= c) if inclusive else (r < c)).astype(BF16)


def _suffix(parts, tri):
    out = _dot(parts[0], tri)
    for p in parts[1:]:
        out = out + _dot(p, tri)
    return out


def stick_attn_fwd(name, q, k, v, scale):
    S = q.arr.shape[0]
    TQ, TK = _att_tiles("sb", S)
    G, dv = ATT_G, v.w

    def body(q_ref, k_ref, v_ref, o_ref, tot_ref):
        i = pl.program_id(1)
        qs = [q_ref[:, q.lanes(g)] for g in range(G)]
        tri = _tri(TK, False)

        def block(g, carry, strict, ks):
            right, acc = carry
            z, lb, l1m = _stick_block(qs[g], k_ref[ks, k.lanes(g)], strict, scale)
            a = jnp.exp(lb + _suffix(_split2(l1m), tri) + right)
            if strict is not None:
                a = jnp.where(strict, a, 0.0)
            acc = acc + _dot(a.astype(BF16), v_ref[ks, v.lanes(g)])
            return right + jnp.sum(l1m, axis=-1, keepdims=True), acc

        def step(j, carry, masked):
            ks = pl.ds(pl.multiple_of(j * TK, TK), TK)
            strict = _strict_mask(i, j, TQ, TK) if masked else None
            return tuple(block(g, carry[g], strict, ks) for g in range(G))

        init = (jnp.zeros((TQ, 1), F32), jnp.zeros((TQ, dv), F32))
        lo, n, per = _key_blocks("sb", i, TQ, TK)
        out = _walk(lo, n, per, step, (init,) * G, descending=True)
        for g in range(G):
            o_ref[:, g * dv:(g + 1) * dv] = out[g][1].astype(BF16)
            tot_ref[:, g * 128:(g + 1) * 128] = jnp.broadcast_to(out[g][0], (TQ, 128))

    return _call(
        body, [q.arr, k.arr, v.arr], name=name, grid=(HEADS // G, S // TQ),
        in_specs=[q.rows(TQ), k.full(S), v.full(S)],
        out_specs=[pl.BlockSpec((TQ, G * dv), lambda p, i: (i, p)), pl.BlockSpec((TQ, G * 128), lambda p, i: (i, p))],
        out_shape=[jax.ShapeDtypeStruct((S, HEADS * dv), BF16), jax.ShapeDtypeStruct((S, HEADS * 128), F32)],
        semantics=("parallel", "parallel"))


def stick_attn_bwd(name, q, k, v, do, total, scale):
    S = q.arr.shape[0]
    TQ, TK = _att_tiles("sb", S)
    G, dqk, dv = ATT_G, q.w, v.w

    def body(q_ref, k_ref, v_ref, do_ref, tot_ref, dq_ref, dk_ref, dv_ref):
        i = pl.program_id(1)

        @pl.when(i == 0)
        def _():
            dk_ref[...] = jnp.zeros_like(dk_ref)
            dv_ref[...] = jnp.zeros_like(dv_ref)

        qs = [q_ref[:, q.lanes(g)] for g in range(G)]
        dos = [do_ref[:, do.lanes(g)] for g in range(G)]
        tots = [tot_ref[:, g * 128:g * 128 + 1] for g in range(G)]
        upto = _tri_prefix(TK, True)
        before = _tri_prefix(TK, False)

        def step(j, carry, masked):
            ks = pl.ds(pl.multiple_of(j * TK, TK), TK)
            strict = _strict_mask(i, j, TQ, TK) if masked else None
            out = []
            for g in range(G):
                left, gleft, dq = carry[g]
                kb = k_ref[ks, k.lanes(g)]
                z, lb, l1m = _stick_block(qs[g], kb, strict, scale)
                a = jnp.exp(lb + (tots[g] - (left + _suffix(_split3(l1m), upto))))
                if strict is not None:
                    a = jnp.where(strict, a, 0.0)
                gg = a * _dot_nt(dos[g], v_ref[ks, v.lanes(g)])
                c = gleft + _suffix(_split3(gg), before)
                sig = 1.0 / (1.0 + jnp.exp(-z))
                dz = gg * (1.0 - sig) - c * sig
                if strict is not None:
                    dz = jnp.where(strict, dz, 0.0)
                dzb = (dz * scale).astype(BF16)
                dk_ref[ks, g * dqk:(g + 1) * dqk] += _dot_tn(dzb, qs[g])
                dv_ref[ks, g * dv:(g + 1) * dv] += _dot_tn(a.astype(BF16), dos[g])
                out.append((left + jnp.sum(l1m, axis=-1, keepdims=True),
                            gleft + jnp.sum(gg, axis=-1, keepdims=True), dq + _dot(dzb, kb)))
            return tuple(out)

        zero = jnp.zeros((TQ, 1), F32)
        lo, n, per = _key_blocks("sb", i, TQ, TK)
        out = _walk(lo, n, per, step, ((zero, zero, jnp.zeros((TQ, dqk), F32)),) * G)
        for g in range(G):
            dq_ref[:, g * dqk:(g + 1) * dqk] = out[g][2]

    return _call(
        body, [q.arr, k.arr, v.arr, do.arr, total], name=name, grid=(HEADS // G, S // TQ),
        in_specs=[q.rows(TQ), k.full(S), v.full(S), do.rows(TQ), pl.BlockSpec((TQ, G * 128), lambda p, i: (i, p))],
        out_specs=[pl.BlockSpec((TQ, G * dqk), lambda p, i: (i, p)), pl.BlockSpec((S, G * dqk), lambda p, i: (0, p)),
                   pl.BlockSpec((S, G * dv), lambda p, i: (0, p))],
        out_shape=[jax.ShapeDtypeStruct((S, HEADS * dqk), F32), jax.ShapeDtypeStruct((S, HEADS * dqk), F32),
                   jax.ShapeDtypeStruct((S, HEADS * dv), F32)],
        semantics=("parallel", "arbitrary"))


def _skew(x, back):
    T = x.shape[0]
    rows = lax.broadcasted_iota(jnp.int32, (T, T), 0)
    for b in range(T.bit_length() - 1):
        shift = T - (1 << b) if back else 1 << b
        x = jnp.where(((rows >> b) & 1) == 1, pltpu.roll(x, shift, 1), x)
    return x


def _table_rows(table):
    t = jnp.pad(table.T, ((0, 0), (0, 2 * REL_CLIP_LEFT - REL_TABLE)))
    return t.reshape(table.shape[1], 2, REL_CLIP_LEFT)


def rel_bias_blocks(name, table, T):
    assert T == 2 * REL_CLIP_LEFT, "the base rows below are laid out for blocks of 256"

    def body(t_ref, o_ref):
        low, high = t_ref[0:1, :], t_ref[1:2, :]
        first = jnp.broadcast_to(t_ref[0:1, 0:1], (1, REL_CLIP_LEFT))
        qq = lax.broadcasted_iota(jnp.int32, (T, T), 0)
        kk = lax.broadcasted_iota(jnp.int32, (T, T), 1)

        def rolled(row):
            return _skew(jnp.broadcast_to(row, (T, T)), False)

        far = jnp.concatenate([first, low], axis=1)
        near = jnp.concatenate([high, jnp.zeros_like(high)], axis=1)
        o_ref[0] = jnp.where(kk >= qq, rolled(near), rolled(far))
        o_ref[1] = jnp.where(kk >= qq, rolled(far), jnp.broadcast_to(t_ref[0:1, 0:1], (T, T)))
        o_ref[2] = jnp.broadcast_to(t_ref[0:1, 0:1], (T, T))

    return pl.pallas_call(
        body, name=name, grid=(HEADS,), in_specs=[pl.BlockSpec((None, 2, REL_CLIP_LEFT), lambda h: (h, 0, 0))],
        out_specs=pl.BlockSpec((None, 3, T, T), lambda h: (h, 0, 0, 0)),
        out_shape=jax.ShapeDtypeStruct((HEADS, 3, T, T), F32), compiler_params=_params("parallel"),
    )(_table_rows(table))


def rel_bias_grad(name, dbias):
    T = dbias.shape[-1]
    L = REL_CLIP_LEFT
    assert T == 2 * L

    def body(d_ref, o_ref):
        qq = lax.broadcasted_iota(jnp.int32, (T, T), 0)
        ll = lax.broadcasted_iota(jnp.int32, (T, T), 1)
        wrapped = ll + qq >= T

        def columns(d):
            x = _skew(d_ref[d], True)
            return (jnp.sum(jnp.where(wrapped, 0.0, x), axis=0, keepdims=True),
                    jnp.sum(jnp.where(wrapped, x, 0.0), axis=0, keepdims=True))

        pos0, neg0 = columns(0)
        pos1, neg1 = columns(1)
        clipped = (jnp.sum(neg0[:, :L]) + jnp.sum(pos1[:, :L]) + jnp.sum(neg1) + jnp.sum(d_ref[2]))
        lane = lax.broadcasted_iota(jnp.int32, (1, L), 1)
        low = neg0[:, L:] + pos1[:, L:]
        o_ref[...] = jnp.zeros_like(o_ref)
        o_ref[0:1, :] = jnp.where(lane == 0, low + clipped, low)
        o_ref[1:2, :] = pos0[:, :L]

    rows = pl.pallas_call(
        body, name=name, grid=(HEADS,), in_specs=[pl.BlockSpec((None, 3, T, T), lambda h: (h, 0, 0, 0))],
        out_specs=pl.BlockSpec((None, 8, L), lambda h: (h, 0, 0)), out_shape=jax.ShapeDtypeStruct((HEADS, 8, L), F32),
        compiler_params=_params("parallel"),
    )(dbias)
    return rows[:, :2, :].reshape(HEADS, 2 * L)[:, :REL_TABLE].T


def _place():
    return lax.axis_index("x"), lax.axis_index("y"), lax.axis_index("c")


def all_gather(name, shards):
    n = len(shards)

    def body(*refs):
        x_refs, out_refs = refs[:n], refs[n:2 * n]
        send_sems, recv_sems, local_sems = refs[2 * n:]
        x, y, c = _place()
        me, sibling = (x, y, c), (x, y, 1 - c)
        chips = [(1 - x, y), (x, 1 - y), (1 - x, 1 - y)]

        def block(t, dev):
            return out_refs[t].at[4 * dev[0] + 2 * dev[1] + dev[2]]

        def copy(t, k, dev, to, src=None):
            return pltpu.make_async_remote_copy(
                src_ref=block(t, dev) if src is None else src, dst_ref=block(t, dev),
                send_sem=send_sems.at[t, k], recv_sem=recv_sems.at[t, k], device_id=to, device_id_type=MESH)

        mine = [pltpu.make_async_copy(x_refs[t], block(t, me), local_sems.at[t]) for t in range(n)]
        for cp in mine:
            cp.start()
        first = []
        for t in range(n):
            first.append(copy(t, 0, me, sibling, src=x_refs[t]))
            first += [copy(t, 1 + j, me, (*chip, c), src=x_refs[t]) for j, chip in enumerate(chips)]
        for cp in first:
            cp.start()
        passed = []
        for j, chip in enumerate(chips):
            for t in range(n):
                copy(t, 1 + j, (*chip, c), me).wait_recv()
                cp = copy(t, 4 + j, (*chip, c), sibling)
                cp.start()
                passed.append(cp)
        for t in range(n):
            copy(t, 0, sibling, me).wait_recv()
            for j, chip in enumerate(chips):
                copy(t, 4 + j, (*chip, 1 - c), me).wait_recv()
        for cp in first + passed:
            cp.wait_send()
        for cp in mine:
            cp.wait()

    return pl.pallas_call(
        body, name=name, in_specs=[HBM] * n, out_specs=[HBM] * n,
        out_shape=[jax.ShapeDtypeStruct((N_DEV, *s.shape), s.dtype) for s in shards],
        scratch_shapes=[pltpu.SemaphoreType.DMA((n, 7)), pltpu.SemaphoreType.DMA((n, 7)), pltpu.SemaphoreType.DMA((n,))],
    )(*shards)


def _remote(src, dst, send_sems, recv_sems, k, to):
    return pltpu.make_async_remote_copy(src_ref=src, dst_ref=dst, send_sem=send_sems.at[k], recv_sem=recv_sems.at[k],
                                        device_id=to, device_id_type=MESH)


class Future:
    def __init__(self):
        self.value = None

    def get(self):
        if self.value is None:
            SCHED.flush()
        return self.value


def gather_jobs(name, shards, wants_chips, want_sibling):
    n, shares = len(shards), len(wants_chips)
    result = Future()
    lands = [jax.ShapeDtypeStruct((N_DEV, *s.shape), s.dtype) for s in shards]

    def to_chips(share):
        def copies(in_refs, out_refs, send_sems, recv_sems):
            x, y, c = _place()
            me = 4 * x + 2 * y + c
            cps = []
            for t in range(n):
                rows = shards[t].shape[0] // shares
                mine = pl.ds(share * rows, rows)
                src, dst = in_refs[t].at[mine], out_refs[t].at[me, mine]
                cps.append(pltpu.make_async_copy(src, dst, send_sems.at[4 * t]))
                for j, chip in enumerate([(1 - x, y), (x, 1 - y), (1 - x, 1 - y)]):
                    cps.append(_remote(src, dst, send_sems, recv_sems, 4 * t + 1 + j, (*chip, c)))
            return cps
        return copies

    def to_sibling(in_refs, out_refs, send_sems, recv_sems):
        x, y, c = _place()
        return [_remote(in_refs[t].at[2 * chip + c], out_refs[t].at[2 * chip + c], send_sems, recv_sems, 4 * t + chip,
                        (x, y, 1 - c)) for t in range(n) for chip in range(4)]

    def post(share, landed):
        if share == shares:
            SCHED.post(Job(name + "_sibling", want_sibling, landed, lands, 4 * n, to_sibling,
                           lambda final: setattr(result, "value", final), aliases={t: t for t in range(n)}))
        else:
            SCHED.post(Job(f"{name}_chips{share}", wants_chips[share], list(shards) + (landed or []), lands, 4 * n,
                           to_chips(share), lambda outs: post(share + 1, outs),
                           aliases={n + t: t for t in range(n)} if landed else None))

    post(0, None)
    return result


def scatter_jobs(name, grads, core, chip, want_sibling, wants_chips):
    n = len(grads)
    result = Future()
    sums = [None] * n

    def to_sibling(in_refs, out_refs, send_sems, recv_sems):
        x, y, c = _place()
        return [_remote(in_refs[t].at[1 - c], out_refs[t], send_sems, recv_sems, t, (x, y, 1 - c)) for t in range(n)]

    def after_sibling(received):
        parts = [add_sibling(f"{name}_add{t}", grads[t], received[t], core) for t in range(n)]
        for group, want in wants_chips:
            def to_chips(in_refs, out_refs, send_sems, recv_sems, m=len(group)):
                x, y, c = _place()
                return [_remote(in_refs[t].at[2 * cx + cy], out_refs[t].at[j], send_sems, recv_sems, 3 * t + j,
                                (cx, cy, c))
                        for t in range(m) for j, (cx, cy) in enumerate([(1 - x, y), (x, 1 - y), (1 - x, 1 - y)])]

            def after_chips(received, group=group):
                for t, r in zip(group, received):
                    sums[t] = sum_chips(f"{name}_sum{t}", parts[t], r, chip)
                if all(s is not None for s in sums):
                    result.value = sums

            mine = [parts[t] for t in group]
            SCHED.post(Job(f"{name}_chips{group[0]}", want, mine,
                           [jax.ShapeDtypeStruct((3, *p.shape[1:]), p.dtype) for p in mine], 3 * len(mine), to_chips,
                           after_chips))

    SCHED.post(Job(name + "_sibling", want_sibling, grads, [jax.ShapeDtypeStruct(g.shape[1:], g.dtype) for g in grads],
                   n, to_sibling, after_sibling))
    return result


def _as_rows(shape):
    return (int(np.prod(shape[:-1])), shape[-1])


ELEMENTWISE_BLOCK = 256 * 1024


def _row_tile(rows, cols):
    return _tile(rows, max(128, ELEMENTWISE_BLOCK // cols // 128 * 128))


def add_sibling(name, grad, recv, core):
    rows, cols = _as_rows(grad.shape[2:])
    tr = _row_tile(rows, cols)

    def body(c_ref, g_ref, r_ref, o_ref):
        o_ref[...] = (g_ref[...].astype(F32) + r_ref[...].astype(F32)).astype(BF16)

    blk = pl.BlockSpec((None, tr, cols), lambda k, i, c_ref: (k, i, 0))
    return pl.pallas_call(
        body, name=name,
        grid_spec=pltpu.PrefetchScalarGridSpec(
            num_scalar_prefetch=1, grid=(4, rows // tr),
            in_specs=[pl.BlockSpec((None, None, tr, cols), lambda k, i, c_ref: (c_ref[0], k, i, 0)), blk],
            out_specs=blk),
        out_shape=jax.ShapeDtypeStruct((4, rows, cols), BF16), compiler_params=_params("parallel", "parallel"),
    )(core, grad.reshape(2, 4, rows, cols), recv.reshape(4, rows, cols)).reshape(recv.shape)


def sum_chips(name, part, recv, chip):
    shape = part.shape[1:]
    rows, cols = _as_rows(shape)
    tr = _row_tile(rows, cols)

    def body(c_ref, p_ref, r_ref, o_ref):
        o_ref[...] = (p_ref[...].astype(F32) + r_ref[0].astype(F32) + r_ref[1].astype(F32) + r_ref[2].astype(F32))

    return pl.pallas_call(
        body, name=name,
        grid_spec=pltpu.PrefetchScalarGridSpec(
            num_scalar_prefetch=1, grid=(rows // tr,),
            in_specs=[pl.BlockSpec((None, tr, cols), lambda i, c_ref: (c_ref[0], i, 0)),
                      pl.BlockSpec((3, tr, cols), lambda i, c_ref: (0, i, 0))],
            out_specs=pl.BlockSpec((tr, cols), lambda i, c_ref: (i, 0))),
        out_shape=jax.ShapeDtypeStruct((rows, cols), F32), compiler_params=_params("parallel"),
    )(chip, part.reshape(4, rows, cols), recv.reshape(3, rows, cols)).reshape(shape)


def sum_devices(name, gathered):
    _, rows, cols = gathered.shape

    def body(g_ref, o_ref):
        acc = g_ref[0]
        for d in range(1, N_DEV):
            acc = acc + g_ref[d]
        o_ref[...] = acc

    return pl.pallas_call(body, name=name, out_shape=jax.ShapeDtypeStruct((rows, cols), F32))(gathered)


def adamw(name, w, g, m, v):
    shape = w.shape
    rows, cols = _as_rows(shape)
    tr = _row_tile(rows, cols) if rows % 8 == 0 else rows
    c1 = 1.0 / (1.0 - ADAM_B1 ** ADAM_STEP)
    c2 = 1.0 / (1.0 - ADAM_B2 ** ADAM_STEP)

    def body(w_ref, g_ref, m_ref, v_ref, d_ref, mo_ref, vo_ref):
        g_ = g_ref[...]
        m_ = ADAM_B1 * m_ref[...] + (1.0 - ADAM_B1) * g_
        v_ = ADAM_B2 * v_ref[...] + (1.0 - ADAM_B2) * (g_ * g_)
        d_ref[...] = -ADAM_LR * ((m_ * c1) / (jnp.sqrt(v_ * c2) + ADAM_EPS) + ADAM_WD * w_ref[...])
        mo_ref[...] = m_
        vo_ref[...] = v_

    blk = pl.BlockSpec((tr, cols), lambda i: (i, 0))
    outs = pl.pallas_call(
        body, name=name, grid=(rows // tr,), in_specs=[blk] * 4, out_specs=[blk] * 3,
        out_shape=[jax.ShapeDtypeStruct((rows, cols), F32)] * 3, compiler_params=_params("parallel"),
    )(*[a.reshape(rows, cols) for a in (w, g, m, v)])
    return [o.reshape(shape) for o in outs]


def _spread_rope(r):
    z = jnp.zeros_like(r[..., :32])
    return jnp.concatenate([r[..., :32], z, r[..., 32:], z], -1)


def _gather_rope(r):
    return jnp.concatenate([r[..., :32], r[..., 64:96]], -1)


def pad_w_uq(w):
    w = w.reshape(w.shape[0], -1, MLA_NOPE + MLA_ROPE)
    return jnp.concatenate([w[..., :MLA_NOPE], _spread_rope(w[..., MLA_NOPE:])], -1).reshape(w.shape[0], -1)


def unpad_w_uq(g):
    g = g.reshape(g.shape[0], -1, 2 * MLA_NOPE)
    return jnp.concatenate([g[..., :MLA_NOPE], _gather_rope(g[..., MLA_NOPE:])], -1).reshape(g.shape[0], -1)


def pad_w_down(w):
    lat = MLA_Q_LORA + MLA_KV_LORA
    return jnp.concatenate([w[:, :lat], _spread_rope(w[:, lat:])], -1)


def unpad_w_down(g):
    lat = MLA_Q_LORA + MLA_KV_LORA
    return jnp.concatenate([g[:, :lat], _gather_rope(g[:, lat:])], -1)


def _heads(arr, width, first=0, off=0, w=None):
    return HeadCols(arr, width, lambda p: first // ATT_G + p, off, w)


def mla_forward(h16, w, gq, gkv, tables):
    cos, sin = tables
    down = mm_nn("mla_down", h16, w["down"], [F32])[0]
    cq, ckv = rms_fwd("mla_rms", down, gq, gkv)
    q = mm_nn("mla_uq", cq, w["uq"], [F32])[0]
    kv = mm_nn("mla_ukv", ckv, w["ukv"], [BF16])[0]
    qr, kp = mla_prep_fwd("mla_prep", q, kv, down, cos, sin)
    scale = (MLA_NOPE + MLA_ROPE) ** -0.5
    o, lse = softmax_attn_fwd("mla_attn", "mla", _heads(qr, 256), _heads(kp, 256), _heads(kv, 256, off=128, w=128), scale)
    m = mm_nn("mla_wo", o, w["wo"], [F32])[0]
    return m, (down, cq, ckv, qr, kp, kv, o, lse)


def mla_backward(du16, h16t, saved, w, gq, gkv, tables):
    cos, sin = tables
    down, cq, ckv, qr, kp, kv, o, lse = saved
    scale = (MLA_NOPE + MLA_ROPE) ** -0.5
    g = {"wo": mm_tn("mla_dwo", o, du16, "row", w["wo"].R, w["wo"].C)}
    do = mm_nt("mla_do", du16, w["wo"], BF16)
    dq, dk, dv = softmax_attn_bwd("mla_attn_bwd", "mla", _heads(qr, 256), _heads(kp, 256), _heads(kv, 256, off=128, w=128),
                                  _heads(o, 128), _heads(do, 128), lse, scale)
    dq16, dkv16, dkr = mla_prep_bwd("mla_prep_bwd", dq, dk, dv, cos, sin)
    g["uq"] = mm_tn("mla_duq", cq, dq16, "col", w["uq"].R, w["uq"].C)
    dcq = mm_nt("mla_dcq", dq16, w["uq"], F32)
    g["ukv"] = mm_tn("mla_dukv", ckv, dkv16, "col", w["ukv"].R, w["ukv"].C)
    dckv = mm_nt("mla_dckv", dkv16, w["ukv"], F32)
    ddown, dgq, dgkv = rms_bwd("mla_rms_bwd", down, dcq, dckv, dkr, gq, gkv)
    g["down"] = mm_tn("mla_ddown", h16t, ddown, "row", w["down"].R, w["down"].C, transposed=True)
    dh = mm_nt("mla_dh", ddown, w["down"], F32)
    return dh, g, (dgq, dgkv)


def qkv_forward(kind, h16, w, bias=None):
    qkv = mm_nn(kind + "_qkv", h16, w["qkv"], [BF16])[0]
    q, k, v = _heads(qkv, 128), _heads(qkv, 128, HEADS), _heads(qkv, 128, 2 * HEADS)
    scale = HEAD_DIM ** -0.5
    if kind == "sb":
        o, lse = stick_attn_fwd("sb_attn", q, k, v, scale)
    else:
        o, lse = softmax_attn_fwd("ca_attn", "ca", q, k, v, scale, bias)
    m = mm_nn(kind + "_wo", o, w["wo"], [F32])[0]
    return m, (qkv, o, lse)


def qkv_backward(kind, du16, h16t, saved, w, bias=None):
    qkv, o, lse = saved
    q, k, v = _heads(qkv, 128), _heads(qkv, 128, HEADS), _heads(qkv, 128, 2 * HEADS)
    scale = HEAD_DIM ** -0.5
    g = {"wo": mm_tn(kind + "_dwo", o, du16, "row", w["wo"].R, w["wo"].C)}
    do = mm_nt(kind + "_do", du16, w["wo"], BF16)
    dbias = None
    if kind == "sb":
        dq, dk, dv = stick_attn_bwd("sb_attn_bwd", q, k, v, _heads(do, 128), lse, scale)
    else:
        dq, dk, dv, dbias = softmax_attn_bwd("ca_attn_bwd", "ca", q, k, v, _heads(o, 128), _heads(do, 128), lse,
                                             scale, bias)
    dqkv = jnp.concatenate([dq, dk, dv], axis=1).astype(BF16)
    g["qkv"] = mm_tn(kind + "_dqkv", h16t, dqkv, "col", w["qkv"].R, w["qkv"].C, transposed=True)
    dh = mm_nt(kind + "_dh", dqkv, w["qkv"], F32)
    return dh, g, dbias


def mlp_forward(h16, w):
    a, z, zt = mm_nn("ffn_in", h16, w["w_in"], [F32, BF16, BF16], epilogue=_relu2_epilogue,
                     transposed=(False, False, True))
    f = mm_nn("ffn_out", z, w["w_out"], [F32])[0]
    return f, (a, zt)


def mlp_backward(du16, h16t, saved, w):
    a, zt = saved
    da = mm_nt("ffn_da", du16, w["w_out"], BF16, epilogue=_mulrelu_epilogue, extra=a)
    g = {"w_out": mm_tn("ffn_dwout", zt, du16, "row", w["w_out"].R, w["w_out"].C, transposed=True)}
    dh = mm_nt("ffn_dh", da, w["w_in"], F32)
    g["w_in"] = mm_tn("ffn_dwin", h16t, da, "col", w["w_in"].R, w["w_in"].C, transposed=True)
    return dh, g


WEIGHTS = ("ln_mix_g", "ln_mix_b", "ln_ffn_g", "ln_ffn_b", "ffn_w_in", "ffn_w_out", "mla_w_down", "mla_q_norm_g",
           "mla_w_uq", "mla_kv_norm_g", "mla_w_ukv", "mla_w_o", "sb_w_qkv", "sb_w_o", "ca_w_qkv", "ca_rel_bias",
           "ca_w_o")
MIXERS = ("mla", "sb", "ca")
LAYER_WEIGHTS = {
    "mla": (("down", "mla_w_down", "row"), ("uq", "mla_w_uq", "col"), ("ukv", "mla_w_ukv", "col"),
            ("wo", "mla_w_o", "row")),
    "sb": (("qkv", "sb_w_qkv", "col"), ("wo", "sb_w_o", "row")),
    "ca": (("qkv", "ca_w_qkv", "col"), ("wo", "ca_w_o", "row")),
    "ffn": (("w_in", "ffn_w_in", "col"), ("w_out", "ffn_w_out", "row")),
}
PAD = {"mla_w_down": pad_w_down, "mla_w_uq": pad_w_uq}
UNPAD = {"mla_w_down": unpad_w_down, "mla_w_uq": unpad_w_uq}


def _pack_rows(vectors):
    flat = jnp.concatenate([v.reshape(-1) for v in vectors])
    n = flat.shape[0]
    rows = -(-n // 1024) * 8
    offsets = np.cumsum([0] + [int(np.prod(v.shape)) for v in vectors])
    return jnp.pad(flat, (0, rows * 128 - n)).reshape(rows, 128), offsets


def _part(i, part):
    group, idx = (MIXERS[i % 3], i // 3) if part == "mix" else ("ffn", i)
    return [(key, name, how, idx) for key, name, how in LAYER_WEIGHTS[group]]


def kernel(x, ln_mix_g, ln_mix_b, ln_ffn_g, ln_ffn_b, ffn_w_in, ffn_w_out, mla_w_down, mla_q_norm_g, mla_w_uq, mla_kv_norm_g, mla_w_ukv, mla_w_o, sb_w_qkv, sb_w_o, ca_w_qkv, ca_rel_bias, ca_w_o, loss_target, m_ln_mix_g, m_ln_mix_b, m_ln_ffn_g, m_ln_ffn_b, m_ffn_w_in, m_ffn_w_out, m_mla_w_down, m_mla_q_norm_g, m_mla_w_uq, m_mla_kv_norm_g, m_mla_w_ukv, m_mla_w_o, m_sb_w_qkv, m_sb_w_o, m_ca_w_qkv, m_ca_rel_bias, m_ca_w_o, v_ln_mix_g, v_ln_mix_b, v_ln_ffn_g, v_ln_ffn_b, v_ffn_w_in, v_ffn_w_out, v_mla_w_down, v_mla_q_norm_g, v_mla_w_uq, v_mla_kv_norm_g, v_mla_w_ukv, v_mla_w_o, v_sb_w_qkv, v_sb_w_o, v_ca_w_qkv, v_ca_rel_bias, v_ca_w_o):
    w = dict(zip(WEIGHTS, (ln_mix_g, ln_mix_b, ln_ffn_g, ln_ffn_b, ffn_w_in, ffn_w_out, mla_w_down, mla_q_norm_g,
                           mla_w_uq, mla_kv_norm_g, mla_w_ukv, mla_w_o, sb_w_qkv, sb_w_o, ca_w_qkv, ca_rel_bias,
                           ca_w_o)))
    mom = dict(zip(WEIGHTS, (m_ln_mix_g, m_ln_mix_b, m_ln_ffn_g, m_ln_ffn_b, m_ffn_w_in, m_ffn_w_out, m_mla_w_down,
                             m_mla_q_norm_g, m_mla_w_uq, m_mla_kv_norm_g, m_mla_w_ukv, m_mla_w_o, m_sb_w_qkv,
                             m_sb_w_o, m_ca_w_qkv, m_ca_rel_bias, m_ca_w_o)))
    var = dict(zip(WEIGHTS, (v_ln_mix_g, v_ln_mix_b, v_ln_ffn_g, v_ln_ffn_b, v_ffn_w_in, v_ffn_w_out, v_mla_w_down,
                             v_mla_q_norm_g, v_mla_w_uq, v_mla_kv_norm_g, v_mla_w_ukv, v_mla_w_o, v_sb_w_qkv,
                             v_sb_w_o, v_ca_w_qkv, v_ca_rel_bias, v_ca_w_o)))
    S, D = x.shape[1], x.shape[2]
    xi, yi, ci = _place()
    core = ci.astype(jnp.int32).reshape(1)
    chip = (2 * xi + yi).astype(jnp.int32).reshape(1)
    me = 4 * xi + 2 * yi + ci
    tables = rope_tables(S)
    n_mla = mla_w_down.shape[0]
    lat = MLA_Q_LORA // N_DEV

    gains = jnp.pad(jnp.stack([mla_q_norm_g.reshape(-1), mla_kv_norm_g.reshape(-1)]), ((0, 6), (0, 128 - n_mla * lat)))
    gains = all_gather("ag_gains", [gains])[0]

    def full_gain(row, slot):
        return gains[:, row, slot * lat:(slot + 1) * lat].reshape(-1)

    first_matmul = {"mla": "mla_down", "sb": "sb_qkv", "ca": "ca_qkv"}

    def post_gather(i, part):
        kind = MIXERS[i % 3]
        specs = _part(i, part)
        shards = [PAD.get(name, lambda a: a)(w[name][idx]).astype(BF16) for _, name, _, idx in specs]
        if part == "mix":
            wants = ([MIXERS[(i - 1) % 3] + "_wo", "ffn_in"], "ffn_out") if i > 0 else ([None], None)
            return [(specs, gather_jobs(f"ag_mix{i}", shards, *wants))]
        w_in = (["ffn_out", first_matmul[kind]], kind + "_attn") if i > 0 else (["mla_down", "mla_uq"], "mla_ukv")
        return [(specs[:1], gather_jobs(f"ag_w_in{i}", shards[:1], *w_in)),
                (specs[1:], gather_jobs(f"ag_w_out{i}", shards[1:], [kind + "_attn"], kind + "_wo"))]

    def gathered(posted):
        return {key: Weight(how, g) for specs, future in posted for (key, _, how, _), g in zip(specs, future.get())}

    long_attention = "sb_attn_bwd"

    def post_scatter(i, part, g):
        kind = MIXERS[i % 3]
        specs = _part(i, part)
        n = len(specs)
        grads_of = lambda sp: [g[key] for key, _, _, _ in sp]
        if part == "ffn":
            w_in_rides = long_attention if MIXERS[(i - 1) % 3] == "sb" and i > 0 else kind + "_attn_bwd"
            return [(specs[:1], scatter_jobs(f"rs_w_in{i}", grads_of(specs[:1]), core, chip, kind + "_dwo",
                                             [([0], w_in_rides)])),
                    (specs[1:], scatter_jobs(f"rs_w_out{i}", grads_of(specs[1:]), core, chip, kind + "_dwo",
                                             [([0], kind + "_attn_bwd")]))]
        if i == 0:
            wants = (None, [(list(range(n)), None)])
        else:
            wants = ("ffn_da", [(list(range(n - 1)), "ffn_dwout"), ([n - 1], "ffn_dh")])
        return [(specs, scatter_jobs(f"rs_mix{i}", grads_of(specs), core, chip, *wants))]

    SCHED.pending.clear()
    bias = rel_bias_blocks("ca_bias", ca_rel_bias[0], _att_tiles("ca", S)[1])

    h, h16 = x[0], x[0].astype(BF16)
    h16t = transpose("x_t", h16)
    saved, layers = [], []
    mix_w, ffn_w = post_gather(0, "mix"), None
    for i in range(DEPTH):
        kind, slot = MIXERS[i % 3], i // 3
        lw = gathered(mix_w)
        if i == 0:
            ffn_w = post_gather(0, "ffn")
        this_ffn = ffn_w
        if i + 1 < DEPTH:
            mix_w, ffn_w = post_gather(i + 1, "mix"), post_gather(i + 1, "ffn")
        if kind == "mla":
            mix, s_mix = mla_forward(h16, lw, full_gain(0, slot), full_gain(1, slot), tables)
        else:
            mix, s_mix = qkv_forward(kind, h16, lw, bias if kind == "ca" else None)
        y, y16, y16t, xh1, rs1 = ln_fwd("ln_mix", h, mix, ln_mix_g[i], ln_mix_b[i])
        lw.update(gathered(this_ffn))
        f, s_mlp = mlp_forward(y16, lw)
        y2, y2_16, y2_16t, xh2, rs2 = ln_fwd("ln_ffn", y, f, ln_ffn_g[i], ln_ffn_b[i])
        saved.append((h16t, s_mix, xh1, rs1, y16t, s_mlp, xh2, rs2))
        layers.append(lw)
        h, h16, h16t = y2, y2_16, y2_16t
    sq, dy = loss_fwd_bwd("loss", h, loss_target[0])
    loss = 0.5 / D * lax.psum(sq[0, 0], ("x", "y", "c"))

    ga, gb = dy, None
    grads = {name: [None] * w[name].shape[0] for name in WEIGHTS}
    dbias = None
    scattered = []
    for i in reversed(range(DEPTH)):
        kind, slot = MIXERS[i % 3], i // 3
        lw = layers[i]
        h16_in, s_mix, xh1, rs1, y16, s_mlp, xh2, rs2 = saved[i]
        du, du16, grads["ln_ffn_g"][i], grads["ln_ffn_b"][i] = ln_bwd("ln_ffn_bwd", ga, gb, xh2, rs2, ln_ffn_g[i])
        dh_mlp, g_mlp = mlp_backward(du16, y16, s_mlp, lw)
        scattered += post_scatter(i, "ffn", g_mlp)
        du, du16, grads["ln_mix_g"][i], grads["ln_mix_b"][i] = ln_bwd("ln_mix_bwd", du, dh_mlp, xh1, rs1, ln_mix_g[i])
        if kind == "mla":
            dh_mix, g_mix, (dgq, dgkv) = mla_backward(du16, h16_in, s_mix, lw, full_gain(0, slot), full_gain(1, slot),
                                                      tables)
            grads["mla_q_norm_g"][slot], grads["mla_kv_norm_g"][slot] = dgq, dgkv
        else:
            dh_mix, g_mix, db = qkv_backward(kind, du16, h16_in, s_mix, lw, bias if kind == "ca" else None)
            dbias = db if kind == "ca" else dbias
        scattered += post_scatter(i, "mix", g_mix)
        ga, gb = du, dh_mix
    grad_x = axpy("grad_x", ga, gb)[None]
    SCHED.flush()
    for specs, future in scattered:
        for (_, name, _, idx), g in zip(specs, future.get()):
            grads[name][idx] = UNPAD.get(name, lambda a: a)(g)
    grads["ca_rel_bias"][0] = rel_bias_grad("ca_bias_grad", dbias)

    small = ("ln_mix_g", "ln_mix_b", "ln_ffn_g", "ln_ffn_b", "ca_rel_bias", "mla_q_norm_g", "mla_kv_norm_g")
    packed, offsets = _pack_rows([g for name in small for g in grads[name]])
    total = sum_devices("sum_small", all_gather("ag_small", [packed])[0]).reshape(-1)
    pos = 0
    for name in small:
        for idx, g in enumerate(grads[name]):
            full = total[offsets[pos]:offsets[pos + 1]]
            pos += 1
            if name in ("mla_q_norm_g", "mla_kv_norm_g"):
                full = lax.dynamic_slice(full, (me * lat,), (lat,))
            grads[name][idx] = full.reshape(w[name].shape[1:])

    g_out, d_out, m_out, v_out = [], [], [], []
    for name in WEIGHTS:
        g = jnp.stack(grads[name])
        delta, new_m, new_v = adamw("adamw_" + name, w[name], g, mom[name], var[name])
        g_out.append(g)
        d_out.append(delta)
        m_out.append(new_m)
        v_out.append(new_v)
    return (loss, grad_x, *g_out, *d_out, *m_out, *v_out)
```

```python
import functools

import numpy as np
import jax
import jax.numpy as jnp
from jax import lax
from jax.experimental import pallas as pl
from jax.experimental.pallas import tpu as pltpu

F32 = jnp.float32
BF16 = jnp.bfloat16
MESH = pl.DeviceIdType.MESH
N_DEV = 8

DEPTH = 4
CHUNK = 64
CHUNK_SHIFT = 6
HEADS = 16
HEAD_DIM = 128
MLA_Q_LORA = 512
MLA_KV_LORA = 512
MLA_NOPE = 128
MLA_ROPE = 64
ROPE_THETA = 10000.0
CA_LEFT_CHUNKS = 8
REL_CLIP_LEFT = 128
REL_TABLE = REL_CLIP_LEFT + CHUNK
LN_EPS = 1e-5
RMS_EPS = 1e-6
ALPHA = (2.0 * DEPTH) ** 0.25
NEG = -1e30
ADAM_LR = 0.001
ADAM_B1 = 0.9
ADAM_B2 = 0.999
ADAM_EPS = 1e-08
ADAM_WD = 0.01
ADAM_STEP = 10

V7X_VMEM_BYTES = 64 * 1024 * 1024
VMEM_LIMIT = V7X_VMEM_BYTES - 8 * 1024 * 1024
ATT_TQ = 512
ATT_TK = 256
ATT_G = 2


def _params(*sem):
    return pltpu.CompilerParams(dimension_semantics=sem if sem else None, vmem_limit_bytes=VMEM_LIMIT)


HBM = pl.BlockSpec(memory_space=pl.ANY)


class Job:
    def __init__(self, name, want, operands, out_shape, n_copies, copies, done, aliases=None):
        self.name, self.want, self.operands, self.out_shape = name, want, list(operands), list(out_shape)
        self.n_copies, self.copies, self.done, self.aliases = n_copies, copies, done, dict(aliases or {})

    def sems(self):
        return [pltpu.SemaphoreType.DMA((self.n_copies,)), pltpu.SemaphoreType.DMA((self.n_copies,))]


class Scheduler:
    def __init__(self):
        self.pending = []

    def post(self, job):
        self.pending.append(job)

    def take(self, name):
        mine = [job for job in self.pending if job.want is not None and job.want in name]
        self.pending = [job for job in self.pending if job not in mine]
        return mine

    def flush(self):
        while self.pending:
            job = self.pending.pop(0)
            n_in, n_out = len(job.operands), len(job.out_shape)

            def body(*refs, job=job, n_in=n_in, n_out=n_out):
                cps = job.copies(refs[:n_in], refs[n_in:n_in + n_out], refs[-2], refs[-1])
                for cp in cps:
                    cp.start()
                for cp in cps:
                    cp.wait()

            outs = pl.pallas_call(
                body, name=job.name, in_specs=[HBM] * n_in, out_specs=[HBM] * n_out, out_shape=job.out_shape,
                scratch_shapes=job.sems(), input_output_aliases=job.aliases)(*job.operands)
            job.done(list(outs))


SCHED = Scheduler()


def _call(body, operands, *, name, grid, in_specs, out_specs, out_shape, scratch_shapes=(), semantics):
    jobs = SCHED.take(name)
    if not jobs:
        return list(pl.pallas_call(
            body, name=name, grid=grid, in_specs=list(in_specs), out_specs=list(out_specs), out_shape=list(out_shape),
            scratch_shapes=list(scratch_shapes), compiler_params=_params(*semantics))(*operands))
    n_in, n_out, n_scr = len(operands), len(out_shape), len(scratch_shapes)
    j_in = np.cumsum([0] + [len(job.operands) for job in jobs])
    j_out = np.cumsum([0] + [len(job.out_shape) for job in jobs])
    a, b = n_in, n_in + int(j_in[-1])
    c, d = b + n_out, b + n_out + int(j_out[-1])

    def carrying(*refs):
        def copies():
            sems = refs[d + n_scr:]
            return [cp for k, job in enumerate(jobs)
                    for cp in job.copies(refs[a + j_in[k]:a + j_in[k + 1]], refs[c + j_out[k]:c + j_out[k + 1]],
                                         sems[2 * k], sems[2 * k + 1])]

        ids = [pl.program_id(k) for k in range(len(grid))]
        first = functools.reduce(jnp.logical_and, [i == 0 for i in ids])
        last = functools.reduce(jnp.logical_and, [i == g - 1 for i, g in zip(ids, grid)])

        @pl.when(first)
        def _():
            for cp in copies():
                cp.start()

        body(*refs[:a], *refs[b:c], *refs[d:d + n_scr])

        @pl.when(last)
        def _():
            for cp in copies():
                cp.wait()

    aliases = {n_in + int(j_in[k]) + i: n_out + int(j_out[k]) + o for k, job in enumerate(jobs)
               for i, o in job.aliases.items()}
    outs = pl.pallas_call(
        carrying, name=name + "_carry", grid=grid, in_specs=list(in_specs) + [HBM] * int(j_in[-1]),
        out_specs=list(out_specs) + [HBM] * int(j_out[-1]),
        out_shape=list(out_shape) + [s for job in jobs for s in job.out_shape],
        scratch_shapes=list(scratch_shapes) + [s for job in jobs for s in job.sems()],
        input_output_aliases=aliases,
        compiler_params=_params(*(["arbitrary"] * len(grid))))(*operands, *[o for job in jobs for o in job.operands])
    for k, job in enumerate(jobs):
        job.done(list(outs[n_out + int(j_out[k]):n_out + int(j_out[k + 1])]))
    return list(outs[:n_out])


def _matmul(name, a, b, *, contract, grid, a_spec, b_spec, o_specs, out_shape, acc_shape,
            epilogue=None, extra=(), extra_specs=()):
    nk = grid[2]
    n_extra = len(extra)
    n_out = len(out_shape)

    def finish(acc, e_refs, o_refs):
        outs = epilogue(acc, *[e[...] for e in e_refs]) if epilogue else (acc,)
        for o_ref, val in zip(o_refs, outs):
            o_ref[...] = val.astype(o_ref.dtype)

    def product(a_ref, b_ref):
        return lax.dot_general(a_ref[...], b_ref[...], (contract, ((), ())), preferred_element_type=F32)

    def body_single(*refs):
        finish(product(refs[0], refs[1]), refs[2:2 + n_extra], refs[2 + n_extra:2 + n_extra + n_out])

    def body(*refs):
        a_ref, b_ref = refs[0], refs[1]
        acc_ref = refs[-1]
        k = pl.program_id(2)

        @pl.when(k == 0)
        def _():
            acc_ref[...] = jnp.zeros_like(acc_ref)

        acc_ref[...] += product(a_ref, b_ref)

        @pl.when(k == nk - 1)
        def _():
            finish(acc_ref[...], refs[2:2 + n_extra], refs[2 + n_extra:2 + n_extra + n_out])

    return _call(
        body_single if nk == 1 else body, [a, b, *extra], name=name, grid=grid,
        in_specs=[a_spec, b_spec, *extra_specs], out_specs=o_specs, out_shape=out_shape,
        scratch_shapes=[] if nk == 1 else [pltpu.VMEM(acc_shape, F32)],
        semantics=("parallel", "parallel", "arbitrary"))


MATMUL_BLOCK_BYTES = 40 * 1024 * 1024
MAX_TK = 2048
MULTI_TK = 512


def _fit_tn(n, tm, tk, nk, out_bytes):
    cands = sorted({n} | {t for t in range(128, n, 128) if n % t == 0}, reverse=True)
    for tn in cands:
        need = 2 * 2 * (tm * tk + tk * tn) + 2 * tm * tn * out_bytes + (tm * tn * 4 if nk > 1 else 0) + tm * tn * 4
        if need <= MATMUL_BLOCK_BYTES:
            return tn
    return cands[-1]


def _itemsize(dtypes):
    return sum(jnp.dtype(d).itemsize for d in dtypes)


def _tile(n, pref):
    if n <= pref:
        return n
    t = pref
    while t >= 128:
        if n % t == 0 and t % 128 == 0:
            return t
        t -= 128
    return n


class Weight:
    def __init__(self, kind, arr):
        self.kind = kind
        self.arr = arr
        self.R, self.C = arr.shape[1], arr.shape[2]

    @property
    def two_d(self):
        return self.arr.reshape(N_DEV * self.R, self.C)


def mm_nn(name, a, w, out_dtypes, epilogue=None, transposed=()):
    M, K = a.shape
    tm = M
    tk = K if K <= MAX_TK else _tile(K, MULTI_TK)
    nk = K // tk
    if w.kind == "row":
        b = w.two_d
        N = w.C
        tn = _fit_tn(N, tm, tk, nk, _itemsize(out_dtypes))
        b_spec = pl.BlockSpec((tk, tn), lambda i, j, k: (k, j))
    else:
        b = w.arr
        N = N_DEV * w.C
        tn = _fit_tn(w.C, tm, tk, nk, _itemsize(out_dtypes))
        per = w.C // tn
        b_spec = pl.BlockSpec((None, tk, tn), lambda i, j, k: (j // per, k, j % per))
    grid = (M // tm, N // tn, nk)
    flip = [t < len(transposed) and transposed[t] for t in range(len(out_dtypes))]
    return _matmul(
        name, a, b, contract=((1,), (0,)), grid=grid,
        a_spec=pl.BlockSpec((tm, tk), lambda i, j, k: (i, k)), b_spec=b_spec,
        o_specs=[pl.BlockSpec((tn, tm), lambda i, j, k: (j, i)) if f else pl.BlockSpec((tm, tn), lambda i, j, k: (i, j))
                 for f in flip],
        out_shape=[jax.ShapeDtypeStruct((N, M) if f else (M, N), d) for f, d in zip(flip, out_dtypes)],
        acc_shape=(tm, tn), epilogue=epilogue)


def mm_nt(name, dy, w, out_dtype, epilogue=None, extra=None):
    M, N = dy.shape
    tm = M
    out_bytes = jnp.dtype(out_dtype).itemsize + (0 if extra is None else extra.dtype.itemsize)
    if w.kind == "row":
        b = w.two_d
        kin = N_DEV * w.R
        tk = N if N <= MAX_TK else _tile(N, MULTI_TK)
        tn = _fit_tn(kin, tm, tk, N // tk, out_bytes)
        b_spec = pl.BlockSpec((tn, tk), lambda i, j, k: (j, k))
    else:
        b = w.arr
        kin = w.R
        tk = _tile(w.C, MULTI_TK)
        per = w.C // tk
        tn = _fit_tn(kin, tm, tk, N // tk, out_bytes)
        b_spec = pl.BlockSpec((None, tn, tk), lambda i, j, k: (k // per, j, k % per))
    grid = (M // tm, kin // tn, N // tk)
    o_spec = pl.BlockSpec((tm, tn), lambda i, j, k: (i, j))
    return _matmul(
        name, dy, b, contract=((1,), (1,)), grid=grid,
        a_spec=pl.BlockSpec((tm, tk), lambda i, j, k: (i, k)), b_spec=b_spec, o_specs=[o_spec],
        out_shape=[jax.ShapeDtypeStruct((M, kin), out_dtype)], acc_shape=(tm, tn), epilogue=epilogue,
        extra=() if extra is None else (extra,), extra_specs=() if extra is None else (o_spec,))[0]


TRANSPOSE_TILE = 512


def transpose(name, x):
    S, n = x.shape
    ts, tn = _tile(S, TRANSPOSE_TILE), _tile(n, TRANSPOSE_TILE)

    def body(x_ref, o_ref):
        o_ref[...] = x_ref[...].T

    return pl.pallas_call(
        body, name=name, grid=(S // ts, n // tn), in_specs=[pl.BlockSpec((ts, tn), lambda i, j: (i, j))],
        out_specs=pl.BlockSpec((tn, ts), lambda i, j: (j, i)), out_shape=jax.ShapeDtypeStruct((n, S), x.dtype),
        compiler_params=_params("parallel", "parallel"),
    )(x)


def mm_tn(name, x, dy, kind, R, C, transposed=False):
    if not transposed:
        x = transpose(name + "_t", x)
    kin, S = x.shape
    N = dy.shape[1]
    tk = S if S <= MAX_TK else _tile(S, MULTI_TK)
    nk = S // tk
    if kind == "col":
        tm = kin
        tn = _fit_tn(C, tm, tk, nk, 2)
        per = C // tn
        grid = (1, N // tn, nk)
        o_spec = pl.BlockSpec((None, None, tm, tn), lambda i, j, k: ((j // per) % 2, (j // per) // 2, 0, j % per))
    else:
        tm = R
        tn = _fit_tn(N, tm, tk, nk, 2)
        grid = (N_DEV, N // tn, nk)
        o_spec = pl.BlockSpec((None, None, tm, tn), lambda i, j, k: (i % 2, i // 2, 0, j))
    return _matmul(
        name, x, dy, contract=((1,), (0,)), grid=grid,
        a_spec=pl.BlockSpec((tm, tk), lambda i, j, k: (i, k)),
        b_spec=pl.BlockSpec((tk, tn), lambda i, j, k: (k, j)), o_specs=[o_spec],
        out_shape=[jax.ShapeDtypeStruct((2, 4, R, C), BF16)], acc_shape=(tm, tn))[0]


def _relu2_epilogue(acc):
    r = jnp.maximum(acc, 0.0)
    z = (r * r).astype(BF16)
    return acc, z, z.T


def _mulrelu_epilogue(acc, a):
    return (acc * (2.0 * jnp.maximum(a, 0.0)),)


ROW_TILE = 256


def ln_fwd(name, h, m, g, b):
    S, D = h.shape
    ts = _tile(S, ROW_TILE)

    def body(h_ref, m_ref, g_ref, b_ref, y_ref, y16_ref, yt_ref, xh_ref, rs_ref):
        u = ALPHA * h_ref[...] + m_ref[...]
        mu = jnp.mean(u, axis=-1, keepdims=True)
        d = u - mu
        var = jnp.mean(d * d, axis=-1, keepdims=True)
        rstd = lax.rsqrt(var + LN_EPS)
        xh = d * rstd
        y = xh * g_ref[...] + b_ref[...]
        y16 = y.astype(BF16)
        y_ref[...] = y
        y16_ref[...] = y16
        yt_ref[...] = y16.T
        xh_ref[...] = xh
        rs_ref[...] = jnp.broadcast_to(rstd, rs_ref.shape)

    row = pl.BlockSpec((ts, D), lambda i: (i, 0))
    vec = pl.BlockSpec((1, D), lambda i: (0, 0))
    return pl.pallas_call(
        body, name=name, grid=(S // ts,), in_specs=[row, row, vec, vec],
        out_specs=[row, row, pl.BlockSpec((D, ts), lambda i: (0, i)), row, pl.BlockSpec((ts, 128), lambda i: (i, 0))],
        out_shape=[jax.ShapeDtypeStruct((S, D), F32), jax.ShapeDtypeStruct((S, D), BF16),
                   jax.ShapeDtypeStruct((D, S), BF16), jax.ShapeDtypeStruct((S, D), F32),
                   jax.ShapeDtypeStruct((S, 128), F32)],
        compiler_params=_params("parallel"),
    )(h, m, g.reshape(1, D), b.reshape(1, D))


def ln_bwd(name, ga, gb, xhat, rstd, g):
    S, D = xhat.shape
    ts = _tile(S, ROW_TILE)
    two = gb is not None

    def body(*refs):
        if two:
            ga_ref, gb_ref, xh_ref, rs_ref, g_ref, du_ref, du16_ref, dg_ref, db_ref = refs
            dy = ALPHA * ga_ref[...] + gb_ref[...]
        else:
            ga_ref, xh_ref, rs_ref, g_ref, du_ref, du16_ref, dg_ref, db_ref = refs
            dy = ga_ref[...]
        xh = xh_ref[...]

        @pl.when(pl.program_id(0) == 0)
        def _():
            dg_ref[...] = jnp.zeros_like(dg_ref)
            db_ref[...] = jnp.zeros_like(db_ref)

        dg_ref[...] += jnp.sum(dy * xh, axis=0, keepdims=True)
        db_ref[...] += jnp.sum(dy, axis=0, keepdims=True)
        dxh = dy * g_ref[...]
        m1 = jnp.mean(dxh, axis=-1, keepdims=True)
        m2 = jnp.mean(dxh * xh, axis=-1, keepdims=True)
        du = rs_ref[:, 0:1] * (dxh - m1 - xh * m2)
        du_ref[...] = du
        du16_ref[...] = du.astype(BF16)

    row = pl.BlockSpec((ts, D), lambda i: (i, 0))
    vec = pl.BlockSpec((1, D), lambda i: (0, 0))
    stat = pl.BlockSpec((ts, 128), lambda i: (i, 0))
    ins = [ga, gb, xhat, rstd, g.reshape(1, D)] if two else [ga, xhat, rstd, g.reshape(1, D)]
    in_specs = [row, row, row, stat, vec] if two else [row, row, stat, vec]
    return pl.pallas_call(
        body, name=name, grid=(S // ts,), in_specs=in_specs, out_specs=[row, row, vec, vec],
        out_shape=[jax.ShapeDtypeStruct((S, D), F32), jax.ShapeDtypeStruct((S, D), BF16),
                   jax.ShapeDtypeStruct((1, D), F32), jax.ShapeDtypeStruct((1, D), F32)],
        compiler_params=_params("arbitrary"),
    )(*ins)


def loss_fwd_bwd(name, y, target):
    S, D = y.shape
    ts = _tile(S, ROW_TILE)

    def body(y_ref, t_ref, l_ref, dy_ref):
        @pl.when(pl.program_id(0) == 0)
        def _():
            l_ref[...] = jnp.zeros_like(l_ref)

        e = y_ref[...] - t_ref[...]
        l_ref[...] += jnp.sum(e * e)
        dy_ref[...] = e * (1.0 / D)

    row = pl.BlockSpec((ts, D), lambda i: (i, 0))
    return pl.pallas_call(
        body, name=name, grid=(S // ts,), in_specs=[row, row],
        out_specs=[pl.BlockSpec((1, 128), lambda i: (0, 0)), row],
        out_shape=[jax.ShapeDtypeStruct((1, 128), F32), jax.ShapeDtypeStruct((S, D), F32)],
        compiler_params=_params("arbitrary"),
    )(y, target)


def axpy(name, ga, gb):
    S, D = ga.shape
    ts = _tile(S, ROW_TILE)

    def body(a_ref, b_ref, o_ref):
        o_ref[...] = ALPHA * a_ref[...] + b_ref[...]

    row = pl.BlockSpec((ts, D), lambda i: (i, 0))
    return pl.pallas_call(body, name=name, grid=(S // ts,), in_specs=[row, row], out_specs=row,
                          out_shape=jax.ShapeDtypeStruct((S, D), F32), compiler_params=_params("parallel"))(ga, gb)


def rms_fwd(name, down, gq, gkv):
    S = down.shape[0]
    ts = _tile(S, ROW_TILE)
    L = MLA_Q_LORA

    def body(d_ref, gq_ref, gkv_ref, q_ref, kv_ref):
        for lo, g_ref, o_ref in ((0, gq_ref, q_ref), (L, gkv_ref, kv_ref)):
            x = d_ref[:, lo:lo + L]
            r = lax.rsqrt(jnp.mean(x * x, axis=-1, keepdims=True) + RMS_EPS)
            o_ref[...] = (x * r * g_ref[...]).astype(BF16)

    vec = pl.BlockSpec((1, L), lambda i: (0, 0))
    out = pl.BlockSpec((ts, L), lambda i: (i, 0))
    return pl.pallas_call(
        body, name=name, grid=(S // ts,), in_specs=[pl.BlockSpec((ts, down.shape[1]), lambda i: (i, 0)), vec, vec],
        out_specs=[out, out], out_shape=[jax.ShapeDtypeStruct((S, L), BF16)] * 2, compiler_params=_params("parallel"),
    )(down, gq.reshape(1, L), gkv.reshape(1, L))


def rms_bwd(name, down, dq, dkv, dkr, gq, gkv):
    S, W = down.shape
    ts = _tile(S, ROW_TILE)
    L = MLA_Q_LORA

    def body(d_ref, dq_ref, dkv_ref, dkr_ref, gq_ref, gkv_ref, o_ref, dgq_ref, dgkv_ref):
        @pl.when(pl.program_id(0) == 0)
        def _():
            dgq_ref[...] = jnp.zeros_like(dgq_ref)
            dgkv_ref[...] = jnp.zeros_like(dgkv_ref)

        for lo, dy_ref, g_ref, dg_ref in ((0, dq_ref, gq_ref, dgq_ref), (L, dkv_ref, gkv_ref, dgkv_ref)):
            x = d_ref[:, lo:lo + L]
            dy = dy_ref[...]
            r = lax.rsqrt(jnp.mean(x * x, axis=-1, keepdims=True) + RMS_EPS)
            dg_ref[...] += jnp.sum(dy * x * r, axis=0, keepdims=True)
            dyg = dy * g_ref[...]
            dx = r * dyg - x * (r * r * r) * jnp.mean(dyg * x, axis=-1, keepdims=True)
            o_ref[:, lo:lo + L] = dx.astype(BF16)
        o_ref[:, 2 * L:] = dkr_ref[...].astype(BF16)

    vec = pl.BlockSpec((1, L), lambda i: (0, 0))
    lat = pl.BlockSpec((ts, L), lambda i: (i, 0))
    full = pl.BlockSpec((ts, W), lambda i: (i, 0))
    return pl.pallas_call(
        body, name=name, grid=(S // ts,),
        in_specs=[full, lat, lat, pl.BlockSpec((ts, 128), lambda i: (i, 0)), vec, vec],
        out_specs=[full, vec, vec],
        out_shape=[jax.ShapeDtypeStruct((S, W), BF16), jax.ShapeDtypeStruct((1, L), F32), jax.ShapeDtypeStruct((1, L), F32)],
        compiler_params=_params("arbitrary"),
    )(down, dq, dkv, dkr, gq.reshape(1, L), gkv.reshape(1, L))


def rope_tables(S):
    half = MLA_ROPE // 2
    inv = (np.float32(ROPE_THETA) ** (-np.arange(half, dtype=np.float32) / np.float32(half))).astype(np.float32)
    ang = np.arange(S, dtype=np.float32)[:, None] * inv[None, :]
    cos, sin = np.cos(ang).astype(np.float32), np.sin(ang).astype(np.float32)
    z = np.zeros_like(cos)
    return (jnp.asarray(np.concatenate([cos, z, cos, z], 1)), jnp.asarray(np.concatenate([-sin, z, sin, z], 1)))


def _rot(x, cos, sin):
    return x * cos + pltpu.roll(x, 64, 1) * sin


def mla_prep_fwd(name, q, kv, down, cos, sin):
    S = q.shape[0]
    ts = _tile(S, ROW_TILE)

    def body(q_ref, kv_ref, kr_ref, c_ref, s_ref, qo_ref, ko_ref):
        c, s = c_ref[...], s_ref[...]
        key = _rot(kr_ref[...], c, s).astype(BF16)
        for h in range(HEADS):
            lo = 256 * h
            qo_ref[:, lo:lo + 128] = q_ref[:, lo:lo + 128].astype(BF16)
            qo_ref[:, lo + 128:lo + 256] = _rot(q_ref[:, lo + 128:lo + 256], c, s).astype(BF16)
            ko_ref[:, lo:lo + 128] = kv_ref[:, lo:lo + 128]
            ko_ref[:, lo + 128:lo + 256] = key

    heads = pl.BlockSpec((ts, HEADS * 256), lambda i: (i, 0))
    tab = pl.BlockSpec((ts, 128), lambda i: (i, 0))
    return pl.pallas_call(
        body, name=name, grid=(S // ts,),
        in_specs=[heads, heads, pl.BlockSpec((ts, 128), lambda i: (i, 2 * MLA_Q_LORA // 128)), tab, tab],
        out_specs=[heads, heads], out_shape=[jax.ShapeDtypeStruct(q.shape, BF16)] * 2,
        compiler_params=_params("parallel"),
    )(q, kv, down, cos, sin)


def mla_prep_bwd(name, dq, dk, dv, cos, sin):
    S = dq.shape[0]
    ts = _tile(S, ROW_TILE)

    def body(dq_ref, dk_ref, dv_ref, c_ref, s_ref, qo_ref, kvo_ref, kr_ref):
        c, s = c_ref[...], -s_ref[...]
        key = jnp.zeros((ts, 128), F32)
        for h in range(HEADS):
            lo = 256 * h
            qo_ref[:, lo:lo + 128] = dq_ref[:, lo:lo + 128].astype(BF16)
            qo_ref[:, lo + 128:lo + 256] = _rot(dq_ref[:, lo + 128:lo + 256], c, s).astype(BF16)
            kvo_ref[:, lo:lo + 128] = dk_ref[:, lo:lo + 128].astype(BF16)
            kvo_ref[:, lo + 128:lo + 256] = dv_ref[:, 128 * h:128 * h + 128].astype(BF16)
            key = key + dk_ref[:, lo + 128:lo + 256]
        kr_ref[...] = _rot(key, c, s)

    heads = pl.BlockSpec((ts, HEADS * 256), lambda i: (i, 0))
    tab = pl.BlockSpec((ts, 128), lambda i: (i, 0))
    return pl.pallas_call(
        body, name=name, grid=(S // ts,),
        in_specs=[heads, heads, pl.BlockSpec((ts, HEADS * 128), lambda i: (i, 0)), tab, tab],
        out_specs=[heads, heads, tab],
        out_shape=[jax.ShapeDtypeStruct(dq.shape, BF16), jax.ShapeDtypeStruct(dq.shape, BF16),
                   jax.ShapeDtypeStruct((S, 128), F32)],
        compiler_params=_params("parallel"),
    )(dq, dk, dv, cos, sin)


def _dot_nt(a, b):
    return lax.dot_general(a, b, (((1,), (1,)), ((), ())), preferred_element_type=F32)


def _dot_tn(a, b):
    return lax.dot_general(a, b, (((0,), (0,)), ((), ())), preferred_element_type=F32)


def _dot(a, b):
    return jnp.dot(a, b, preferred_element_type=F32)


def _positions(i, j, TQ, TK):
    row = i * TQ + lax.broadcasted_iota(jnp.int32, (TQ, TK), 0)
    col = j * TK + lax.broadcasted_iota(jnp.int32, (TQ, TK), 1)
    return row, col


def _softmax_mask(mode, row, col):
    rc, cc = row >> CHUNK_SHIFT, col >> CHUNK_SHIFT
    if mode == "mla":
        return cc <= rc
    return (cc <= rc) & (cc >= rc - CA_LEFT_CHUNKS)


def _key_blocks(mode, i, TQ, TK):
    per = TQ // TK
    if mode == "ca":
        lo = jnp.maximum(i - (CA_LEFT_CHUNKS * CHUNK) // TK, 0)
        return lo, 0, i - lo + 1
    return 0, i * per, per


class HeadCols:
    def __init__(self, arr, width, index, off=0, w=None):
        self.arr, self.width, self.index, self.off = arr, width, index, off
        self.w = width if w is None else w

    def rows(self, T):
        return pl.BlockSpec((T, ATT_G * self.width), lambda p, i: (i, self.index(p)))

    def full(self, S):
        return pl.BlockSpec((S, ATT_G * self.width), lambda p, i: (0, self.index(p)))

    def lanes(self, g):
        lo = g * self.width + self.off
        return slice(lo, lo + self.w)


def _att_tiles(mode, S):
    tk = min(ATT_TK, S)
    return (tk if mode == "ca" else min(ATT_TQ, S)), tk


def _walk(lo, n, per, step, carry, descending=False):
    tail = [lo + n + d for d in range(per)]
    if descending:
        for j in reversed(tail):
            carry = step(j, carry, True)
        return lax.fori_loop(0, n, lambda t, c: step(lo + n - 1 - t, c, False), carry)
    carry = lax.fori_loop(0, n, lambda t, c: step(lo + t, c, False), carry)
    for j in tail:
        carry = step(j, carry, True)
    return carry


def softmax_attn_fwd(name, mode, q, k, v, scale, bias=None):
    S = q.arr.shape[0]
    TQ, TK = _att_tiles(mode, S)
    G, dv = ATT_G, v.w

    def body(*refs):
        if bias is not None:
            q_ref, k_ref, v_ref, b_ref, o_ref, lse_ref = refs
        else:
            q_ref, k_ref, v_ref, o_ref, lse_ref = refs
        i = pl.program_id(1)
        qs = [q_ref[:, q.lanes(g)] for g in range(G)]

        def block(g, j, carry, mask, ks):
            m, l, acc = carry
            s = _dot_nt(qs[g], k_ref[ks, k.lanes(g)]) * scale
            if bias is not None:
                s = s + b_ref[g, jnp.minimum(i - j, 2)]
            if mask is not None:
                s = jnp.where(mask, s, NEG)
            m_new = jnp.maximum(m, jnp.max(s, axis=-1, keepdims=True))
            a = jnp.exp(m - m_new)
            p = jnp.exp(s - m_new)
            if mask is not None:
                p = jnp.where(mask, p, 0.0)
            l = a * l + jnp.sum(p, axis=-1, keepdims=True)
            acc = a * acc + _dot(p.astype(BF16), v_ref[ks, v.lanes(g)])
            return m_new, l, acc

        def step(j, carry, masked):
            ks = pl.ds(pl.multiple_of(j * TK, TK), TK)
            mask = _softmax_mask(mode, *_positions(i, j, TQ, TK)) if masked or mode == "ca" else None
            return tuple(block(g, j, carry[g], mask, ks) for g in range(G))

        init = (jnp.full((TQ, 1), NEG, F32), jnp.zeros((TQ, 1), F32), jnp.zeros((TQ, dv), F32))
        lo, n, per = _key_blocks(mode, i, TQ, TK)
        if mode == "ca":
            out = lax.fori_loop(lo, lo + per, lambda j, c: step(j, c, True), (init,) * G)
        else:
            out = _walk(lo, n, per, step, (init,) * G)
        for g, (m, l, acc) in enumerate(out):
            o_ref[:, g * dv:(g + 1) * dv] = (acc / l).astype(BF16)
            lse_ref[:, g * 128:(g + 1) * 128] = jnp.broadcast_to(m + jnp.log(l), (TQ, 128))

    in_specs = [q.rows(TQ), k.full(S), v.full(S)]
    ins = [q.arr, k.arr, v.arr]
    if bias is not None:
        in_specs.append(pl.BlockSpec((G, 3, TK, TK), lambda p, i: (p, 0, 0, 0)))
        ins.append(bias)
    return _call(
        body, ins, name=name, grid=(HEADS // G, S // TQ), in_specs=in_specs,
        out_specs=[pl.BlockSpec((TQ, G * dv), lambda p, i: (i, p)), pl.BlockSpec((TQ, G * 128), lambda p, i: (i, p))],
        out_shape=[jax.ShapeDtypeStruct((S, HEADS * dv), BF16), jax.ShapeDtypeStruct((S, HEADS * 128), F32)],
        semantics=("parallel", "parallel"))


def softmax_attn_bwd(name, mode, q, k, v, o, do, lse, scale, bias=None):
    S = q.arr.shape[0]
    TQ, TK = _att_tiles(mode, S)
    G, dqk, dv = ATT_G, q.w, v.w

    def body(*refs):
        if bias is not None:
            q_ref, k_ref, v_ref, o_ref, do_ref, lse_ref, b_ref, dq_ref, dk_ref, dv_ref, db_ref = refs
        else:
            q_ref, k_ref, v_ref, o_ref, do_ref, lse_ref, dq_ref, dk_ref, dv_ref = refs
        i = pl.program_id(1)

        @pl.when(i == 0)
        def _():
            dk_ref[...] = jnp.zeros_like(dk_ref)
            dv_ref[...] = jnp.zeros_like(dv_ref)
            if bias is not None:
                db_ref[...] = jnp.zeros_like(db_ref)

        qs = [q_ref[:, q.lanes(g)] for g in range(G)]
        dos = [do_ref[:, do.lanes(g)] for g in range(G)]
        lses = [lse_ref[:, g * 128:g * 128 + 1] for g in range(G)]
        deltas = [jnp.sum(dos[g].astype(F32) * o_ref[:, o.lanes(g)].astype(F32), axis=-1, keepdims=True)
                  for g in range(G)]

        def block(g, j, dq, mask, ks):
            kb, vb = k_ref[ks, k.lanes(g)], v_ref[ks, v.lanes(g)]
            s = _dot_nt(qs[g], kb) * scale
            if bias is not None:
                slot = jnp.minimum(i - j, 2)
                s = s + b_ref[g, slot]
            p = jnp.exp(s - lses[g])
            if mask is not None:
                p = jnp.where(mask, p, 0.0)
            ds = p * (_dot_nt(dos[g], vb) - deltas[g])
            if bias is not None:
                db_ref[g, slot] += ds
            dsb = (ds * scale).astype(BF16)
            dk_ref[ks, g * dqk:(g + 1) * dqk] += _dot_tn(dsb, qs[g])
            dv_ref[ks, g * dv:(g + 1) * dv] += _dot_tn(p.astype(BF16), dos[g])
            return dq + _dot(dsb, kb)

        def step(j, carry, masked):
            ks = pl.ds(pl.multiple_of(j * TK, TK), TK)
            mask = _softmax_mask(mode, *_positions(i, j, TQ, TK)) if masked or mode == "ca" else None
            return tuple(block(g, j, carry[g], mask, ks) for g in range(G))

        init = (jnp.zeros((TQ, dqk), F32),) * G
        lo, n, per = _key_blocks(mode, i, TQ, TK)
        if mode == "ca":
            out = lax.fori_loop(lo, lo + per, lambda j, c: step(j, c, True), init)
        else:
            out = _walk(lo, n, per, step, init)
        for g in range(G):
            dq_ref[:, g * dqk:(g + 1) * dqk] = out[g]

    in_specs = [q.rows(TQ), k.full(S), v.full(S), o.rows(TQ), do.rows(TQ),
                pl.BlockSpec((TQ, G * 128), lambda p, i: (i, p))]
    ins = [q.arr, k.arr, v.arr, o.arr, do.arr, lse]
    out_specs = [pl.BlockSpec((TQ, G * dqk), lambda p, i: (i, p)), pl.BlockSpec((S, G * dqk), lambda p, i: (0, p)),
                 pl.BlockSpec((S, G * dv), lambda p, i: (0, p))]
    out_shape = [jax.ShapeDtypeStruct((S, HEADS * dqk), F32), jax.ShapeDtypeStruct((S, HEADS * dqk), F32),
                 jax.ShapeDtypeStruct((S, HEADS * dv), F32)]
    if bias is not None:
        bspec = pl.BlockSpec((G, 3, TK, TK), lambda p, i: (p, 0, 0, 0))
        in_specs.append(bspec)
        ins.append(bias)
        out_specs.append(bspec)
        out_shape.append(jax.ShapeDtypeStruct(bias.shape, F32))
    return _call(body, ins, name=name, grid=(HEADS // G, S // TQ), in_specs=in_specs, out_specs=out_specs,
                 out_shape=out_shape, semantics=("parallel", "arbitrary"))


def _split2(x):
    hi = x.astype(BF16)
    return hi, (x - hi.astype(F32)).astype(BF16)


def _split3(x):
    hi = x.astype(BF16)
    r = x - hi.astype(F32)
    mid = r.astype(BF16)
    return hi, mid, (r - mid.astype(F32)).astype(BF16)


def _stick_block(qb, kb, strict, scale):
    z = _dot_nt(qb, kb) * scale
    sp = jnp.log(1.0 + jnp.exp(-jnp.abs(z)))
    lb = jnp.minimum(z, 0.0) - sp
    l1m = jnp.minimum(-z, 0.0) - sp
    if strict is not None:
        l1m = jnp.where(strict, l1m, 0.0)
    return z, lb, l1m


def _strict_mask(i, j, TQ, TK):
    row, col = _positions(i, j, TQ, TK)
    return col < row


def _tri(T, inclusive):
    r = lax.broadcasted_iota(jnp.int32, (T, T), 0)
    c = lax.broadcasted_iota(jnp.int32, (T, T), 1)
    return ((r >= c) if inclusive else (r > c)).astype(BF16)


def _tri_prefix(T, inclusive):
    r = lax.broadcasted_iota(jnp.int32, (T, T), 0)
    c = lax.broadcasted_iota(jnp.int32, (T, T), 1)
    return ((r <= c) if inclusive else (r < c)).astype(BF16)


def _suffix(parts, tri):
    out = _dot(parts[0], tri)
    for p in parts[1:]:
        out = out + _dot(p, tri)
    return out


def stick_attn_fwd(name, q, k, v, scale):
    S = q.arr.shape[0]
    TQ, TK = _att_tiles("sb", S)
    G, dv = ATT_G, v.w

    def body(q_ref, k_ref, v_ref, o_ref, tot_ref):
        i = pl.program_id(1)
        qs = [q_ref[:, q.lanes(g)] for g in range(G)]
        tri = _tri(TK, False)

        def block(g, carry, strict, ks):
            right, acc = carry
            z, lb, l1m = _stick_block(qs[g], k_ref[ks, k.lanes(g)], strict, scale)
            a = jnp.exp(lb + _suffix(_split2(l1m), tri) + right)
            if strict is not None:
                a = jnp.where(strict, a, 0.0)
            acc = acc + _dot(a.astype(BF16), v_ref[ks, v.lanes(g)])
            return right + jnp.sum(l1m, axis=-1, keepdims=True), acc

        def step(j, carry, masked):
            ks = pl.ds(pl.multiple_of(j * TK, TK), TK)
            strict = _strict_mask(i, j, TQ, TK) if masked else None
            return tuple(block(g, carry[g], strict, ks) for g in range(G))

        init = (jnp.zeros((TQ, 1), F32), jnp.zeros((TQ, dv), F32))
        lo, n, per = _key_blocks("sb", i, TQ, TK)
        out = _walk(lo, n, per, step, (init,) * G, descending=True)
        for g in range(G):
            o_ref[:, g * dv:(g + 1) * dv] = out[g][1].astype(BF16)
            tot_ref[:, g * 128:(g + 1) * 128] = jnp.broadcast_to(out[g][0], (TQ, 128))

    return _call(
        body, [q.arr, k.arr, v.arr], name=name, grid=(HEADS // G, S // TQ),
        in_specs=[q.rows(TQ), k.full(S), v.full(S)],
        out_specs=[pl.BlockSpec((TQ, G * dv), lambda p, i: (i, p)), pl.BlockSpec((TQ, G * 128), lambda p, i: (i, p))],
        out_shape=[jax.ShapeDtypeStruct((S, HEADS * dv), BF16), jax.ShapeDtypeStruct((S, HEADS * 128), F32)],
        semantics=("parallel", "parallel"))


def stick_attn_bwd(name, q, k, v, do, total, scale):
    S = q.arr.shape[0]
    TQ, TK = _att_tiles("sb", S)
    G, dqk, dv = ATT_G, q.w, v.w

    def body(q_ref, k_ref, v_ref, do_ref, tot_ref, dq_ref, dk_ref, dv_ref):
        i = pl.program_id(1)

        @pl.when(i == 0)
        def _():
            dk_ref[...] = jnp.zeros_like(dk_ref)
            dv_ref[...] = jnp.zeros_like(dv_ref)

        qs = [q_ref[:, q.lanes(g)] for g in range(G)]
        dos = [do_ref[:, do.lanes(g)] for g in range(G)]
        tots = [tot_ref[:, g * 128:g * 128 + 1] for g in range(G)]
        upto = _tri_prefix(TK, True)
        before = _tri_prefix(TK, False)

        def step(j, carry, masked):
            ks = pl.ds(pl.multiple_of(j * TK, TK), TK)
            strict = _strict_mask(i, j, TQ, TK) if masked else None
            out = []
            for g in range(G):
                left, gleft, dq = carry[g]
                kb = k_ref[ks, k.lanes(g)]
                z, lb, l1m = _stick_block(qs[g], kb, strict, scale)
                a = jnp.exp(lb + (tots[g] - (left + _suffix(_split3(l1m), upto))))
                if strict is not None:
                    a = jnp.where(strict, a, 0.0)
                gg = a * _dot_nt(dos[g], v_ref[ks, v.lanes(g)])
                c = gleft + _suffix(_split3(gg), before)
                sig = 1.0 / (1.0 + jnp.exp(-z))
                dz = gg * (1.0 - sig) - c * sig
                if strict is not None:
                    dz = jnp.where(strict, dz, 0.0)
                dzb = (dz * scale).astype(BF16)
                dk_ref[ks, g * dqk:(g + 1) * dqk] += _dot_tn(dzb, qs[g])
                dv_ref[ks, g * dv:(g + 1) * dv] += _dot_tn(a.astype(BF16), dos[g])
                out.append((left + jnp.sum(l1m, axis=-1, keepdims=True),
                            gleft + jnp.sum(gg, axis=-1, keepdims=True), dq + _dot(dzb, kb)))
            return tuple(out)

        zero = jnp.zeros((TQ, 1), F32)
        lo, n, per = _key_blocks("sb", i, TQ, TK)
        out = _walk(lo, n, per, step, ((zero, zero, jnp.zeros((TQ, dqk), F32)),) * G)
        for g in range(G):
            dq_ref[:, g * dqk:(g + 1) * dqk] = out[g][2]

    return _call(
        body, [q.arr, k.arr, v.arr, do.arr, total], name=name, grid=(HEADS // G, S // TQ),
        in_specs=[q.rows(TQ), k.full(S), v.full(S), do.rows(TQ), pl.BlockSpec((TQ, G * 128), lambda p, i: (i, p))],
        out_specs=[pl.BlockSpec((TQ, G * dqk), lambda p, i: (i, p)), pl.BlockSpec((S, G * dqk), lambda p, i: (0, p)),
                   pl.BlockSpec((S, G * dv), lambda p, i: (0, p))],
        out_shape=[jax.ShapeDtypeStruct((S, HEADS * dqk), F32), jax.ShapeDtypeStruct((S, HEADS * dqk), F32),
                   jax.ShapeDtypeStruct((S, HEADS * dv), F32)],
        semantics=("parallel", "arbitrary"))


def _skew(x, back):
    T = x.shape[0]
    rows = lax.broadcasted_iota(jnp.int32, (T, T), 0)
    for b in range(T.bit_length() - 1):
        shift = T - (1 << b) if back else 1 << b
        x = jnp.where(((rows >> b) & 1) == 1, pltpu.roll(x, shift, 1), x)
    return x


def _table_rows(table):
    t = jnp.pad(table.T, ((0, 0), (0, 2 * REL_CLIP_LEFT - REL_TABLE)))
    return t.reshape(table.shape[1], 2, REL_CLIP_LEFT)


def rel_bias_blocks(name, table, T):
    assert T == 2 * REL_CLIP_LEFT, "the base rows below are laid out for blocks of 256"

    def body(t_ref, o_ref):
        low, high = t_ref[0:1, :], t_ref[1:2, :]
        first = jnp.broadcast_to(t_ref[0:1, 0:1], (1, REL_CLIP_LEFT))
        qq = lax.broadcasted_iota(jnp.int32, (T, T), 0)
        kk = lax.broadcasted_iota(jnp.int32, (T, T), 1)

        def rolled(row):
            return _skew(jnp.broadcast_to(row, (T, T)), False)

        far = jnp.concatenate([first, low], axis=1)
        near = jnp.concatenate([high, jnp.zeros_like(high)], axis=1)
        o_ref[0] = jnp.where(kk >= qq, rolled(near), rolled(far))
        o_ref[1] = jnp.where(kk >= qq, rolled(far), jnp.broadcast_to(t_ref[0:1, 0:1], (T, T)))
        o_ref[2] = jnp.broadcast_to(t_ref[0:1, 0:1], (T, T))

    return pl.pallas_call(
        body, name=name, grid=(HEADS,), in_specs=[pl.BlockSpec((None, 2, REL_CLIP_LEFT), lambda h: (h, 0, 0))],
        out_specs=pl.BlockSpec((None, 3, T, T), lambda h: (h, 0, 0, 0)),
        out_shape=jax.ShapeDtypeStruct((HEADS, 3, T, T), F32), compiler_params=_params("parallel"),
    )(_table_rows(table))


def rel_bias_grad(name, dbias):
    T = dbias.shape[-1]
    L = REL_CLIP_LEFT
    assert T == 2 * L

    def body(d_ref, o_ref):
        qq = lax.broadcasted_iota(jnp.int32, (T, T), 0)
        ll = lax.broadcasted_iota(jnp.int32, (T, T), 1)
        wrapped = ll + qq >= T

        def columns(d):
            x = _skew(d_ref[d], True)
            return (jnp.sum(jnp.where(wrapped, 0.0, x), axis=0, keepdims=True),
                    jnp.sum(jnp.where(wrapped, x, 0.0), axis=0, keepdims=True))

        pos0, neg0 = columns(0)
        pos1, neg1 = columns(1)
        clipped = (jnp.sum(neg0[:, :L]) + jnp.sum(pos1[:, :L]) + jnp.sum(neg1) + jnp.sum(d_ref[2]))
        lane = lax.broadcasted_iota(jnp.int32, (1, L), 1)
        low = neg0[:, L:] + pos1[:, L:]
        o_ref[...] = jnp.zeros_like(o_ref)
        o_ref[0:1, :] = jnp.where(lane == 0, low + clipped, low)
        o_ref[1:2, :] = pos0[:, :L]

    rows = pl.pallas_call(
        body, name=name, grid=(HEADS,), in_specs=[pl.BlockSpec((None, 3, T, T), lambda h: (h, 0, 0, 0))],
        out_specs=pl.BlockSpec((None, 8, L), lambda h: (h, 0, 0)), out_shape=jax.ShapeDtypeStruct((HEADS, 8, L), F32),
        compiler_params=_params("parallel"),
    )(dbias)
    return rows[:, :2, :].reshape(HEADS, 2 * L)[:, :REL_TABLE].T


def _place():
    return lax.axis_index("x"), lax.axis_index("y"), lax.axis_index("c")


def all_gather(name, shards):
    n = len(shards)

    def body(*refs):
        x_refs, out_refs = refs[:n], refs[n:2 * n]
        send_sems, recv_sems, local_sems = refs[2 * n:]
        x, y, c = _place()
        me, sibling = (x, y, c), (x, y, 1 - c)
        chips = [(1 - x, y), (x, 1 - y), (1 - x, 1 - y)]

        def block(t, dev):
            return out_refs[t].at[4 * dev[0] + 2 * dev[1] + dev[2]]

        def copy(t, k, dev, to, src=None):
            return pltpu.make_async_remote_copy(
                src_ref=block(t, dev) if src is None else src, dst_ref=block(t, dev),
                send_sem=send_sems.at[t, k], recv_sem=recv_sems.at[t, k], device_id=to, device_id_type=MESH)

        mine = [pltpu.make_async_copy(x_refs[t], block(t, me), local_sems.at[t]) for t in range(n)]
        for cp in mine:
            cp.start()
        first = []
        for t in range(n):
            first.append(copy(t, 0, me, sibling, src=x_refs[t]))
            first += [copy(t, 1 + j, me, (*chip, c), src=x_refs[t]) for j, chip in enumerate(chips)]
        for cp in first:
            cp.start()
        passed = []
        for j, chip in enumerate(chips):
            for t in range(n):
                copy(t, 1 + j, (*chip, c), me).wait_recv()
                cp = copy(t, 4 + j, (*chip, c), sibling)
                cp.start()
                passed.append(cp)
        for t in range(n):
            copy(t, 0, sibling, me).wait_recv()
            for j, chip in enumerate(chips):
                copy(t, 4 + j, (*chip, 1 - c), me).wait_recv()
        for cp in first + passed:
            cp.wait_send()
        for cp in mine:
            cp.wait()

    return pl.pallas_call(
        body, name=name, in_specs=[HBM] * n, out_specs=[HBM] * n,
        out_shape=[jax.ShapeDtypeStruct((N_DEV, *s.shape), s.dtype) for s in shards],
        scratch_shapes=[pltpu.SemaphoreType.DMA((n, 7)), pltpu.SemaphoreType.DMA((n, 7)), pltpu.SemaphoreType.DMA((n,))],
    )(*shards)


def _remote(src, dst, send_sems, recv_sems, k, to):
    return pltpu.make_async_remote_copy(src_ref=src, dst_ref=dst, send_sem=send_sems.at[k], recv_sem=recv_sems.at[k],
                                        device_id=to, device_id_type=MESH)


class Future:
    def __init__(self):
        self.value = None

    def get(self):
        if self.value is None:
            SCHED.flush()
        return self.value


def gather_jobs(name, shards, wants_chips, want_sibling):
    n, shares = len(shards), len(wants_chips)
    result = Future()
    lands = [jax.ShapeDtypeStruct((N_DEV, *s.shape), s.dtype) for s in shards]

    def to_chips(share):
        def copies(in_refs, out_refs, send_sems, recv_sems):
            x, y, c = _place()
            me = 4 * x + 2 * y + c
            cps = []
            for t in range(n):
                rows = shards[t].shape[0] // shares
                mine = pl.ds(share * rows, rows)
                src, dst = in_refs[t].at[mine], out_refs[t].at[me, mine]
                cps.append(pltpu.make_async_copy(src, dst, send_sems.at[4 * t]))
                for j, chip in enumerate([(1 - x, y), (x, 1 - y), (1 - x, 1 - y)]):
                    cps.append(_remote(src, dst, send_sems, recv_sems, 4 * t + 1 + j, (*chip, c)))
            return cps
        return copies

    def to_sibling(in_refs, out_refs, send_sems, recv_sems):
        x, y, c = _place()
        return [_remote(in_refs[t].at[2 * chip + c], out_refs[t].at[2 * chip + c], send_sems, recv_sems, 4 * t + chip,
                        (x, y, 1 - c)) for t in range(n) for chip in range(4)]

    def post(share, landed):
        if share == shares:
            SCHED.post(Job(name + "_sibling", want_sibling, landed, lands, 4 * n, to_sibling,
                           lambda final: setattr(result, "value", final), aliases={t: t for t in range(n)}))
        else:
            SCHED.post(Job(f"{name}_chips{share}", wants_chips[share], list(shards) + (landed or []), lands, 4 * n,
                           to_chips(share), lambda outs: post(share + 1, outs),
                           aliases={n + t: t for t in range(n)} if landed else None))

    post(0, None)
    return result


def scatter_jobs(name, grads, core, chip, want_sibling, wants_chips):
    n = len(grads)
    result = Future()
    sums = [None] * n

    def to_sibling(in_refs, out_refs, send_sems, recv_sems):
        x, y, c = _place()
        return [_remote(in_refs[t].at[1 - c], out_refs[t], send_sems, recv_sems, t, (x, y, 1 - c)) for t in range(n)]

    def after_sibling(received):
        parts = [add_sibling(f"{name}_add{t}", grads[t], received[t], core) for t in range(n)]
        for group, want in wants_chips:
            def to_chips(in_refs, out_refs, send_sems, recv_sems, m=len(group)):
                x, y, c = _place()
                return [_remote(in_refs[t].at[2 * cx + cy], out_refs[t].at[j], send_sems, recv_sems, 3 * t + j,
                                (cx, cy, c))
                        for t in range(m) for j, (cx, cy) in enumerate([(1 - x, y), (x, 1 - y), (1 - x, 1 - y)])]

            def after_chips(received, group=group):
                for t, r in zip(group, received):
                    sums[t] = sum_chips(f"{name}_sum{t}", parts[t], r, chip)
                if all(s is not None for s in sums):
                    result.value = sums

            mine = [parts[t] for t in group]
            SCHED.post(Job(f"{name}_chips{group[0]}", want, mine,
                           [jax.ShapeDtypeStruct((3, *p.shape[1:]), p.dtype) for p in mine], 3 * len(mine), to_chips,
                           after_chips))

    SCHED.post(Job(name + "_sibling", want_sibling, grads, [jax.ShapeDtypeStruct(g.shape[1:], g.dtype) for g in grads],
                   n, to_sibling, after_sibling))
    return result


def _as_rows(shape):
    return (int(np.prod(shape[:-1])), shape[-1])


ELEMENTWISE_BLOCK = 256 * 1024


def _row_tile(rows, cols):
    return _tile(rows, max(128, ELEMENTWISE_BLOCK // cols // 128 * 128))


def add_sibling(name, grad, recv, core):
    rows, cols = _as_rows(grad.shape[2:])
    tr = _row_tile(rows, cols)

    def body(c_ref, g_ref, r_ref, o_ref):
        o_ref[...] = (g_ref[...].astype(F32) + r_ref[...].astype(F32)).astype(BF16)

    blk = pl.BlockSpec((None, tr, cols), lambda k, i, c_ref: (k, i, 0))
    return pl.pallas_call(
        body, name=name,
        grid_spec=pltpu.PrefetchScalarGridSpec(
            num_scalar_prefetch=1, grid=(4, rows // tr),
            in_specs=[pl.BlockSpec((None, None, tr, cols), lambda k, i, c_ref: (c_ref[0], k, i, 0)), blk],
            out_specs=blk),
        out_shape=jax.ShapeDtypeStruct((4, rows, cols), BF16), compiler_params=_params("parallel", "parallel"),
    )(core, grad.reshape(2, 4, rows, cols), recv.reshape(4, rows, cols)).reshape(recv.shape)


def sum_chips(name, part, recv, chip):
    shape = part.shape[1:]
    rows, cols = _as_rows(shape)
    tr = _row_tile(rows, cols)

    def body(c_ref, p_ref, r_ref, o_ref):
        o_ref[...] = (p_ref[...].astype(F32) + r_ref[0].astype(F32) + r_ref[1].astype(F32) + r_ref[2].astype(F32))

    return pl.pallas_call(
        body, name=name,
        grid_spec=pltpu.PrefetchScalarGridSpec(
            num_scalar_prefetch=1, grid=(rows // tr,),
            in_specs=[pl.BlockSpec((None, tr, cols), lambda i, c_ref: (c_ref[0], i, 0)),
                      pl.BlockSpec((3, tr, cols), lambda i, c_ref: (0, i, 0))],
            out_specs=pl.BlockSpec((tr, cols), lambda i, c_ref: (i, 0))),
        out_shape=jax.ShapeDtypeStruct((rows, cols), F32), compiler_params=_params("parallel"),
    )(chip, part.reshape(4, rows, cols), recv.reshape(3, rows, cols)).reshape(shape)


def sum_devices(name, gathered):
    _, rows, cols = gathered.shape

    def body(g_ref, o_ref):
        acc = g_ref[0]
        for d in range(1, N_DEV):
            acc = acc + g_ref[d]
        o_ref[...] = acc

    return pl.pallas_call(body, name=name, out_shape=jax.ShapeDtypeStruct((rows, cols), F32))(gathered)


def adamw(name, w, g, m, v):
    shape = w.shape
    rows, cols = _as_rows(shape)
    tr = _row_tile(rows, cols) if rows % 8 == 0 else rows
    c1 = 1.0 / (1.0 - ADAM_B1 ** ADAM_STEP)
    c2 = 1.0 / (1.0 - ADAM_B2 ** ADAM_STEP)

    def body(w_ref, g_ref, m_ref, v_ref, d_ref, mo_ref, vo_ref):
        g_ = g_ref[...]
        m_ = ADAM_B1 * m_ref[...] + (1.0 - ADAM_B1) * g_
        v_ = ADAM_B2 * v_ref[...] + (1.0 - ADAM_B2) * (g_ * g_)
        d_ref[...] = -ADAM_LR * ((m_ * c1) / (jnp.sqrt(v_ * c2) + ADAM_EPS) + ADAM_WD * w_ref[...])
        mo_ref[...] = m_
        vo_ref[...] = v_

    blk = pl.BlockSpec((tr, cols), lambda i: (i, 0))
    outs = pl.pallas_call(
        body, name=name, grid=(rows // tr,), in_specs=[blk] * 4, out_specs=[blk] * 3,
        out_shape=[jax.ShapeDtypeStruct((rows, cols), F32)] * 3, compiler_params=_params("parallel"),
    )(*[a.reshape(rows, cols) for a in (w, g, m, v)])
    return [o.reshape(shape) for o in outs]


def _spread_rope(r):
    z = jnp.zeros_like(r[..., :32])
    return jnp.concatenate([r[..., :32], z, r[..., 32:], z], -1)


def _gather_rope(r):
    return jnp.concatenate([r[..., :32], r[..., 64:96]], -1)


def pad_w_uq(w):
    w = w.reshape(w.shape[0], -1, MLA_NOPE + MLA_ROPE)
    return jnp.concatenate([w[..., :MLA_NOPE], _spread_rope(w[..., MLA_NOPE:])], -1).reshape(w.shape[0], -1)


def unpad_w_uq(g):
    g = g.reshape(g.shape[0], -1, 2 * MLA_NOPE)
    return jnp.concatenate([g[..., :MLA_NOPE], _gather_rope(g[..., MLA_NOPE:])], -1).reshape(g.shape[0], -1)


def pad_w_down(w):
    lat = MLA_Q_LORA + MLA_KV_LORA
    return jnp.concatenate([w[:, :lat], _spread_rope(w[:, lat:])], -1)


def unpad_w_down(g):
    lat = MLA_Q_LORA + MLA_KV_LORA
    return jnp.concatenate([g[:, :lat], _gather_rope(g[:, lat:])], -1)


def _heads(arr, width, first=0, off=0, w=None):
    return HeadCols(arr, width, lambda p: first // ATT_G + p, off, w)


def mla_forward(h16, w, gq, gkv, tables):
    cos, sin = tables
    down = mm_nn("mla_down", h16, w["down"], [F32])[0]
    cq, ckv = rms_fwd("mla_rms", down, gq, gkv)
    q = mm_nn("mla_uq", cq, w["uq"], [F32])[0]
    kv = mm_nn("mla_ukv", ckv, w["ukv"], [BF16])[0]
    qr, kp = mla_prep_fwd("mla_prep", q, kv, down, cos, sin)
    scale = (MLA_NOPE + MLA_ROPE) ** -0.5
    o, lse = softmax_attn_fwd("mla_attn", "mla", _heads(qr, 256), _heads(kp, 256), _heads(kv, 256, off=128, w=128), scale)
    m = mm_nn("mla_wo", o, w["wo"], [F32])[0]
    return m, (down, cq, ckv, qr, kp, kv, o, lse)


def mla_backward(du16, h16t, saved, w, gq, gkv, tables, emit):
    cos, sin = tables
    down, cq, ckv, qr, kp, kv, o, lse = saved
    scale = (MLA_NOPE + MLA_ROPE) ** -0.5
    emit("wo", mm_tn("mla_dwo", o, du16, "row", w["wo"].R, w["wo"].C))
    do = mm_nt("mla_do", du16, w["wo"], BF16)
    dq, dk, dv = softmax_attn_bwd("mla_attn_bwd", "mla", _heads(qr, 256), _heads(kp, 256), _heads(kv, 256, off=128, w=128),
                                  _heads(o, 128), _heads(do, 128), lse, scale)
    dq16, dkv16, dkr = mla_prep_bwd("mla_prep_bwd", dq, dk, dv, cos, sin)
    emit("uq", mm_tn("mla_duq", cq, dq16, "col", w["uq"].R, w["uq"].C))
    dcq = mm_nt("mla_dcq", dq16, w["uq"], F32)
    emit("ukv", mm_tn("mla_dukv", ckv, dkv16, "col", w["ukv"].R, w["ukv"].C))
    dckv = mm_nt("mla_dckv", dkv16, w["ukv"], F32)
    ddown, dgq, dgkv = rms_bwd("mla_rms_bwd", down, dcq, dckv, dkr, gq, gkv)
    emit("down", mm_tn("mla_ddown", h16t, ddown, "row", w["down"].R, w["down"].C, transposed=True))
    dh = mm_nt("mla_dh", ddown, w["down"], F32)
    return dh, (dgq, dgkv)


def qkv_forward(kind, h16, w, bias=None):
    qkv = mm_nn(kind + "_qkv", h16, w["qkv"], [BF16])[0]
    q, k, v = _heads(qkv, 128), _heads(qkv, 128, HEADS), _heads(qkv, 128, 2 * HEADS)
    scale = HEAD_DIM ** -0.5
    if kind == "sb":
        o, lse = stick_attn_fwd("sb_attn", q, k, v, scale)
    else:
        o, lse = softmax_attn_fwd("ca_attn", "ca", q, k, v, scale, bias)
    m = mm_nn(kind + "_wo", o, w["wo"], [F32])[0]
    return m, (qkv, o, lse)


def qkv_backward(kind, du16, h16t, saved, w, bias=None):
    qkv, o, lse = saved
    q, k, v = _heads(qkv, 128), _heads(qkv, 128, HEADS), _heads(qkv, 128, 2 * HEADS)
    scale = HEAD_DIM ** -0.5
    g = {"wo": mm_tn(kind + "_dwo", o, du16, "row", w["wo"].R, w["wo"].C)}
    do = mm_nt(kind + "_do", du16, w["wo"], BF16)
    dbias = None
    if kind == "sb":
        dq, dk, dv = stick_attn_bwd("sb_attn_bwd", q, k, v, _heads(do, 128), lse, scale)
    else:
        dq, dk, dv, dbias = softmax_attn_bwd("ca_attn_bwd", "ca", q, k, v, _heads(o, 128), _heads(do, 128), lse,
                                             scale, bias)
    dqkv = jnp.concatenate([dq, dk, dv], axis=1).astype(BF16)
    g["qkv"] = mm_tn(kind + "_dqkv", h16t, dqkv, "col", w["qkv"].R, w["qkv"].C, transposed=True)
    dh = mm_nt(kind + "_dh", dqkv, w["qkv"], F32)
    return dh, g, dbias


def mlp_forward(h16, w):
    a, z, zt = mm_nn("ffn_in", h16, w["w_in"], [F32, BF16, BF16], epilogue=_relu2_epilogue,
                     transposed=(False, False, True))
    f = mm_nn("ffn_out", z, w["w_out"], [F32])[0]
    return f, (a, zt)


def mlp_backward(du16, h16t, saved, w):
    a, zt = saved
    da = mm_nt("ffn_da", du16, w["w_out"], BF16, epilogue=_mulrelu_epilogue, extra=a)
    g = {"w_out": mm_tn("ffn_dwout", zt, du16, "row", w["w_out"].R, w["w_out"].C, transposed=True)}
    dh = mm_nt("ffn_dh", da, w["w_in"], F32)
    g["w_in"] = mm_tn("ffn_dwin", h16t, da, "col", w["w_in"].R, w["w_in"].C, transposed=True)
    return dh, g


WEIGHTS = ("ln_mix_g", "ln_mix_b", "ln_ffn_g", "ln_ffn_b", "ffn_w_in", "ffn_w_out", "mla_w_down", "mla_q_norm_g",
           "mla_w_uq", "mla_kv_norm_g", "mla_w_ukv", "mla_w_o", "sb_w_qkv", "sb_w_o", "ca_w_qkv", "ca_rel_bias",
           "ca_w_o")
MIXERS = ("mla", "sb", "ca")
LAYER_WEIGHTS = {
    "mla": (("down", "mla_w_down", "row"), ("uq", "mla_w_uq", "col"), ("ukv", "mla_w_ukv", "col"),
            ("wo", "mla_w_o", "row")),
    "sb": (("qkv", "sb_w_qkv", "col"), ("wo", "sb_w_o", "row")),
    "ca": (("qkv", "ca_w_qkv", "col"), ("wo", "ca_w_o", "row")),
    "ffn": (("w_in", "ffn_w_in", "col"), ("w_out", "ffn_w_out", "row")),
}
PAD = {"mla_w_down": pad_w_down, "mla_w_uq": pad_w_uq}
UNPAD = {"mla_w_down": unpad_w_down, "mla_w_uq": unpad_w_uq}


def _pack_rows(vectors):
    flat = jnp.concatenate([v.reshape(-1) for v in vectors])
    n = flat.shape[0]
    rows = -(-n // 1024) * 8
    offsets = np.cumsum([0] + [int(np.prod(v.shape)) for v in vectors])
    return jnp.pad(flat, (0, rows * 128 - n)).reshape(rows, 128), offsets


def _part(i, part):
    group, idx = (MIXERS[i % 3], i // 3) if part == "mix" else ("ffn", i)
    return [(key, name, how, idx) for key, name, how in LAYER_WEIGHTS[group]]


def kernel(x, ln_mix_g, ln_mix_b, ln_ffn_g, ln_ffn_b, ffn_w_in, ffn_w_out, mla_w_down, mla_q_norm_g, mla_w_uq, mla_kv_norm_g, mla_w_ukv, mla_w_o, sb_w_qkv, sb_w_o, ca_w_qkv, ca_rel_bias, ca_w_o, loss_target, m_ln_mix_g, m_ln_mix_b, m_ln_ffn_g, m_ln_ffn_b, m_ffn_w_in, m_ffn_w_out, m_mla_w_down, m_mla_q_norm_g, m_mla_w_uq, m_mla_kv_norm_g, m_mla_w_ukv, m_mla_w_o, m_sb_w_qkv, m_sb_w_o, m_ca_w_qkv, m_ca_rel_bias, m_ca_w_o, v_ln_mix_g, v_ln_mix_b, v_ln_ffn_g, v_ln_ffn_b, v_ffn_w_in, v_ffn_w_out, v_mla_w_down, v_mla_q_norm_g, v_mla_w_uq, v_mla_kv_norm_g, v_mla_w_ukv, v_mla_w_o, v_sb_w_qkv, v_sb_w_o, v_ca_w_qkv, v_ca_rel_bias, v_ca_w_o):
    w = dict(zip(WEIGHTS, (ln_mix_g, ln_mix_b, ln_ffn_g, ln_ffn_b, ffn_w_in, ffn_w_out, mla_w_down, mla_q_norm_g,
                           mla_w_uq, mla_kv_norm_g, mla_w_ukv, mla_w_o, sb_w_qkv, sb_w_o, ca_w_qkv, ca_rel_bias,
                           ca_w_o)))
    mom = dict(zip(WEIGHTS, (m_ln_mix_g, m_ln_mix_b, m_ln_ffn_g, m_ln_ffn_b, m_ffn_w_in, m_ffn_w_out, m_mla_w_down,
                             m_mla_q_norm_g, m_mla_w_uq, m_mla_kv_norm_g, m_mla_w_ukv, m_mla_w_o, m_sb_w_qkv,
                             m_sb_w_o, m_ca_w_qkv, m_ca_rel_bias, m_ca_w_o)))
    var = dict(zip(WEIGHTS, (v_ln_mix_g, v_ln_mix_b, v_ln_ffn_g, v_ln_ffn_b, v_ffn_w_in, v_ffn_w_out, v_mla_w_down,
                             v_mla_q_norm_g, v_mla_w_uq, v_mla_kv_norm_g, v_mla_w_ukv, v_mla_w_o, v_sb_w_qkv,
                             v_sb_w_o, v_ca_w_qkv, v_ca_rel_bias, v_ca_w_o)))
    S, D = x.shape[1], x.shape[2]
    xi, yi, ci = _place()
    core = ci.astype(jnp.int32).reshape(1)
    chip = (2 * xi + yi).astype(jnp.int32).reshape(1)
    me = 4 * xi + 2 * yi + ci
    tables = rope_tables(S)
    n_mla = mla_w_down.shape[0]
    lat = MLA_Q_LORA // N_DEV

    gains = jnp.pad(jnp.stack([mla_q_norm_g.reshape(-1), mla_kv_norm_g.reshape(-1)]), ((0, 6), (0, 128 - n_mla * lat)))
    gains = all_gather("ag_gains", [gains])[0]

    def full_gain(row, slot):
        return gains[:, row, slot * lat:(slot + 1) * lat].reshape(-1)

    first_matmul = {"mla": "mla_down", "sb": "sb_qkv", "ca": "ca_qkv"}

    def post_gather(i, part):
        kind = MIXERS[i % 3]
        specs = _part(i, part)
        shards = [PAD.get(name, lambda a: a)(w[name][idx]).astype(BF16) for _, name, _, idx in specs]
        if part == "mix":
            wants = ([MIXERS[(i - 1) % 3] + "_wo", "ffn_in"], "ffn_out") if i > 0 else ([None], None)
            return [(specs, gather_jobs(f"ag_mix{i}", shards, *wants))]
        w_in = (["ffn_out", first_matmul[kind]], kind + "_attn") if i > 0 else (["mla_down", "mla_uq"], "mla_ukv")
        return [(specs[:1], gather_jobs(f"ag_w_in{i}", shards[:1], *w_in)),
                (specs[1:], gather_jobs(f"ag_w_out{i}", shards[1:], [kind + "_attn"], kind + "_wo"))]

    def gathered(posted):
        return {key: Weight(how, g) for specs, future in posted for (key, _, how, _), g in zip(specs, future.get())}

    long_attention = "sb_attn_bwd"

    def post_scatter(i, part, g):
        kind = MIXERS[i % 3]
        specs = _part(i, part)
        n = len(specs)
        grads_of = lambda sp: [g[key] for key, _, _, _ in sp]
        if part == "ffn":
            w_in_rides = long_attention if MIXERS[(i - 1) % 3] == "sb" and i > 0 else kind + "_attn_bwd"
            return [(specs[:1], scatter_jobs(f"rs_w_in{i}", grads_of(specs[:1]), core, chip, kind + "_dwo",
                                             [([0], w_in_rides)])),
                    (specs[1:], scatter_jobs(f"rs_w_out{i}", grads_of(specs[1:]), core, chip, kind + "_dwo",
                                             [([0], kind + "_attn_bwd")]))]
        wants = ("ffn_da", [(list(range(n - 1)), "ffn_dwout"), ([n - 1], "ffn_dh")])
        return [(specs, scatter_jobs(f"rs_mix{i}", grads_of(specs), core, chip, *wants))]

    SCHED.pending.clear()
    bias = rel_bias_blocks("ca_bias", ca_rel_bias[0], _att_tiles("ca", S)[1])

    h, h16 = x[0], x[0].astype(BF16)
    h16t = transpose("x_t", h16)
    saved, layers = [], []
    mix_w, ffn_w = post_gather(0, "mix"), None
    for i in range(DEPTH):
        kind, slot = MIXERS[i % 3], i // 3
        lw = gathered(mix_w)
        if i == 0:
            ffn_w = post_gather(0, "ffn")
        this_ffn = ffn_w
        if i + 1 < DEPTH:
            mix_w, ffn_w = post_gather(i + 1, "mix"), post_gather(i + 1, "ffn")
        if kind == "mla":
            mix, s_mix = mla_forward(h16, lw, full_gain(0, slot), full_gain(1, slot), tables)
        else:
            mix, s_mix = qkv_forward(kind, h16, lw, bias if kind == "ca" else None)
        y, y16, y16t, xh1, rs1 = ln_fwd("ln_mix", h, mix, ln_mix_g[i], ln_mix_b[i])
        lw.update(gathered(this_ffn))
        f, s_mlp = mlp_forward(y16, lw)
        y2, y2_16, y2_16t, xh2, rs2 = ln_fwd("ln_ffn", y, f, ln_ffn_g[i], ln_ffn_b[i])
        saved.append((h16t, s_mix, xh1, rs1, y16t, s_mlp, xh2, rs2))
        layers.append(lw)
        h, h16, h16t = y2, y2_16, y2_16t
    sq, dy = loss_fwd_bwd("loss", h, loss_target[0])
    loss = 0.5 / D * lax.psum(sq[0, 0], ("x", "y", "c"))

    ga, gb = dy, None
    grads = {name: [None] * w[name].shape[0] for name in WEIGHTS}
    dbias = None
    scattered = []
    for i in reversed(range(DEPTH)):
        kind, slot = MIXERS[i % 3], i // 3
        lw = layers[i]
        h16_in, s_mix, xh1, rs1, y16, s_mlp, xh2, rs2 = saved[i]
        du, du16, grads["ln_ffn_g"][i], grads["ln_ffn_b"][i] = ln_bwd("ln_ffn_bwd", ga, gb, xh2, rs2, ln_ffn_g[i])
        dh_mlp, g_mlp = mlp_backward(du16, y16, s_mlp, lw)
        scattered += post_scatter(i, "ffn", g_mlp)
        du, du16, grads["ln_mix_g"][i], grads["ln_mix_b"][i] = ln_bwd("ln_mix_bwd", du, dh_mlp, xh1, rs1, ln_mix_g[i])
        if kind == "mla":
            rides = {"wo": ("mla_attn_bwd", "mla_duq"), "uq": ("mla_dcq", "mla_dukv"), "ukv": ("mla_dckv", "mla_ddown"),
                     "down": ("mla_dh", "ffn_da" if i > 0 else None)}
            by_key = {spec[0]: spec for spec in _part(i, "mix")}

            def emit(key, grad, i=i):
                sibling, chips = rides[key]
                scattered.append(([by_key[key]], scatter_jobs(f"rs_{key}{i}", [grad], core, chip, sibling,
                                                              [([0], chips)])))

            dh_mix, (dgq, dgkv) = mla_backward(du16, h16_in, s_mix, lw, full_gain(0, slot), full_gain(1, slot),
                                               tables, emit)
            grads["mla_q_norm_g"][slot], grads["mla_kv_norm_g"][slot] = dgq, dgkv
        else:
            dh_mix, g_mix, db = qkv_backward(kind, du16, h16_in, s_mix, lw, bias if kind == "ca" else None)
            dbias = db if kind == "ca" else dbias
            scattered += post_scatter(i, "mix", g_mix)
        ga, gb = du, dh_mix
    grad_x = axpy("grad_x", ga, gb)[None]
    SCHED.flush()
    for specs, future in scattered:
        for (_, name, _, idx), g in zip(specs, future.get()):
            grads[name][idx] = UNPAD.get(name, lambda a: a)(g)
    grads["ca_rel_bias"][0] = rel_bias_grad("ca_bias_grad", dbias)

    small = ("ln_mix_g", "ln_mix_b", "ln_ffn_g", "ln_ffn_b", "ca_rel_bias", "mla_q_norm_g", "mla_kv_norm_g")
    packed, offsets = _pack_rows([g for name in small for g in grads[name]])
    total = sum_devices("sum_small", all_gather("ag_small", [packed])[0]).reshape(-1)
    pos = 0
    for name in small:
        for idx, g in enumerate(grads[name]):
            full = total[offsets[pos]:offsets[pos + 1]]
            pos += 1
            if name in ("mla_q_norm_g", "mla_kv_norm_g"):
                full = lax.dynamic_slice(full, (me * lat,), (lat,))
            grads[name][idx] = full.reshape(w[name].shape[1:])

    g_out, d_out, m_out, v_out = [], [], [], []
    for name in WEIGHTS:
        g = jnp.stack(grads[name])
        delta, new_m, new_v = adamw("adamw_" + name, w[name], g, mom[name], var[name])
        g_out.append(g)
        d_out.append(delta)
        m_out.append(new_m)
        v_out.append(new_v)
    return (loss, grad_x, *g_out, *d_out, *m_out, *v_out)
```

```python
import functools

import numpy as np
import jax
import jax.numpy as jnp
from jax import lax
from jax.experimental import pallas as pl
from jax.experimental.pallas import tpu as pltpu

F32 = jnp.float32
BF16 = jnp.bfloat16
MESH = pl.DeviceIdType.MESH
N_DEV = 8

DEPTH = 4
CHUNK = 64
CHUNK_SHIFT = 6
HEADS = 16
HEAD_DIM = 128
MLA_Q_LORA = 512
MLA_KV_LORA = 512
MLA_NOPE = 128
MLA_ROPE = 64
ROPE_THETA = 10000.0
CA_LEFT_CHUNKS = 8
REL_CLIP_LEFT = 128
REL_TABLE = REL_CLIP_LEFT + CHUNK
LN_EPS = 1e-5
RMS_EPS = 1e-6
ALPHA = (2.0 * DEPTH) ** 0.25
NEG = -1e30
ADAM_LR = 0.001
ADAM_B1 = 0.9
ADAM_B2 = 0.999
ADAM_EPS = 1e-08
ADAM_WD = 0.01
ADAM_STEP = 10

V7X_VMEM_BYTES = 64 * 1024 * 1024
VMEM_LIMIT = V7X_VMEM_BYTES - 8 * 1024 * 1024
ATT_TQ = 512
ATT_TK = 256
ATT_G = 2


def _params(*sem):
    return pltpu.CompilerParams(dimension_semantics=sem if sem else None, vmem_limit_bytes=VMEM_LIMIT)


HBM = pl.BlockSpec(memory_space=pl.ANY)


class Job:
    def __init__(self, name, want, operands, out_shape, n_copies, copies, done, aliases=None):
        self.name, self.want, self.operands, self.out_shape = name, want, list(operands), list(out_shape)
        self.n_copies, self.copies, self.done, self.aliases = n_copies, copies, done, dict(aliases or {})

    def sems(self):
        return [pltpu.SemaphoreType.DMA((self.n_copies,)), pltpu.SemaphoreType.DMA((self.n_copies,))]


class Scheduler:
    def __init__(self):
        self.pending = []

    def post(self, job):
        self.pending.append(job)

    def take(self, name):
        mine = [job for job in self.pending if job.want is not None and job.want in name]
        self.pending = [job for job in self.pending if job not in mine]
        return mine

    def flush(self):
        while self.pending:
            job = self.pending.pop(0)
            n_in, n_out = len(job.operands), len(job.out_shape)

            def body(*refs, job=job, n_in=n_in, n_out=n_out):
                cps = job.copies(refs[:n_in], refs[n_in:n_in + n_out], refs[-2], refs[-1])
                for cp in cps:
                    cp.start()
                for cp in cps:
                    cp.wait()

            outs = pl.pallas_call(
                body, name=job.name, in_specs=[HBM] * n_in, out_specs=[HBM] * n_out, out_shape=job.out_shape,
                scratch_shapes=job.sems(), input_output_aliases=job.aliases)(*job.operands)
            job.done(list(outs))


SCHED = Scheduler()


def _call(body, operands, *, name, grid, in_specs, out_specs, out_shape, scratch_shapes=(), semantics):
    jobs = SCHED.take(name)
    if not jobs:
        return list(pl.pallas_call(
            body, name=name, grid=grid, in_specs=list(in_specs), out_specs=list(out_specs), out_shape=list(out_shape),
            scratch_shapes=list(scratch_shapes), compiler_params=_params(*semantics))(*operands))
    n_in, n_out, n_scr = len(operands), len(out_shape), len(scratch_shapes)
    j_in = np.cumsum([0] + [len(job.operands) for job in jobs])
    j_out = np.cumsum([0] + [len(job.out_shape) for job in jobs])
    a, b = n_in, n_in + int(j_in[-1])
    c, d = b + n_out, b + n_out + int(j_out[-1])

    def carrying(*refs):
        def copies():
            sems = refs[d + n_scr:]
            return [cp for k, job in enumerate(jobs)
                    for cp in job.copies(refs[a + j_in[k]:a + j_in[k + 1]], refs[c + j_out[k]:c + j_out[k + 1]],
                                         sems[2 * k], sems[2 * k + 1])]

        ids = [pl.program_id(k) for k in range(len(grid))]
        first = functools.reduce(jnp.logical_and, [i == 0 for i in ids])
        last = functools.reduce(jnp.logical_and, [i == g - 1 for i, g in zip(ids, grid)])

        @pl.when(first)
        def _():
            for cp in copies():
                cp.start()

        body(*refs[:a], *refs[b:c], *refs[d:d + n_scr])

        @pl.when(last)
        def _():
            for cp in copies():
                cp.wait()

    aliases = {n_in + int(j_in[k]) + i: n_out + int(j_out[k]) + o for k, job in enumerate(jobs)
               for i, o in job.aliases.items()}
    outs = pl.pallas_call(
        carrying, name=name + "_carry", grid=grid, in_specs=list(in_specs) + [HBM] * int(j_in[-1]),
        out_specs=list(out_specs) + [HBM] * int(j_out[-1]),
        out_shape=list(out_shape) + [s for job in jobs for s in job.out_shape],
        scratch_shapes=list(scratch_shapes) + [s for job in jobs for s in job.sems()],
        input_output_aliases=aliases,
        compiler_params=_params(*(["arbitrary"] * len(grid))))(*operands, *[o for job in jobs for o in job.operands])
    for k, job in enumerate(jobs):
        job.done(list(outs[n_out + int(j_out[k]):n_out + int(j_out[k + 1])]))
    return list(outs[:n_out])


def _matmul(name, a, b, *, contract, grid, a_spec, b_spec, o_specs, out_shape, acc_shape,
            epilogue=None, extra=(), extra_specs=()):
    nk = grid[2]
    n_extra = len(extra)
    n_out = len(out_shape)

    def finish(acc, e_refs, o_refs):
        outs = epilogue(acc, *[e[...] for e in e_refs]) if epilogue else (acc,)
        for o_ref, val in zip(o_refs, outs):
            o_ref[...] = val.astype(o_ref.dtype)

    def product(a_ref, b_ref):
        return lax.dot_general(a_ref[...], b_ref[...], (contract, ((), ())), preferred_element_type=F32)

    def body_single(*refs):
        finish(product(refs[0], refs[1]), refs[2:2 + n_extra], refs[2 + n_extra:2 + n_extra + n_out])

    def body(*refs):
        a_ref, b_ref = refs[0], refs[1]
        acc_ref = refs[-1]
        k = pl.program_id(2)

        @pl.when(k == 0)
        def _():
            acc_ref[...] = jnp.zeros_like(acc_ref)

        acc_ref[...] += product(a_ref, b_ref)

        @pl.when(k == nk - 1)
        def _():
            finish(acc_ref[...], refs[2:2 + n_extra], refs[2 + n_extra:2 + n_extra + n_out])

    return _call(
        body_single if nk == 1 else body, [a, b, *extra], name=name, grid=grid,
        in_specs=[a_spec, b_spec, *extra_specs], out_specs=o_specs, out_shape=out_shape,
        scratch_shapes=[] if nk == 1 else [pltpu.VMEM(acc_shape, F32)],
        semantics=("parallel", "parallel", "arbitrary"))


MATMUL_BLOCK_BYTES = 40 * 1024 * 1024
MAX_TK = 2048
MULTI_TK = 512


def _fit_tn(n, tm, tk, nk, out_bytes):
    cands = sorted({n} | {t for t in range(128, n, 128) if n % t == 0}, reverse=True)
    for tn in cands:
        need = 2 * 2 * (tm * tk + tk * tn) + 2 * tm * tn * out_bytes + (tm * tn * 4 if nk > 1 else 0) + tm * tn * 4
        if need <= MATMUL_BLOCK_BYTES:
            return tn
    return cands[-1]


def _itemsize(dtypes):
    return sum(jnp.dtype(d).itemsize for d in dtypes)


def _tile(n, pref):
    if n <= pref:
        return n
    t = pref
    while t >= 128:
        if n % t == 0 and t % 128 == 0:
            return t
        t -= 128
    return n


class Weight:
    def __init__(self, kind, arr):
        self.kind = kind
        self.arr = arr
        self.R, self.C = arr.shape[1], arr.shape[2]

    @property
    def two_d(self):
        return self.arr.reshape(N_DEV * self.R, self.C)


def mm_nn(name, a, w, out_dtypes, epilogue=None, transposed=()):
    M, K = a.shape
    tm = M
    tk = K if K <= MAX_TK else _tile(K, MULTI_TK)
    nk = K // tk
    if w.kind == "row":
        b = w.two_d
        N = w.C
        tn = _fit_tn(N, tm, tk, nk, _itemsize(out_dtypes))
        b_spec = pl.BlockSpec((tk, tn), lambda i, j, k: (k, j))
    else:
        b = w.arr
        N = N_DEV * w.C
        tn = _fit_tn(w.C, tm, tk, nk, _itemsize(out_dtypes))
        per = w.C // tn
        b_spec = pl.BlockSpec((None, tk, tn), lambda i, j, k: (j // per, k, j % per))
    grid = (M // tm, N // tn, nk)
    flip = [t < len(transposed) and transposed[t] for t in range(len(out_dtypes))]
    return _matmul(
        name, a, b, contract=((1,), (0,)), grid=grid,
        a_spec=pl.BlockSpec((tm, tk), lambda i, j, k: (i, k)), b_spec=b_spec,
        o_specs=[pl.BlockSpec((tn, tm), lambda i, j, k: (j, i)) if f else pl.BlockSpec((tm, tn), lambda i, j, k: (i, j))
                 for f in flip],
        out_shape=[jax.ShapeDtypeStruct((N, M) if f else (M, N), d) for f, d in zip(flip, out_dtypes)],
        acc_shape=(tm, tn), epilogue=epilogue)


def mm_nt(name, dy, w, out_dtype, epilogue=None, extra=None):
    M, N = dy.shape
    tm = M
    out_bytes = jnp.dtype(out_dtype).itemsize + (0 if extra is None else extra.dtype.itemsize)
    if w.kind == "row":
        b = w.two_d
        kin = N_DEV * w.R
        tk = N if N <= MAX_TK else _tile(N, MULTI_TK)
        tn = _fit_tn(kin, tm, tk, N // tk, out_bytes)
        b_spec = pl.BlockSpec((tn, tk), lambda i, j, k: (j, k))
    else:
        b = w.arr
        kin = w.R
        tk = _tile(w.C, MULTI_TK)
        per = w.C // tk
        tn = _fit_tn(kin, tm, tk, N // tk, out_bytes)
        b_spec = pl.BlockSpec((None, tn, tk), lambda i, j, k: (k // per, j, k % per))
    grid = (M // tm, kin // tn, N // tk)
    o_spec = pl.BlockSpec((tm, tn), lambda i, j, k: (i, j))
    return _matmul(
        name, dy, b, contract=((1,), (1,)), grid=grid,
        a_spec=pl.BlockSpec((tm, tk), lambda i, j, k: (i, k)), b_spec=b_spec, o_specs=[o_spec],
        out_shape=[jax.ShapeDtypeStruct((M, kin), out_dtype)], acc_shape=(tm, tn), epilogue=epilogue,
        extra=() if extra is None else (extra,), extra_specs=() if extra is None else (o_spec,))[0]


TRANSPOSE_TILE = 512


def transpose(name, x):
    S, n = x.shape
    ts, tn = _tile(S, TRANSPOSE_TILE), _tile(n, TRANSPOSE_TILE)

    def body(x_ref, o_ref):
        o_ref[...] = x_ref[...].T

    return pl.pallas_call(
        body, name=name, grid=(S // ts, n // tn), in_specs=[pl.BlockSpec((ts, tn), lambda i, j: (i, j))],
        out_specs=pl.BlockSpec((tn, ts), lambda i, j: (j, i)), out_shape=jax.ShapeDtypeStruct((n, S), x.dtype),
        compiler_params=_params("parallel", "parallel"),
    )(x)


def mm_tn(name, x, dy, kind, R, C, transposed=False):
    if not transposed:
        x = transpose(name + "_t", x)
    kin, S = x.shape
    N = dy.shape[1]
    tk = S if S <= MAX_TK else _tile(S, MULTI_TK)
    nk = S // tk
    if kind == "col":
        tm = kin
        tn = _fit_tn(C, tm, tk, nk, 2)
        per = C // tn
        grid = (1, N // tn, nk)
        o_spec = pl.BlockSpec((None, None, tm, tn), lambda i, j, k: ((j // per) % 2, (j // per) // 2, 0, j % per))
    else:
        tm = R
        tn = _fit_tn(N, tm, tk, nk, 2)
        grid = (N_DEV, N // tn, nk)
        o_spec = pl.BlockSpec((None, None, tm, tn), lambda i, j, k: (i % 2, i // 2, 0, j))
    return _matmul(
        name, x, dy, contract=((1,), (0,)), grid=grid,
        a_spec=pl.BlockSpec((tm, tk), lambda i, j, k: (i, k)),
        b_spec=pl.BlockSpec((tk, tn), lambda i, j, k: (k, j)), o_specs=[o_spec],
        out_shape=[jax.ShapeDtypeStruct((2, 4, R, C), BF16)], acc_shape=(tm, tn))[0]


def _relu2_epilogue(acc):
    r = jnp.maximum(acc, 0.0)
    z = (r * r).astype(BF16)
    return acc, z, z.T


def _mulrelu_epilogue(acc, a):
    return (acc * (2.0 * jnp.maximum(a, 0.0)),)


ROW_TILE = 256


def ln_fwd(name, h, m, g, b):
    S, D = h.shape
    ts = _tile(S, ROW_TILE)

    def body(h_ref, m_ref, g_ref, b_ref, y_ref, y16_ref, yt_ref, xh_ref, rs_ref):
        u = ALPHA * h_ref[...] + m_ref[...]
        mu = jnp.mean(u, axis=-1, keepdims=True)
        d = u - mu
        var = jnp.mean(d * d, axis=-1, keepdims=True)
        rstd = lax.rsqrt(var + LN_EPS)
        xh = d * rstd
        y = xh * g_ref[...] + b_ref[...]
        y16 = y.astype(BF16)
        y_ref[...] = y
        y16_ref[...] = y16
        yt_ref[...] = y16.T
        xh_ref[...] = xh
        rs_ref[...] = jnp.broadcast_to(rstd, rs_ref.shape)

    row = pl.BlockSpec((ts, D), lambda i: (i, 0))
    vec = pl.BlockSpec((1, D), lambda i: (0, 0))
    return pl.pallas_call(
        body, name=name, grid=(S // ts,), in_specs=[row, row, vec, vec],
        out_specs=[row, row, pl.BlockSpec((D, ts), lambda i: (0, i)), row, pl.BlockSpec((ts, 128), lambda i: (i, 0))],
        out_shape=[jax.ShapeDtypeStruct((S, D), F32), jax.ShapeDtypeStruct((S, D), BF16),
                   jax.ShapeDtypeStruct((D, S), BF16), jax.ShapeDtypeStruct((S, D), F32),
                   jax.ShapeDtypeStruct((S, 128), F32)],
        compiler_params=_params("parallel"),
    )(h, m, g.reshape(1, D), b.reshape(1, D))


def ln_bwd(name, ga, gb, xhat, rstd, g):
    S, D = xhat.shape
    ts = _tile(S, ROW_TILE)
    two = gb is not None

    def body(*refs):
        if two:
            ga_ref, gb_ref, xh_ref, rs_ref, g_ref, du_ref, du16_ref, dg_ref, db_ref = refs
            dy = ALPHA * ga_ref[...] + gb_ref[...]
        else:
            ga_ref, xh_ref, rs_ref, g_ref, du_ref, du16_ref, dg_ref, db_ref = refs
            dy = ga_ref[...]
        xh = xh_ref[...]

        @pl.when(pl.program_id(0) == 0)
        def _():
            dg_ref[...] = jnp.zeros_like(dg_ref)
            db_ref[...] = jnp.zeros_like(db_ref)

        dg_ref[...] += jnp.sum(dy * xh, axis=0, keepdims=True)
        db_ref[...] += jnp.sum(dy, axis=0, keepdims=True)
        dxh = dy * g_ref[...]
        m1 = jnp.mean(dxh, axis=-1, keepdims=True)
        m2 = jnp.mean(dxh * xh, axis=-1, keepdims=True)
        du = rs_ref[:, 0:1] * (dxh - m1 - xh * m2)
        du_ref[...] = du
        du16_ref[...] = du.astype(BF16)

    row = pl.BlockSpec((ts, D), lambda i: (i, 0))
    vec = pl.BlockSpec((1, D), lambda i: (0, 0))
    stat = pl.BlockSpec((ts, 128), lambda i: (i, 0))
    ins = [ga, gb, xhat, rstd, g.reshape(1, D)] if two else [ga, xhat, rstd, g.reshape(1, D)]
    in_specs = [row, row, row, stat, vec] if two else [row, row, stat, vec]
    return pl.pallas_call(
        body, name=name, grid=(S // ts,), in_specs=in_specs, out_specs=[row, row, vec, vec],
        out_shape=[jax.ShapeDtypeStruct((S, D), F32), jax.ShapeDtypeStruct((S, D), BF16),
                   jax.ShapeDtypeStruct((1, D), F32), jax.ShapeDtypeStruct((1, D), F32)],
        compiler_params=_params("arbitrary"),
    )(*ins)


def loss_fwd_bwd(name, y, target):
    S, D = y.shape
    ts = _tile(S, ROW_TILE)

    def body(y_ref, t_ref, l_ref, dy_ref):
        @pl.when(pl.program_id(0) == 0)
        def _():
            l_ref[...] = jnp.zeros_like(l_ref)

        e = y_ref[...] - t_ref[...]
        l_ref[...] += jnp.sum(e * e)
        dy_ref[...] = e * (1.0 / D)

    row = pl.BlockSpec((ts, D), lambda i: (i, 0))
    return pl.pallas_call(
        body, name=name, grid=(S // ts,), in_specs=[row, row],
        out_specs=[pl.BlockSpec((1, 128), lambda i: (0, 0)), row],
        out_shape=[jax.ShapeDtypeStruct((1, 128), F32), jax.ShapeDtypeStruct((S, D), F32)],
        compiler_params=_params("arbitrary"),
    )(y, target)


def axpy(name, ga, gb):
    S, D = ga.shape
    ts = _tile(S, ROW_TILE)

    def body(a_ref, b_ref, o_ref):
        o_ref[...] = ALPHA * a_ref[...] + b_ref[...]

    row = pl.BlockSpec((ts, D), lambda i: (i, 0))
    return pl.pallas_call(body, name=name, grid=(S // ts,), in_specs=[row, row], out_specs=row,
                          out_shape=jax.ShapeDtypeStruct((S, D), F32), compiler_params=_params("parallel"))(ga, gb)


def rms_fwd(name, down, gq, gkv):
    S = down.shape[0]
    ts = _tile(S, ROW_TILE)
    L = MLA_Q_LORA

    def body(d_ref, gq_ref, gkv_ref, q_ref, kv_ref):
        for lo, g_ref, o_ref in ((0, gq_ref, q_ref), (L, gkv_ref, kv_ref)):
            x = d_ref[:, lo:lo + L]
            r = lax.rsqrt(jnp.mean(x * x, axis=-1, keepdims=True) + RMS_EPS)
            o_ref[...] = (x * r * g_ref[...]).astype(BF16)

    vec = pl.BlockSpec((1, L), lambda i: (0, 0))
    out = pl.BlockSpec((ts, L), lambda i: (i, 0))
    return pl.pallas_call(
        body, name=name, grid=(S // ts,), in_specs=[pl.BlockSpec((ts, down.shape[1]), lambda i: (i, 0)), vec, vec],
        out_specs=[out, out], out_shape=[jax.ShapeDtypeStruct((S, L), BF16)] * 2, compiler_params=_params("parallel"),
    )(down, gq.reshape(1, L), gkv.reshape(1, L))


def rms_bwd(name, down, dq, dkv, dkr, gq, gkv):
    S, W = down.shape
    ts = _tile(S, ROW_TILE)
    L = MLA_Q_LORA

    def body(d_ref, dq_ref, dkv_ref, dkr_ref, gq_ref, gkv_ref, o_ref, dgq_ref, dgkv_ref):
        @pl.when(pl.program_id(0) == 0)
        def _():
            dgq_ref[...] = jnp.zeros_like(dgq_ref)
            dgkv_ref[...] = jnp.zeros_like(dgkv_ref)

        for lo, dy_ref, g_ref, dg_ref in ((0, dq_ref, gq_ref, dgq_ref), (L, dkv_ref, gkv_ref, dgkv_ref)):
            x = d_ref[:, lo:lo + L]
            dy = dy_ref[...]
            r = lax.rsqrt(jnp.mean(x * x, axis=-1, keepdims=True) + RMS_EPS)
            dg_ref[...] += jnp.sum(dy * x * r, axis=0, keepdims=True)
            dyg = dy * g_ref[...]
            dx = r * dyg - x * (r * r * r) * jnp.mean(dyg * x, axis=-1, keepdims=True)
            o_ref[:, lo:lo + L] = dx.astype(BF16)
        o_ref[:, 2 * L:] = dkr_ref[...].astype(BF16)

    vec = pl.BlockSpec((1, L), lambda i: (0, 0))
    lat = pl.BlockSpec((ts, L), lambda i: (i, 0))
    full = pl.BlockSpec((ts, W), lambda i: (i, 0))
    return pl.pallas_call(
        body, name=name, grid=(S // ts,),
        in_specs=[full, lat, lat, pl.BlockSpec((ts, 128), lambda i: (i, 0)), vec, vec],
        out_specs=[full, vec, vec],
        out_shape=[jax.ShapeDtypeStruct((S, W), BF16), jax.ShapeDtypeStruct((1, L), F32), jax.ShapeDtypeStruct((1, L), F32)],
        compiler_params=_params("arbitrary"),
    )(down, dq, dkv, dkr, gq.reshape(1, L), gkv.reshape(1, L))


def rope_tables(S):
    half = MLA_ROPE // 2
    inv = (np.float32(ROPE_THETA) ** (-np.arange(half, dtype=np.float32) / np.float32(half))).astype(np.float32)
    ang = np.arange(S, dtype=np.float32)[:, None] * inv[None, :]
    cos, sin = np.cos(ang).astype(np.float32), np.sin(ang).astype(np.float32)
    z = np.zeros_like(cos)
    return (jnp.asarray(np.concatenate([cos, z, cos, z], 1)), jnp.asarray(np.concatenate([-sin, z, sin, z], 1)))


def _rot(x, cos, sin):
    return x * cos + pltpu.roll(x, 64, 1) * sin


def mla_prep_fwd(name, q, kv, down, cos, sin):
    S = q.shape[0]
    ts = _tile(S, ROW_TILE)

    def body(q_ref, kv_ref, kr_ref, c_ref, s_ref, qo_ref, ko_ref):
        c, s = c_ref[...], s_ref[...]
        key = _rot(kr_ref[...], c, s).astype(BF16)
        for h in range(HEADS):
            lo = 256 * h
            qo_ref[:, lo:lo + 128] = q_ref[:, lo:lo + 128].astype(BF16)
            qo_ref[:, lo + 128:lo + 256] = _rot(q_ref[:, lo + 128:lo + 256], c, s).astype(BF16)
            ko_ref[:, lo:lo + 128] = kv_ref[:, lo:lo + 128]
            ko_ref[:, lo + 128:lo + 256] = key

    heads = pl.BlockSpec((ts, HEADS * 256), lambda i: (i, 0))
    tab = pl.BlockSpec((ts, 128), lambda i: (i, 0))
    return pl.pallas_call(
        body, name=name, grid=(S // ts,),
        in_specs=[heads, heads, pl.BlockSpec((ts, 128), lambda i: (i, 2 * MLA_Q_LORA // 128)), tab, tab],
        out_specs=[heads, heads], out_shape=[jax.ShapeDtypeStruct(q.shape, BF16)] * 2,
        compiler_params=_params("parallel"),
    )(q, kv, down, cos, sin)


def mla_prep_bwd(name, dq, dk, dv, cos, sin):
    S = dq.shape[0]
    ts = _tile(S, ROW_TILE)

    def body(dq_ref, dk_ref, dv_ref, c_ref, s_ref, qo_ref, kvo_ref, kr_ref):
        c, s = c_ref[...], -s_ref[...]
        key = jnp.zeros((ts, 128), F32)
        for h in range(HEADS):
            lo = 256 * h
            qo_ref[:, lo:lo + 128] = dq_ref[:, lo:lo + 128].astype(BF16)
            qo_ref[:, lo + 128:lo + 256] = _rot(dq_ref[:, lo + 128:lo + 256], c, s).astype(BF16)
            kvo_ref[:, lo:lo + 128] = dk_ref[:, lo:lo + 128].astype(BF16)
            kvo_ref[:, lo + 128:lo + 256] = dv_ref[:, 128 * h:128 * h + 128].astype(BF16)
            key = key + dk_ref[:, lo + 128:lo + 256]
        kr_ref[...] = _rot(key, c, s)

    heads = pl.BlockSpec((ts, HEADS * 256), lambda i: (i, 0))
    tab = pl.BlockSpec((ts, 128), lambda i: (i, 0))
    return pl.pallas_call(
        body, name=name, grid=(S // ts,),
        in_specs=[heads, heads, pl.BlockSpec((ts, HEADS * 128), lambda i: (i, 0)), tab, tab],
        out_specs=[heads, heads, tab],
        out_shape=[jax.ShapeDtypeStruct(dq.shape, BF16), jax.ShapeDtypeStruct(dq.shape, BF16),
                   jax.ShapeDtypeStruct((S, 128), F32)],
        compiler_params=_params("parallel"),
    )(dq, dk, dv, cos, sin)


def _dot_nt(a, b):
    return lax.dot_general(a, b, (((1,), (1,)), ((), ())), preferred_element_type=F32)


def _dot_tn(a, b):
    return lax.dot_general(a, b, (((0,), (0,)), ((), ())), preferred_element_type=F32)


def _dot(a, b):
    return jnp.dot(a, b, preferred_element_type=F32)


def _positions(i, j, TQ, TK):
    row = i * TQ + lax.broadcasted_iota(jnp.int32, (TQ, TK), 0)
    col = j * TK + lax.broadcasted_iota(jnp.int32, (TQ, TK), 1)
    return row, col


def _softmax_mask(mode, row, col):
    rc, cc = row >> CHUNK_SHIFT, col >> CHUNK_SHIFT
    if mode == "mla":
        return cc <= rc
    return (cc <= rc) & (cc >= rc - CA_LEFT_CHUNKS)


def _key_blocks(mode, i, TQ, TK):
    per = TQ // TK
    if mode == "ca":
        lo = jnp.maximum(i - (CA_LEFT_CHUNKS * CHUNK) // TK, 0)
        return lo, 0, i - lo + 1
    return 0, i * per, per


class HeadCols:
    def __init__(self, arr, width, index, off=0, w=None):
        self.arr, self.width, self.index, self.off = arr, width, index, off
        self.w = width if w is None else w

    def rows(self, T):
        return pl.BlockSpec((T, ATT_G * self.width), lambda p, i: (i, self.index(p)))

    def full(self, S):
        return pl.BlockSpec((S, ATT_G * self.width), lambda p, i: (0, self.index(p)))

    def lanes(self, g):
        lo = g * self.width + self.off
        return slice(lo, lo + self.w)


def _att_tiles(mode, S):
    tk = min(ATT_TK, S)
    return (tk if mode == "ca" else min(ATT_TQ, S)), tk


def _walk(lo, n, per, step, carry, descending=False):
    tail = [lo + n + d for d in range(per)]
    if descending:
        for j in reversed(tail):
            carry = step(j, carry, True)
        return lax.fori_loop(0, n, lambda t, c: step(lo + n - 1 - t, c, False), carry)
    carry = lax.fori_loop(0, n, lambda t, c: step(lo + t, c, False), carry)
    for j in tail:
        carry = step(j, carry, True)
    return carry


def softmax_attn_fwd(name, mode, q, k, v, scale, bias=None):
    S = q.arr.shape[0]
    TQ, TK = _att_tiles(mode, S)
    G, dv = ATT_G, v.w

    def body(*refs):
        if bias is not None:
            q_ref, k_ref, v_ref, b_ref, o_ref, lse_ref = refs
        else:
            q_ref, k_ref, v_ref, o_ref, lse_ref = refs
        i = pl.program_id(1)
        qs = [q_ref[:, q.lanes(g)] for g in range(G)]

        def block(g, j, carry, mask, ks):
            m, l, acc = carry
            s = _dot_nt(qs[g], k_ref[ks, k.lanes(g)]) * scale
            if bias is not None:
                s = s + b_ref[g, jnp.minimum(i - j, 2)]
            if mask is not None:
                s = jnp.where(mask, s, NEG)
            m_new = jnp.maximum(m, jnp.max(s, axis=-1, keepdims=True))
            a = jnp.exp(m - m_new)
            p = jnp.exp(s - m_new)
            if mask is not None:
                p = jnp.where(mask, p, 0.0)
            l = a * l + jnp.sum(p, axis=-1, keepdims=True)
            acc = a * acc + _dot(p.astype(BF16), v_ref[ks, v.lanes(g)])
            return m_new, l, acc

        def step(j, carry, masked):
            ks = pl.ds(pl.multiple_of(j * TK, TK), TK)
            mask = _softmax_mask(mode, *_positions(i, j, TQ, TK)) if masked or mode == "ca" else None
            return tuple(block(g, j, carry[g], mask, ks) for g in range(G))

        init = (jnp.full((TQ, 1), NEG, F32), jnp.zeros((TQ, 1), F32), jnp.zeros((TQ, dv), F32))
        lo, n, per = _key_blocks(mode, i, TQ, TK)
        if mode == "ca":
            out = lax.fori_loop(lo, lo + per, lambda j, c: step(j, c, True), (init,) * G)
        else:
            out = _walk(lo, n, per, step, (init,) * G)
        for g, (m, l, acc) in enumerate(out):
            o_ref[:, g * dv:(g + 1) * dv] = (acc / l).astype(BF16)
            lse_ref[:, g * 128:(g + 1) * 128] = jnp.broadcast_to(m + jnp.log(l), (TQ, 128))

    in_specs = [q.rows(TQ), k.full(S), v.full(S)]
    ins = [q.arr, k.arr, v.arr]
    if bias is not None:
        in_specs.append(pl.BlockSpec((G, 3, TK, TK), lambda p, i: (p, 0, 0, 0)))
        ins.append(bias)
    return _call(
        body, ins, name=name, grid=(HEADS // G, S // TQ), in_specs=in_specs,
        out_specs=[pl.BlockSpec((TQ, G * dv), lambda p, i: (i, p)), pl.BlockSpec((TQ, G * 128), lambda p, i: (i, p))],
        out_shape=[jax.ShapeDtypeStruct((S, HEADS * dv), BF16), jax.ShapeDtypeStruct((S, HEADS * 128), F32)],
        semantics=("parallel", "parallel"))


def softmax_attn_bwd(name, mode, q, k, v, o, do, lse, scale, bias=None):
    S = q.arr.shape[0]
    TQ, TK = _att_tiles(mode, S)
    G, dqk, dv = ATT_G, q.w, v.w

    def body(*refs):
        if bias is not None:
            q_ref, k_ref, v_ref, o_ref, do_ref, lse_ref, b_ref, dq_ref, dk_ref, dv_ref, db_ref = refs
        else:
            q_ref, k_ref, v_ref, o_ref, do_ref, lse_ref, dq_ref, dk_ref, dv_ref = refs
        i = pl.program_id(1)

        @pl.when(i == 0)
        def _():
            dk_ref[...] = jnp.zeros_like(dk_ref)
            dv_ref[...] = jnp.zeros_like(dv_ref)
            if bias is not None:
                db_ref[...] = jnp.zeros_like(db_ref)

        qs = [q_ref[:, q.lanes(g)] for g in range(G)]
        dos = [do_ref[:, do.lanes(g)] for g in range(G)]
        lses = [lse_ref[:, g * 128:g * 128 + 1] for g in range(G)]
        deltas = [jnp.sum(dos[g].astype(F32) * o_ref[:, o.lanes(g)].astype(F32), axis=-1, keepdims=True)
                  for g in range(G)]

        def block(g, j, dq, mask, ks):
            kb, vb = k_ref[ks, k.lanes(g)], v_ref[ks, v.lanes(g)]
            s = _dot_nt(qs[g], kb) * scale
            if bias is not None:
                slot = jnp.minimum(i - j, 2)
                s = s + b_ref[g, slot]
            p = jnp.exp(s - lses[g])
            if mask is not None:
                p = jnp.where(mask, p, 0.0)
            ds = p * (_dot_nt(dos[g], vb) - deltas[g])
            if bias is not None:
                db_ref[g, slot] += ds
            dsb = (ds * scale).astype(BF16)
            dk_ref[ks, g * dqk:(g + 1) * dqk] += _dot_tn(dsb, qs[g])
            dv_ref[ks, g * dv:(g + 1) * dv] += _dot_tn(p.astype(BF16), dos[g])
            return dq + _dot(dsb, kb)

        def step(j, carry, masked):
            ks = pl.ds(pl.multiple_of(j * TK, TK), TK)
            mask = _softmax_mask(mode, *_positions(i, j, TQ, TK)) if masked or mode == "ca" else None
            return tuple(block(g, j, carry[g], mask, ks) for g in range(G))

        init = (jnp.zeros((TQ, dqk), F32),) * G
        lo, n, per = _key_blocks(mode, i, TQ, TK)
        if mode == "ca":
            out = lax.fori_loop(lo, lo + per, lambda j, c: step(j, c, True), init)
        else:
            out = _walk(lo, n, per, step, init)
        for g in range(G):
            dq_ref[:, g * dqk:(g + 1) * dqk] = out[g]

    in_specs = [q.rows(TQ), k.full(S), v.full(S), o.rows(TQ), do.rows(TQ),
                pl.BlockSpec((TQ, G * 128), lambda p, i: (i, p))]
    ins = [q.arr, k.arr, v.arr, o.arr, do.arr, lse]
    out_specs = [pl.BlockSpec((TQ, G * dqk), lambda p, i: (i, p)), pl.BlockSpec((S, G * dqk), lambda p, i: (0, p)),
                 pl.BlockSpec((S, G * dv), lambda p, i: (0, p))]
    out_shape = [jax.ShapeDtypeStruct((S, HEADS * dqk), F32), jax.ShapeDtypeStruct((S, HEADS * dqk), F32),
                 jax.ShapeDtypeStruct((S, HEADS * dv), F32)]
    if bias is not None:
        bspec = pl.BlockSpec((G, 3, TK, TK), lambda p, i: (p, 0, 0, 0))
        in_specs.append(bspec)
        ins.append(bias)
        out_specs.append(bspec)
        out_shape.append(jax.ShapeDtypeStruct(bias.shape, F32))
    return _call(body, ins, name=name, grid=(HEADS // G, S // TQ), in_specs=in_specs, out_specs=out_specs,
                 out_shape=out_shape, semantics=("parallel", "arbitrary"))


def _split2(x):
    hi = x.astype(BF16)
    return hi, (x - hi.astype(F32)).astype(BF16)


def _split3(x):
    hi = x.astype(BF16)
    r = x - hi.astype(F32)
    mid = r.astype(BF16)
    return hi, mid, (r - mid.astype(F32)).astype(BF16)


def _stick_block(qb, kb, strict, scale):
    z = _dot_nt(qb, kb) * scale
    sp = jnp.log(1.0 + jnp.exp(-jnp.abs(z)))
    lb = jnp.minimum(z, 0.0) - sp
    l1m = jnp.minimum(-z, 0.0) - sp
    if strict is not None:
        l1m = jnp.where(strict, l1m, 0.0)
    return z, lb, l1m


def _strict_mask(i, j, TQ, TK):
    row, col = _positions(i, j, TQ, TK)
    return col < row


def _tri(T, inclusive):
    r = lax.broadcasted_iota(jnp.int32, (T, T), 0)
    c = lax.broadcasted_iota(jnp.int32, (T, T), 1)
    return ((r >= c) if inclusive else (r > c)).astype(BF16)


def _tri_prefix(T, inclusive):
    r = lax.broadcasted_iota(jnp.int32, (T, T), 0)
    c = lax.broadcasted_iota(jnp.int32, (T, T), 1)
    return ((r <= c) if inclusive else (r < c)).astype(BF16)


def _suffix(parts, tri):
    out = _dot(parts[0], tri)
    for p in parts[1:]:
        out = out + _dot(p, tri)
    return out


def stick_attn_fwd(name, q, k, v, scale):
    S = q.arr.shape[0]
    TQ, TK = _att_tiles("sb", S)
    G, dv = ATT_G, v.w

    def body(q_ref, k_ref, v_ref, o_ref, tot_ref):
        i = pl.program_id(1)
        qs = [q_ref[:, q.lanes(g)] for g in range(G)]
        tri = _tri(TK, False)

        def block(g, carry, strict, ks):
            right, acc = carry
            z, lb, l1m = _stick_block(qs[g], k_ref[ks, k.lanes(g)], strict, scale)
            a = jnp.exp(lb + _suffix(_split2(l1m), tri) + right)
            if strict is not None:
                a = jnp.where(strict, a, 0.0)
            acc = acc + _dot(a.astype(BF16), v_ref[ks, v.lanes(g)])
            return right + jnp.sum(l1m, axis=-1, keepdims=True), acc

        def step(j, carry, masked):
            ks = pl.ds(pl.multiple_of(j * TK, TK), TK)
            strict = _strict_mask(i, j, TQ, TK) if masked else None
            return tuple(block(g, carry[g], strict, ks) for g in range(G))

        init = (jnp.zeros((TQ, 1), F32), jnp.zeros((TQ, dv), F32))
        lo, n, per = _key_blocks("sb", i, TQ, TK)
        out = _walk(lo, n, per, step, (init,) * G, descending=True)
        for g in range(G):
            o_ref[:, g * dv:(g + 1) * dv] = out[g][1].astype(BF16)
            tot_ref[:, g * 128:(g + 1) * 128] = jnp.broadcast_to(out[g][0], (TQ, 128))

    return _call(
        body, [q.arr, k.arr, v.arr], name=name, grid=(HEADS // G, S // TQ),
        in_specs=[q.rows(TQ), k.full(S), v.full(S)],
        out_specs=[pl.BlockSpec((TQ, G * dv), lambda p, i: (i, p)), pl.BlockSpec((TQ, G * 128), lambda p, i: (i, p))],
        out_shape=[jax.ShapeDtypeStruct((S, HEADS * dv), BF16), jax.ShapeDtypeStruct((S, HEADS * 128), F32)],
        semantics=("parallel", "parallel"))


def stick_attn_bwd(name, q, k, v, do, total, scale):
    S = q.arr.shape[0]
    TQ, TK = _att_tiles("sb", S)
    G, dqk, dv = ATT_G, q.w, v.w

    def body(q_ref, k_ref, v_ref, do_ref, tot_ref, dq_ref, dk_ref, dv_ref):
        i = pl.program_id(1)

        @pl.when(i == 0)
        def _():
            dk_ref[...] = jnp.zeros_like(dk_ref)
            dv_ref[...] = jnp.zeros_like(dv_ref)

        qs = [q_ref[:, q.lanes(g)] for g in range(G)]
        dos = [do_ref[:, do.lanes(g)] for g in range(G)]
        tots = [tot_ref[:, g * 128:g * 128 + 1] for g in range(G)]
        upto = _tri_prefix(TK, True)
        before = _tri_prefix(TK, False)

        def step(j, carry, masked):
            ks = pl.ds(pl.multiple_of(j * TK, TK), TK)
            strict = _strict_mask(i, j, TQ, TK) if masked else None
            out = []
            for g in range(G):
                left, gleft, dq = carry[g]
                kb = k_ref[ks, k.lanes(g)]
                z, lb, l1m = _stick_block(qs[g], kb, strict, scale)
                a = jnp.exp(lb + (tots[g] - (left + _suffix(_split3(l1m), upto))))
                if strict is not None:
                    a = jnp.where(strict, a, 0.0)
                gg = a * _dot_nt(dos[g], v_ref[ks, v.lanes(g)])
                c = gleft + _suffix(_split3(gg), before)
                sig = 1.0 / (1.0 + jnp.exp(-z))
                dz = gg * (1.0 - sig) - c * sig
                if strict is not None:
                    dz = jnp.where(strict, dz, 0.0)
                dzb = (dz * scale).astype(BF16)
                dk_ref[ks, g * dqk:(g + 1) * dqk] += _dot_tn(dzb, qs[g])
                dv_ref[ks, g * dv:(g + 1) * dv] += _dot_tn(a.astype(BF16), dos[g])
                out.append((left + jnp.sum(l1m, axis=-1, keepdims=True),
                            gleft + jnp.sum(gg, axis=-1, keepdims=True), dq + _dot(dzb, kb)))
            return tuple(out)

        zero = jnp.zeros((TQ, 1), F32)
        lo, n, per = _key_blocks("sb", i, TQ, TK)
        out = _walk(lo, n, per, step, ((zero, zero, jnp.zeros((TQ, dqk), F32)),) * G)
        for g in range(G):
            dq_ref[:, g * dqk:(g + 1) * dqk] = out[g][2]

    return _call(
        body, [q.arr, k.arr, v.arr, do.arr, total], name=name, grid=(HEADS // G, S // TQ),
        in_specs=[q.rows(TQ), k.full(S), v.full(S), do.rows(TQ), pl.BlockSpec((TQ, G * 128), lambda p, i: (i, p))],
        out_specs=[pl.BlockSpec((TQ, G * dqk), lambda p, i: (i, p)), pl.BlockSpec((S, G * dqk), lambda p, i: (0, p)),
                   pl.BlockSpec((S, G * dv), lambda p, i: (0, p))],
        out_shape=[jax.ShapeDtypeStruct((S, HEADS * dqk), F32), jax.ShapeDtypeStruct((S, HEADS * dqk), F32),
                   jax.ShapeDtypeStruct((S, HEADS * dv), F32)],
        semantics=("parallel", "arbitrary"))


def _skew(x, back):
    T = x.shape[0]
    rows = lax.broadcasted_iota(jnp.int32, (T, T), 0)
    for b in range(T.bit_length() - 1):
        shift = T - (1 << b) if back else 1 << b
        x = jnp.where(((rows >> b) & 1) == 1, pltpu.roll(x, shift, 1), x)
    return x


def _table_rows(table):
    t = jnp.pad(table.T, ((0, 0), (0, 2 * REL_CLIP_LEFT - REL_TABLE)))
    return t.reshape(table.shape[1], 2, REL_CLIP_LEFT)


def rel_bias_blocks(name, table, T):
    assert T == 2 * REL_CLIP_LEFT, "the base rows below are laid out for blocks of 256"

    def body(t_ref, o_ref):
        low, high = t_ref[0:1, :], t_ref[1:2, :]
        first = jnp.broadcast_to(t_ref[0:1, 0:1], (1, REL_CLIP_LEFT))
        qq = lax.broadcasted_iota(jnp.int32, (T, T), 0)
        kk = lax.broadcasted_iota(jnp.int32, (T, T), 1)

        def rolled(row):
            return _skew(jnp.broadcast_to(row, (T, T)), False)

        far = jnp.concatenate([first, low], axis=1)
        near = jnp.concatenate([high, jnp.zeros_like(high)], axis=1)
        o_ref[0] = jnp.where(kk >= qq, rolled(near), rolled(far))
        o_ref[1] = jnp.where(kk >= qq, rolled(far), jnp.broadcast_to(t_ref[0:1, 0:1], (T, T)))
        o_ref[2] = jnp.broadcast_to(t_ref[0:1, 0:1], (T, T))

    return pl.pallas_call(
        body, name=name, grid=(HEADS,), in_specs=[pl.BlockSpec((None, 2, REL_CLIP_LEFT), lambda h: (h, 0, 0))],
        out_specs=pl.BlockSpec((None, 3, T, T), lambda h: (h, 0, 0, 0)),
        out_shape=jax.ShapeDtypeStruct((HEADS, 3, T, T), F32), compiler_params=_params("parallel"),
    )(_table_rows(table))


def rel_bias_grad(name, dbias):
    T = dbias.shape[-1]
    L = REL_CLIP_LEFT
    assert T == 2 * L

    def body(d_ref, o_ref):
        qq = lax.broadcasted_iota(jnp.int32, (T, T), 0)
        ll = lax.broadcasted_iota(jnp.int32, (T, T), 1)
        wrapped = ll + qq >= T

        def columns(d):
            x = _skew(d_ref[d], True)
            return (jnp.sum(jnp.where(wrapped, 0.0, x), axis=0, keepdims=True),
                    jnp.sum(jnp.where(wrapped, x, 0.0), axis=0, keepdims=True))

        pos0, neg0 = columns(0)
        pos1, neg1 = columns(1)
        clipped = (jnp.sum(neg0[:, :L]) + jnp.sum(pos1[:, :L]) + jnp.sum(neg1) + jnp.sum(d_ref[2]))
        lane = lax.broadcasted_iota(jnp.int32, (1, L), 1)
        low = neg0[:, L:] + pos1[:, L:]
        o_ref[...] = jnp.zeros_like(o_ref)
        o_ref[0:1, :] = jnp.where(lane == 0, low + clipped, low)
        o_ref[1:2, :] = pos0[:, :L]

    rows = pl.pallas_call(
        body, name=name, grid=(HEADS,), in_specs=[pl.BlockSpec((None, 3, T, T), lambda h: (h, 0, 0, 0))],
        out_specs=pl.BlockSpec((None, 8, L), lambda h: (h, 0, 0)), out_shape=jax.ShapeDtypeStruct((HEADS, 8, L), F32),
        compiler_params=_params("parallel"),
    )(dbias)
    return rows[:, :2, :].reshape(HEADS, 2 * L)[:, :REL_TABLE].T


def _place():
    return lax.axis_index("x"), lax.axis_index("y"), lax.axis_index("c")


def all_gather(name, shards):
    n = len(shards)

    def body(*refs):
        x_refs, out_refs = refs[:n], refs[n:2 * n]
        send_sems, recv_sems, local_sems = refs[2 * n:]
        x, y, c = _place()
        me, sibling = (x, y, c), (x, y, 1 - c)
        chips = [(1 - x, y), (x, 1 - y), (1 - x, 1 - y)]

        def block(t, dev):
            return out_refs[t].at[4 * dev[0] + 2 * dev[1] + dev[2]]

        def copy(t, k, dev, to, src=None):
            return pltpu.make_async_remote_copy(
                src_ref=block(t, dev) if src is None else src, dst_ref=block(t, dev),
                send_sem=send_sems.at[t, k], recv_sem=recv_sems.at[t, k], device_id=to, device_id_type=MESH)

        mine = [pltpu.make_async_copy(x_refs[t], block(t, me), local_sems.at[t]) for t in range(n)]
        for cp in mine:
            cp.start()
        first = []
        for t in range(n):
            first.append(copy(t, 0, me, sibling, src=x_refs[t]))
            first += [copy(t, 1 + j, me, (*chip, c), src=x_refs[t]) for j, chip in enumerate(chips)]
        for cp in first:
            cp.start()
        passed = []
        for j, chip in enumerate(chips):
            for t in range(n):
                copy(t, 1 + j, (*chip, c), me).wait_recv()
                cp = copy(t, 4 + j, (*chip, c), sibling)
                cp.start()
                passed.append(cp)
        for t in range(n):
            copy(t, 0, sibling, me).wait_recv()
            for j, chip in enumerate(chips):
                copy(t, 4 + j, (*chip, 1 - c), me).wait_recv()
        for cp in first + passed:
            cp.wait_send()
        for cp in mine:
            cp.wait()

    return pl.pallas_call(
        body, name=name, in_specs=[HBM] * n, out_specs=[HBM] * n,
        out_shape=[jax.ShapeDtypeStruct((N_DEV, *s.shape), s.dtype) for s in shards],
        scratch_shapes=[pltpu.SemaphoreType.DMA((n, 7)), pltpu.SemaphoreType.DMA((n, 7)), pltpu.SemaphoreType.DMA((n,))],
    )(*shards)


def _remote(src, dst, send_sems, recv_sems, k, to):
    return pltpu.make_async_remote_copy(src_ref=src, dst_ref=dst, send_sem=send_sems.at[k], recv_sem=recv_sems.at[k],
                                        device_id=to, device_id_type=MESH)


class Future:
    def __init__(self):
        self.value = None

    def get(self):
        if self.value is None:
            SCHED.flush()
        return self.value


def gather_jobs(name, shards, wants_chips, want_sibling):
    n, shares = len(shards), len(wants_chips)
    result = Future()
    lands = [jax.ShapeDtypeStruct((N_DEV, *s.shape), s.dtype) for s in shards]

    def to_chips(share):
        def copies(in_refs, out_refs, send_sems, recv_sems):
            x, y, c = _place()
            me = 4 * x + 2 * y + c
            cps = []
            for t in range(n):
                rows = shards[t].shape[0] // shares
                mine = pl.ds(share * rows, rows)
                src, dst = in_refs[t].at[mine], out_refs[t].at[me, mine]
                cps.append(pltpu.make_async_copy(src, dst, send_sems.at[4 * t]))
                for j, chip in enumerate([(1 - x, y), (x, 1 - y), (1 - x, 1 - y)]):
                    cps.append(_remote(src, dst, send_sems, recv_sems, 4 * t + 1 + j, (*chip, c)))
            return cps
        return copies

    def to_sibling(in_refs, out_refs, send_sems, recv_sems):
        x, y, c = _place()
        return [_remote(in_refs[t].at[2 * chip + c], out_refs[t].at[2 * chip + c], send_sems, recv_sems, 4 * t + chip,
                        (x, y, 1 - c)) for t in range(n) for chip in range(4)]

    def post(share, landed):
        if share == shares:
            SCHED.post(Job(name + "_sibling", want_sibling, landed, lands, 4 * n, to_sibling,
                           lambda final: setattr(result, "value", final), aliases={t: t for t in range(n)}))
        else:
            SCHED.post(Job(f"{name}_chips{share}", wants_chips[share], list(shards) + (landed or []), lands, 4 * n,
                           to_chips(share), lambda outs: post(share + 1, outs),
                           aliases={n + t: t for t in range(n)} if landed else None))

    post(0, None)
    return result


def scatter_jobs(name, grads, core, chip, want_sibling, wants_chips):
    n = len(grads)
    result = Future()
    sums = [None] * n

    def to_sibling(in_refs, out_refs, send_sems, recv_sems):
        x, y, c = _place()
        return [_remote(in_refs[t].at[1 - c], out_refs[t], send_sems, recv_sems, t, (x, y, 1 - c)) for t in range(n)]

    def after_sibling(received):
        parts = [add_sibling(f"{name}_add{t}", grads[t], received[t], core) for t in range(n)]
        for group, want in wants_chips:
            def to_chips(in_refs, out_refs, send_sems, recv_sems, m=len(group)):
                x, y, c = _place()
                return [_remote(in_refs[t].at[2 * cx + cy], out_refs[t].at[j], send_sems, recv_sems, 3 * t + j,
                                (cx, cy, c))
                        for t in range(m) for j, (cx, cy) in enumerate([(1 - x, y), (x, 1 - y), (1 - x, 1 - y)])]

            def after_chips(received, group=group):
                for t, r in zip(group, received):
                    sums[t] = sum_chips(f"{name}_sum{t}", parts[t], r, chip)
                if all(s is not None for s in sums):
                    result.value = sums

            mine = [parts[t] for t in group]
            SCHED.post(Job(f"{name}_chips{group[0]}", want, mine,
                           [jax.ShapeDtypeStruct((3, *p.shape[1:]), p.dtype) for p in mine], 3 * len(mine), to_chips,
                           after_chips))

    SCHED.post(Job(name + "_sibling", want_sibling, grads, [jax.ShapeDtypeStruct(g.shape[1:], g.dtype) for g in grads],
                   n, to_sibling, after_sibling))
    return result


def _as_rows(shape):
    return (int(np.prod(shape[:-1])), shape[-1])


ELEMENTWISE_BLOCK = 256 * 1024
PARTIAL_SUM_BLOCK = 1024 * 1024


def _row_tile(rows, cols, block=ELEMENTWISE_BLOCK):
    return _tile(rows, max(128, block // cols // 128 * 128))


def add_sibling(name, grad, recv, core):
    rows, cols = _as_rows(grad.shape[2:])
    tr = _row_tile(rows, cols, PARTIAL_SUM_BLOCK)

    def body(c_ref, g_ref, r_ref, o_ref):
        o_ref[...] = (g_ref[...].astype(F32) + r_ref[...].astype(F32)).astype(BF16)

    blk = pl.BlockSpec((None, tr, cols), lambda k, i, c_ref: (k, i, 0))
    return pl.pallas_call(
        body, name=name,
        grid_spec=pltpu.PrefetchScalarGridSpec(
            num_scalar_prefetch=1, grid=(4, rows // tr),
            in_specs=[pl.BlockSpec((None, None, tr, cols), lambda k, i, c_ref: (c_ref[0], k, i, 0)), blk],
            out_specs=blk),
        out_shape=jax.ShapeDtypeStruct((4, rows, cols), BF16), compiler_params=_params("parallel", "parallel"),
    )(core, grad.reshape(2, 4, rows, cols), recv.reshape(4, rows, cols)).reshape(recv.shape)


def sum_chips(name, part, recv, chip):
    shape = part.shape[1:]
    rows, cols = _as_rows(shape)
    tr = _row_tile(rows, cols, PARTIAL_SUM_BLOCK)

    def body(c_ref, p_ref, r_ref, o_ref):
        o_ref[...] = (p_ref[...].astype(F32) + r_ref[0].astype(F32) + r_ref[1].astype(F32) + r_ref[2].astype(F32))

    return pl.pallas_call(
        body, name=name,
        grid_spec=pltpu.PrefetchScalarGridSpec(
            num_scalar_prefetch=1, grid=(rows // tr,),
            in_specs=[pl.BlockSpec((None, tr, cols), lambda i, c_ref: (c_ref[0], i, 0)),
                      pl.BlockSpec((3, tr, cols), lambda i, c_ref: (0, i, 0))],
            out_specs=pl.BlockSpec((tr, cols), lambda i, c_ref: (i, 0))),
        out_shape=jax.ShapeDtypeStruct((rows, cols), F32), compiler_params=_params("parallel"),
    )(chip, part.reshape(4, rows, cols), recv.reshape(3, rows, cols)).reshape(shape)


def sum_devices(name, gathered):
    _, rows, cols = gathered.shape

    def body(g_ref, o_ref):
        acc = g_ref[0]
        for d in range(1, N_DEV):
            acc = acc + g_ref[d]
        o_ref[...] = acc

    return pl.pallas_call(body, name=name, out_shape=jax.ShapeDtypeStruct((rows, cols), F32))(gathered)


def adamw(name, w, g, m, v):
    shape = w.shape
    rows, cols = _as_rows(shape)
    tr = _row_tile(rows, cols) if rows % 8 == 0 else rows
    c1 = 1.0 / (1.0 - ADAM_B1 ** ADAM_STEP)
    c2 = 1.0 / (1.0 - ADAM_B2 ** ADAM_STEP)

    def body(w_ref, g_ref, m_ref, v_ref, d_ref, mo_ref, vo_ref):
        g_ = g_ref[...]
        m_ = ADAM_B1 * m_ref[...] + (1.0 - ADAM_B1) * g_
        v_ = ADAM_B2 * v_ref[...] + (1.0 - ADAM_B2) * (g_ * g_)
        d_ref[...] = -ADAM_LR * ((m_ * c1) / (jnp.sqrt(v_ * c2) + ADAM_EPS) + ADAM_WD * w_ref[...])
        mo_ref[...] = m_
        vo_ref[...] = v_

    blk = pl.BlockSpec((tr, cols), lambda i: (i, 0))
    outs = pl.pallas_call(
        body, name=name, grid=(rows // tr,), in_specs=[blk] * 4, out_specs=[blk] * 3,
        out_shape=[jax.ShapeDtypeStruct((rows, cols), F32)] * 3, compiler_params=_params("parallel"),
    )(*[a.reshape(rows, cols) for a in (w, g, m, v)])
    return [o.reshape(shape) for o in outs]


def _spread_rope(r):
    z = jnp.zeros_like(r[..., :32])
    return jnp.concatenate([r[..., :32], z, r[..., 32:], z], -1)


def _gather_rope(r):
    return jnp.concatenate([r[..., :32], r[..., 64:96]], -1)


def pad_w_uq(w):
    w = w.reshape(w.shape[0], -1, MLA_NOPE + MLA_ROPE)
    return jnp.concatenate([w[..., :MLA_NOPE], _spread_rope(w[..., MLA_NOPE:])], -1).reshape(w.shape[0], -1)


def unpad_w_uq(g):
    g = g.reshape(g.shape[0], -1, 2 * MLA_NOPE)
    return jnp.concatenate([g[..., :MLA_NOPE], _gather_rope(g[..., MLA_NOPE:])], -1).reshape(g.shape[0], -1)


def pad_w_down(w):
    lat = MLA_Q_LORA + MLA_KV_LORA
    return jnp.concatenate([w[:, :lat], _spread_rope(w[:, lat:])], -1)


def unpad_w_down(g):
    lat = MLA_Q_LORA + MLA_KV_LORA
    return jnp.concatenate([g[:, :lat], _gather_rope(g[:, lat:])], -1)


def _heads(arr, width, first=0, off=0, w=None):
    return HeadCols(arr, width, lambda p: first // ATT_G + p, off, w)


def mla_forward(h16, w, gq, gkv, tables):
    cos, sin = tables
    down = mm_nn("mla_down", h16, w["down"], [F32])[0]
    cq, ckv = rms_fwd("mla_rms", down, gq, gkv)
    q = mm_nn("mla_uq", cq, w["uq"], [F32])[0]
    kv = mm_nn("mla_ukv", ckv, w["ukv"], [BF16])[0]
    qr, kp = mla_prep_fwd("mla_prep", q, kv, down, cos, sin)
    scale = (MLA_NOPE + MLA_ROPE) ** -0.5
    o, lse = softmax_attn_fwd("mla_attn", "mla", _heads(qr, 256), _heads(kp, 256), _heads(kv, 256, off=128, w=128), scale)
    m = mm_nn("mla_wo", o, w["wo"], [F32])[0]
    return m, (down, cq, ckv, qr, kp, kv, o, lse)


def mla_backward(du16, h16t, saved, w, gq, gkv, tables, emit):
    cos, sin = tables
    down, cq, ckv, qr, kp, kv, o, lse = saved
    scale = (MLA_NOPE + MLA_ROPE) ** -0.5
    emit("wo", mm_tn("mla_dwo", o, du16, "row", w["wo"].R, w["wo"].C))
    do = mm_nt("mla_do", du16, w["wo"], BF16)
    dq, dk, dv = softmax_attn_bwd("mla_attn_bwd", "mla", _heads(qr, 256), _heads(kp, 256), _heads(kv, 256, off=128, w=128),
                                  _heads(o, 128), _heads(do, 128), lse, scale)
    dq16, dkv16, dkr = mla_prep_bwd("mla_prep_bwd", dq, dk, dv, cos, sin)
    emit("uq", mm_tn("mla_duq", cq, dq16, "col", w["uq"].R, w["uq"].C))
    dcq = mm_nt("mla_dcq", dq16, w["uq"], F32)
    emit("ukv", mm_tn("mla_dukv", ckv, dkv16, "col", w["ukv"].R, w["ukv"].C))
    dckv = mm_nt("mla_dckv", dkv16, w["ukv"], F32)
    ddown, dgq, dgkv = rms_bwd("mla_rms_bwd", down, dcq, dckv, dkr, gq, gkv)
    emit("down", mm_tn("mla_ddown", h16t, ddown, "row", w["down"].R, w["down"].C, transposed=True))
    dh = mm_nt("mla_dh", ddown, w["down"], F32)
    return dh, (dgq, dgkv)


def qkv_forward(kind, h16, w, bias=None):
    qkv = mm_nn(kind + "_qkv", h16, w["qkv"], [BF16])[0]
    q, k, v = _heads(qkv, 128), _heads(qkv, 128, HEADS), _heads(qkv, 128, 2 * HEADS)
    scale = HEAD_DIM ** -0.5
    if kind == "sb":
        o, lse = stick_attn_fwd("sb_attn", q, k, v, scale)
    else:
        o, lse = softmax_attn_fwd("ca_attn", "ca", q, k, v, scale, bias)
    m = mm_nn(kind + "_wo", o, w["wo"], [F32])[0]
    return m, (qkv, o, lse)


def qkv_backward(kind, du16, h16t, saved, w, bias=None):
    qkv, o, lse = saved
    q, k, v = _heads(qkv, 128), _heads(qkv, 128, HEADS), _heads(qkv, 128, 2 * HEADS)
    scale = HEAD_DIM ** -0.5
    g = {"wo": mm_tn(kind + "_dwo", o, du16, "row", w["wo"].R, w["wo"].C)}
    do = mm_nt(kind + "_do", du16, w["wo"], BF16)
    dbias = None
    if kind == "sb":
        dq, dk, dv = stick_attn_bwd("sb_attn_bwd", q, k, v, _heads(do, 128), lse, scale)
    else:
        dq, dk, dv, dbias = softmax_attn_bwd("ca_attn_bwd", "ca", q, k, v, _heads(o, 128), _heads(do, 128), lse,
                                             scale, bias)
    dqkv = jnp.concatenate([dq, dk, dv], axis=1).astype(BF16)
    g["qkv"] = mm_tn(kind + "_dqkv", h16t, dqkv, "col", w["qkv"].R, w["qkv"].C, transposed=True)
    dh = mm_nt(kind + "_dh", dqkv, w["qkv"], F32)
    return dh, g, dbias


def mlp_forward(h16, w):
    a, z, zt = mm_nn("ffn_in", h16, w["w_in"], [F32, BF16, BF16], epilogue=_relu2_epilogue,
                     transposed=(False, False, True))
    f = mm_nn("ffn_out", z, w["w_out"], [F32])[0]
    return f, (a, zt)


def mlp_backward(du16, h16t, saved, w):
    a, zt = saved
    da = mm_nt("ffn_da", du16, w["w_out"], BF16, epilogue=_mulrelu_epilogue, extra=a)
    g = {"w_out": mm_tn("ffn_dwout", zt, du16, "row", w["w_out"].R, w["w_out"].C, transposed=True)}
    dh = mm_nt("ffn_dh", da, w["w_in"], F32)
    g["w_in"] = mm_tn("ffn_dwin", h16t, da, "col", w["w_in"].R, w["w_in"].C, transposed=True)
    return dh, g


WEIGHTS = ("ln_mix_g", "ln_mix_b", "ln_ffn_g", "ln_ffn_b", "ffn_w_in", "ffn_w_out", "mla_w_down", "mla_q_norm_g",
           "mla_w_uq", "mla_kv_norm_g", "mla_w_ukv", "mla_w_o", "sb_w_qkv", "sb_w_o", "ca_w_qkv", "ca_rel_bias",
           "ca_w_o")
MIXERS = ("mla", "sb", "ca")
LAYER_WEIGHTS = {
    "mla": (("down", "mla_w_down", "row"), ("uq", "mla_w_uq", "col"), ("ukv", "mla_w_ukv", "col"),
            ("wo", "mla_w_o", "row")),
    "sb": (("qkv", "sb_w_qkv", "col"), ("wo", "sb_w_o", "row")),
    "ca": (("qkv", "ca_w_qkv", "col"), ("wo", "ca_w_o", "row")),
    "ffn": (("w_in", "ffn_w_in", "col"), ("w_out", "ffn_w_out", "row")),
}
PAD = {"mla_w_down": pad_w_down, "mla_w_uq": pad_w_uq}
UNPAD = {"mla_w_down": unpad_w_down, "mla_w_uq": unpad_w_uq}


def _pack_rows(vectors):
    flat = jnp.concatenate([v.reshape(-1) for v in vectors])
    n = flat.shape[0]
    rows = -(-n // 1024) * 8
    offsets = np.cumsum([0] + [int(np.prod(v.shape)) for v in vectors])
    return jnp.pad(flat, (0, rows * 128 - n)).reshape(rows, 128), offsets


def _part(i, part):
    group, idx = (MIXERS[i % 3], i // 3) if part == "mix" else ("ffn", i)
    return [(key, name, how, idx) for key, name, how in LAYER_WEIGHTS[group]]


def kernel(x, ln_mix_g, ln_mix_b, ln_ffn_g, ln_ffn_b, ffn_w_in, ffn_w_out, mla_w_down, mla_q_norm_g, mla_w_uq, mla_kv_norm_g, mla_w_ukv, mla_w_o, sb_w_qkv, sb_w_o, ca_w_qkv, ca_rel_bias, ca_w_o, loss_target, m_ln_mix_g, m_ln_mix_b, m_ln_ffn_g, m_ln_ffn_b, m_ffn_w_in, m_ffn_w_out, m_mla_w_down, m_mla_q_norm_g, m_mla_w_uq, m_mla_kv_norm_g, m_mla_w_ukv, m_mla_w_o, m_sb_w_qkv, m_sb_w_o, m_ca_w_qkv, m_ca_rel_bias, m_ca_w_o, v_ln_mix_g, v_ln_mix_b, v_ln_ffn_g, v_ln_ffn_b, v_ffn_w_in, v_ffn_w_out, v_mla_w_down, v_mla_q_norm_g, v_mla_w_uq, v_mla_kv_norm_g, v_mla_w_ukv, v_mla_w_o, v_sb_w_qkv, v_sb_w_o, v_ca_w_qkv, v_ca_rel_bias, v_ca_w_o):
    w = dict(zip(WEIGHTS, (ln_mix_g, ln_mix_b, ln_ffn_g, ln_ffn_b, ffn_w_in, ffn_w_out, mla_w_down, mla_q_norm_g,
                           mla_w_uq, mla_kv_norm_g, mla_w_ukv, mla_w_o, sb_w_qkv, sb_w_o, ca_w_qkv, ca_rel_bias,
                           ca_w_o)))
    mom = dict(zip(WEIGHTS, (m_ln_mix_g, m_ln_mix_b, m_ln_ffn_g, m_ln_ffn_b, m_ffn_w_in, m_ffn_w_out, m_mla_w_down,
                             m_mla_q_norm_g, m_mla_w_uq, m_mla_kv_norm_g, m_mla_w_ukv, m_mla_w_o, m_sb_w_qkv,
                             m_sb_w_o, m_ca_w_qkv, m_ca_rel_bias, m_ca_w_o)))
    var = dict(zip(WEIGHTS, (v_ln_mix_g, v_ln_mix_b, v_ln_ffn_g, v_ln_ffn_b, v_ffn_w_in, v_ffn_w_out, v_mla_w_down,
                             v_mla_q_norm_g, v_mla_w_uq, v_mla_kv_norm_g, v_mla_w_ukv, v_mla_w_o, v_sb_w_qkv,
                             v_sb_w_o, v_ca_w_qkv, v_ca_rel_bias, v_ca_w_o)))
    S, D = x.shape[1], x.shape[2]
    xi, yi, ci = _place()
    core = ci.astype(jnp.int32).reshape(1)
    chip = (2 * xi + yi).astype(jnp.int32).reshape(1)
    me = 4 * xi + 2 * yi + ci
    tables = rope_tables(S)
    n_mla = mla_w_down.shape[0]
    lat = MLA_Q_LORA // N_DEV

    gains = jnp.pad(jnp.stack([mla_q_norm_g.reshape(-1), mla_kv_norm_g.reshape(-1)]), ((0, 6), (0, 128 - n_mla * lat)))
    gains = all_gather("ag_gains", [gains])[0]

    def full_gain(row, slot):
        return gains[:, row, slot * lat:(slot + 1) * lat].reshape(-1)

    first_matmul = {"mla": "mla_down", "sb": "sb_qkv", "ca": "ca_qkv"}

    def post_gather(i, part):
        kind = MIXERS[i % 3]
        specs = _part(i, part)
        shards = [PAD.get(name, lambda a: a)(w[name][idx]).astype(BF16) for _, name, _, idx in specs]
        if part == "mix":
            wants = ([MIXERS[(i - 1) % 3] + "_wo", "ffn_in"], "ffn_out") if i > 0 else ([None], None)
            return [(specs, gather_jobs(f"ag_mix{i}", shards, *wants))]
        w_in = (["ffn_out", first_matmul[kind]], kind + "_attn") if i > 0 else (["mla_down", "mla_uq"], "mla_ukv")
        return [(specs[:1], gather_jobs(f"ag_w_in{i}", shards[:1], *w_in)),
                (specs[1:], gather_jobs(f"ag_w_out{i}", shards[1:], [kind + "_attn"], kind + "_wo"))]

    def gathered(posted):
        return {key: Weight(how, g) for specs, future in posted for (key, _, how, _), g in zip(specs, future.get())}

    long_attention = "sb_attn_bwd"

    def post_scatter(i, part, g):
        kind = MIXERS[i % 3]
        specs = _part(i, part)
        n = len(specs)
        grads_of = lambda sp: [g[key] for key, _, _, _ in sp]
        if part == "ffn":
            w_in_rides = long_attention if MIXERS[(i - 1) % 3] == "sb" and i > 0 else kind + "_attn_bwd"
            return [(specs[:1], scatter_jobs(f"rs_w_in{i}", grads_of(specs[:1]), core, chip, kind + "_dwo",
                                             [([0], w_in_rides)])),
                    (specs[1:], scatter_jobs(f"rs_w_out{i}", grads_of(specs[1:]), core, chip, kind + "_dwo",
                                             [([0], kind + "_attn_bwd")]))]
        wants = ("ffn_da", [(list(range(n - 1)), "ffn_dwout"), ([n - 1], "ffn_dh")])
        return [(specs, scatter_jobs(f"rs_mix{i}", grads_of(specs), core, chip, *wants))]

    SCHED.pending.clear()
    bias = rel_bias_blocks("ca_bias", ca_rel_bias[0], _att_tiles("ca", S)[1])

    h, h16 = x[0], x[0].astype(BF16)
    h16t = transpose("x_t", h16)
    saved, layers = [], []
    mix_w, ffn_w = post_gather(0, "mix"), None
    for i in range(DEPTH):
        kind, slot = MIXERS[i % 3], i // 3
        lw = gathered(mix_w)
        if i == 0:
            ffn_w = post_gather(0, "ffn")
        this_ffn = ffn_w
        if i + 1 < DEPTH:
            mix_w, ffn_w = post_gather(i + 1, "mix"), post_gather(i + 1, "ffn")
        if kind == "mla":
            mix, s_mix = mla_forward(h16, lw, full_gain(0, slot), full_gain(1, slot), tables)
        else:
            mix, s_mix = qkv_forward(kind, h16, lw, bias if kind == "ca" else None)
        y, y16, y16t, xh1, rs1 = ln_fwd("ln_mix", h, mix, ln_mix_g[i], ln_mix_b[i])
        lw.update(gathered(this_ffn))
        f, s_mlp = mlp_forward(y16, lw)
        y2, y2_16, y2_16t, xh2, rs2 = ln_fwd("ln_ffn", y, f, ln_ffn_g[i], ln_ffn_b[i])
        saved.append((h16t, s_mix, xh1, rs1, y16t, s_mlp, xh2, rs2))
        layers.append(lw)
        h, h16, h16t = y2, y2_16, y2_16t
    sq, dy = loss_fwd_bwd("loss", h, loss_target[0])
    loss = 0.5 / D * lax.psum(sq[0, 0], ("x", "y", "c"))

    ga, gb = dy, None
    grads = {name: [None] * w[name].shape[0] for name in WEIGHTS}
    dbias = None
    scattered = []
    for i in reversed(range(DEPTH)):
        kind, slot = MIXERS[i % 3], i // 3
        lw = layers[i]
        h16_in, s_mix, xh1, rs1, y16, s_mlp, xh2, rs2 = saved[i]
        du, du16, grads["ln_ffn_g"][i], grads["ln_ffn_b"][i] = ln_bwd("ln_ffn_bwd", ga, gb, xh2, rs2, ln_ffn_g[i])
        dh_mlp, g_mlp = mlp_backward(du16, y16, s_mlp, lw)
        scattered += post_scatter(i, "ffn", g_mlp)
        du, du16, grads["ln_mix_g"][i], grads["ln_mix_b"][i] = ln_bwd("ln_mix_bwd", du, dh_mlp, xh1, rs1, ln_mix_g[i])
        if kind == "mla":
            rides = {"wo": ("mla_attn_bwd", "mla_duq"), "uq": ("mla_dcq", "mla_dukv"), "ukv": ("mla_dckv", "mla_ddown"),
                     "down": ("mla_dh", None)}
            by_key = {spec[0]: spec for spec in _part(i, "mix")}
            g_mix = {}

            def emit_alone(key, grad):
                sibling, chips = rides[key]
                scattered.append(([by_key[key]], scatter_jobs(f"rs_{key}{i}", [grad], core, chip, sibling,
                                                              [([0], chips)])))

            dh_mix, (dgq, dgkv) = mla_backward(du16, h16_in, s_mix, lw, full_gain(0, slot), full_gain(1, slot),
                                               tables, emit_alone if i == 0 else g_mix.__setitem__)
            if i > 0:
                scattered += post_scatter(i, "mix", g_mix)
            grads["mla_q_norm_g"][slot], grads["mla_kv_norm_g"][slot] = dgq, dgkv
        else:
            dh_mix, g_mix, db = qkv_backward(kind, du16, h16_in, s_mix, lw, bias if kind == "ca" else None)
            dbias = db if kind == "ca" else dbias
            scattered += post_scatter(i, "mix", g_mix)
        ga, gb = du, dh_mix
    grad_x = axpy("grad_x", ga, gb)[None]
    SCHED.flush()
    for specs, future in scattered:
        for (_, name, _, idx), g in zip(specs, future.get()):
            grads[name][idx] = UNPAD.get(name, lambda a: a)(g)
    grads["ca_rel_bias"][0] = rel_bias_grad("ca_bias_grad", dbias)

    small = ("ln_mix_g", "ln_mix_b", "ln_ffn_g", "ln_ffn_b", "ca_rel_bias", "mla_q_norm_g", "mla_kv_norm_g")
    packed, offsets = _pack_rows([g for name in small for g in grads[name]])
    total = sum_devices("sum_small", all_gather("ag_small", [packed])[0]).reshape(-1)
    pos = 0
    for name in small:
        for idx, g in enumerate(grads[name]):
            full = total[offsets[pos]:offsets[pos + 1]]
            pos += 1
            if name in ("mla_q_norm_g", "mla_kv_norm_g"):
                full = lax.dynamic_slice(full, (me * lat,), (lat,))
            grads[name][idx] = full.reshape(w[name].shape[1:])

    g_out, d_out, m_out, v_out = [], [], [], []
    for name in WEIGHTS:
        g = jnp.stack(grads[name])
        delta, new_m, new_v = adamw("adamw_" + name, w[name], g, mom[name], var[name])
        g_out.append(g)
        d_out.append(delta)
        m_out.append(new_m)
        v_out.append(new_v)
    return (loss, grad_x, *g_out, *d_out, *m_out, *v_out)
```

```python
import functools

import numpy as np
import jax
import jax.numpy as jnp
from jax import lax
from jax.experimental import pallas as pl
from jax.experimental.pallas import tpu as pltpu

F32 = jnp.float32
BF16 = jnp.bfloat16
MESH = pl.DeviceIdType.MESH
N_DEV = 8

DEPTH = 4
CHUNK = 64
CHUNK_SHIFT = 6
HEADS = 16
HEAD_DIM = 128
MLA_Q_LORA = 512
MLA_KV_LORA = 512
MLA_NOPE = 128
MLA_ROPE = 64
ROPE_THETA = 10000.0
CA_LEFT_CHUNKS = 8
REL_CLIP_LEFT = 128
REL_TABLE = REL_CLIP_LEFT + CHUNK
LN_EPS = 1e-5
RMS_EPS = 1e-6
ALPHA = (2.0 * DEPTH) ** 0.25
NEG = -1e30
ADAM_LR = 0.001
ADAM_B1 = 0.9
ADAM_B2 = 0.999
ADAM_EPS = 1e-08
ADAM_WD = 0.01
ADAM_STEP = 10

V7X_VMEM_BYTES = 64 * 1024 * 1024
VMEM_LIMIT = V7X_VMEM_BYTES - 8 * 1024 * 1024
ATT_TQ = 512
ATT_TK = 256
ATT_G = 2


def _params(*sem):
    return pltpu.CompilerParams(dimension_semantics=sem if sem else None, vmem_limit_bytes=VMEM_LIMIT)


HBM = pl.BlockSpec(memory_space=pl.ANY)


class Job:
    def __init__(self, name, want, operands, out_shape, n_copies, copies, done, aliases=None):
        self.name, self.want, self.operands, self.out_shape = name, want, list(operands), list(out_shape)
        self.n_copies, self.copies, self.done, self.aliases = n_copies, copies, done, dict(aliases or {})

    def sems(self):
        return [pltpu.SemaphoreType.DMA((self.n_copies,)), pltpu.SemaphoreType.DMA((self.n_copies,))]


class Scheduler:
    def __init__(self):
        self.pending = []

    def post(self, job):
        self.pending.append(job)

    def take(self, name):
        mine = [job for job in self.pending if job.want is not None and job.want in name]
        self.pending = [job for job in self.pending if job not in mine]
        return mine

    def flush(self):
        while self.pending:
            job = self.pending.pop(0)
            n_in, n_out = len(job.operands), len(job.out_shape)

            def body(*refs, job=job, n_in=n_in, n_out=n_out):
                cps = job.copies(refs[:n_in], refs[n_in:n_in + n_out], refs[-2], refs[-1])
                for cp in cps:
                    cp.start()
                for cp in cps:
                    cp.wait()

            outs = pl.pallas_call(
                body, name=job.name, in_specs=[HBM] * n_in, out_specs=[HBM] * n_out, out_shape=job.out_shape,
                scratch_shapes=job.sems(), input_output_aliases=job.aliases)(*job.operands)
            job.done(list(outs))


SCHED = Scheduler()


def _call(body, operands, *, name, grid, in_specs, out_specs, out_shape, scratch_shapes=(), semantics):
    jobs = SCHED.take(name)
    if not jobs:
        return list(pl.pallas_call(
            body, name=name, grid=grid, in_specs=list(in_specs), out_specs=list(out_specs), out_shape=list(out_shape),
            scratch_shapes=list(scratch_shapes), compiler_params=_params(*semantics))(*operands))
    n_in, n_out, n_scr = len(operands), len(out_shape), len(scratch_shapes)
    j_in = np.cumsum([0] + [len(job.operands) for job in jobs])
    j_out = np.cumsum([0] + [len(job.out_shape) for job in jobs])
    a, b = n_in, n_in + int(j_in[-1])
    c, d = b + n_out, b + n_out + int(j_out[-1])

    def carrying(*refs):
        def copies():
            sems = refs[d + n_scr:]
            return [cp for k, job in enumerate(jobs)
                    for cp in job.copies(refs[a + j_in[k]:a + j_in[k + 1]], refs[c + j_out[k]:c + j_out[k + 1]],
                                         sems[2 * k], sems[2 * k + 1])]

        ids = [pl.program_id(k) for k in range(len(grid))]
        first = functools.reduce(jnp.logical_and, [i == 0 for i in ids])
        last = functools.reduce(jnp.logical_and, [i == g - 1 for i, g in zip(ids, grid)])

        @pl.when(first)
        def _():
            for cp in copies():
                cp.start()

        body(*refs[:a], *refs[b:c], *refs[d:d + n_scr])

        @pl.when(last)
        def _():
            for cp in copies():
                cp.wait()

    aliases = {n_in + int(j_in[k]) + i: n_out + int(j_out[k]) + o for k, job in enumerate(jobs)
               for i, o in job.aliases.items()}
    outs = pl.pallas_call(
        carrying, name=name + "_carry", grid=grid, in_specs=list(in_specs) + [HBM] * int(j_in[-1]),
        out_specs=list(out_specs) + [HBM] * int(j_out[-1]),
        out_shape=list(out_shape) + [s for job in jobs for s in job.out_shape],
        scratch_shapes=list(scratch_shapes) + [s for job in jobs for s in job.sems()],
        input_output_aliases=aliases,
        compiler_params=_params(*(["arbitrary"] * len(grid))))(*operands, *[o for job in jobs for o in job.operands])
    for k, job in enumerate(jobs):
        job.done(list(outs[n_out + int(j_out[k]):n_out + int(j_out[k + 1])]))
    return list(outs[:n_out])


def _matmul(name, a, b, *, contract, grid, a_spec, b_spec, o_specs, out_shape, acc_shape,
            epilogue=None, extra=(), extra_specs=()):
    nk = grid[2]
    n_extra = len(extra)
    n_out = len(out_shape)

    def finish(acc, e_refs, o_refs):
        outs = epilogue(acc, *[e[...] for e in e_refs]) if epilogue else (acc,)
        for o_ref, val in zip(o_refs, outs):
            o_ref[...] = val.astype(o_ref.dtype)

    def product(a_ref, b_ref):
        return lax.dot_general(a_ref[...], b_ref[...], (contract, ((), ())), preferred_element_type=F32)

    def body_single(*refs):
        finish(product(refs[0], refs[1]), refs[2:2 + n_extra], refs[2 + n_extra:2 + n_extra + n_out])

    def body(*refs):
        a_ref, b_ref = refs[0], refs[1]
        acc_ref = refs[-1]
        k = pl.program_id(2)

        @pl.when(k == 0)
        def _():
            acc_ref[...] = jnp.zeros_like(acc_ref)

        acc_ref[...] += product(a_ref, b_ref)

        @pl.when(k == nk - 1)
        def _():
            finish(acc_ref[...], refs[2:2 + n_extra], refs[2 + n_extra:2 + n_extra + n_out])

    return _call(
        body_single if nk == 1 else body, [a, b, *extra], name=name, grid=grid,
        in_specs=[a_spec, b_spec, *extra_specs], out_specs=o_specs, out_shape=out_shape,
        scratch_shapes=[] if nk == 1 else [pltpu.VMEM(acc_shape, F32)],
        semantics=("parallel", "parallel", "arbitrary"))


MATMUL_BLOCK_BYTES = 40 * 1024 * 1024
MAX_TK = 2048
MULTI_TK = 512


def _fit_tn(n, tm, tk, nk, out_bytes):
    cands = sorted({n} | {t for t in range(128, n, 128) if n % t == 0}, reverse=True)
    for tn in cands:
        need = 2 * 2 * (tm * tk + tk * tn) + 2 * tm * tn * out_bytes + (tm * tn * 4 if nk > 1 else 0) + tm * tn * 4
        if need <= MATMUL_BLOCK_BYTES:
            return tn
    return cands[-1]


def _itemsize(dtypes):
    return sum(jnp.dtype(d).itemsize for d in dtypes)


def _tile(n, pref):
    if n <= pref:
        return n
    t = pref
    while t >= 128:
        if n % t == 0 and t % 128 == 0:
            return t
        t -= 128
    return n


class Weight:
    def __init__(self, kind, arr):
        self.kind = kind
        self.arr = arr
        self.R, self.C = arr.shape[1], arr.shape[2]

    @property
    def two_d(self):
        return self.arr.reshape(N_DEV * self.R, self.C)


def mm_nn(name, a, w, out_dtypes, epilogue=None, transposed=()):
    M, K = a.shape
    tm = M
    tk = K if K <= MAX_TK else _tile(K, MULTI_TK)
    nk = K // tk
    if w.kind == "row":
        b = w.two_d
        N = w.C
        tn = _fit_tn(N, tm, tk, nk, _itemsize(out_dtypes))
        b_spec = pl.BlockSpec((tk, tn), lambda i, j, k: (k, j))
    else:
        b = w.arr
        N = N_DEV * w.C
        tn = _fit_tn(w.C, tm, tk, nk, _itemsize(out_dtypes))
        per = w.C // tn
        b_spec = pl.BlockSpec((None, tk, tn), lambda i, j, k: (j // per, k, j % per))
    grid = (M // tm, N // tn, nk)
    flip = [t < len(transposed) and transposed[t] for t in range(len(out_dtypes))]
    return _matmul(
        name, a, b, contract=((1,), (0,)), grid=grid,
        a_spec=pl.BlockSpec((tm, tk), lambda i, j, k: (i, k)), b_spec=b_spec,
        o_specs=[pl.BlockSpec((tn, tm), lambda i, j, k: (j, i)) if f else pl.BlockSpec((tm, tn), lambda i, j, k: (i, j))
                 for f in flip],
        out_shape=[jax.ShapeDtypeStruct((N, M) if f else (M, N), d) for f, d in zip(flip, out_dtypes)],
        acc_shape=(tm, tn), epilogue=epilogue)


def mm_nt(name, dy, w, out_dtype, epilogue=None, extra=None):
    M, N = dy.shape
    tm = M
    out_bytes = jnp.dtype(out_dtype).itemsize + (0 if extra is None else extra.dtype.itemsize)
    if w.kind == "row":
        b = w.two_d
        kin = N_DEV * w.R
        tk = N if N <= MAX_TK else _tile(N, MULTI_TK)
        tn = _fit_tn(kin, tm, tk, N // tk, out_bytes)
        b_spec = pl.BlockSpec((tn, tk), lambda i, j, k: (j, k))
    else:
        b = w.arr
        kin = w.R
        tk = _tile(w.C, MULTI_TK)
        per = w.C // tk
        tn = _fit_tn(kin, tm, tk, N // tk, out_bytes)
        b_spec = pl.BlockSpec((None, tn, tk), lambda i, j, k: (k // per, j, k % per))
    grid = (M // tm, kin // tn, N // tk)
    o_spec = pl.BlockSpec((tm, tn), lambda i, j, k: (i, j))
    return _matmul(
        name, dy, b, contract=((1,), (1,)), grid=grid,
        a_spec=pl.BlockSpec((tm, tk), lambda i, j, k: (i, k)), b_spec=b_spec, o_specs=[o_spec],
        out_shape=[jax.ShapeDtypeStruct((M, kin), out_dtype)], acc_shape=(tm, tn), epilogue=epilogue,
        extra=() if extra is None else (extra,), extra_specs=() if extra is None else (o_spec,))[0]


TRANSPOSE_TILE = 512


def transpose(name, x):
    S, n = x.shape
    ts, tn = _tile(S, TRANSPOSE_TILE), _tile(n, TRANSPOSE_TILE)

    def body(x_ref, o_ref):
        o_ref[...] = x_ref[...].T

    return pl.pallas_call(
        body, name=name, grid=(S // ts, n // tn), in_specs=[pl.BlockSpec((ts, tn), lambda i, j: (i, j))],
        out_specs=pl.BlockSpec((tn, ts), lambda i, j: (j, i)), out_shape=jax.ShapeDtypeStruct((n, S), x.dtype),
        compiler_params=_params("parallel", "parallel"),
    )(x)


def mm_tn(name, x, dy, kind, R, C, transposed=False):
    if not transposed:
        x = transpose(name + "_t", x)
    kin, S = x.shape
    N = dy.shape[1]
    tk = S if S <= MAX_TK else _tile(S, MULTI_TK)
    nk = S // tk
    if kind == "col":
        tm = kin
        tn = _fit_tn(C, tm, tk, nk, 2)
        per = C // tn
        grid = (1, N // tn, nk)
        o_spec = pl.BlockSpec((None, None, tm, tn), lambda i, j, k: ((j // per) % 2, (j // per) // 2, 0, j % per))
    else:
        tm = R
        tn = _fit_tn(N, tm, tk, nk, 2)
        grid = (N_DEV, N // tn, nk)
        o_spec = pl.BlockSpec((None, None, tm, tn), lambda i, j, k: (i % 2, i // 2, 0, j))
    return _matmul(
        name, x, dy, contract=((1,), (0,)), grid=grid,
        a_spec=pl.BlockSpec((tm, tk), lambda i, j, k: (i, k)),
        b_spec=pl.BlockSpec((tk, tn), lambda i, j, k: (k, j)), o_specs=[o_spec],
        out_shape=[jax.ShapeDtypeStruct((2, 4, R, C), BF16)], acc_shape=(tm, tn))[0]


def _relu2_epilogue(acc):
    r = jnp.maximum(acc, 0.0)
    z = (r * r).astype(BF16)
    return acc, z, z.T


def _mulrelu_epilogue(acc, a):
    return (acc * (2.0 * jnp.maximum(a, 0.0)),)


ROW_TILE = 256


def ln_fwd(name, h, m, g, b):
    S, D = h.shape
    ts = _tile(S, ROW_TILE)

    def body(h_ref, m_ref, g_ref, b_ref, y_ref, y16_ref, yt_ref, xh_ref, rs_ref):
        u = ALPHA * h_ref[...] + m_ref[...]
        mu = jnp.mean(u, axis=-1, keepdims=True)
        d = u - mu
        var = jnp.mean(d * d, axis=-1, keepdims=True)
        rstd = lax.rsqrt(var + LN_EPS)
        xh = d * rstd
        y = xh * g_ref[...] + b_ref[...]
        y16 = y.astype(BF16)
        y_ref[...] = y
        y16_ref[...] = y16
        yt_ref[...] = y16.T
        xh_ref[...] = xh
        rs_ref[...] = jnp.broadcast_to(rstd, rs_ref.shape)

    row = pl.BlockSpec((ts, D), lambda i: (i, 0))
    vec = pl.BlockSpec((1, D), lambda i: (0, 0))
    return pl.pallas_call(
        body, name=name, grid=(S // ts,), in_specs=[row, row, vec, vec],
        out_specs=[row, row, pl.BlockSpec((D, ts), lambda i: (0, i)), row, pl.BlockSpec((ts, 128), lambda i: (i, 0))],
        out_shape=[jax.ShapeDtypeStruct((S, D), F32), jax.ShapeDtypeStruct((S, D), BF16),
                   jax.ShapeDtypeStruct((D, S), BF16), jax.ShapeDtypeStruct((S, D), F32),
                   jax.ShapeDtypeStruct((S, 128), F32)],
        compiler_params=_params("parallel"),
    )(h, m, g.reshape(1, D), b.reshape(1, D))


def ln_bwd(name, ga, gb, xhat, rstd, g):
    S, D = xhat.shape
    ts = _tile(S, ROW_TILE)
    two = gb is not None

    def body(*refs):
        if two:
            ga_ref, gb_ref, xh_ref, rs_ref, g_ref, du_ref, du16_ref, dg_ref, db_ref = refs
            dy = ALPHA * ga_ref[...] + gb_ref[...]
        else:
            ga_ref, xh_ref, rs_ref, g_ref, du_ref, du16_ref, dg_ref, db_ref = refs
            dy = ga_ref[...]
        xh = xh_ref[...]

        @pl.when(pl.program_id(0) == 0)
        def _():
            dg_ref[...] = jnp.zeros_like(dg_ref)
            db_ref[...] = jnp.zeros_like(db_ref)

        dg_ref[...] += jnp.sum(dy * xh, axis=0, keepdims=True)
        db_ref[...] += jnp.sum(dy, axis=0, keepdims=True)
        dxh = dy * g_ref[...]
        m1 = jnp.mean(dxh, axis=-1, keepdims=True)
        m2 = jnp.mean(dxh * xh, axis=-1, keepdims=True)
        du = rs_ref[:, 0:1] * (dxh - m1 - xh * m2)
        du_ref[...] = du
        du16_ref[...] = du.astype(BF16)

    row = pl.BlockSpec((ts, D), lambda i: (i, 0))
    vec = pl.BlockSpec((1, D), lambda i: (0, 0))
    stat = pl.BlockSpec((ts, 128), lambda i: (i, 0))
    ins = [ga, gb, xhat, rstd, g.reshape(1, D)] if two else [ga, xhat, rstd, g.reshape(1, D)]
    in_specs = [row, row, row, stat, vec] if two else [row, row, stat, vec]
    return pl.pallas_call(
        body, name=name, grid=(S // ts,), in_specs=in_specs, out_specs=[row, row, vec, vec],
        out_shape=[jax.ShapeDtypeStruct((S, D), F32), jax.ShapeDtypeStruct((S, D), BF16),
                   jax.ShapeDtypeStruct((1, D), F32), jax.ShapeDtypeStruct((1, D), F32)],
        compiler_params=_params("arbitrary"),
    )(*ins)


def loss_fwd_bwd(name, y, target):
    S, D = y.shape
    ts = _tile(S, ROW_TILE)

    def body(y_ref, t_ref, l_ref, dy_ref):
        @pl.when(pl.program_id(0) == 0)
        def _():
            l_ref[...] = jnp.zeros_like(l_ref)

        e = y_ref[...] - t_ref[...]
        l_ref[...] += jnp.sum(e * e)
        dy_ref[...] = e * (1.0 / D)

    row = pl.BlockSpec((ts, D), lambda i: (i, 0))
    return pl.pallas_call(
        body, name=name, grid=(S // ts,), in_specs=[row, row],
        out_specs=[pl.BlockSpec((1, 128), lambda i: (0, 0)), row],
        out_shape=[jax.ShapeDtypeStruct((1, 128), F32), jax.ShapeDtypeStruct((S, D), F32)],
        compiler_params=_params("arbitrary"),
    )(y, target)


def axpy(name, ga, gb):
    S, D = ga.shape
    ts = _tile(S, ROW_TILE)

    def body(a_ref, b_ref, o_ref):
        o_ref[...] = ALPHA * a_ref[...] + b_ref[...]

    row = pl.BlockSpec((ts, D), lambda i: (i, 0))
    return pl.pallas_call(body, name=name, grid=(S // ts,), in_specs=[row, row], out_specs=row,
                          out_shape=jax.ShapeDtypeStruct((S, D), F32), compiler_params=_params("parallel"))(ga, gb)


def rms_fwd(name, down, gq, gkv):
    S = down.shape[0]
    ts = _tile(S, ROW_TILE)
    L = MLA_Q_LORA

    def body(d_ref, gq_ref, gkv_ref, q_ref, kv_ref):
        for lo, g_ref, o_ref in ((0, gq_ref, q_ref), (L, gkv_ref, kv_ref)):
            x = d_ref[:, lo:lo + L]
            r = lax.rsqrt(jnp.mean(x * x, axis=-1, keepdims=True) + RMS_EPS)
            o_ref[...] = (x * r * g_ref[...]).astype(BF16)

    vec = pl.BlockSpec((1, L), lambda i: (0, 0))
    out = pl.BlockSpec((ts, L), lambda i: (i, 0))
    return pl.pallas_call(
        body, name=name, grid=(S // ts,), in_specs=[pl.BlockSpec((ts, down.shape[1]), lambda i: (i, 0)), vec, vec],
        out_specs=[out, out], out_shape=[jax.ShapeDtypeStruct((S, L), BF16)] * 2, compiler_params=_params("parallel"),
    )(down, gq.reshape(1, L), gkv.reshape(1, L))


def rms_bwd(name, down, dq, dkv, dkr, gq, gkv):
    S, W = down.shape
    ts = _tile(S, ROW_TILE)
    L = MLA_Q_LORA

    def body(d_ref, dq_ref, dkv_ref, dkr_ref, gq_ref, gkv_ref, o_ref, dgq_ref, dgkv_ref):
        @pl.when(pl.program_id(0) == 0)
        def _():
            dgq_ref[...] = jnp.zeros_like(dgq_ref)
            dgkv_ref[...] = jnp.zeros_like(dgkv_ref)

        for lo, dy_ref, g_ref, dg_ref in ((0, dq_ref, gq_ref, dgq_ref), (L, dkv_ref, gkv_ref, dgkv_ref)):
            x = d_ref[:, lo:lo + L]
            dy = dy_ref[...]
            r = lax.rsqrt(jnp.mean(x * x, axis=-1, keepdims=True) + RMS_EPS)
            dg_ref[...] += jnp.sum(dy * x * r, axis=0, keepdims=True)
            dyg = dy * g_ref[...]
            dx = r * dyg - x * (r * r * r) * jnp.mean(dyg * x, axis=-1, keepdims=True)
            o_ref[:, lo:lo + L] = dx.astype(BF16)
        o_ref[:, 2 * L:] = dkr_ref[...].astype(BF16)

    vec = pl.BlockSpec((1, L), lambda i: (0, 0))
    lat = pl.BlockSpec((ts, L), lambda i: (i, 0))
    full = pl.BlockSpec((ts, W), lambda i: (i, 0))
    return pl.pallas_call(
        body, name=name, grid=(S // ts,),
        in_specs=[full, lat, lat, pl.BlockSpec((ts, 128), lambda i: (i, 0)), vec, vec],
        out_specs=[full, vec, vec],
        out_shape=[jax.ShapeDtypeStruct((S, W), BF16), jax.ShapeDtypeStruct((1, L), F32), jax.ShapeDtypeStruct((1, L), F32)],
        compiler_params=_params("arbitrary"),
    )(down, dq, dkv, dkr, gq.reshape(1, L), gkv.reshape(1, L))


def rope_tables(S):
    half = MLA_ROPE // 2
    inv = (np.float32(ROPE_THETA) ** (-np.arange(half, dtype=np.float32) / np.float32(half))).astype(np.float32)
    ang = np.arange(S, dtype=np.float32)[:, None] * inv[None, :]
    cos, sin = np.cos(ang).astype(np.float32), np.sin(ang).astype(np.float32)
    z = np.zeros_like(cos)
    return (jnp.asarray(np.concatenate([cos, z, cos, z], 1)), jnp.asarray(np.concatenate([-sin, z, sin, z], 1)))


def _rot(x, cos, sin):
    return x * cos + pltpu.roll(x, 64, 1) * sin


def mla_prep_fwd(name, q, kv, down, cos, sin):
    S = q.shape[0]
    ts = _tile(S, ROW_TILE)

    def body(q_ref, kv_ref, kr_ref, c_ref, s_ref, qo_ref, ko_ref):
        c, s = c_ref[...], s_ref[...]
        key = _rot(kr_ref[...], c, s).astype(BF16)
        for h in range(HEADS):
            lo = 256 * h
            qo_ref[:, lo:lo + 128] = q_ref[:, lo:lo + 128].astype(BF16)
            qo_ref[:, lo + 128:lo + 256] = _rot(q_ref[:, lo + 128:lo + 256], c, s).astype(BF16)
            ko_ref[:, lo:lo + 128] = kv_ref[:, lo:lo + 128]
            ko_ref[:, lo + 128:lo + 256] = key

    heads = pl.BlockSpec((ts, HEADS * 256), lambda i: (i, 0))
    tab = pl.BlockSpec((ts, 128), lambda i: (i, 0))
    return pl.pallas_call(
        body, name=name, grid=(S // ts,),
        in_specs=[heads, heads, pl.BlockSpec((ts, 128), lambda i: (i, 2 * MLA_Q_LORA // 128)), tab, tab],
        out_specs=[heads, heads], out_shape=[jax.ShapeDtypeStruct(q.shape, BF16)] * 2,
        compiler_params=_params("parallel"),
    )(q, kv, down, cos, sin)


def mla_prep_bwd(name, dq, dk, dv, cos, sin):
    S = dq.shape[0]
    ts = _tile(S, ROW_TILE)

    def body(dq_ref, dk_ref, dv_ref, c_ref, s_ref, qo_ref, kvo_ref, kr_ref):
        c, s = c_ref[...], -s_ref[...]
        key = jnp.zeros((ts, 128), F32)
        for h in range(HEADS):
            lo = 256 * h
            qo_ref[:, lo:lo + 128] = dq_ref[:, lo:lo + 128].astype(BF16)
            qo_ref[:, lo + 128:lo + 256] = _rot(dq_ref[:, lo + 128:lo + 256], c, s).astype(BF16)
            kvo_ref[:, lo:lo + 128] = dk_ref[:, lo:lo + 128].astype(BF16)
            kvo_ref[:, lo + 128:lo + 256] = dv_ref[:, 128 * h:128 * h + 128].astype(BF16)
            key = key + dk_ref[:, lo + 128:lo + 256]
        kr_ref[...] = _rot(key, c, s)

    heads = pl.BlockSpec((ts, HEADS * 256), lambda i: (i, 0))
    tab = pl.BlockSpec((ts, 128), lambda i: (i, 0))
    return pl.pallas_call(
        body, name=name, grid=(S // ts,),
        in_specs=[heads, heads, pl.BlockSpec((ts, HEADS * 128), lambda i: (i, 0)), tab, tab],
        out_specs=[heads, heads, tab],
        out_shape=[jax.ShapeDtypeStruct(dq.shape, BF16), jax.ShapeDtypeStruct(dq.shape, BF16),
                   jax.ShapeDtypeStruct((S, 128), F32)],
        compiler_params=_params("parallel"),
    )(dq, dk, dv, cos, sin)


def _dot_nt(a, b):
    return lax.dot_general(a, b, (((1,), (1,)), ((), ())), preferred_element_type=F32)


def _dot_tn(a, b):
    return lax.dot_general(a, b, (((0,), (0,)), ((), ())), preferred_element_type=F32)


def _dot(a, b):
    return jnp.dot(a, b, preferred_element_type=F32)


def _positions(i, j, TQ, TK):
    row = i * TQ + lax.broadcasted_iota(jnp.int32, (TQ, TK), 0)
    col = j * TK + lax.broadcasted_iota(jnp.int32, (TQ, TK), 1)
    return row, col


def _softmax_mask(mode, row, col):
    rc, cc = row >> CHUNK_SHIFT, col >> CHUNK_SHIFT
    if mode == "mla":
        return cc <= rc
    return (cc <= rc) & (cc >= rc - CA_LEFT_CHUNKS)


def _key_blocks(mode, i, TQ, TK):
    per = TQ // TK
    if mode == "ca":
        lo = jnp.maximum(i - (CA_LEFT_CHUNKS * CHUNK) // TK, 0)
        return lo, 0, i - lo + 1
    return 0, i * per, per


class HeadCols:
    def __init__(self, arr, width, index, off=0, w=None):
        self.arr, self.width, self.index, self.off = arr, width, index, off
        self.w = width if w is None else w

    def rows(self, T):
        return pl.BlockSpec((T, ATT_G * self.width), lambda p, i: (i, self.index(p)))

    def full(self, S):
        return pl.BlockSpec((S, ATT_G * self.width), lambda p, i: (0, self.index(p)))

    def lanes(self, g):
        lo = g * self.width + self.off
        return slice(lo, lo + self.w)


def _att_tiles(mode, S):
    tk = min(ATT_TK, S)
    return (tk if mode == "ca" else min(ATT_TQ, S)), tk


def _walk(lo, n, per, step, carry, descending=False):
    tail = [lo + n + d for d in range(per)]
    if descending:
        for j in reversed(tail):
            carry = step(j, carry, True)
        return lax.fori_loop(0, n, lambda t, c: step(lo + n - 1 - t, c, False), carry)
    carry = lax.fori_loop(0, n, lambda t, c: step(lo + t, c, False), carry)
    for j in tail:
        carry = step(j, carry, True)
    return carry


def softmax_attn_fwd(name, mode, q, k, v, scale, bias=None):
    S = q.arr.shape[0]
    TQ, TK = _att_tiles(mode, S)
    G, dv = ATT_G, v.w

    def body(*refs):
        if bias is not None:
            q_ref, k_ref, v_ref, b_ref, o_ref, lse_ref = refs
        else:
            q_ref, k_ref, v_ref, o_ref, lse_ref = refs
        i = pl.program_id(1)
        qs = [q_ref[:, q.lanes(g)] for g in range(G)]

        def block(g, j, carry, mask, ks):
            m, l, acc = carry
            s = _dot_nt(qs[g], k_ref[ks, k.lanes(g)]) * scale
            if bias is not None:
                s = s + b_ref[g, jnp.minimum(i - j, 2)]
            if mask is not None:
                s = jnp.where(mask, s, NEG)
            m_new = jnp.maximum(m, jnp.max(s, axis=-1, keepdims=True))
            a = jnp.exp(m - m_new)
            p = jnp.exp(s - m_new)
            if mask is not None:
                p = jnp.where(mask, p, 0.0)
            l = a * l + jnp.sum(p, axis=-1, keepdims=True)
            acc = a * acc + _dot(p.astype(BF16), v_ref[ks, v.lanes(g)])
            return m_new, l, acc

        def step(j, carry, masked):
            ks = pl.ds(pl.multiple_of(j * TK, TK), TK)
            mask = _softmax_mask(mode, *_positions(i, j, TQ, TK)) if masked or mode == "ca" else None
            return tuple(block(g, j, carry[g], mask, ks) for g in range(G))

        init = (jnp.full((TQ, 1), NEG, F32), jnp.zeros((TQ, 1), F32), jnp.zeros((TQ, dv), F32))
        lo, n, per = _key_blocks(mode, i, TQ, TK)
        if mode == "ca":
            out = lax.fori_loop(lo, lo + per, lambda j, c: step(j, c, True), (init,) * G)
        else:
            out = _walk(lo, n, per, step, (init,) * G)
        for g, (m, l, acc) in enumerate(out):
            o_ref[:, g * dv:(g + 1) * dv] = (acc / l).astype(BF16)
            lse_ref[:, g * 128:(g + 1) * 128] = jnp.broadcast_to(m + jnp.log(l), (TQ, 128))

    in_specs = [q.rows(TQ), k.full(S), v.full(S)]
    ins = [q.arr, k.arr, v.arr]
    if bias is not None:
        in_specs.append(pl.BlockSpec((G, 3, TK, TK), lambda p, i: (p, 0, 0, 0)))
        ins.append(bias)
    return _call(
        body, ins, name=name, grid=(HEADS // G, S // TQ), in_specs=in_specs,
        out_specs=[pl.BlockSpec((TQ, G * dv), lambda p, i: (i, p)), pl.BlockSpec((TQ, G * 128), lambda p, i: (i, p))],
        out_shape=[jax.ShapeDtypeStruct((S, HEADS * dv), BF16), jax.ShapeDtypeStruct((S, HEADS * 128), F32)],
        semantics=("parallel", "parallel"))


def softmax_attn_bwd(name, mode, q, k, v, o, do, lse, scale, bias=None):
    S = q.arr.shape[0]
    TQ, TK = _att_tiles(mode, S)
    G, dqk, dv = ATT_G, q.w, v.w

    def body(*refs):
        if bias is not None:
            q_ref, k_ref, v_ref, o_ref, do_ref, lse_ref, b_ref, dq_ref, dk_ref, dv_ref, db_ref = refs
        else:
            q_ref, k_ref, v_ref, o_ref, do_ref, lse_ref, dq_ref, dk_ref, dv_ref = refs
        i = pl.program_id(1)

        @pl.when(i == 0)
        def _():
            dk_ref[...] = jnp.zeros_like(dk_ref)
            dv_ref[...] = jnp.zeros_like(dv_ref)
            if bias is not None:
                db_ref[...] = jnp.zeros_like(db_ref)

        qs = [q_ref[:, q.lanes(g)] for g in range(G)]
        dos = [do_ref[:, do.lanes(g)] for g in range(G)]
        lses = [lse_ref[:, g * 128:g * 128 + 1] for g in range(G)]
        deltas = [jnp.sum(dos[g].astype(F32) * o_ref[:, o.lanes(g)].astype(F32), axis=-1, keepdims=True)
                  for g in range(G)]

        def block(g, j, dq, mask, ks):
            kb, vb = k_ref[ks, k.lanes(g)], v_ref[ks, v.lanes(g)]
            s = _dot_nt(qs[g], kb) * scale
            if bias is not None:
                slot = jnp.minimum(i - j, 2)
                s = s + b_ref[g, slot]
            p = jnp.exp(s - lses[g])
            if mask is not None:
                p = jnp.where(mask, p, 0.0)
            ds = p * (_dot_nt(dos[g], vb) - deltas[g])
            if bias is not None:
                db_ref[g, slot] += ds
            dsb = (ds * scale).astype(BF16)
            dk_ref[ks, g * dqk:(g + 1) * dqk] += _dot_tn(dsb, qs[g])
            dv_ref[ks, g * dv:(g + 1) * dv] += _dot_tn(p.astype(BF16), dos[g])
            return dq + _dot(dsb, kb)

        def step(j, carry, masked):
            ks = pl.ds(pl.multiple_of(j * TK, TK), TK)
            mask = _softmax_mask(mode, *_positions(i, j, TQ, TK)) if masked or mode == "ca" else None
            return tuple(block(g, j, carry[g], mask, ks) for g in range(G))

        init = (jnp.zeros((TQ, dqk), F32),) * G
        lo, n, per = _key_blocks(mode, i, TQ, TK)
        if mode == "ca":
            out = lax.fori_loop(lo, lo + per, lambda j, c: step(j, c, True), init)
        else:
            out = _walk(lo, n, per, step, init)
        for g in range(G):
            dq_ref[:, g * dqk:(g + 1) * dqk] = out[g]

    in_specs = [q.rows(TQ), k.full(S), v.full(S), o.rows(TQ), do.rows(TQ),
                pl.BlockSpec((TQ, G * 128), lambda p, i: (i, p))]
    ins = [q.arr, k.arr, v.arr, o.arr, do.arr, lse]
    out_specs = [pl.BlockSpec((TQ, G * dqk), lambda p, i: (i, p)), pl.BlockSpec((S, G * dqk), lambda p, i: (0, p)),
                 pl.BlockSpec((S, G * dv), lambda p, i: (0, p))]
    out_shape = [jax.ShapeDtypeStruct((S, HEADS * dqk), F32), jax.ShapeDtypeStruct((S, HEADS * dqk), F32),
                 jax.ShapeDtypeStruct((S, HEADS * dv), F32)]
    if bias is not None:
        bspec = pl.BlockSpec((G, 3, TK, TK), lambda p, i: (p, 0, 0, 0))
        in_specs.append(bspec)
        ins.append(bias)
        out_specs.append(bspec)
        out_shape.append(jax.ShapeDtypeStruct(bias.shape, F32))
    return _call(body, ins, name=name, grid=(HEADS // G, S // TQ), in_specs=in_specs, out_specs=out_specs,
                 out_shape=out_shape, semantics=("parallel", "arbitrary"))


def _split2(x):
    hi = x.astype(BF16)
    return hi, (x - hi.astype(F32)).astype(BF16)


def _split3(x):
    hi = x.astype(BF16)
    r = x - hi.astype(F32)
    mid = r.astype(BF16)
    return hi, mid, (r - mid.astype(F32)).astype(BF16)


def _stick_block(qb, kb, strict, scale):
    z = _dot_nt(qb, kb) * scale
    sp = jnp.log(1.0 + jnp.exp(-jnp.abs(z)))
    lb = jnp.minimum(z, 0.0) - sp
    l1m = jnp.minimum(-z, 0.0) - sp
    if strict is not None:
        l1m = jnp.where(strict, l1m, 0.0)
    return z, lb, l1m


def _strict_mask(i, j, TQ, TK):
    row, col = _positions(i, j, TQ, TK)
    return col < row


def _tri(T, inclusive):
    r = lax.broadcasted_iota(jnp.int32, (T, T), 0)
    c = lax.broadcasted_iota(jnp.int32, (T, T), 1)
    return ((r >= c) if inclusive else (r > c)).astype(BF16)


def _tri_prefix(T, inclusive):
    r = lax.broadcasted_iota(jnp.int32, (T, T), 0)
    c = lax.broadcasted_iota(jnp.int32, (T, T), 1)
    return ((r <= c) if inclusive else (r < c)).astype(BF16)


def _suffix(parts, tri):
    out = _dot(parts[0], tri)
    for p in parts[1:]:
        out = out + _dot(p, tri)
    return out


def stick_attn_fwd(name, q, k, v, scale):
    S = q.arr.shape[0]
    TQ, TK = _att_tiles("sb", S)
    G, dv = ATT_G, v.w

    def body(q_ref, k_ref, v_ref, o_ref, tot_ref):
        i = pl.program_id(1)
        qs = [q_ref[:, q.lanes(g)] for g in range(G)]
        tri = _tri(TK, False)

        def block(g, carry, strict, ks):
            right, acc = carry
            z, lb, l1m = _stick_block(qs[g], k_ref[ks, k.lanes(g)], strict, scale)
            a = jnp.exp(lb + _suffix(_split2(l1m), tri) + right)
            if strict is not None:
                a = jnp.where(strict, a, 0.0)
            acc = acc + _dot(a.astype(BF16), v_ref[ks, v.lanes(g)])
            return right + jnp.sum(l1m, axis=-1, keepdims=True), acc

        def step(j, carry, masked):
            ks = pl.ds(pl.multiple_of(j * TK, TK), TK)
            strict = _strict_mask(i, j, TQ, TK) if masked else None
            return tuple(block(g, carry[g], strict, ks) for g in range(G))

        init = (jnp.zeros((TQ, 1), F32), jnp.zeros((TQ, dv), F32))
        lo, n, per = _key_blocks("sb", i, TQ, TK)
        out = _walk(lo, n, per, step, (init,) * G, descending=True)
        for g in range(G):
            o_ref[:, g * dv:(g + 1) * dv] = out[g][1].astype(BF16)
            tot_ref[:, g * 128:(g + 1) * 128] = jnp.broadcast_to(out[g][0], (TQ, 128))

    return _call(
        body, [q.arr, k.arr, v.arr], name=name, grid=(HEADS // G, S // TQ),
        in_specs=[q.rows(TQ), k.full(S), v.full(S)],
        out_specs=[pl.BlockSpec((TQ, G * dv), lambda p, i: (i, p)), pl.BlockSpec((TQ, G * 128), lambda p, i: (i, p))],
        out_shape=[jax.ShapeDtypeStruct((S, HEADS * dv), BF16), jax.ShapeDtypeStruct((S, HEADS * 128), F32)],
        semantics=("parallel", "parallel"))


def stick_attn_bwd(name, q, k, v, do, total, scale):
    S = q.arr.shape[0]
    TQ, TK = _att_tiles("sb", S)
    G, dqk, dv = ATT_G, q.w, v.w

    def body(q_ref, k_ref, v_ref, do_ref, tot_ref, dq_ref, dk_ref, dv_ref):
        i = pl.program_id(1)

        @pl.when(i == 0)
        def _():
            dk_ref[...] = jnp.zeros_like(dk_ref)
            dv_ref[...] = jnp.zeros_like(dv_ref)

        qs = [q_ref[:, q.lanes(g)] for g in range(G)]
        dos = [do_ref[:, do.lanes(g)] for g in range(G)]
        tots = [tot_ref[:, g * 128:g * 128 + 1] for g in range(G)]
        upto = _tri_prefix(TK, True)
        before = _tri_prefix(TK, False)

        def step(j, carry, masked):
            ks = pl.ds(pl.multiple_of(j * TK, TK), TK)
            strict = _strict_mask(i, j, TQ, TK) if masked else None
            out = []
            for g in range(G):
                left, gleft, dq = carry[g]
                kb = k_ref[ks, k.lanes(g)]
                z, lb, l1m = _stick_block(qs[g], kb, strict, scale)
                a = jnp.exp(lb + (tots[g] - (left + _suffix(_split3(l1m), upto))))
                if strict is not None:
                    a = jnp.where(strict, a, 0.0)
                gg = a * _dot_nt(dos[g], v_ref[ks, v.lanes(g)])
                c = gleft + _suffix(_split3(gg), before)
                sig = 1.0 / (1.0 + jnp.exp(-z))
                dz = gg * (1.0 - sig) - c * sig
                if strict is not None:
                    dz = jnp.where(strict, dz, 0.0)
                dzb = (dz * scale).astype(BF16)
                dk_ref[ks, g * dqk:(g + 1) * dqk] += _dot_tn(dzb, qs[g])
                dv_ref[ks, g * dv:(g + 1) * dv] += _dot_tn(a.astype(BF16), dos[g])
                out.append((left + jnp.sum(l1m, axis=-1, keepdims=True),
                            gleft + jnp.sum(gg, axis=-1, keepdims=True), dq + _dot(dzb, kb)))
            return tuple(out)

        zero = jnp.zeros((TQ, 1), F32)
        lo, n, per = _key_blocks("sb", i, TQ, TK)
        out = _walk(lo, n, per, step, ((zero, zero, jnp.zeros((TQ, dqk), F32)),) * G)
        for g in range(G):
            dq_ref[:, g * dqk:(g + 1) * dqk] = out[g][2]

    return _call(
        body, [q.arr, k.arr, v.arr, do.arr, total], name=name, grid=(HEADS // G, S // TQ),
        in_specs=[q.rows(TQ), k.full(S), v.full(S), do.rows(TQ), pl.BlockSpec((TQ, G * 128), lambda p, i: (i, p))],
        out_specs=[pl.BlockSpec((TQ, G * dqk), lambda p, i: (i, p)), pl.BlockSpec((S, G * dqk), lambda p, i: (0, p)),
                   pl.BlockSpec((S, G * dv), lambda p, i: (0, p))],
        out_shape=[jax.ShapeDtypeStruct((S, HEADS * dqk), F32), jax.ShapeDtypeStruct((S, HEADS * dqk), F32),
                   jax.ShapeDtypeStruct((S, HEADS * dv), F32)],
        semantics=("parallel", "arbitrary"))


def _skew(x, back):
    T = x.shape[0]
    rows = lax.broadcasted_iota(jnp.int32, (T, T), 0)
    for b in range(T.bit_length() - 1):
        shift = T - (1 << b) if back else 1 << b
        x = jnp.where(((rows >> b) & 1) == 1, pltpu.roll(x, shift, 1), x)
    return x


def _table_rows(table):
    t = jnp.pad(table.T, ((0, 0), (0, 2 * REL_CLIP_LEFT - REL_TABLE)))
    return t.reshape(table.shape[1], 2, REL_CLIP_LEFT)


def rel_bias_blocks(name, table, T):
    assert T == 2 * REL_CLIP_LEFT, "the base rows below are laid out for blocks of 256"

    def body(t_ref, o_ref):
        low, high = t_ref[0:1, :], t_ref[1:2, :]
        first = jnp.broadcast_to(t_ref[0:1, 0:1], (1, REL_CLIP_LEFT))
        qq = lax.broadcasted_iota(jnp.int32, (T, T), 0)
        kk = lax.broadcasted_iota(jnp.int32, (T, T), 1)

        def rolled(row):
            return _skew(jnp.broadcast_to(row, (T, T)), False)

        far = jnp.concatenate([first, low], axis=1)
        near = jnp.concatenate([high, jnp.zeros_like(high)], axis=1)
        o_ref[0] = jnp.where(kk >= qq, rolled(near), rolled(far))
        o_ref[1] = jnp.where(kk >= qq, rolled(far), jnp.broadcast_to(t_ref[0:1, 0:1], (T, T)))
        o_ref[2] = jnp.broadcast_to(t_ref[0:1, 0:1], (T, T))

    return pl.pallas_call(
        body, name=name, grid=(HEADS,), in_specs=[pl.BlockSpec((None, 2, REL_CLIP_LEFT), lambda h: (h, 0, 0))],
        out_specs=pl.BlockSpec((None, 3, T, T), lambda h: (h, 0, 0, 0)),
        out_shape=jax.ShapeDtypeStruct((HEADS, 3, T, T), F32), compiler_params=_params("parallel"),
    )(_table_rows(table))


def rel_bias_grad(name, dbias):
    T = dbias.shape[-1]
    L = REL_CLIP_LEFT
    assert T == 2 * L

    def body(d_ref, o_ref):
        qq = lax.broadcasted_iota(jnp.int32, (T, T), 0)
        ll = lax.broadcasted_iota(jnp.int32, (T, T), 1)
        wrapped = ll + qq >= T

        def columns(d):
            x = _skew(d_ref[d], True)
            return (jnp.sum(jnp.where(wrapped, 0.0, x), axis=0, keepdims=True),
                    jnp.sum(jnp.where(wrapped, x, 0.0), axis=0, keepdims=True))

        pos0, neg0 = columns(0)
        pos1, neg1 = columns(1)
        clipped = (jnp.sum(neg0[:, :L]) + jnp.sum(pos1[:, :L]) + jnp.sum(neg1) + jnp.sum(d_ref[2]))
        lane = lax.broadcasted_iota(jnp.int32, (1, L), 1)
        low = neg0[:, L:] + pos1[:, L:]
        o_ref[...] = jnp.zeros_like(o_ref)
        o_ref[0:1, :] = jnp.where(lane == 0, low + clipped, low)
        o_ref[1:2, :] = pos0[:, :L]

    rows = pl.pallas_call(
        body, name=name, grid=(HEADS,), in_specs=[pl.BlockSpec((None, 3, T, T), lambda h: (h, 0, 0, 0))],
        out_specs=pl.BlockSpec((None, 8, L), lambda h: (h, 0, 0)), out_shape=jax.ShapeDtypeStruct((HEADS, 8, L), F32),
        compiler_params=_params("parallel"),
    )(dbias)
    return rows[:, :2, :].reshape(HEADS, 2 * L)[:, :REL_TABLE].T


def _place():
    return lax.axis_index("x"), lax.axis_index("y"), lax.axis_index("c")


def all_gather(name, shards):
    n = len(shards)

    def body(*refs):
        x_refs, out_refs = refs[:n], refs[n:2 * n]
        send_sems, recv_sems, local_sems = refs[2 * n:]
        x, y, c = _place()
        me, sibling = (x, y, c), (x, y, 1 - c)
        chips = [(1 - x, y), (x, 1 - y), (1 - x, 1 - y)]

        def block(t, dev):
            return out_refs[t].at[4 * dev[0] + 2 * dev[1] + dev[2]]

        def copy(t, k, dev, to, src=None):
            return pltpu.make_async_remote_copy(
                src_ref=block(t, dev) if src is None else src, dst_ref=block(t, dev),
                send_sem=send_sems.at[t, k], recv_sem=recv_sems.at[t, k], device_id=to, device_id_type=MESH)

        mine = [pltpu.make_async_copy(x_refs[t], block(t, me), local_sems.at[t]) for t in range(n)]
        for cp in mine:
            cp.start()
        first = []
        for t in range(n):
            first.append(copy(t, 0, me, sibling, src=x_refs[t]))
            first += [copy(t, 1 + j, me, (*chip, c), src=x_refs[t]) for j, chip in enumerate(chips)]
        for cp in first:
            cp.start()
        passed = []
        for j, chip in enumerate(chips):
            for t in range(n):
                copy(t, 1 + j, (*chip, c), me).wait_recv()
                cp = copy(t, 4 + j, (*chip, c), sibling)
                cp.start()
                passed.append(cp)
        for t in range(n):
            copy(t, 0, sibling, me).wait_recv()
            for j, chip in enumerate(chips):
                copy(t, 4 + j, (*chip, 1 - c), me).wait_recv()
        for cp in first + passed:
            cp.wait_send()
        for cp in mine:
            cp.wait()

    return pl.pallas_call(
        body, name=name, in_specs=[HBM] * n, out_specs=[HBM] * n,
        out_shape=[jax.ShapeDtypeStruct((N_DEV, *s.shape), s.dtype) for s in shards],
        scratch_shapes=[pltpu.SemaphoreType.DMA((n, 7)), pltpu.SemaphoreType.DMA((n, 7)), pltpu.SemaphoreType.DMA((n,))],
    )(*shards)


def _remote(src, dst, send_sems, recv_sems, k, to):
    return pltpu.make_async_remote_copy(src_ref=src, dst_ref=dst, send_sem=send_sems.at[k], recv_sem=recv_sems.at[k],
                                        device_id=to, device_id_type=MESH)


class Future:
    def __init__(self):
        self.value = None

    def get(self):
        if self.value is None:
            SCHED.flush()
        return self.value


def gather_jobs(name, shards, wants_chips, want_sibling):
    n, shares = len(shards), len(wants_chips)
    result = Future()
    lands = [jax.ShapeDtypeStruct((N_DEV, *s.shape), s.dtype) for s in shards]

    def to_chips(share):
        def copies(in_refs, out_refs, send_sems, recv_sems):
            x, y, c = _place()
            me = 4 * x + 2 * y + c
            cps = []
            for t in range(n):
                rows = shards[t].shape[0] // shares
                mine = pl.ds(share * rows, rows)
                src, dst = in_refs[t].at[mine], out_refs[t].at[me, mine]
                cps.append(pltpu.make_async_copy(src, dst, send_sems.at[4 * t]))
                for j, chip in enumerate([(1 - x, y), (x, 1 - y), (1 - x, 1 - y)]):
                    cps.append(_remote(src, dst, send_sems, recv_sems, 4 * t + 1 + j, (*chip, c)))
            return cps
        return copies

    def to_sibling(in_refs, out_refs, send_sems, recv_sems):
        x, y, c = _place()
        return [_remote(in_refs[t].at[2 * chip + c], out_refs[t].at[2 * chip + c], send_sems, recv_sems, 4 * t + chip,
                        (x, y, 1 - c)) for t in range(n) for chip in range(4)]

    def post(share, landed):
        if share == shares:
            SCHED.post(Job(name + "_sibling", want_sibling, landed, lands, 4 * n, to_sibling,
                           lambda final: setattr(result, "value", final), aliases={t: t for t in range(n)}))
        else:
            SCHED.post(Job(f"{name}_chips{share}", wants_chips[share], list(shards) + (landed or []), lands, 4 * n,
                           to_chips(share), lambda outs: post(share + 1, outs),
                           aliases={n + t: t for t in range(n)} if landed else None))

    post(0, None)
    return result


def scatter_jobs(name, grads, core, chip, want_sibling, wants_chips):
    n = len(grads)
    result = Future()
    sums = [None] * n

    def to_sibling(in_refs, out_refs, send_sems, recv_sems):
        x, y, c = _place()
        return [_remote(in_refs[t].at[1 - c], out_refs[t], send_sems, recv_sems, t, (x, y, 1 - c)) for t in range(n)]

    def after_sibling(received):
        parts = [add_sibling(f"{name}_add{t}", grads[t], received[t], core) for t in range(n)]
        for group, want in wants_chips:
            def to_chips(in_refs, out_refs, send_sems, recv_sems, m=len(group)):
                x, y, c = _place()
                return [_remote(in_refs[t].at[2 * cx + cy], out_refs[t].at[j], send_sems, recv_sems, 3 * t + j,
                                (cx, cy, c))
                        for t in range(m) for j, (cx, cy) in enumerate([(1 - x, y), (x, 1 - y), (1 - x, 1 - y)])]

            def after_chips(received, group=group):
                for t, r in zip(group, received):
                    sums[t] = sum_chips(f"{name}_sum{t}", parts[t], r, chip)
                if all(s is not None for s in sums):
                    result.value = sums

            mine = [parts[t] for t in group]
            SCHED.post(Job(f"{name}_chips{group[0]}", want, mine,
                           [jax.ShapeDtypeStruct((3, *p.shape[1:]), p.dtype) for p in mine], 3 * len(mine), to_chips,
                           after_chips))

    SCHED.post(Job(name + "_sibling", want_sibling, grads, [jax.ShapeDtypeStruct(g.shape[1:], g.dtype) for g in grads],
                   n, to_sibling, after_sibling))
    return result


def _as_rows(shape):
    return (int(np.prod(shape[:-1])), shape[-1])


ELEMENTWISE_BLOCK = 256 * 1024
PARTIAL_SUM_BLOCK = 2 * 1024 * 1024


def _row_tile(rows, cols, block=ELEMENTWISE_BLOCK):
    return _tile(rows, max(128, block // cols // 128 * 128))


def add_sibling(name, grad, recv, core):
    rows, cols = _as_rows(grad.shape[2:])
    tr = _row_tile(rows, cols, PARTIAL_SUM_BLOCK)

    def body(c_ref, g_ref, r_ref, o_ref):
        o_ref[...] = (g_ref[...].astype(F32) + r_ref[...].astype(F32)).astype(BF16)

    blk = pl.BlockSpec((None, tr, cols), lambda k, i, c_ref: (k, i, 0))
    return pl.pallas_call(
        body, name=name,
        grid_spec=pltpu.PrefetchScalarGridSpec(
            num_scalar_prefetch=1, grid=(4, rows // tr),
            in_specs=[pl.BlockSpec((None, None, tr, cols), lambda k, i, c_ref: (c_ref[0], k, i, 0)), blk],
            out_specs=blk),
        out_shape=jax.ShapeDtypeStruct((4, rows, cols), BF16), compiler_params=_params("parallel", "parallel"),
    )(core, grad.reshape(2, 4, rows, cols), recv.reshape(4, rows, cols)).reshape(recv.shape)


def sum_chips(name, part, recv, chip):
    shape = part.shape[1:]
    rows, cols = _as_rows(shape)
    tr = _row_tile(rows, cols, PARTIAL_SUM_BLOCK)

    def body(c_ref, p_ref, r_ref, o_ref):
        o_ref[...] = (p_ref[...].astype(F32) + r_ref[0].astype(F32) + r_ref[1].astype(F32) + r_ref[2].astype(F32))

    return pl.pallas_call(
        body, name=name,
        grid_spec=pltpu.PrefetchScalarGridSpec(
            num_scalar_prefetch=1, grid=(rows // tr,),
            in_specs=[pl.BlockSpec((None, tr, cols), lambda i, c_ref: (c_ref[0], i, 0)),
                      pl.BlockSpec((3, tr, cols), lambda i, c_ref: (0, i, 0))],
            out_specs=pl.BlockSpec((tr, cols), lambda i, c_ref: (i, 0))),
        out_shape=jax.ShapeDtypeStruct((rows, cols), F32), compiler_params=_params("parallel"),
    )(chip, part.reshape(4, rows, cols), recv.reshape(3, rows, cols)).reshape(shape)


def sum_devices(name, gathered):
    _, rows, cols = gathered.shape

    def body(g_ref, o_ref):
        acc = g_ref[0]
        for d in range(1, N_DEV):
            acc = acc + g_ref[d]
        o_ref[...] = acc

    return pl.pallas_call(body, name=name, out_shape=jax.ShapeDtypeStruct((rows, cols), F32))(gathered)


def adamw(name, w, g, m, v):
    shape = w.shape
    rows, cols = _as_rows(shape)
    tr = _row_tile(rows, cols) if rows % 8 == 0 else rows
    c1 = 1.0 / (1.0 - ADAM_B1 ** ADAM_STEP)
    c2 = 1.0 / (1.0 - ADAM_B2 ** ADAM_STEP)

    def body(w_ref, g_ref, m_ref, v_ref, d_ref, mo_ref, vo_ref):
        g_ = g_ref[...]
        m_ = ADAM_B1 * m_ref[...] + (1.0 - ADAM_B1) * g_
        v_ = ADAM_B2 * v_ref[...] + (1.0 - ADAM_B2) * (g_ * g_)
        d_ref[...] = -ADAM_LR * ((m_ * c1) / (jnp.sqrt(v_ * c2) + ADAM_EPS) + ADAM_WD * w_ref[...])
        mo_ref[...] = m_
        vo_ref[...] = v_

    blk = pl.BlockSpec((tr, cols), lambda i: (i, 0))
    outs = pl.pallas_call(
        body, name=name, grid=(rows // tr,), in_specs=[blk] * 4, out_specs=[blk] * 3,
        out_shape=[jax.ShapeDtypeStruct((rows, cols), F32)] * 3, compiler_params=_params("parallel"),
    )(*[a.reshape(rows, cols) for a in (w, g, m, v)])
    return [o.reshape(shape) for o in outs]


def _spread_rope(r):
    z = jnp.zeros_like(r[..., :32])
    return jnp.concatenate([r[..., :32], z, r[..., 32:], z], -1)


def _gather_rope(r):
    return jnp.concatenate([r[..., :32], r[..., 64:96]], -1)


def pad_w_uq(w):
    w = w.reshape(w.shape[0], -1, MLA_NOPE + MLA_ROPE)
    return jnp.concatenate([w[..., :MLA_NOPE], _spread_rope(w[..., MLA_NOPE:])], -1).reshape(w.shape[0], -1)


def unpad_w_uq(g):
    g = g.reshape(g.shape[0], -1, 2 * MLA_NOPE)
    return jnp.concatenate([g[..., :MLA_NOPE], _gather_rope(g[..., MLA_NOPE:])], -1).reshape(g.shape[0], -1)


def pad_w_down(w):
    lat = MLA_Q_LORA + MLA_KV_LORA
    return jnp.concatenate([w[:, :lat], _spread_rope(w[:, lat:])], -1)


def unpad_w_down(g):
    lat = MLA_Q_LORA + MLA_KV_LORA
    return jnp.concatenate([g[:, :lat], _gather_rope(g[:, lat:])], -1)


def _heads(arr, width, first=0, off=0, w=None):
    return HeadCols(arr, width, lambda p: first // ATT_G + p, off, w)


def mla_forward(h16, w, gq, gkv, tables):
    cos, sin = tables
    down = mm_nn("mla_down", h16, w["down"], [F32])[0]
    cq, ckv = rms_fwd("mla_rms", down, gq, gkv)
    q = mm_nn("mla_uq", cq, w["uq"], [F32])[0]
    kv = mm_nn("mla_ukv", ckv, w["ukv"], [BF16])[0]
    qr, kp = mla_prep_fwd("mla_prep", q, kv, down, cos, sin)
    scale = (MLA_NOPE + MLA_ROPE) ** -0.5
    o, lse = softmax_attn_fwd("mla_attn", "mla", _heads(qr, 256), _heads(kp, 256), _heads(kv, 256, off=128, w=128), scale)
    m = mm_nn("mla_wo", o, w["wo"], [F32])[0]
    return m, (down, cq, ckv, qr, kp, kv, o, lse)


def mla_backward(du16, h16t, saved, w, gq, gkv, tables, emit):
    cos, sin = tables
    down, cq, ckv, qr, kp, kv, o, lse = saved
    scale = (MLA_NOPE + MLA_ROPE) ** -0.5
    emit("wo", mm_tn("mla_dwo", o, du16, "row", w["wo"].R, w["wo"].C))
    do = mm_nt("mla_do", du16, w["wo"], BF16)
    dq, dk, dv = softmax_attn_bwd("mla_attn_bwd", "mla", _heads(qr, 256), _heads(kp, 256), _heads(kv, 256, off=128, w=128),
                                  _heads(o, 128), _heads(do, 128), lse, scale)
    dq16, dkv16, dkr = mla_prep_bwd("mla_prep_bwd", dq, dk, dv, cos, sin)
    emit("uq", mm_tn("mla_duq", cq, dq16, "col", w["uq"].R, w["uq"].C))
    dcq = mm_nt("mla_dcq", dq16, w["uq"], F32)
    emit("ukv", mm_tn("mla_dukv", ckv, dkv16, "col", w["ukv"].R, w["ukv"].C))
    dckv = mm_nt("mla_dckv", dkv16, w["ukv"], F32)
    ddown, dgq, dgkv = rms_bwd("mla_rms_bwd", down, dcq, dckv, dkr, gq, gkv)
    emit("down", mm_tn("mla_ddown", h16t, ddown, "row", w["down"].R, w["down"].C, transposed=True))
    dh = mm_nt("mla_dh", ddown, w["down"], F32)
    return dh, (dgq, dgkv)


def qkv_forward(kind, h16, w, bias=None):
    qkv = mm_nn(kind + "_qkv", h16, w["qkv"], [BF16])[0]
    q, k, v = _heads(qkv, 128), _heads(qkv, 128, HEADS), _heads(qkv, 128, 2 * HEADS)
    scale = HEAD_DIM ** -0.5
    if kind == "sb":
        o, lse = stick_attn_fwd("sb_attn", q, k, v, scale)
    else:
        o, lse = softmax_attn_fwd("ca_attn", "ca", q, k, v, scale, bias)
    m = mm_nn(kind + "_wo", o, w["wo"], [F32])[0]
    return m, (qkv, o, lse)


def qkv_backward(kind, du16, h16t, saved, w, bias=None):
    qkv, o, lse = saved
    q, k, v = _heads(qkv, 128), _heads(qkv, 128, HEADS), _heads(qkv, 128, 2 * HEADS)
    scale = HEAD_DIM ** -0.5
    g = {"wo": mm_tn(kind + "_dwo", o, du16, "row", w["wo"].R, w["wo"].C)}
    do = mm_nt(kind + "_do", du16, w["wo"], BF16)
    dbias = None
    if kind == "sb":
        dq, dk, dv = stick_attn_bwd("sb_attn_bwd", q, k, v, _heads(do, 128), lse, scale)
    else:
        dq, dk, dv, dbias = softmax_attn_bwd("ca_attn_bwd", "ca", q, k, v, _heads(o, 128), _heads(do, 128), lse,
                                             scale, bias)
    dqkv = jnp.concatenate([dq, dk, dv], axis=1).astype(BF16)
    g["qkv"] = mm_tn(kind + "_dqkv", h16t, dqkv, "col", w["qkv"].R, w["qkv"].C, transposed=True)
    dh = mm_nt(kind + "_dh", dqkv, w["qkv"], F32)
    return dh, g, dbias


def mlp_forward(h16, w):
    a, z, zt = mm_nn("ffn_in", h16, w["w_in"], [F32, BF16, BF16], epilogue=_relu2_epilogue,
                     transposed=(False, False, True))
    f = mm_nn("ffn_out", z, w["w_out"], [F32])[0]
    return f, (a, zt)


def mlp_backward(du16, h16t, saved, w):
    a, zt = saved
    da = mm_nt("ffn_da", du16, w["w_out"], BF16, epilogue=_mulrelu_epilogue, extra=a)
    g = {"w_out": mm_tn("ffn_dwout", zt, du16, "row", w["w_out"].R, w["w_out"].C, transposed=True)}
    dh = mm_nt("ffn_dh", da, w["w_in"], F32)
    g["w_in"] = mm_tn("ffn_dwin", h16t, da, "col", w["w_in"].R, w["w_in"].C, transposed=True)
    return dh, g


WEIGHTS = ("ln_mix_g", "ln_mix_b", "ln_ffn_g", "ln_ffn_b", "ffn_w_in", "ffn_w_out", "mla_w_down", "mla_q_norm_g",
           "mla_w_uq", "mla_kv_norm_g", "mla_w_ukv", "mla_w_o", "sb_w_qkv", "sb_w_o", "ca_w_qkv", "ca_rel_bias",
           "ca_w_o")
MIXERS = ("mla", "sb", "ca")
LAYER_WEIGHTS = {
    "mla": (("down", "mla_w_down", "row"), ("uq", "mla_w_uq", "col"), ("ukv", "mla_w_ukv", "col"),
            ("wo", "mla_w_o", "row")),
    "sb": (("qkv", "sb_w_qkv", "col"), ("wo", "sb_w_o", "row")),
    "ca": (("qkv", "ca_w_qkv", "col"), ("wo", "ca_w_o", "row")),
    "ffn": (("w_in", "ffn_w_in", "col"), ("w_out", "ffn_w_out", "row")),
}
PAD = {"mla_w_down": pad_w_down, "mla_w_uq": pad_w_uq}
UNPAD = {"mla_w_down": unpad_w_down, "mla_w_uq": unpad_w_uq}


def _pack_rows(vectors):
    flat = jnp.concatenate([v.reshape(-1) for v in vectors])
    n = flat.shape[0]
    rows = -(-n // 1024) * 8
    offsets = np.cumsum([0] + [int(np.prod(v.shape)) for v in vectors])
    return jnp.pad(flat, (0, rows * 128 - n)).reshape(rows, 128), offsets


def _part(i, part):
    group, idx = (MIXERS[i % 3], i // 3) if part == "mix" else ("ffn", i)
    return [(key, name, how, idx) for key, name, how in LAYER_WEIGHTS[group]]


def kernel(x, ln_mix_g, ln_mix_b, ln_ffn_g, ln_ffn_b, ffn_w_in, ffn_w_out, mla_w_down, mla_q_norm_g, mla_w_uq, mla_kv_norm_g, mla_w_ukv, mla_w_o, sb_w_qkv, sb_w_o, ca_w_qkv, ca_rel_bias, ca_w_o, loss_target, m_ln_mix_g, m_ln_mix_b, m_ln_ffn_g, m_ln_ffn_b, m_ffn_w_in, m_ffn_w_out, m_mla_w_down, m_mla_q_norm_g, m_mla_w_uq, m_mla_kv_norm_g, m_mla_w_ukv, m_mla_w_o, m_sb_w_qkv, m_sb_w_o, m_ca_w_qkv, m_ca_rel_bias, m_ca_w_o, v_ln_mix_g, v_ln_mix_b, v_ln_ffn_g, v_ln_ffn_b, v_ffn_w_in, v_ffn_w_out, v_mla_w_down, v_mla_q_norm_g, v_mla_w_uq, v_mla_kv_norm_g, v_mla_w_ukv, v_mla_w_o, v_sb_w_qkv, v_sb_w_o, v_ca_w_qkv, v_ca_rel_bias, v_ca_w_o):
    w = dict(zip(WEIGHTS, (ln_mix_g, ln_mix_b, ln_ffn_g, ln_ffn_b, ffn_w_in, ffn_w_out, mla_w_down, mla_q_norm_g,
                           mla_w_uq, mla_kv_norm_g, mla_w_ukv, mla_w_o, sb_w_qkv, sb_w_o, ca_w_qkv, ca_rel_bias,
                           ca_w_o)))
    mom = dict(zip(WEIGHTS, (m_ln_mix_g, m_ln_mix_b, m_ln_ffn_g, m_ln_ffn_b, m_ffn_w_in, m_ffn_w_out, m_mla_w_down,
                             m_mla_q_norm_g, m_mla_w_uq, m_mla_kv_norm_g, m_mla_w_ukv, m_mla_w_o, m_sb_w_qkv,
                             m_sb_w_o, m_ca_w_qkv, m_ca_rel_bias, m_ca_w_o)))
    var = dict(zip(WEIGHTS, (v_ln_mix_g, v_ln_mix_b, v_ln_ffn_g, v_ln_ffn_b, v_ffn_w_in, v_ffn_w_out, v_mla_w_down,
                             v_mla_q_norm_g, v_mla_w_uq, v_mla_kv_norm_g, v_mla_w_ukv, v_mla_w_o, v_sb_w_qkv,
                             v_sb_w_o, v_ca_w_qkv, v_ca_rel_bias, v_ca_w_o)))
    S, D = x.shape[1], x.shape[2]
    xi, yi, ci = _place()
    core = ci.astype(jnp.int32).reshape(1)
    chip = (2 * xi + yi).astype(jnp.int32).reshape(1)
    me = 4 * xi + 2 * yi + ci
    tables = rope_tables(S)
    n_mla = mla_w_down.shape[0]
    lat = MLA_Q_LORA // N_DEV

    gains = jnp.pad(jnp.stack([mla_q_norm_g.reshape(-1), mla_kv_norm_g.reshape(-1)]), ((0, 6), (0, 128 - n_mla * lat)))
    gains = all_gather("ag_gains", [gains])[0]

    def full_gain(row, slot):
        return gains[:, row, slot * lat:(slot + 1) * lat].reshape(-1)

    first_matmul = {"mla": "mla_down", "sb": "sb_qkv", "ca": "ca_qkv"}

    def post_gather(i, part):
        kind = MIXERS[i % 3]
        specs = _part(i, part)
        shards = [PAD.get(name, lambda a: a)(w[name][idx]).astype(BF16) for _, name, _, idx in specs]
        if part == "mix":
            wants = ([MIXERS[(i - 1) % 3] + "_wo", "ffn_in"], "ffn_out") if i > 0 else ([None], None)
            return [(specs, gather_jobs(f"ag_mix{i}", shards, *wants))]
        w_in = (["ffn_out", first_matmul[kind]], kind + "_attn") if i > 0 else (["mla_down", "mla_uq"], "mla_ukv")
        return [(specs[:1], gather_jobs(f"ag_w_in{i}", shards[:1], *w_in)),
                (specs[1:], gather_jobs(f"ag_w_out{i}", shards[1:], [kind + "_attn"], kind + "_wo"))]

    def gathered(posted):
        return {key: Weight(how, g) for specs, future in posted for (key, _, how, _), g in zip(specs, future.get())}

    long_attention = "sb_attn_bwd"

    def post_scatter(i, part, g):
        kind = MIXERS[i % 3]
        specs = _part(i, part)
        n = len(specs)
        grads_of = lambda sp: [g[key] for key, _, _, _ in sp]
        if part == "ffn":
            w_in_rides = long_attention if MIXERS[(i - 1) % 3] == "sb" and i > 0 else kind + "_attn_bwd"
            return [(specs[:1], scatter_jobs(f"rs_w_in{i}", grads_of(specs[:1]), core, chip, kind + "_dwo",
                                             [([0], w_in_rides)])),
                    (specs[1:], scatter_jobs(f"rs_w_out{i}", grads_of(specs[1:]), core, chip, kind + "_dwo",
                                             [([0], kind + "_attn_bwd")]))]
        wants = ("ffn_da", [(list(range(n - 1)), "ffn_dwout"), ([n - 1], "ffn_dh")])
        return [(specs, scatter_jobs(f"rs_mix{i}", grads_of(specs), core, chip, *wants))]

    SCHED.pending.clear()
    bias = rel_bias_blocks("ca_bias", ca_rel_bias[0], _att_tiles("ca", S)[1])

    h, h16 = x[0], x[0].astype(BF16)
    h16t = transpose("x_t", h16)
    saved, layers = [], []
    mix_w, ffn_w = post_gather(0, "mix"), None
    for i in range(DEPTH):
        kind, slot = MIXERS[i % 3], i // 3
        lw = gathered(mix_w)
        if i == 0:
            ffn_w = post_gather(0, "ffn")
        this_ffn = ffn_w
        if i + 1 < DEPTH:
            mix_w, ffn_w = post_gather(i + 1, "mix"), post_gather(i + 1, "ffn")
        if kind == "mla":
            mix, s_mix = mla_forward(h16, lw, full_gain(0, slot), full_gain(1, slot), tables)
        else:
            mix, s_mix = qkv_forward(kind, h16, lw, bias if kind == "ca" else None)
        y, y16, y16t, xh1, rs1 = ln_fwd("ln_mix", h, mix, ln_mix_g[i], ln_mix_b[i])
        lw.update(gathered(this_ffn))
        f, s_mlp = mlp_forward(y16, lw)
        y2, y2_16, y2_16t, xh2, rs2 = ln_fwd("ln_ffn", y, f, ln_ffn_g[i], ln_ffn_b[i])
        saved.append((h16t, s_mix, xh1, rs1, y16t, s_mlp, xh2, rs2))
        layers.append(lw)
        h, h16, h16t = y2, y2_16, y2_16t
    sq, dy = loss_fwd_bwd("loss", h, loss_target[0])
    loss = 0.5 / D * lax.psum(sq[0, 0], ("x", "y", "c"))

    ga, gb = dy, None
    grads = {name: [None] * w[name].shape[0] for name in WEIGHTS}
    dbias = None
    scattered = []
    for i in reversed(range(DEPTH)):
        kind, slot = MIXERS[i % 3], i // 3
        lw = layers[i]
        h16_in, s_mix, xh1, rs1, y16, s_mlp, xh2, rs2 = saved[i]
        du, du16, grads["ln_ffn_g"][i], grads["ln_ffn_b"][i] = ln_bwd("ln_ffn_bwd", ga, gb, xh2, rs2, ln_ffn_g[i])
        dh_mlp, g_mlp = mlp_backward(du16, y16, s_mlp, lw)
        scattered += post_scatter(i, "ffn", g_mlp)
        du, du16, grads["ln_mix_g"][i], grads["ln_mix_b"][i] = ln_bwd("ln_mix_bwd", du, dh_mlp, xh1, rs1, ln_mix_g[i])
        if kind == "mla":
            rides = {"wo": ("mla_attn_bwd", "mla_duq"), "uq": ("mla_dcq", "mla_dukv"), "ukv": ("mla_dckv", "mla_ddown"),
                     "down": ("mla_dh", None)}
            by_key = {spec[0]: spec for spec in _part(i, "mix")}
            g_mix = {}

            def emit_alone(key, grad):
                sibling, chips = rides[key]
                scattered.append(([by_key[key]], scatter_jobs(f"rs_{key}{i}", [grad], core, chip, sibling,
                                                              [([0], chips)])))

            dh_mix, (dgq, dgkv) = mla_backward(du16, h16_in, s_mix, lw, full_gain(0, slot), full_gain(1, slot),
                                               tables, emit_alone if i == 0 else g_mix.__setitem__)
            if i > 0:
                scattered += post_scatter(i, "mix", g_mix)
            grads["mla_q_norm_g"][slot], grads["mla_kv_norm_g"][slot] = dgq, dgkv
        else:
            dh_mix, g_mix, db = qkv_backward(kind, du16, h16_in, s_mix, lw, bias if kind == "ca" else None)
            dbias = db if kind == "ca" else dbias
            scattered += post_scatter(i, "mix", g_mix)
        ga, gb = du, dh_mix
    grad_x = axpy("grad_x", ga, gb)[None]
    SCHED.flush()
    for specs, future in scattered:
        for (_, name, _, idx), g in zip(specs, future.get()):
            grads[name][idx] = UNPAD.get(name, lambda a: a)(g)
    grads["ca_rel_bias"][0] = rel_bias_grad("ca_bias_grad", dbias)

    small = ("ln_mix_g", "ln_mix_b", "ln_ffn_g", "ln_ffn_b", "ca_rel_bias", "mla_q_norm_g", "mla_kv_norm_g")
    packed, offsets = _pack_rows([g for name in small for g in grads[name]])
    total = sum_devices("sum_small", all_gather("ag_small", [packed])[0]).reshape(-1)
    pos = 0
    for name in small:
        for idx, g in enumerate(grads[name]):
            full = total[offsets[pos]:offsets[pos + 1]]
            pos += 1
            if name in ("mla_q_norm_g", "mla_kv_norm_g"):
                full = lax.dynamic_slice(full, (me * lat,), (lat,))
            grads[name][idx] = full.reshape(w[name].shape[1:])

    g_out, d_out, m_out, v_out = [], [], [], []
    for name in WEIGHTS:
        g = jnp.stack(grads[name])
        delta, new_m, new_v = adamw("adamw_" + name, w[name], g, mom[name], var[name])
        g_out.append(g)
        d_out.append(delta)
        m_out.append(new_m)
        v_out.append(new_v)
    return (loss, grad_x, *g_out, *d_out, *m_out, *v_out)
```
